```python
import jax, jax.numpy as jnp
from jax import lax
import numpy as np

D_MODEL = 2048
BATCH = 4
SEQ = 2048
DEPTH = 1

HEAD_DIM = 64
ATTN_SCALE = HEAD_DIM ** -0.5
SWA_Q_HEADS = 16
SWA_KV_HEADS = 2
SWA_WINDOW = 128
SWA_BLOCK = 128
MOBA_HEADS = 16
MOBA_BLOCK = 256
MOBA_TOPK = 3
MOBA_Q_CHUNK = 16
N_GROUPS = 4
EXPERTS_PER_GROUP = 4
N_EXPERTS = N_GROUPS * EXPERTS_PER_GROUP
D_EXPERT = 512
TOPK_IN_GROUP = 2
EPS = 1e-6

SWA_Q_DIM = SWA_Q_HEADS * HEAD_DIM
SWA_KV_DIM = SWA_KV_HEADS * HEAD_DIM
MOBA_DIM = MOBA_HEADS * HEAD_DIM
IN_COLS = SWA_Q_DIM + 2 * SWA_KV_DIM + 3 * MOBA_DIM + 2 * D_MODEL

kernel_name = "hybrid_swa_moba_hier_moe"


def rms_norm(x, g):
    xf = x.astype(jnp.float32)
    y = xf * lax.rsqrt(jnp.mean(xf * xf, axis=-1, keepdims=True) + EPS)
    return (y * g.astype(jnp.float32)).astype(x.dtype)


def alibi_slopes(n):
    return jnp.exp2(-8.0 * jnp.arange(1, n + 1, dtype=jnp.float32) / n)


def swa_attention(q, k, v, sinks):
    B, S, Hq, dh = q.shape
    Hkv = k.shape[2]
    G = Hq // Hkv
    L = SWA_BLOCK
    nb = S // L
    qb = q.reshape(B, nb, L, Hkv, G, dh)
    kb = k.reshape(B, nb, L, Hkv, dh)
    vb = v.reshape(B, nb, L, Hkv, dh)
    zero = jnp.zeros_like(kb[:, :1])
    kw = jnp.concatenate([jnp.concatenate([zero, kb[:, :-1]], 1), kb], 2)
    vw = jnp.concatenate([jnp.concatenate([zero, vb[:, :-1]], 1), vb], 2)
    logits = jnp.einsum('bnqhgd,bnkhd->bnhgqk', qb, kw).astype(jnp.float32) * ATTN_SCALE
    qpos = jnp.arange(L)[:, None] + L
    kpos = jnp.arange(2 * L)[None, :]
    dist = qpos - kpos
    blk_ok = (jnp.arange(nb)[:, None, None] > 0) | (kpos[None] >= L)
    mask = (dist >= 0) & (dist < SWA_WINDOW) & blk_ok
    slopes = alibi_slopes(Hq).reshape(Hkv, G)[:, :, None, None]
    logits = logits - slopes * dist.astype(jnp.float32)
    logits = jnp.where(mask[None, :, None, None], logits, -jnp.inf)
    sink = jnp.broadcast_to(sinks.astype(jnp.float32).reshape(Hkv, G)[None, None, :, :, None, None],
                            logits.shape[:-1] + (1,))
    p = jax.nn.softmax(jnp.concatenate([logits, sink], -1), axis=-1)[..., :-1]
    out = jnp.einsum('bnhgqk,bnkhd->bnqhgd', p.astype(v.dtype), vw)
    return out.reshape(B, S, Hq, dh)


def moba_attention(q, k, v):
    B, S, H, dh = q.shape
    L = MOBA_BLOCK
    C = MOBA_Q_CHUNK
    nb = -(-S // L)
    Sp = nb * L
    padw = ((0, 0), (0, Sp - S), (0, 0), (0, 0))
    q, k, v = [jnp.pad(t, padw).transpose(0, 2, 1, 3) for t in (q, k, v)]
    kb = k.reshape(B, H, nb, L, dh)
    vb = v.reshape(B, H, nb, L, dh)
    t_blk = jnp.arange(Sp) // L
    slopes = alibi_slopes(H)[None, :, None, None]
    topk = min(MOBA_TOPK, nb - 1)
    if topk > 0:
        k_mean = jnp.mean(kb.astype(jnp.float32), axis=3)
        gate = jnp.einsum('bhtd,bhnd->bhtn', q.astype(jnp.float32), k_mean)
        past = jnp.arange(nb)[None, :] < t_blk[:, None]
        gate = jnp.where(past, gate, -jnp.inf)
        _, g_idx = lax.top_k(gate, topk)
        sel_ok = jnp.arange(topk)[None, :] < t_blk[:, None]
    gather = jax.vmap(jax.vmap(lambda blocks, ix: blocks[ix]))

    def chunk(c):
        start = c * C
        qc = lax.dynamic_slice_in_dim(q, start, C, axis=2)
        tq = start + jnp.arange(C)
        own = start // L
        k_own = lax.dynamic_index_in_dim(kb, own, axis=2, keepdims=False)
        v_own = lax.dynamic_index_in_dim(vb, own, axis=2, keepdims=False)
        d_own = (tq[:, None] - (own * L + jnp.arange(L))[None, :])
        lo = jnp.einsum('bhcd,bhkd->bhck', qc, k_own).astype(jnp.float32) * ATTN_SCALE
        lo = jnp.where(d_own >= 0, lo - slopes * d_own.astype(jnp.float32), -jnp.inf)
        if topk > 0:
            idx_c = lax.dynamic_slice_in_dim(g_idx, start, C, axis=2)
            ok_c = lax.dynamic_slice_in_dim(sel_ok, start, C, axis=0)
            kg = gather(kb, idx_c)
            vg = gather(vb, idx_c)
            d_g = tq[:, None, None] - (idx_c[..., None] * L + jnp.arange(L))
            lg = jnp.einsum('bhcd,bhcjkd->bhcjk', qc, kg).astype(jnp.float32) * ATTN_SCALE
            lg = lg - slopes[..., None] * d_g.astype(jnp.float32)
            lg = jnp.where(ok_c[:, :, None], lg, -jnp.inf).reshape(B, H, C, topk * L)
            p = jax.nn.softmax(jnp.concatenate([lg, lo], -1), axis=-1).astype(v.dtype)
            pg = p[..., :topk * L].reshape(B, H, C, topk, L)
            po = p[..., topk * L:]
            return (jnp.einsum('bhcjk,bhcjkd->bhcd', pg, vg)
                    + jnp.einsum('bhck,bhkd->bhcd', po, v_own))
        po = jax.nn.softmax(lo, axis=-1).astype(v.dtype)
        return jnp.einsum('bhck,bhkd->bhcd', po, v_own)

    outs = lax.map(chunk, jnp.arange(Sp // C))
    out = outs.transpose(1, 2, 0, 3, 4).reshape(B, H, Sp, dh)[:, :, :S]
    return out.transpose(0, 2, 1, 3)


def hierarchical_moe(h, w_router_group, w_router_expert, w_gate_e, w_up_e, w_down_e):
    B, S, D = h.shape
    t = h.reshape(B * S, D)
    grp_prob = jax.nn.softmax((t @ w_router_group).astype(jnp.float32), axis=-1)
    g_p, g_i = lax.top_k(grp_prob, 1)
    exp_logits = jnp.einsum('td,gde->tge', t, w_router_expert).astype(jnp.float32)
    sel_logits = jnp.take_along_axis(exp_logits, g_i[:, :, None], axis=1)[:, 0]
    e_p, e_i = lax.top_k(jax.nn.softmax(sel_logits, axis=-1), TOPK_IN_GROUP)
    w = g_p * (e_p / jnp.sum(e_p, axis=-1, keepdims=True))
    expert_id = g_i * EXPERTS_PER_GROUP + e_i
    combine = jnp.sum(jax.nn.one_hot(expert_id, N_EXPERTS, dtype=jnp.float32) * w[..., None], axis=1)
    hid = jax.nn.silu(jnp.einsum('td,edf->tef', t, w_gate_e)) * jnp.einsum('td,edf->tef', t, w_up_e)
    out = jnp.einsum('tef,efd->td', hid * combine[:, :, None].astype(hid.dtype), w_down_e)
    return out.reshape(B, S, D)


def setup_inputs(seed: int = 0) -> dict:
    key = jax.random.key(seed)
    ks = jax.random.split(key, 20)
    nrm = lambda k, shape, fan: jax.random.normal(k, shape, jnp.float32) * fan ** -0.5
    gain = lambda k, n: 1.0 + 0.1 * jax.random.normal(k, (n,), jnp.float32)
    return {
        "x": jax.random.normal(ks[0], (BATCH, SEQ, D_MODEL), jnp.float32),
        "g_mix": gain(ks[1], D_MODEL),
        "w_in": nrm(ks[2], (D_MODEL, IN_COLS), D_MODEL),
        "q_norm_swa": gain(ks[3], HEAD_DIM),
        "k_norm_swa": gain(ks[4], HEAD_DIM),
        "sinks": jax.random.normal(ks[5], (SWA_Q_HEADS,), jnp.float32),
        "q_norm_moba": gain(ks[6], HEAD_DIM),
        "k_norm_moba": gain(ks[7], HEAD_DIM),
        "w_up_swa": nrm(ks[8], (SWA_Q_DIM, D_MODEL), SWA_Q_DIM),
        "w_up_moba": nrm(ks[9], (MOBA_DIM, D_MODEL), MOBA_DIM),
        "w_out": nrm(ks[10], (D_MODEL, D_MODEL), D_MODEL),
        "g_ffn": gain(ks[11], D_MODEL),
        "w_router_group": nrm(ks[12], (D_MODEL, N_GROUPS), D_MODEL),
        "w_router_expert": nrm(ks[13], (N_GROUPS, D_MODEL, EXPERTS_PER_GROUP), D_MODEL),
        "w_gate_e": nrm(ks[14], (N_EXPERTS, D_MODEL, D_EXPERT), D_MODEL),
        "w_up_e": nrm(ks[15], (N_EXPERTS, D_MODEL, D_EXPERT), D_MODEL),
        "w_down_e": nrm(ks[16], (N_EXPERTS, D_EXPERT, D_MODEL), D_EXPERT),
    }


def reference(x, g_mix, w_in, q_norm_swa, k_norm_swa, sinks, q_norm_moba, k_norm_moba,
              w_up_swa, w_up_moba, w_out, g_ffn, w_router_group, w_router_expert,
              w_gate_e, w_up_e, w_down_e):
    B, S, D = x.shape
    for _ in range(DEPTH):
        h = rms_norm(x, g_mix)
        proj = h @ w_in
        cuts = np.cumsum([SWA_Q_DIM, SWA_KV_DIM, SWA_KV_DIM, MOBA_DIM, MOBA_DIM, MOBA_DIM, D_MODEL]).tolist()
        qa, ka, va, qb, kb, vb, gate_a, gate_b = jnp.split(proj, cuts, axis=-1)
        qa = rms_norm(qa.reshape(B, S, SWA_Q_HEADS, HEAD_DIM), q_norm_swa)
        ka = rms_norm(ka.reshape(B, S, SWA_KV_HEADS, HEAD_DIM), k_norm_swa)
        va = va.reshape(B, S, SWA_KV_HEADS, HEAD_DIM)
        qb = rms_norm(qb.reshape(B, S, MOBA_HEADS, HEAD_DIM), q_norm_moba)
        kb = rms_norm(kb.reshape(B, S, MOBA_HEADS, HEAD_DIM), k_norm_moba)
        vb = vb.reshape(B, S, MOBA_HEADS, HEAD_DIM)
        y_a = swa_attention(qa, ka, va, sinks).reshape(B, S, SWA_Q_DIM) @ w_up_swa
        y_b = moba_attention(qb, kb, vb).reshape(B, S, MOBA_DIM) @ w_up_moba
        merged = jax.nn.sigmoid(gate_a) * y_a + jax.nn.sigmoid(gate_b) * y_b
        x = x + merged @ w_out
        h = rms_norm(x, g_ffn)
        x = x + hierarchical_moe(h, w_router_group, w_router_expert, w_gate_e, w_up_e, w_down_e)
    return x
```

```python
import functools

import jax
import jax.numpy as jnp
from jax import lax
from jax.experimental import pallas as pl
from jax.experimental.pallas import tpu as pltpu

D_MODEL = 2048
HEAD_DIM = 64
ATTN_SCALE = HEAD_DIM ** -0.5
SWA_Q_HEADS = 16
SWA_KV_HEADS = 2
SWA_WINDOW = 128
SWA_BLOCK = 128
MOBA_HEADS = 16
MOBA_BLOCK = 256
MOBA_TOPK = 3
N_GROUPS = 4
EXPERTS_PER_GROUP = 4
N_EXPERTS = N_GROUPS * EXPERTS_PER_GROUP
D_EXPERT = 512
EPS = 1e-6

SWA_Q_DIM = SWA_Q_HEADS * HEAD_DIM
SWA_KV_DIM = SWA_KV_HEADS * HEAD_DIM
MOBA_DIM = MOBA_HEADS * HEAD_DIM
IN_COLS = SWA_Q_DIM + 2 * SWA_KV_DIM + 3 * MOBA_DIM + 2 * D_MODEL

LANES = 128
VMEM_LIMIT = 56 * 1024 * 1024
NEG = -1e30

COL_GATE_A = 0
COL_GATE_B = D_MODEL // LANES
COL_QA = 2 * D_MODEL // LANES
COL_KA = COL_QA + SWA_Q_DIM // LANES
COL_VA = COL_KA + 1
COL_QB = COL_VA + 1
COL_KB = COL_QB + MOBA_DIM // LANES
COL_VB = COL_KB + MOBA_DIM // LANES

bf16 = jnp.bfloat16
f32 = jnp.float32


def _sigmoid(x):
    return 1.0 / (1.0 + jnp.exp(-x))


def _pair_rms(x, gain):
    lane = lax.broadcasted_iota(jnp.int32, x.shape, 1)
    lo = lane < HEAD_DIM
    sq = x * x
    s0 = jnp.sum(jnp.where(lo, sq, 0.0), axis=-1, keepdims=True)
    s1 = jnp.sum(jnp.where(lo, 0.0, sq), axis=-1, keepdims=True)
    r0 = lax.rsqrt(s0 * (1.0 / HEAD_DIM) + EPS)
    r1 = lax.rsqrt(s1 * (1.0 / HEAD_DIM) + EPS)
    return x * jnp.where(lo, r0, r1) * gain


def _dot_nt(a, b):
    return lax.dot_general(a, b, (((1,), (1,)), ((), ())), preferred_element_type=f32)


def _split2(x):
    hi = x.astype(bf16)
    lo = (x - hi.astype(f32)).astype(bf16)
    return hi, lo


def _inproj_body(x_ref, g_ref, w_ref, o_ref, h_ref):
    @pl.when(pl.program_id(1) == 0)
    def _():
        x = x_ref[...]
        ms = jnp.mean(x * x, axis=-1, keepdims=True)
        h_ref[...] = (x * lax.rsqrt(ms + EPS) * g_ref[...]).astype(bf16)

    o_ref[...] = jnp.dot(h_ref[...], w_ref[...], preferred_element_type=f32)


def _inproj(x2, g, w, tm=1024, tn=768):
    T = x2.shape[0]
    N = w.shape[1]
    return pl.pallas_call(
        _inproj_body,
        grid=(T // tm, N // tn),
        in_specs=[
            pl.BlockSpec((tm, D_MODEL), lambda i, j: (i, 0)),
            pl.BlockSpec((1, D_MODEL), lambda i, j: (0, 0)),
            pl.BlockSpec((D_MODEL, tn), lambda i, j: (0, j)),
        ],
        out_specs=pl.BlockSpec((tm, tn), lambda i, j: (i, j)),
        out_shape=jax.ShapeDtypeStruct((T, N), f32),
        scratch_shapes=[pltpu.VMEM((tm, D_MODEL), bf16)],
        compiler_params=pltpu.CompilerParams(
            dimension_semantics=("arbitrary", "arbitrary"),
            vmem_limit_bytes=VMEM_LIMIT),
        name="inproj",
    )(x2, g.reshape(1, D_MODEL), w)


def _swa_body(sinks_ref, slopes_ref, q_ref, kp_ref, kc_ref, vp_ref, vc_ref,
              qn_ref, kn_ref, o_ref):
    L = SWA_BLOCK
    n = pl.program_id(1)
    k2 = jnp.concatenate([kp_ref[...], kc_ref[...]], axis=0)
    v2 = jnp.concatenate([vp_ref[...], vc_ref[...]], axis=0).astype(bf16)
    k2n = _pair_rms(k2, kn_ref[...]).astype(bf16)

    row = lax.broadcasted_iota(jnp.int32, (L, 2 * L), 0)
    col = lax.broadcasted_iota(jnp.int32, (L, 2 * L), 1)
    dist = row + L - col
    ok = (dist >= 0) & (dist < SWA_WINDOW) & ((n > 0) | (col >= L))
    distf = dist.astype(f32)
    lane = lax.broadcasted_iota(jnp.int32, (L, LANES), 1)
    lane_lo = lane < HEAD_DIM

    heads_per_kv = SWA_Q_HEADS // SWA_KV_HEADS
    for pp in range(SWA_Q_HEADS // 2):
        g = (2 * pp) // heads_per_kv
        qn = _pair_rms(q_ref[:, pp * LANES:(pp + 1) * LANES], qn_ref[...]) * ATTN_SCALE
        qn_sw = pltpu.roll(qn, HEAD_DIM, axis=1)
        in_g = lane_lo if g == 0 else jnp.logical_not(lane_lo)
        o_pair = None
        for hh in range(2):
            h = 2 * pp + hh
            src = qn if hh == g else qn_sw
            qm = jnp.where(in_g, src, 0.0).astype(bf16)
            s = _dot_nt(qm, k2n) - slopes_ref[h] * distf
            s = jnp.where(ok, s, -jnp.inf)
            sink = sinks_ref[h]
            m = jnp.maximum(jnp.max(s, axis=-1, keepdims=True), sink)
            p = jnp.exp(s - m)
            denom = jnp.sum(p, axis=-1, keepdims=True) + jnp.exp(sink - m)
            o = jnp.dot(p.astype(bf16), v2, preferred_element_type=f32) / denom
            o = o if hh == g else pltpu.roll(o, HEAD_DIM, axis=1)
            keep = lane_lo if hh == 0 else jnp.logical_not(lane_lo)
            o_pair = jnp.where(keep, o, 0.0) if o_pair is None else jnp.where(keep, o, o_pair)
        o_ref[:, pp * LANES:(pp + 1) * LANES] = o_pair.astype(bf16)


def _swa(proj, sinks, slopes, q_norm, k_norm, B, S):
    L = SWA_BLOCK
    nb = S // L
    T = B * S
    smem = pl.BlockSpec(memory_space=pltpu.SMEM)

    def prev(b, n):
        return b * nb + jnp.maximum(n - 1, 0)

    return pl.pallas_call(
        _swa_body,
        grid=(B, nb),
        in_specs=[
            smem, smem,
            pl.BlockSpec((L, SWA_Q_DIM), lambda b, n: (b * nb + n, COL_QA // (SWA_Q_DIM // LANES))),
            pl.BlockSpec((L, LANES), lambda b, n: (prev(b, n), COL_KA)),
            pl.BlockSpec((L, LANES), lambda b, n: (b * nb + n, COL_KA)),
            pl.BlockSpec((L, LANES), lambda b, n: (prev(b, n), COL_VA)),
            pl.BlockSpec((L, LANES), lambda b, n: (b * nb + n, COL_VA)),
            pl.BlockSpec((1, LANES), lambda b, n: (0, 0)),
            pl.BlockSpec((1, LANES), lambda b, n: (0, 0)),
        ],
        out_specs=pl.BlockSpec((L, SWA_Q_DIM), lambda b, n: (b * nb + n, 0)),
        out_shape=jax.ShapeDtypeStruct((T, SWA_Q_DIM), bf16),
        compiler_params=pltpu.CompilerParams(
            dimension_semantics=("arbitrary", "arbitrary"),
            vmem_limit_bytes=VMEM_LIMIT),
        name="swa",
    )(sinks, slopes, proj, proj, proj, proj, proj,
      jnp.tile(q_norm, 2).reshape(1, LANES), jnp.tile(k_norm, 2).reshape(1, LANES))


N_BIAS_PARTS = 3
NB_ROWS = 8


def _moba_body(slopes_ref, q_ref, k_ref, v_ref, qn_ref, kn_ref, o_ref,
               kaug_ref, vt_ref, km_ref, tab_ref, *, nb):
    L = MOBA_BLOCK
    p = pl.program_id(1)
    i = pl.program_id(2)

    @pl.when(i == 0)
    def _prepare_keys():
        kn = _pair_rms(k_ref[...], kn_ref[...])
        lane = lax.broadcasted_iota(jnp.int32, (L, LANES), 1)
        lane_lo = lane < HEAD_DIM
        sub8 = lax.broadcasted_iota(jnp.int32, (8, LANES), 0)
        means = []
        for n in range(nb):
            kblk = kn[n * L:(n + 1) * L, :]
            means.append(jnp.mean(kblk, axis=0, keepdims=True))
            c0 = lane - HEAD_DIM
            hot0 = ((c0 >= 0) & (c0 < NB_ROWS * N_BIAS_PARTS) & ((c0 % NB_ROWS) == n)).astype(f32)
            hot1 = ((lane < NB_ROWS * N_BIAS_PARTS) & ((lane % NB_ROWS) == n)).astype(f32)
            kaug_ref[0, n] = jnp.where(lane_lo, kblk, hot0).astype(bf16)
            kaug_ref[1, n] = jnp.where(lane_lo, hot1, kblk).astype(bf16)
            vt_ref[n] = v_ref[n * L:(n + 1) * L, :].T.astype(bf16)
        means += [jnp.zeros((1, LANES), f32)] * (NB_ROWS - nb)
        km = jnp.concatenate(means, axis=0)
        lane8 = lax.broadcasted_iota(jnp.int32, (NB_ROWS, LANES), 1)
        km_pair = jnp.concatenate(
            [jnp.where(lane8 < HEAD_DIM, km, 0.0), jnp.where(lane8 < HEAD_DIM, 0.0, km)], axis=0)
        hi, lo = _split2(km_pair)
        km_ref[0] = hi
        km_ref[1] = lo
        r = lax.broadcasted_iota(jnp.int32, (L, L), 1)
        c = lax.broadcasted_iota(jnp.int32, (L, L), 0)
        d = (r - c).astype(f32)
        for hh in range(2):
            t = -slopes_ref[2 * p + hh] * d
            tab_ref[hh, 0] = t
            tab_ref[hh, 1] = jnp.where(r >= c, t, NEG)

    qs = _pair_rms(q_ref[...], qn_ref[...]) * ATTN_SCALE
    qt = qs.T
    qt_hi, qt_lo = _split2(qt)
    gate = (jnp.dot(km_ref[0], qt_hi, preferred_element_type=f32)
            + (jnp.dot(km_ref[0], qt_lo, preferred_element_type=f32)
               + jnp.dot(km_ref[1], qt_hi, preferred_element_type=f32)))

    blk = lax.broadcasted_iota(jnp.int32, (NB_ROWS, L), 0)
    past = blk < i
    outs = []
    for hh in range(2):
        g = jnp.where(past, gate[hh * NB_ROWS:(hh + 1) * NB_ROWS, :], -jnp.inf)
        rank = jnp.zeros((NB_ROWS, L), jnp.int32)
        for m in range(nb):
            gm = g[m:m + 1, :]
            ahead = (gm > g) | ((gm == g) & (m < blk))
            rank = rank + ahead.astype(jnp.int32)
        sel = past & (rank < MOBA_TOPK)
        slope = slopes_ref[2 * p + hh]
        bias = jnp.where(sel, (-slope * L) * (i - blk).astype(f32), NEG)
        bias = jnp.where(blk == i, 0.0, bias)
        b0 = bias.astype(bf16).astype(f32)
        r1 = bias - b0
        b1 = r1.astype(bf16).astype(f32)
        b2 = (r1 - b1).astype(bf16).astype(f32)
        extra = jnp.concatenate(
            [b0, b1, b2, jnp.zeros((HEAD_DIM - NB_ROWS * N_BIAS_PARTS, L), f32)], axis=0)
        if hh == 0:
            qa = jnp.concatenate([qt[:HEAD_DIM], extra], axis=0)
        else:
            qa = jnp.concatenate([extra, qt[HEAD_DIM:]], axis=0)
        qa = qa.astype(bf16)

        def scores(n, table):
            return jnp.dot(kaug_ref[hh, n], qa, preferred_element_type=f32) + table

        s = scores(i, tab_ref[hh, 1])
        m0 = jnp.max(s, axis=0, keepdims=True)
        e = jnp.exp(s - m0)
        l0 = jnp.sum(e, axis=0, keepdims=True)
        vt_own = vt_ref[i]
        acc0 = jnp.dot(vt_own[hh * HEAD_DIM:(hh + 1) * HEAD_DIM, :], e.astype(bf16),
                       preferred_element_type=f32)

        def step(n, carry):
            m_run, l_run, acc = carry
            s = scores(n, tab_ref[hh, 0])
            m_new = jnp.maximum(m_run, jnp.max(s, axis=0, keepdims=True))
            alpha = jnp.exp(m_run - m_new)
            e = jnp.exp(s - m_new)
            l_new = alpha * l_run + jnp.sum(e, axis=0, keepdims=True)
            vt_n = vt_ref[n]
            pv = jnp.dot(vt_n[hh * HEAD_DIM:(hh + 1) * HEAD_DIM, :], e.astype(bf16),
                         preferred_element_type=f32)
            return m_new, l_new, alpha * acc + pv

        _, l_fin, acc = lax.fori_loop(0, i, step, (m0, l0, acc0))
        outs.append(acc / l_fin)

    o_ref[...] = jnp.concatenate(outs, axis=0).T.astype(bf16)


def _moba(proj, slopes, q_norm, k_norm, B, S):
    L = MOBA_BLOCK
    nb = S // L
    assert nb <= NB_ROWS
    T = B * S
    n_pairs = MOBA_HEADS // 2
    return pl.pallas_call(
        functools.partial(_moba_body, nb=nb),
        grid=(B, n_pairs, nb),
        in_specs=[
            pl.BlockSpec(memory_space=pltpu.SMEM),
            pl.BlockSpec((L, LANES), lambda b, p, i: (b * nb + i, COL_QB + p)),
            pl.BlockSpec((S, LANES), lambda b, p, i: (b, COL_KB + p)),
            pl.BlockSpec((S, LANES), lambda b, p, i: (b, COL_VB + p)),
            pl.BlockSpec((1, LANES), lambda b, p, i: (0, 0)),
            pl.BlockSpec((1, LANES), lambda b, p, i: (0, 0)),
        ],
        out_specs=pl.BlockSpec((L, LANES), lambda b, p, i: (b * nb + i, p)),
        out_shape=jax.ShapeDtypeStruct((T, MOBA_DIM), bf16),
        scratch_shapes=[
            pltpu.VMEM((2, nb, L, LANES), bf16),
            pltpu.VMEM((nb, LANES, L), bf16),
            pltpu.VMEM((2, 2 * NB_ROWS, LANES), bf16),
            pltpu.VMEM((2, 2, L, L), f32),
        ],
        compiler_params=pltpu.CompilerParams(
            dimension_semantics=("arbitrary", "arbitrary", "arbitrary"),
            vmem_limit_bytes=VMEM_LIMIT),
        name="moba",
    )(slopes, proj, proj, proj,
      jnp.tile(q_norm, 2).reshape(1, LANES), jnp.tile(k_norm, 2).reshape(1, LANES))


ROUTER_ROWS = 32


def _mix_body(oa_ref, ob_ref, ga_ref, gb_ref, x_ref, wua_ref, wub_ref, wout_ref,
              gffn_ref, wr_ref, x1_ref, h2_ref, comb_ref):
    ya = jnp.dot(oa_ref[...], wua_ref[...], preferred_element_type=f32)
    yb = jnp.dot(ob_ref[...], wub_ref[...], preferred_element_type=f32)
    merged = _sigmoid(ga_ref[...]) * ya + _sigmoid(gb_ref[...]) * yb
    x1 = x_ref[...] + jnp.dot(merged.astype(bf16), wout_ref[...], preferred_element_type=f32)
    x1_ref[...] = x1
    ms = jnp.mean(x1 * x1, axis=-1, keepdims=True)
    h2 = x1 * lax.rsqrt(ms + EPS) * gffn_ref[...]
    h2_ref[...] = h2.astype(bf16)

    h_hi, h_lo = _split2(h2)
    lt = _dot_nt(wr_ref[0], h_hi) + (_dot_nt(wr_ref[0], h_lo) + _dot_nt(wr_ref[1], h_hi))
    tm = lt.shape[1]
    gl = [lt[g:g + 1, :] for g in range(N_GROUPS)]
    gmax = functools.reduce(jnp.maximum, gl)
    gsum = functools.reduce(lambda a, b: a + b, [jnp.exp(v - gmax) for v in gl])
    g_p = 1.0 / gsum
    g_i = jnp.full((1, tm), N_GROUPS - 1, jnp.int32)
    for g in reversed(range(N_GROUPS)):
        g_i = jnp.where(gl[g] == gmax, g, g_i)

    el = []
    for e in range(EXPERTS_PER_GROUP):
        v = jnp.zeros((1, tm), f32)
        for g in range(N_GROUPS):
            r = N_GROUPS + g * EXPERTS_PER_GROUP + e
            v = jnp.where(g_i == g, lt[r:r + 1, :], v)
        el.append(v)
    emax = functools.reduce(jnp.maximum, el)
    ex = [jnp.exp(v - emax) for v in el]
    esum = functools.reduce(lambda a, b: a + b, ex)
    ep = [v / esum for v in ex]
    p1 = functools.reduce(jnp.maximum, ep)
    i1 = jnp.full((1, tm), EXPERTS_PER_GROUP - 1, jnp.int32)
    for e in reversed(range(EXPERTS_PER_GROUP)):
        i1 = jnp.where(ep[e] == p1, e, i1)
    rest = [jnp.where(i1 == e, -1.0, ep[e]) for e in range(EXPERTS_PER_GROUP)]
    p2 = functools.reduce(jnp.maximum, rest)
    i2 = jnp.full((1, tm), EXPERTS_PER_GROUP - 1, jnp.int32)
    for e in reversed(range(EXPERTS_PER_GROUP)):
        i2 = jnp.where(rest[e] == p2, e, i2)
    w1 = g_p * (p1 / (p1 + p2))
    w2 = g_p * (p2 / (p1 + p2))
    rows = []
    for g in range(N_GROUPS):
        for e in range(EXPERTS_PER_GROUP):
            in_g = g_i == g
            rows.append(jnp.where(in_g & (i1 == e), w1, 0.0) + jnp.where(in_g & (i2 == e), w2, 0.0))
    rows.append(jnp.zeros((LANES - N_EXPERTS, tm), f32))
    comb_ref[...] = jnp.concatenate(rows, axis=0).T


def _mix(oa, ob, proj, x2, wua, wub, wout, g_ffn, wr, tm=256):
    T = x2.shape[0]
    const = lambda i: (0, 0)
    single = pl.Buffered(1)
    return pl.pallas_call(
        _mix_body,
        grid=(T // tm,),
        in_specs=[
            pl.BlockSpec((tm, SWA_Q_DIM), lambda i: (i, 0)),
            pl.BlockSpec((tm, MOBA_DIM), lambda i: (i, 0)),
            pl.BlockSpec((tm, D_MODEL), lambda i: (i, 0)),
            pl.BlockSpec((tm, D_MODEL), lambda i: (i, 1)),
            pl.BlockSpec((tm, D_MODEL), lambda i: (i, 0)),
            pl.BlockSpec((SWA_Q_DIM, D_MODEL), const, pipeline_mode=single),
            pl.BlockSpec((MOBA_DIM, D_MODEL), const, pipeline_mode=single),
            pl.BlockSpec((D_MODEL, D_MODEL), const, pipeline_mode=single),
            pl.BlockSpec((1, D_MODEL), const),
            pl.BlockSpec((2, ROUTER_ROWS, D_MODEL), lambda i: (0, 0, 0)),
        ],
        out_specs=[
            pl.BlockSpec((tm, D_MODEL), lambda i: (i, 0)),
            pl.BlockSpec((tm, D_MODEL), lambda i: (i, 0)),
            pl.BlockSpec((tm, LANES), lambda i: (i, 0)),
        ],
        out_shape=[
            jax.ShapeDtypeStruct((T, D_MODEL), f32),
            jax.ShapeDtypeStruct((T, D_MODEL), bf16),
            jax.ShapeDtypeStruct((T, LANES), f32),
        ],
        compiler_params=pltpu.CompilerParams(
            dimension_semantics=("arbitrary",),
            vmem_limit_bytes=VMEM_LIMIT),
        name="mix",
    )(oa, ob, proj, proj, x2, wua, wub, wout, g_ffn.reshape(1, D_MODEL), wr)


def _moe_body(x1_ref, h2_ref, comb_ref, wg_ref, wu_ref, wd_ref, o_ref):
    e = pl.program_id(1)

    @pl.when(e == 0)
    def _():
        o_ref[...] = x1_ref[...]

    h = h2_ref[...]
    lane = lax.broadcasted_iota(jnp.int32, comb_ref.shape, 1)
    c = jnp.sum(jnp.where(lane == e, comb_ref[...], 0.0), axis=-1, keepdims=True)
    gte = jnp.dot(h, wg_ref[...], preferred_element_type=f32)
    up = jnp.dot(h, wu_ref[...], preferred_element_type=f32)
    hid = (gte * _sigmoid(gte)) * up * c
    o_ref[...] += jnp.dot(hid.astype(bf16), wd_ref[...], preferred_element_type=f32)


def _moe(x1, h2, comb, wg, wu, wd, tm=512):
    T = x1.shape[0]
    return pl.pallas_call(
        _moe_body,
        grid=(T // tm, N_EXPERTS),
        in_specs=[
            pl.BlockSpec((tm, D_MODEL), lambda i, e: (i, 0)),
            pl.BlockSpec((tm, D_MODEL), lambda i, e: (i, 0)),
            pl.BlockSpec((tm, LANES), lambda i, e: (i, 0)),
            pl.BlockSpec((None, D_MODEL, D_EXPERT), lambda i, e: (e, 0, 0)),
            pl.BlockSpec((None, D_MODEL, D_EXPERT), lambda i, e: (e, 0, 0)),
            pl.BlockSpec((None, D_EXPERT, D_MODEL), lambda i, e: (e, 0, 0)),
        ],
        out_specs=pl.BlockSpec((tm, D_MODEL), lambda i, e: (i, 0)),
        out_shape=jax.ShapeDtypeStruct((T, D_MODEL), f32),
        compiler_params=pltpu.CompilerParams(
            dimension_semantics=("arbitrary", "arbitrary"),
            vmem_limit_bytes=VMEM_LIMIT),
        name="moe",
    )(x1, h2, comb, wg, wu, wd)


def _alibi_slopes(n):
    return jnp.exp2(-8.0 * jnp.arange(1, n + 1, dtype=f32) / n)


def kernel(x, g_mix, w_in, q_norm_swa, k_norm_swa, sinks, q_norm_moba, k_norm_moba,
           w_up_swa, w_up_moba, w_out, g_ffn, w_router_group, w_router_expert,
           w_gate_e, w_up_e, w_down_e):
    B, S, D = x.shape
    assert D == D_MODEL and S % MOBA_BLOCK == 0 and S % SWA_BLOCK == 0
    T = B * S
    x2 = x.reshape(T, D)

    n_qkv = IN_COLS - 2 * D_MODEL
    w_in_p = jnp.concatenate([w_in[:, n_qkv:], w_in[:, :n_qkv]], axis=1).astype(bf16)
    proj = _inproj(x2, g_mix, w_in_p)

    oa = _swa(proj, sinks.astype(f32), _alibi_slopes(SWA_Q_HEADS), q_norm_swa, k_norm_swa, B, S)
    ob = _moba(proj, _alibi_slopes(MOBA_HEADS), q_norm_moba, k_norm_moba, B, S)

    wr = jnp.concatenate(
        [w_router_group.T,
         w_router_expert.transpose(0, 2, 1).reshape(N_EXPERTS, D),
         jnp.zeros((ROUTER_ROWS - N_GROUPS - N_EXPERTS, D), f32)], axis=0)
    wr_hi = wr.astype(bf16)
    wr_lo = (wr - wr_hi.astype(f32)).astype(bf16)
    x1, h2, comb = _mix(oa, ob, proj, x2, w_up_swa.astype(bf16), w_up_moba.astype(bf16),
                        w_out.astype(bf16), g_ffn, jnp.stack([wr_hi, wr_lo]))

    y = _moe(x1, h2, comb, w_gate_e.astype(bf16), w_up_e.astype(bf16), w_down_e.astype(bf16))
    return y.reshape(B, S, D)
```

```python
import functools

import jax
import jax.numpy as jnp
from jax import lax
from jax.experimental import pallas as pl
from jax.experimental.pallas import tpu as pltpu

D_MODEL = 2048
HEAD_DIM = 64
ATTN_SCALE = HEAD_DIM ** -0.5
SWA_Q_HEADS = 16
SWA_KV_HEADS = 2
SWA_WINDOW = 128
SWA_BLOCK = 128
MOBA_HEADS = 16
MOBA_BLOCK = 256
MOBA_TOPK = 3
N_GROUPS = 4
EXPERTS_PER_GROUP = 4
N_EXPERTS = N_GROUPS * EXPERTS_PER_GROUP
D_EXPERT = 512
EPS = 1e-6

SWA_Q_DIM = SWA_Q_HEADS * HEAD_DIM
SWA_KV_DIM = SWA_KV_HEADS * HEAD_DIM
MOBA_DIM = MOBA_HEADS * HEAD_DIM
IN_COLS = SWA_Q_DIM + 2 * SWA_KV_DIM + 3 * MOBA_DIM + 2 * D_MODEL

LANES = 128
VMEM_LIMIT = 56 * 1024 * 1024
NEG = -1e30

COL_GATE_A = 0
COL_GATE_B = D_MODEL // LANES
COL_QA = 2 * D_MODEL // LANES
COL_KA = COL_QA + SWA_Q_DIM // LANES
COL_VA = COL_KA + 1
COL_QB = COL_VA + 1
COL_KB = COL_QB + MOBA_DIM // LANES
COL_VB = COL_KB + MOBA_DIM // LANES

bf16 = jnp.bfloat16
f32 = jnp.float32


def _sigmoid(x):
    return 1.0 / (1.0 + jnp.exp(-x))


def _pair_rms(x, gain):
    lane = lax.broadcasted_iota(jnp.int32, x.shape, 1)
    lo = lane < HEAD_DIM
    sq = x * x
    s0 = jnp.sum(jnp.where(lo, sq, 0.0), axis=-1, keepdims=True)
    s1 = jnp.sum(jnp.where(lo, 0.0, sq), axis=-1, keepdims=True)
    r0 = lax.rsqrt(s0 * (1.0 / HEAD_DIM) + EPS)
    r1 = lax.rsqrt(s1 * (1.0 / HEAD_DIM) + EPS)
    return x * jnp.where(lo, r0, r1) * gain


def _dot_nt(a, b):
    return lax.dot_general(a, b, (((1,), (1,)), ((), ())), preferred_element_type=f32)


def _split2(x):
    hi = x.astype(bf16)
    lo = (x - hi.astype(f32)).astype(bf16)
    return hi, lo


def _inproj_body(x_ref, g_ref, w_ref, o_ref, h_ref):
    @pl.when(pl.program_id(1) == 0)
    def _():
        x = x_ref[...]
        ms = jnp.mean(x * x, axis=-1, keepdims=True)
        h_ref[...] = (x * lax.rsqrt(ms + EPS) * g_ref[...]).astype(bf16)

    o_ref[...] = jnp.dot(h_ref[...], w_ref[...], preferred_element_type=f32)


def _inproj(x2, g, w, tm=1024, tn=768):
    T = x2.shape[0]
    N = w.shape[1]
    return pl.pallas_call(
        _inproj_body,
        grid=(T // tm, N // tn),
        in_specs=[
            pl.BlockSpec((tm, D_MODEL), lambda i, j: (i, 0)),
            pl.BlockSpec((1, D_MODEL), lambda i, j: (0, 0)),
            pl.BlockSpec((D_MODEL, tn), lambda i, j: (0, j)),
        ],
        out_specs=pl.BlockSpec((tm, tn), lambda i, j: (i, j)),
        out_shape=jax.ShapeDtypeStruct((T, N), f32),
        scratch_shapes=[pltpu.VMEM((tm, D_MODEL), bf16)],
        compiler_params=pltpu.CompilerParams(
            dimension_semantics=("arbitrary", "arbitrary"),
            vmem_limit_bytes=VMEM_LIMIT),
        name="inproj",
    )(x2, g.reshape(1, D_MODEL), w)


def _swa_body(sinks_ref, slopes_ref, q_ref, kp_ref, kc_ref, vp_ref, vc_ref,
              qn_ref, kn_ref, o_ref):
    L = SWA_BLOCK
    n = pl.program_id(1)
    k2 = jnp.concatenate([kp_ref[...], kc_ref[...]], axis=0)
    v2 = jnp.concatenate([vp_ref[...], vc_ref[...]], axis=0).astype(bf16)
    k2n = _pair_rms(k2, kn_ref[...]).astype(bf16)

    row = lax.broadcasted_iota(jnp.int32, (L, 2 * L), 0)
    col = lax.broadcasted_iota(jnp.int32, (L, 2 * L), 1)
    dist = row + L - col
    ok = (dist >= 0) & (dist < SWA_WINDOW) & ((n > 0) | (col >= L))
    distf = dist.astype(f32)
    lane = lax.broadcasted_iota(jnp.int32, (L, LANES), 1)
    lane_lo = lane < HEAD_DIM

    heads_per_kv = SWA_Q_HEADS // SWA_KV_HEADS
    for pp in range(SWA_Q_HEADS // 2):
        g = (2 * pp) // heads_per_kv
        qn = _pair_rms(q_ref[:, pp * LANES:(pp + 1) * LANES], qn_ref[...]) * ATTN_SCALE
        qn_sw = pltpu.roll(qn, HEAD_DIM, axis=1)
        in_g = lane_lo if g == 0 else jnp.logical_not(lane_lo)
        o_pair = None
        for hh in range(2):
            h = 2 * pp + hh
            src = qn if hh == g else qn_sw
            qm = jnp.where(in_g, src, 0.0).astype(bf16)
            s = _dot_nt(qm, k2n) - slopes_ref[h] * distf
            s = jnp.where(ok, s, -jnp.inf)
            sink = sinks_ref[h]
            m = jnp.maximum(jnp.max(s, axis=-1, keepdims=True), sink)
            p = jnp.exp(s - m)
            denom = jnp.sum(p, axis=-1, keepdims=True) + jnp.exp(sink - m)
            o = jnp.dot(p.astype(bf16), v2, preferred_element_type=f32) / denom
            o = o if hh == g else pltpu.roll(o, HEAD_DIM, axis=1)
            keep = lane_lo if hh == 0 else jnp.logical_not(lane_lo)
            o_pair = jnp.where(keep, o, 0.0) if o_pair is None else jnp.where(keep, o, o_pair)
        o_ref[:, pp * LANES:(pp + 1) * LANES] = o_pair.astype(bf16)


def _swa(proj, sinks, slopes, q_norm, k_norm, B, S):
    L = SWA_BLOCK
    nb = S // L
    T = B * S
    smem = pl.BlockSpec(memory_space=pltpu.SMEM)

    def prev(b, n):
        return b * nb + jnp.maximum(n - 1, 0)

    return pl.pallas_call(
        _swa_body,
        grid=(B, nb),
        in_specs=[
            smem, smem,
            pl.BlockSpec((L, SWA_Q_DIM), lambda b, n: (b * nb + n, COL_QA // (SWA_Q_DIM // LANES))),
            pl.BlockSpec((L, LANES), lambda b, n: (prev(b, n), COL_KA)),
            pl.BlockSpec((L, LANES), lambda b, n: (b * nb + n, COL_KA)),
            pl.BlockSpec((L, LANES), lambda b, n: (prev(b, n), COL_VA)),
            pl.BlockSpec((L, LANES), lambda b, n: (b * nb + n, COL_VA)),
            pl.BlockSpec((1, LANES), lambda b, n: (0, 0)),
            pl.BlockSpec((1, LANES), lambda b, n: (0, 0)),
        ],
        out_specs=pl.BlockSpec((L, SWA_Q_DIM), lambda b, n: (b * nb + n, 0)),
        out_shape=jax.ShapeDtypeStruct((T, SWA_Q_DIM), bf16),
        compiler_params=pltpu.CompilerParams(
            dimension_semantics=("arbitrary", "arbitrary"),
            vmem_limit_bytes=VMEM_LIMIT),
        name="swa",
    )(sinks, slopes, proj, proj, proj, proj, proj,
      jnp.tile(q_norm, 2).reshape(1, LANES), jnp.tile(k_norm, 2).reshape(1, LANES))


N_BIAS_PARTS = 3
NB_ROWS = 8


def _moba_body(slopes_ref, q_ref, k_ref, v_ref, qn_ref, kn_ref, o_ref, kaug_ref, vt_ref, *, nb):
    L = MOBA_BLOCK
    S = nb * L
    p = pl.program_id(1)

    kn = _pair_rms(k_ref[...], kn_ref[...])
    lane = lax.broadcasted_iota(jnp.int32, (S, LANES), 1)
    kblk = lax.broadcasted_iota(jnp.int32, (S, LANES), 0) // L
    lane_lo = lane < HEAD_DIM
    c0 = lane - HEAD_DIM
    hot0 = ((c0 >= 0) & (c0 < NB_ROWS * N_BIAS_PARTS) & ((c0 % NB_ROWS) == kblk)).astype(f32)
    hot1 = ((lane < NB_ROWS * N_BIAS_PARTS) & ((lane % NB_ROWS) == kblk)).astype(f32)
    kaug_ref[0] = jnp.where(lane_lo, kn, hot0).astype(bf16)
    kaug_ref[1] = jnp.where(lane_lo, hot1, kn).astype(bf16)
    vt_ref[...] = v_ref[...].T.astype(bf16)
    means = [jnp.mean(kn[n * L:(n + 1) * L, :], axis=0, keepdims=True) for n in range(nb)]
    means += [jnp.zeros((1, LANES), f32)] * (NB_ROWS - nb)
    km = jnp.concatenate(means, axis=0)
    lane8 = lax.broadcasted_iota(jnp.int32, (NB_ROWS, LANES), 1)
    km_pair = jnp.concatenate(
        [jnp.where(lane8 < HEAD_DIM, km, 0.0), jnp.where(lane8 < HEAD_DIM, 0.0, km)], axis=0)
    km_hi, km_lo = _split2(km_pair)

    qs = _pair_rms(q_ref[...], qn_ref[...]) * ATTN_SCALE
    qt = qs.T
    qt_hi, qt_lo = _split2(qt)
    gate = (jnp.dot(km_hi, qt_hi, preferred_element_type=f32)
            + (jnp.dot(km_hi, qt_lo, preferred_element_type=f32)
               + jnp.dot(km_lo, qt_hi, preferred_element_type=f32)))

    blk = lax.broadcasted_iota(jnp.int32, (NB_ROWS, S), 0)
    qblk = lax.broadcasted_iota(jnp.int32, (NB_ROWS, S), 1) // L
    past = blk < qblk
    r = lax.broadcasted_iota(jnp.int32, (L, L), 1)
    c = lax.broadcasted_iota(jnp.int32, (L, L), 0)
    d = (r - c).astype(f32)

    qa, tab_past, tab_own = [], [], []
    for hh in range(2):
        g = jnp.where(past, gate[hh * NB_ROWS:(hh + 1) * NB_ROWS, :], -jnp.inf)
        rank = jnp.zeros((NB_ROWS, S), jnp.int32)
        for m in range(nb):
            gm = g[m:m + 1, :]
            ahead = (gm > g) | ((gm == g) & (m < blk))
            rank = rank + ahead.astype(jnp.int32)
        sel = past & (rank < MOBA_TOPK)
        slope = slopes_ref[2 * p + hh]
        bias = jnp.where(sel, (-slope * L) * (qblk - blk).astype(f32), NEG)
        bias = jnp.where(blk == qblk, 0.0, bias)
        b0 = bias.astype(bf16).astype(f32)
        r1 = bias - b0
        b1 = r1.astype(bf16).astype(f32)
        b2 = (r1 - b1).astype(bf16).astype(f32)
        extra = jnp.concatenate(
            [b0, b1, b2, jnp.zeros((HEAD_DIM - NB_ROWS * N_BIAS_PARTS, S), f32)], axis=0)
        if hh == 0:
            qa.append(jnp.concatenate([qt[:HEAD_DIM], extra], axis=0).astype(bf16))
        else:
            qa.append(jnp.concatenate([extra, qt[HEAD_DIM:]], axis=0).astype(bf16))
        t = -slope * d
        tab_past.append(t)
        tab_own.append(jnp.where(r >= c, t, NEG))

    for i in range(nb):
        outs = []
        for hh in range(2):
            nk = (i + 1) * L
            s = jnp.dot(kaug_ref[hh, 0:nk, :], qa[hh][:, i * L:(i + 1) * L],
                        preferred_element_type=f32)
            tiles = [s[n * L:(n + 1) * L, :] + (tab_own[hh] if n == i else tab_past[hh])
                     for n in range(i + 1)]
            m = functools.reduce(jnp.maximum, [jnp.max(t, axis=0, keepdims=True) for t in tiles])
            es = [jnp.exp(t - m) for t in tiles]
            l = functools.reduce(lambda a, b: a + b, [jnp.sum(e, axis=0, keepdims=True) for e in es])
            e_all = jnp.concatenate([e.astype(bf16) for e in es], axis=0)
            acc = jnp.dot(vt_ref[hh * HEAD_DIM:(hh + 1) * HEAD_DIM, 0:nk], e_all,
                          preferred_element_type=f32)
            outs.append(acc / l)
        o_ref[i * L:(i + 1) * L, :] = jnp.concatenate(outs, axis=0).T.astype(bf16)


def _moba(proj, slopes, q_norm, k_norm, B, S):
    L = MOBA_BLOCK
    nb = S // L
    assert nb <= NB_ROWS
    T = B * S
    n_pairs = MOBA_HEADS // 2
    return pl.pallas_call(
        functools.partial(_moba_body, nb=nb),
        grid=(B, n_pairs),
        in_specs=[
            pl.BlockSpec(memory_space=pltpu.SMEM),
            pl.BlockSpec((S, LANES), lambda b, p: (b, COL_QB + p)),
            pl.BlockSpec((S, LANES), lambda b, p: (b, COL_KB + p)),
            pl.BlockSpec((S, LANES), lambda b, p: (b, COL_VB + p)),
            pl.BlockSpec((1, LANES), lambda b, p: (0, 0)),
            pl.BlockSpec((1, LANES), lambda b, p: (0, 0)),
        ],
        out_specs=pl.BlockSpec((S, LANES), lambda b, p: (b, p)),
        out_shape=jax.ShapeDtypeStruct((T, MOBA_DIM), bf16),
        scratch_shapes=[
            pltpu.VMEM((2, S, LANES), bf16),
            pltpu.VMEM((LANES, S), bf16),
        ],
        compiler_params=pltpu.CompilerParams(
            dimension_semantics=("arbitrary", "arbitrary"),
            vmem_limit_bytes=VMEM_LIMIT),
        name="moba",
    )(slopes, proj, proj, proj,
      jnp.tile(q_norm, 2).reshape(1, LANES), jnp.tile(k_norm, 2).reshape(1, LANES))


ROUTER_ROWS = 32


def _mix_body(oa_ref, ob_ref, ga_ref, gb_ref, x_ref, wua_ref, wub_ref, wout_ref,
              gffn_ref, wr_ref, x1_ref, h2_ref, comb_ref):
    ya = jnp.dot(oa_ref[...], wua_ref[...], preferred_element_type=f32)
    yb = jnp.dot(ob_ref[...], wub_ref[...], preferred_element_type=f32)
    merged = _sigmoid(ga_ref[...]) * ya + _sigmoid(gb_ref[...]) * yb
    x1 = x_ref[...] + jnp.dot(merged.astype(bf16), wout_ref[...], preferred_element_type=f32)
    x1_ref[...] = x1
    ms = jnp.mean(x1 * x1, axis=-1, keepdims=True)
    h2 = x1 * lax.rsqrt(ms + EPS) * gffn_ref[...]
    h2_ref[...] = h2.astype(bf16)

    h_hi, h_lo = _split2(h2)
    lt = _dot_nt(wr_ref[0], h_hi) + (_dot_nt(wr_ref[0], h_lo) + _dot_nt(wr_ref[1], h_hi))
    tm = lt.shape[1]
    gl = [lt[g:g + 1, :] for g in range(N_GROUPS)]
    gmax = functools.reduce(jnp.maximum, gl)
    gsum = functools.reduce(lambda a, b: a + b, [jnp.exp(v - gmax) for v in gl])
    g_p = 1.0 / gsum
    g_i = jnp.full((1, tm), N_GROUPS - 1, jnp.int32)
    for g in reversed(range(N_GROUPS)):
        g_i = jnp.where(gl[g] == gmax, g, g_i)

    el = []
    for e in range(EXPERTS_PER_GROUP):
        v = jnp.zeros((1, tm), f32)
        for g in range(N_GROUPS):
            r = N_GROUPS + g * EXPERTS_PER_GROUP + e
            v = jnp.where(g_i == g, lt[r:r + 1, :], v)
        el.append(v)
    emax = functools.reduce(jnp.maximum, el)
    ex = [jnp.exp(v - emax) for v in el]
    esum = functools.reduce(lambda a, b: a + b, ex)
    ep = [v / esum for v in ex]
    p1 = functools.reduce(jnp.maximum, ep)
    i1 = jnp.full((1, tm), EXPERTS_PER_GROUP - 1, jnp.int32)
    for e in reversed(range(EXPERTS_PER_GROUP)):
        i1 = jnp.where(ep[e] == p1, e, i1)
    rest = [jnp.where(i1 == e, -1.0, ep[e]) for e in range(EXPERTS_PER_GROUP)]
    p2 = functools.reduce(jnp.maximum, rest)
    i2 = jnp.full((1, tm), EXPERTS_PER_GROUP - 1, jnp.int32)
    for e in reversed(range(EXPERTS_PER_GROUP)):
        i2 = jnp.where(rest[e] == p2, e, i2)
    w1 = g_p * (p1 / (p1 + p2))
    w2 = g_p * (p2 / (p1 + p2))
    rows = []
    for g in range(N_GROUPS):
        for e in range(EXPERTS_PER_GROUP):
            in_g = g_i == g
            rows.append(jnp.where(in_g & (i1 == e), w1, 0.0) + jnp.where(in_g & (i2 == e), w2, 0.0))
    rows.append(jnp.zeros((LANES - N_EXPERTS, tm), f32))
    comb_ref[...] = jnp.concatenate(rows, axis=0).T


def _mix(oa, ob, proj, x2, wua, wub, wout, g_ffn, wr, tm=256):
    T = x2.shape[0]
    const = lambda i: (0, 0)
    single = pl.Buffered(1)
    return pl.pallas_call(
        _mix_body,
        grid=(T // tm,),
        in_specs=[
            pl.BlockSpec((tm, SWA_Q_DIM), lambda i: (i, 0)),
            pl.BlockSpec((tm, MOBA_DIM), lambda i: (i, 0)),
            pl.BlockSpec((tm, D_MODEL), lambda i: (i, 0)),
            pl.BlockSpec((tm, D_MODEL), lambda i: (i, 1)),
            pl.BlockSpec((tm, D_MODEL), lambda i: (i, 0)),
            pl.BlockSpec((SWA_Q_DIM, D_MODEL), const, pipeline_mode=single),
            pl.BlockSpec((MOBA_DIM, D_MODEL), const, pipeline_mode=single),
            pl.BlockSpec((D_MODEL, D_MODEL), const, pipeline_mode=single),
            pl.BlockSpec((1, D_MODEL), const),
            pl.BlockSpec((2, ROUTER_ROWS, D_MODEL), lambda i: (0, 0, 0)),
        ],
        out_specs=[
            pl.BlockSpec((tm, D_MODEL), lambda i: (i, 0)),
            pl.BlockSpec((tm, D_MODEL), lambda i: (i, 0)),
            pl.BlockSpec((tm, LANES), lambda i: (i, 0)),
        ],
        out_shape=[
            jax.ShapeDtypeStruct((T, D_MODEL), f32),
            jax.ShapeDtypeStruct((T, D_MODEL), bf16),
            jax.ShapeDtypeStruct((T, LANES), f32),
        ],
        compiler_params=pltpu.CompilerParams(
            dimension_semantics=("arbitrary",),
            vmem_limit_bytes=VMEM_LIMIT),
        name="mix",
    )(oa, ob, proj, proj, x2, wua, wub, wout, g_ffn.reshape(1, D_MODEL), wr)


def _moe_body(x1_ref, h2_ref, comb_ref, wg_ref, wu_ref, wd_ref, o_ref):
    e = pl.program_id(1)

    @pl.when(e == 0)
    def _():
        o_ref[...] = x1_ref[...]

    h = h2_ref[...]
    lane = lax.broadcasted_iota(jnp.int32, comb_ref.shape, 1)
    c = jnp.sum(jnp.where(lane == e, comb_ref[...], 0.0), axis=-1, keepdims=True)
    gte = jnp.dot(h, wg_ref[...], preferred_element_type=f32)
    up = jnp.dot(h, wu_ref[...], preferred_element_type=f32)
    hid = (gte * _sigmoid(gte)) * up * c
    o_ref[...] += jnp.dot(hid.astype(bf16), wd_ref[...], preferred_element_type=f32)


def _moe(x1, h2, comb, wg, wu, wd, tm=512):
    T = x1.shape[0]
    return pl.pallas_call(
        _moe_body,
        grid=(T // tm, N_EXPERTS),
        in_specs=[
            pl.BlockSpec((tm, D_MODEL), lambda i, e: (i, 0)),
            pl.BlockSpec((tm, D_MODEL), lambda i, e: (i, 0)),
            pl.BlockSpec((tm, LANES), lambda i, e: (i, 0)),
            pl.BlockSpec((None, D_MODEL, D_EXPERT), lambda i, e: (e, 0, 0)),
            pl.BlockSpec((None, D_MODEL, D_EXPERT), lambda i, e: (e, 0, 0)),
            pl.BlockSpec((None, D_EXPERT, D_MODEL), lambda i, e: (e, 0, 0)),
        ],
        out_specs=pl.BlockSpec((tm, D_MODEL), lambda i, e: (i, 0)),
        out_shape=jax.ShapeDtypeStruct((T, D_MODEL), f32),
        compiler_params=pltpu.CompilerParams(
            dimension_semantics=("arbitrary", "arbitrary"),
            vmem_limit_bytes=VMEM_LIMIT),
        name="moe",
    )(x1, h2, comb, wg, wu, wd)


def _alibi_slopes(n):
    return jnp.exp2(-8.0 * jnp.arange(1, n + 1, dtype=f32) / n)


def kernel(x, g_mix, w_in, q_norm_swa, k_norm_swa, sinks, q_norm_moba, k_norm_moba,
           w_up_swa, w_up_moba, w_out, g_ffn, w_router_group, w_router_expert,
           w_gate_e, w_up_e, w_down_e):
    B, S, D = x.shape
    assert D == D_MODEL and S % MOBA_BLOCK == 0 and S % SWA_BLOCK == 0
    T = B * S
    x2 = x.reshape(T, D)

    n_qkv = IN_COLS - 2 * D_MODEL
    w_in_p = jnp.concatenate([w_in[:, n_qkv:], w_in[:, :n_qkv]], axis=1).astype(bf16)
    proj = _inproj(x2, g_mix, w_in_p)

    oa = _swa(proj, sinks.astype(f32), _alibi_slopes(SWA_Q_HEADS), q_norm_swa, k_norm_swa, B, S)
    ob = _moba(proj, _alibi_slopes(MOBA_HEADS), q_norm_moba, k_norm_moba, B, S)

    wr = jnp.concatenate(
        [w_router_group.T,
         w_router_expert.transpose(0, 2, 1).reshape(N_EXPERTS, D),
         jnp.zeros((ROUTER_ROWS - N_GROUPS - N_EXPERTS, D), f32)], axis=0)
    wr_hi = wr.astype(bf16)
    wr_lo = (wr - wr_hi.astype(f32)).astype(bf16)
    x1, h2, comb = _mix(oa, ob, proj, x2, w_up_swa.astype(bf16), w_up_moba.astype(bf16),
                        w_out.astype(bf16), g_ffn, jnp.stack([wr_hi, wr_lo]))

    y = _moe(x1, h2, comb, w_gate_e.astype(bf16), w_up_e.astype(bf16), w_down_e.astype(bf16))
    return y.reshape(B, S, D)
```

```python
import functools

import jax
import jax.numpy as jnp
from jax import lax
from jax.experimental import pallas as pl
from jax.experimental.pallas import tpu as pltpu

D_MODEL = 2048
HEAD_DIM = 64
ATTN_SCALE = HEAD_DIM ** -0.5
SWA_Q_HEADS = 16
SWA_KV_HEADS = 2
SWA_WINDOW = 128
SWA_BLOCK = 128
MOBA_HEADS = 16
MOBA_BLOCK = 256
MOBA_TOPK = 3
N_GROUPS = 4
EXPERTS_PER_GROUP = 4
N_EXPERTS = N_GROUPS * EXPERTS_PER_GROUP
D_EXPERT = 512
EPS = 1e-6

SWA_Q_DIM = SWA_Q_HEADS * HEAD_DIM
SWA_KV_DIM = SWA_KV_HEADS * HEAD_DIM
MOBA_DIM = MOBA_HEADS * HEAD_DIM
IN_COLS = SWA_Q_DIM + 2 * SWA_KV_DIM + 3 * MOBA_DIM + 2 * D_MODEL

LANES = 128
ROW_SLAB = D_MODEL // LANES
MOE_TILE = 512
VMEM_LIMIT = 56 * 1024 * 1024
NEG = -1e30

COL_GATE_A = 0
COL_GATE_B = D_MODEL // LANES
COL_QA = 2 * D_MODEL // LANES
COL_KA = COL_QA + SWA_Q_DIM // LANES
COL_VA = COL_KA + 1
COL_QB = COL_VA + 1
COL_KB = COL_QB + MOBA_DIM // LANES
COL_VB = COL_KB + MOBA_DIM // LANES

bf16 = jnp.bfloat16
f32 = jnp.float32


def _sigmoid(x):
    return 1.0 / (1.0 + jnp.exp(-x))


def _pair_rms(x, gain):
    lane = lax.broadcasted_iota(jnp.int32, x.shape, 1)
    lo = lane < HEAD_DIM
    sq = x * x
    s0 = jnp.sum(jnp.where(lo, sq, 0.0), axis=-1, keepdims=True)
    s1 = jnp.sum(jnp.where(lo, 0.0, sq), axis=-1, keepdims=True)
    r0 = lax.rsqrt(s0 * (1.0 / HEAD_DIM) + EPS)
    r1 = lax.rsqrt(s1 * (1.0 / HEAD_DIM) + EPS)
    return x * jnp.where(lo, r0, r1) * gain


def _dot_nt(a, b):
    return lax.dot_general(a, b, (((1,), (1,)), ((), ())), preferred_element_type=f32)


def _split2(x):
    hi = x.astype(bf16)
    lo = (x - hi.astype(f32)).astype(bf16)
    return hi, lo


def _inproj_body(x_ref, g_ref, w_ref, o_ref, h_ref):
    @pl.when(pl.program_id(1) == 0)
    def _():
        x = x_ref[...]
        ms = jnp.mean(x * x, axis=-1, keepdims=True)
        h_ref[...] = (x * lax.rsqrt(ms + EPS) * g_ref[...]).astype(bf16)

    o_ref[...] = jnp.dot(h_ref[...], w_ref[...], preferred_element_type=f32)


def _inproj(x2, g, w, tm=1024, tn=768):
    T = x2.shape[0]
    N = w.shape[1]
    return pl.pallas_call(
        _inproj_body,
        grid=(T // tm, N // tn),
        in_specs=[
            pl.BlockSpec((tm, D_MODEL), lambda i, j: (i, 0)),
            pl.BlockSpec((1, D_MODEL), lambda i, j: (0, 0)),
            pl.BlockSpec((D_MODEL, tn), lambda i, j: (0, j)),
        ],
        out_specs=pl.BlockSpec((tm, tn), lambda i, j: (i, j)),
        out_shape=jax.ShapeDtypeStruct((T, N), f32),
        scratch_shapes=[pltpu.VMEM((tm, D_MODEL), bf16)],
        compiler_params=pltpu.CompilerParams(
            dimension_semantics=("arbitrary", "arbitrary"),
            vmem_limit_bytes=VMEM_LIMIT),
        name="inproj",
    )(x2, g.reshape(1, D_MODEL), w)


def _swa_body(sinks_ref, slopes_ref, q_ref, kp_ref, kc_ref, vp_ref, vc_ref,
              qn_ref, kn_ref, o_ref):
    L = SWA_BLOCK
    n = pl.program_id(1)
    k2 = jnp.concatenate([kp_ref[...], kc_ref[...]], axis=0)
    v2 = jnp.concatenate([vp_ref[...], vc_ref[...]], axis=0).astype(bf16)
    k2n = _pair_rms(k2, kn_ref[...]).astype(bf16)

    row = lax.broadcasted_iota(jnp.int32, (L, 2 * L), 0)
    col = lax.broadcasted_iota(jnp.int32, (L, 2 * L), 1)
    dist = row + L - col
    ok = (dist >= 0) & (dist < SWA_WINDOW) & ((n > 0) | (col >= L))
    distf = dist.astype(f32)
    lane = lax.broadcasted_iota(jnp.int32, (L, LANES), 1)
    lane_lo = lane < HEAD_DIM

    heads_per_kv = SWA_Q_HEADS // SWA_KV_HEADS
    for pp in range(SWA_Q_HEADS // 2):
        g = (2 * pp) // heads_per_kv
        qn = _pair_rms(q_ref[:, pp * LANES:(pp + 1) * LANES], qn_ref[...]) * ATTN_SCALE
        qn_sw = pltpu.roll(qn, HEAD_DIM, axis=1)
        in_g = lane_lo if g == 0 else jnp.logical_not(lane_lo)
        o_pair = None
        for hh in range(2):
            h = 2 * pp + hh
            src = qn if hh == g else qn_sw
            qm = jnp.where(in_g, src, 0.0).astype(bf16)
            s = _dot_nt(qm, k2n) - slopes_ref[h] * distf
            s = jnp.where(ok, s, -jnp.inf)
            sink = sinks_ref[h]
            m = jnp.maximum(jnp.max(s, axis=-1, keepdims=True), sink)
            p = jnp.exp(s - m)
            denom = jnp.sum(p, axis=-1, keepdims=True) + jnp.exp(sink - m)
            o = jnp.dot(p.astype(bf16), v2, preferred_element_type=f32) / denom
            o = o if hh == g else pltpu.roll(o, HEAD_DIM, axis=1)
            keep = lane_lo if hh == 0 else jnp.logical_not(lane_lo)
            o_pair = jnp.where(keep, o, 0.0) if o_pair is None else jnp.where(keep, o, o_pair)
        o_ref[:, pp * LANES:(pp + 1) * LANES] = o_pair.astype(bf16)


def _swa(proj, sinks, slopes, q_norm, k_norm, B, S):
    L = SWA_BLOCK
    nb = S // L
    T = B * S
    smem = pl.BlockSpec(memory_space=pltpu.SMEM)

    def prev(b, n):
        return b * nb + jnp.maximum(n - 1, 0)

    return pl.pallas_call(
        _swa_body,
        grid=(B, nb),
        in_specs=[
            smem, smem,
            pl.BlockSpec((L, SWA_Q_DIM), lambda b, n: (b * nb + n, COL_QA // (SWA_Q_DIM // LANES))),
            pl.BlockSpec((L, LANES), lambda b, n: (prev(b, n), COL_KA)),
            pl.BlockSpec((L, LANES), lambda b, n: (b * nb + n, COL_KA)),
            pl.BlockSpec((L, LANES), lambda b, n: (prev(b, n), COL_VA)),
            pl.BlockSpec((L, LANES), lambda b, n: (b * nb + n, COL_VA)),
            pl.BlockSpec((1, LANES), lambda b, n: (0, 0)),
            pl.BlockSpec((1, LANES), lambda b, n: (0, 0)),
        ],
        out_specs=pl.BlockSpec((L, SWA_Q_DIM), lambda b, n: (b * nb + n, 0)),
        out_shape=jax.ShapeDtypeStruct((T, SWA_Q_DIM), bf16),
        compiler_params=pltpu.CompilerParams(
            dimension_semantics=("arbitrary", "arbitrary"),
            vmem_limit_bytes=VMEM_LIMIT),
        name="swa",
    )(sinks, slopes, proj, proj, proj, proj, proj,
      jnp.tile(q_norm, 2).reshape(1, LANES), jnp.tile(k_norm, 2).reshape(1, LANES))


N_BIAS_PARTS = 3
NB_ROWS = 8


def _moba_body(slopes_ref, q_ref, k_ref, v_ref, qn_ref, kn_ref, o_ref, kaug_ref, vt_ref, *, nb):
    L = MOBA_BLOCK
    S = nb * L
    p = pl.program_id(1)

    kn = _pair_rms(k_ref[...], kn_ref[...])
    lane = lax.broadcasted_iota(jnp.int32, (S, LANES), 1)
    kblk = lax.broadcasted_iota(jnp.int32, (S, LANES), 0) // L
    lane_lo = lane < HEAD_DIM
    c0 = lane - HEAD_DIM
    hot0 = ((c0 >= 0) & (c0 < NB_ROWS * N_BIAS_PARTS) & ((c0 % NB_ROWS) == kblk)).astype(f32)
    hot1 = ((lane < NB_ROWS * N_BIAS_PARTS) & ((lane % NB_ROWS) == kblk)).astype(f32)
    kaug_ref[0] = jnp.where(lane_lo, kn, hot0).astype(bf16)
    kaug_ref[1] = jnp.where(lane_lo, hot1, kn).astype(bf16)
    vt_ref[...] = v_ref[...].T.astype(bf16)
    means = [jnp.mean(kn[n * L:(n + 1) * L, :], axis=0, keepdims=True) for n in range(nb)]
    means += [jnp.zeros((1, LANES), f32)] * (NB_ROWS - nb)
    km = jnp.concatenate(means, axis=0)
    lane8 = lax.broadcasted_iota(jnp.int32, (NB_ROWS, LANES), 1)
    km_pair = jnp.concatenate(
        [jnp.where(lane8 < HEAD_DIM, km, 0.0), jnp.where(lane8 < HEAD_DIM, 0.0, km)], axis=0)
    km_hi, km_lo = _split2(km_pair)

    qs = _pair_rms(q_ref[...], qn_ref[...]) * ATTN_SCALE
    qt = qs.T
    qt_hi, qt_lo = _split2(qt)
    gate = (jnp.dot(km_hi, qt_hi, preferred_element_type=f32)
            + (jnp.dot(km_hi, qt_lo, preferred_element_type=f32)
               + jnp.dot(km_lo, qt_hi, preferred_element_type=f32)))

    blk = lax.broadcasted_iota(jnp.int32, (NB_ROWS, S), 0)
    qblk = lax.broadcasted_iota(jnp.int32, (NB_ROWS, S), 1) // L
    past = blk < qblk
    r = lax.broadcasted_iota(jnp.int32, (L, L), 1)
    c = lax.broadcasted_iota(jnp.int32, (L, L), 0)
    d = (r - c).astype(f32)

    qa, tab_past, tab_own = [], [], []
    for hh in range(2):
        g = jnp.where(past, gate[hh * NB_ROWS:(hh + 1) * NB_ROWS, :], -jnp.inf)
        rank = jnp.zeros((NB_ROWS, S), jnp.int32)
        for m in range(nb):
            gm = g[m:m + 1, :]
            ahead = (gm > g) | ((gm == g) & (m < blk))
            rank = rank + ahead.astype(jnp.int32)
        sel = past & (rank < MOBA_TOPK)
        slope = slopes_ref[2 * p + hh]
        bias = jnp.where(sel, (-slope * L) * (qblk - blk).astype(f32), NEG)
        bias = jnp.where(blk == qblk, 0.0, bias)
        b0 = bias.astype(bf16).astype(f32)
        r1 = bias - b0
        b1 = r1.astype(bf16).astype(f32)
        b2 = (r1 - b1).astype(bf16).astype(f32)
        extra = jnp.concatenate(
            [b0, b1, b2, jnp.zeros((HEAD_DIM - NB_ROWS * N_BIAS_PARTS, S), f32)], axis=0)
        if hh == 0:
            qa.append(jnp.concatenate([qt[:HEAD_DIM], extra], axis=0).astype(bf16))
        else:
            qa.append(jnp.concatenate([extra, qt[HEAD_DIM:]], axis=0).astype(bf16))
        t = -slope * d
        tab_past.append(t)
        tab_own.append(jnp.where(r >= c, t, NEG))

    for i in range(nb):
        outs = []
        for hh in range(2):
            nk = (i + 1) * L
            s = jnp.dot(kaug_ref[hh, 0:nk, :], qa[hh][:, i * L:(i + 1) * L],
                        preferred_element_type=f32)
            tiles = [s[n * L:(n + 1) * L, :] + (tab_own[hh] if n == i else tab_past[hh])
                     for n in range(i + 1)]
            m = functools.reduce(jnp.maximum, [jnp.max(t, axis=0, keepdims=True) for t in tiles])
            es = [jnp.exp(t - m) for t in tiles]
            l = functools.reduce(lambda a, b: a + b, [jnp.sum(e, axis=0, keepdims=True) for e in es])
            e_all = jnp.concatenate([e.astype(bf16) for e in es], axis=0)
            acc = jnp.dot(vt_ref[hh * HEAD_DIM:(hh + 1) * HEAD_DIM, 0:nk], e_all,
                          preferred_element_type=f32)
            outs.append(acc / l)
        o_ref[i * L:(i + 1) * L, :] = jnp.concatenate(outs, axis=0).T.astype(bf16)


def _moba(proj, slopes, q_norm, k_norm, B, S):
    L = MOBA_BLOCK
    nb = S // L
    assert nb <= NB_ROWS
    T = B * S
    n_pairs = MOBA_HEADS // 2
    return pl.pallas_call(
        functools.partial(_moba_body, nb=nb),
        grid=(B, n_pairs),
        in_specs=[
            pl.BlockSpec(memory_space=pltpu.SMEM),
            pl.BlockSpec((S, LANES), lambda b, p: (b, COL_QB + p)),
            pl.BlockSpec((S, LANES), lambda b, p: (b, COL_KB + p)),
            pl.BlockSpec((S, LANES), lambda b, p: (b, COL_VB + p)),
            pl.BlockSpec((1, LANES), lambda b, p: (0, 0)),
            pl.BlockSpec((1, LANES), lambda b, p: (0, 0)),
        ],
        out_specs=pl.BlockSpec((S, LANES), lambda b, p: (b, p)),
        out_shape=jax.ShapeDtypeStruct((T, MOBA_DIM), bf16),
        scratch_shapes=[
            pltpu.VMEM((2, S, LANES), bf16),
            pltpu.VMEM((LANES, S), bf16),
        ],
        compiler_params=pltpu.CompilerParams(
            dimension_semantics=("arbitrary", "arbitrary"),
            vmem_limit_bytes=VMEM_LIMIT),
        name="moba",
    )(slopes, proj, proj, proj,
      jnp.tile(q_norm, 2).reshape(1, LANES), jnp.tile(k_norm, 2).reshape(1, LANES))


ROUTER_ROWS = 32


def _mix_body(oa_ref, ob_ref, ga_ref, gb_ref, x_ref, wua_ref, wub_ref, wout_ref,
              gffn_ref, wr_ref, x1_ref, h2_ref, eid_ref, wts_ref):
    ya = jnp.dot(oa_ref[...], wua_ref[...], preferred_element_type=f32)
    yb = jnp.dot(ob_ref[...], wub_ref[...], preferred_element_type=f32)
    merged = _sigmoid(ga_ref[...]) * ya + _sigmoid(gb_ref[...]) * yb
    x1 = x_ref[...] + jnp.dot(merged.astype(bf16), wout_ref[...], preferred_element_type=f32)
    x1_ref[...] = x1
    ms = jnp.mean(x1 * x1, axis=-1, keepdims=True)
    h2 = x1 * lax.rsqrt(ms + EPS) * gffn_ref[...]
    tm = h2.shape[0]
    for c in range(ROW_SLAB):
        h2_ref[pl.ds(c, tm, stride=ROW_SLAB), :] = h2[:, c * LANES:(c + 1) * LANES]

    h_hi, h_lo = _split2(h2)
    lt = _dot_nt(wr_ref[0], h_hi) + (_dot_nt(wr_ref[0], h_lo) + _dot_nt(wr_ref[1], h_hi))
    gl = [lt[g:g + 1, :] for g in range(N_GROUPS)]
    gmax = functools.reduce(jnp.maximum, gl)
    gsum = functools.reduce(lambda a, b: a + b, [jnp.exp(v - gmax) for v in gl])
    g_p = 1.0 / gsum
    g_i = jnp.full((1, tm), N_GROUPS - 1, jnp.int32)
    for g in reversed(range(N_GROUPS)):
        g_i = jnp.where(gl[g] == gmax, g, g_i)

    el = []
    for e in range(EXPERTS_PER_GROUP):
        v = jnp.zeros((1, tm), f32)
        for g in range(N_GROUPS):
            r = N_GROUPS + g * EXPERTS_PER_GROUP + e
            v = jnp.where(g_i == g, lt[r:r + 1, :], v)
        el.append(v)
    emax = functools.reduce(jnp.maximum, el)
    ex = [jnp.exp(v - emax) for v in el]
    esum = functools.reduce(lambda a, b: a + b, ex)
    ep = [v / esum for v in ex]
    p1 = functools.reduce(jnp.maximum, ep)
    i1 = jnp.full((1, tm), EXPERTS_PER_GROUP - 1, jnp.int32)
    for e in reversed(range(EXPERTS_PER_GROUP)):
        i1 = jnp.where(ep[e] == p1, e, i1)
    rest = [jnp.where(i1 == e, -1.0, ep[e]) for e in range(EXPERTS_PER_GROUP)]
    p2 = functools.reduce(jnp.maximum, rest)
    i2 = jnp.full((1, tm), EXPERTS_PER_GROUP - 1, jnp.int32)
    for e in reversed(range(EXPERTS_PER_GROUP)):
        i2 = jnp.where(rest[e] == p2, e, i2)
    w1 = g_p * (p1 / (p1 + p2))
    w2 = g_p * (p2 / (p1 + p2))
    e1 = g_i * EXPERTS_PER_GROUP + i1
    e2 = g_i * EXPERTS_PER_GROUP + i2
    eid_ref[...] = jnp.concatenate([e1, e2, jnp.zeros((6, tm), jnp.int32)], axis=0)
    wts_ref[...] = jnp.concatenate([w1, w2, jnp.zeros((6, tm), f32)], axis=0)


def _mix(oa, ob, proj, x2, wua, wub, wout, g_ffn, wr, tm=256):
    T = x2.shape[0]
    const = lambda i: (0, 0)
    single = pl.Buffered(1)
    return pl.pallas_call(
        _mix_body,
        grid=(T // tm,),
        in_specs=[
            pl.BlockSpec((tm, SWA_Q_DIM), lambda i: (i, 0)),
            pl.BlockSpec((tm, MOBA_DIM), lambda i: (i, 0)),
            pl.BlockSpec((tm, D_MODEL), lambda i: (i, 0)),
            pl.BlockSpec((tm, D_MODEL), lambda i: (i, 1)),
            pl.BlockSpec((tm, D_MODEL), lambda i: (i, 0)),
            pl.BlockSpec((SWA_Q_DIM, D_MODEL), const, pipeline_mode=single),
            pl.BlockSpec((MOBA_DIM, D_MODEL), const, pipeline_mode=single),
            pl.BlockSpec((D_MODEL, D_MODEL), const, pipeline_mode=single),
            pl.BlockSpec((1, D_MODEL), const),
            pl.BlockSpec((2, ROUTER_ROWS, D_MODEL), lambda i: (0, 0, 0)),
        ],
        out_specs=[
            pl.BlockSpec((tm, D_MODEL), lambda i: (i, 0)),
            pl.BlockSpec((tm * ROW_SLAB, LANES), lambda i: (i, 0)),
            pl.BlockSpec((8, tm), lambda i: (0, i)),
            pl.BlockSpec((8, tm), lambda i: (0, i)),
        ],
        out_shape=[
            jax.ShapeDtypeStruct((T, D_MODEL), f32),
            jax.ShapeDtypeStruct((T * ROW_SLAB, LANES), f32),
            jax.ShapeDtypeStruct((8, T), jnp.int32),
            jax.ShapeDtypeStruct((8, T), f32),
        ],
        compiler_params=pltpu.CompilerParams(
            dimension_semantics=("arbitrary",),
            vmem_limit_bytes=VMEM_LIMIT),
        name="mix",
    )(oa, ob, proj, proj, x2, wua, wub, wout, g_ffn.reshape(1, D_MODEL), wr)


def _route_tables(eid, wts, T, tm):
    nt = 2 * T // tm + N_EXPERTS
    ptot = (nt + 1) * tm
    e_flat = eid[:2].reshape(2 * T)
    w_flat = wts[:2].reshape(2 * T)
    onehot = (e_flat[:, None] == jnp.arange(N_EXPERTS, dtype=jnp.int32)[None, :]).astype(jnp.int32)
    incl = jnp.cumsum(onehot, axis=0)
    pos = jnp.sum(onehot * incl, axis=1) - 1
    counts = incl[-1]
    padded = (counts + tm - 1) // tm * tm
    seg_end = jnp.cumsum(padded)
    dest = (seg_end - padded)[e_flat] + pos
    a = jnp.arange(2 * T, dtype=jnp.int32)
    tok = jnp.zeros((ptot,), jnp.int32).at[dest].set(jnp.where(a >= T, a - T, a))
    is_pad = jnp.ones((ptot,), jnp.int32).at[dest].set(0)
    pad_row = jnp.minimum(2 * T + jnp.cumsum(is_pad) - 1, nt * tm - 1)
    sdest = pad_row.astype(jnp.int32).at[dest].set(a)
    w_sorted = jnp.zeros((ptot,), f32).at[dest].set(w_flat)
    tile_start = jnp.arange(nt, dtype=jnp.int32) * tm
    tile_e = jnp.minimum(jnp.searchsorted(seg_end, tile_start, side="right"), N_EXPERTS - 1)
    nvalid = (seg_end[-1] // tm).astype(jnp.int32).reshape(1)
    return (nvalid, tile_e.astype(jnp.int32), tok.reshape(nt + 1, 1, tm),
            sdest.reshape(nt + 1, 1, tm), w_sorted.reshape(ptot, 1))


def _moe_body(nvalid_ref, te_ref, tok0_ref, tokn_ref, sd_ref, w_ref, wg_ref, wu_ref, wd_ref,
              h2r_hbm, ytok_hbm, gbuf, ybuf, gsem, ssem, *, tm, nt):
    t = pl.program_id(0)
    slot = lax.rem(t, 2)
    other = 1 - slot

    def gather_copy(tok, s, r):
        return pltpu.make_async_copy(
            h2r_hbm.at[tok], gbuf.at[pl.ds((s * tm + r) * ROW_SLAB, ROW_SLAB), :], gsem.at[s])

    def scatter_copy(dst, s, r):
        return pltpu.make_async_copy(
            ybuf.at[pl.ds((s * tm + r) * ROW_SLAB, ROW_SLAB), :], ytok_hbm.at[dst], ssem.at[s])

    def for_rows(fn):
        def body(r, carry):
            fn(r)
            return carry
        lax.fori_loop(0, tm, body, 0, unroll=8)

    def wait_gather(s):
        for_rows(lambda r: gather_copy(0, s, r).wait())

    def wait_scatter(s):
        for_rows(lambda r: scatter_copy(0, s, r).wait())

    @pl.when(t == 0)
    def _():
        ybuf[...] = jnp.zeros(ybuf.shape, f32)
        for_rows(lambda r: gather_copy(tok0_ref[0, 0, r], 0, r).start())

    for_rows(lambda r: gather_copy(tokn_ref[0, 0, r], other, r).start())
    wait_gather(slot)

    @pl.when(t >= 2)
    def _():
        wait_scatter(slot)

    @pl.when(t < nvalid_ref[0])
    def _():
        gbase = slot * tm * ROW_SLAB
        x = jnp.concatenate(
            [gbuf[pl.ds(gbase + c, tm, stride=ROW_SLAB), :] for c in range(ROW_SLAB)],
            axis=1).astype(bf16)
        gte = jnp.dot(x, wg_ref[...], preferred_element_type=f32)
        up = jnp.dot(x, wu_ref[...], preferred_element_type=f32)
        hid = (gte * _sigmoid(gte)) * up * w_ref[...]
        y = jnp.dot(hid.astype(bf16), wd_ref[...], preferred_element_type=f32)
        for c in range(ROW_SLAB):
            ybuf[pl.ds(gbase + c, tm, stride=ROW_SLAB), :] = y[:, c * LANES:(c + 1) * LANES]

    for_rows(lambda r: scatter_copy(sd_ref[0, 0, r], slot, r).start())

    @pl.when(t == nt - 1)
    def _():
        wait_gather(other)
        wait_scatter(other)
        wait_scatter(slot)


def _moe(h2r, tables, wg, wu, wd, T, tm):
    nvalid, tile_e, tok, sdest, w_sorted = tables
    nt = tile_e.shape[0]
    assert nt >= 2
    n_rows = nt * tm
    smem_tile = lambda index_map: pl.BlockSpec((1, 1, tm), index_map, memory_space=pltpu.SMEM)
    grid_spec = pltpu.PrefetchScalarGridSpec(
        num_scalar_prefetch=2,
        grid=(nt,),
        in_specs=[
            smem_tile(lambda t, nv, te: (0, 0, 0)),
            smem_tile(lambda t, nv, te: (t + 1, 0, 0)),
            smem_tile(lambda t, nv, te: (t, 0, 0)),
            pl.BlockSpec((tm, 1), lambda t, nv, te: (t, 0)),
            pl.BlockSpec((None, D_MODEL, D_EXPERT), lambda t, nv, te: (te[t], 0, 0)),
            pl.BlockSpec((None, D_MODEL, D_EXPERT), lambda t, nv, te: (te[t], 0, 0)),
            pl.BlockSpec((None, D_EXPERT, D_MODEL), lambda t, nv, te: (te[t], 0, 0)),
            pl.BlockSpec(memory_space=pl.ANY),
        ],
        out_specs=pl.BlockSpec(memory_space=pl.ANY),
        scratch_shapes=[
            pltpu.VMEM((2 * tm * ROW_SLAB, LANES), f32),
            pltpu.VMEM((2 * tm * ROW_SLAB, LANES), f32),
            pltpu.SemaphoreType.DMA((2,)),
            pltpu.SemaphoreType.DMA((2,)),
        ],
    )
    return pl.pallas_call(
        functools.partial(_moe_body, tm=tm, nt=nt),
        grid_spec=grid_spec,
        out_shape=jax.ShapeDtypeStruct((n_rows, ROW_SLAB, LANES), f32),
        compiler_params=pltpu.CompilerParams(
            dimension_semantics=("arbitrary",),
            vmem_limit_bytes=VMEM_LIMIT),
        name="moe",
    )(nvalid, tile_e, tok, tok, sdest, w_sorted, wg, wu, wd, h2r)


def _combine_body(x1_ref, ya_ref, yb_ref, o_ref):
    tm = x1_ref.shape[0]
    for c in range(ROW_SLAB):
        cols = slice(c * LANES, (c + 1) * LANES)
        rows = pl.ds(c, tm, stride=ROW_SLAB)
        o_ref[:, cols] = x1_ref[:, cols] + (ya_ref[rows, :] + yb_ref[rows, :])


def _combine(x1, ytok2d, tm=256):
    T = x1.shape[0]
    nblk = T // tm
    return pl.pallas_call(
        _combine_body,
        grid=(nblk,),
        in_specs=[
            pl.BlockSpec((tm, D_MODEL), lambda i: (i, 0)),
            pl.BlockSpec((tm * ROW_SLAB, LANES), lambda i: (i, 0)),
            pl.BlockSpec((tm * ROW_SLAB, LANES), lambda i: (nblk + i, 0)),
        ],
        out_specs=pl.BlockSpec((tm, D_MODEL), lambda i: (i, 0)),
        out_shape=jax.ShapeDtypeStruct((T, D_MODEL), f32),
        compiler_params=pltpu.CompilerParams(
            dimension_semantics=("arbitrary",),
            vmem_limit_bytes=VMEM_LIMIT),
        name="combine",
    )(x1, ytok2d, ytok2d)


def _alibi_slopes(n):
    return jnp.exp2(-8.0 * jnp.arange(1, n + 1, dtype=f32) / n)


def kernel(x, g_mix, w_in, q_norm_swa, k_norm_swa, sinks, q_norm_moba, k_norm_moba,
           w_up_swa, w_up_moba, w_out, g_ffn, w_router_group, w_router_expert,
           w_gate_e, w_up_e, w_down_e):
    B, S, D = x.shape
    assert D == D_MODEL and S % MOBA_BLOCK == 0 and S % SWA_BLOCK == 0
    T = B * S
    x2 = x.reshape(T, D)

    n_qkv = IN_COLS - 2 * D_MODEL
    w_in_p = jnp.concatenate([w_in[:, n_qkv:], w_in[:, :n_qkv]], axis=1).astype(bf16)
    proj = _inproj(x2, g_mix, w_in_p)

    oa = _swa(proj, sinks.astype(f32), _alibi_slopes(SWA_Q_HEADS), q_norm_swa, k_norm_swa, B, S)
    ob = _moba(proj, _alibi_slopes(MOBA_HEADS), q_norm_moba, k_norm_moba, B, S)

    wr = jnp.concatenate(
        [w_router_group.T,
         w_router_expert.transpose(0, 2, 1).reshape(N_EXPERTS, D),
         jnp.zeros((ROUTER_ROWS - N_GROUPS - N_EXPERTS, D), f32)], axis=0)
    wr_hi = wr.astype(bf16)
    wr_lo = (wr - wr_hi.astype(f32)).astype(bf16)
    x1, h2s, eid, wts = _mix(oa, ob, proj, x2, w_up_swa.astype(bf16), w_up_moba.astype(bf16),
                             w_out.astype(bf16), g_ffn, jnp.stack([wr_hi, wr_lo]))

    tables = _route_tables(eid, wts, T, MOE_TILE)
    ytok = _moe(h2s.reshape(T, ROW_SLAB, LANES), tables,
                w_gate_e.astype(bf16), w_up_e.astype(bf16), w_down_e.astype(bf16), T, MOE_TILE)
    y = _combine(x1, ytok.reshape(-1, LANES))
    return y.reshape(B, S, D)
```

```python
import functools

import jax
import jax.numpy as jnp
from jax import lax
from jax.experimental import pallas as pl
from jax.experimental.pallas import tpu as pltpu

D_MODEL = 2048
HEAD_DIM = 64
ATTN_SCALE = HEAD_DIM ** -0.5
SWA_Q_HEADS = 16
SWA_KV_HEADS = 2
SWA_WINDOW = 128
SWA_BLOCK = 128
MOBA_HEADS = 16
MOBA_BLOCK = 256
MOBA_TOPK = 3
N_GROUPS = 4
EXPERTS_PER_GROUP = 4
N_EXPERTS = N_GROUPS * EXPERTS_PER_GROUP
D_EXPERT = 512
EPS = 1e-6

SWA_Q_DIM = SWA_Q_HEADS * HEAD_DIM
SWA_KV_DIM = SWA_KV_HEADS * HEAD_DIM
MOBA_DIM = MOBA_HEADS * HEAD_DIM
IN_COLS = SWA_Q_DIM + 2 * SWA_KV_DIM + 3 * MOBA_DIM + 2 * D_MODEL

LANES = 128
ROW_SLAB = D_MODEL // LANES
MOE_TILE = 512
VMEM_LIMIT = 56 * 1024 * 1024
NEG = -1e30

COL_GATE_A = 0
COL_GATE_B = D_MODEL // LANES
COL_QA = 2 * D_MODEL // LANES
COL_KA = COL_QA + SWA_Q_DIM // LANES
COL_VA = COL_KA + 1
COL_QB = COL_VA + 1
COL_KB = COL_QB + MOBA_DIM // LANES
COL_VB = COL_KB + MOBA_DIM // LANES

bf16 = jnp.bfloat16
f32 = jnp.float32


def _sigmoid(x):
    return 1.0 / (1.0 + jnp.exp(-x))


def _pair_rms(x, gain):
    lane = lax.broadcasted_iota(jnp.int32, x.shape, 1)
    lo = lane < HEAD_DIM
    sq = x * x
    s0 = jnp.sum(jnp.where(lo, sq, 0.0), axis=-1, keepdims=True)
    s1 = jnp.sum(jnp.where(lo, 0.0, sq), axis=-1, keepdims=True)
    r0 = lax.rsqrt(s0 * (1.0 / HEAD_DIM) + EPS)
    r1 = lax.rsqrt(s1 * (1.0 / HEAD_DIM) + EPS)
    return x * jnp.where(lo, r0, r1) * gain


def _dot_nt(a, b):
    return lax.dot_general(a, b, (((1,), (1,)), ((), ())), preferred_element_type=f32)


def _split2(x):
    hi = x.astype(bf16)
    lo = (x - hi.astype(f32)).astype(bf16)
    return hi, lo


def _inproj_body(x_ref, g_ref, w_ref, o_ref, h_ref):
    @pl.when(pl.program_id(1) == 0)
    def _():
        x = x_ref[...]
        ms = jnp.mean(x * x, axis=-1, keepdims=True)
        h_ref[...] = (x * lax.rsqrt(ms + EPS) * g_ref[...]).astype(bf16)

    o_ref[...] = jnp.dot(h_ref[...], w_ref[...], preferred_element_type=f32)


def _inproj(x2, g, w, tm=1024, tn=768):
    T = x2.shape[0]
    N = w.shape[1]
    return pl.pallas_call(
        _inproj_body,
        grid=(T // tm, N // tn),
        in_specs=[
            pl.BlockSpec((tm, D_MODEL), lambda i, j: (i, 0)),
            pl.BlockSpec((1, D_MODEL), lambda i, j: (0, 0)),
            pl.BlockSpec((D_MODEL, tn), lambda i, j: (0, j)),
        ],
        out_specs=pl.BlockSpec((tm, tn), lambda i, j: (i, j)),
        out_shape=jax.ShapeDtypeStruct((T, N), f32),
        scratch_shapes=[pltpu.VMEM((tm, D_MODEL), bf16)],
        compiler_params=pltpu.CompilerParams(
            dimension_semantics=("arbitrary", "arbitrary"),
            vmem_limit_bytes=VMEM_LIMIT),
        name="inproj",
    )(x2, g.reshape(1, D_MODEL), w)


def _swa_body(sinks_ref, slopes_ref, q_ref, kp_ref, kc_ref, vp_ref, vc_ref,
              qn_ref, kn_ref, o_ref):
    L = SWA_BLOCK
    n = pl.program_id(1)
    k2 = jnp.concatenate([kp_ref[...], kc_ref[...]], axis=0)
    v2 = jnp.concatenate([vp_ref[...], vc_ref[...]], axis=0).astype(bf16)
    k2n = _pair_rms(k2, kn_ref[...]).astype(bf16)

    row = lax.broadcasted_iota(jnp.int32, (L, 2 * L), 0)
    col = lax.broadcasted_iota(jnp.int32, (L, 2 * L), 1)
    dist = row + L - col
    ok = (dist >= 0) & (dist < SWA_WINDOW) & ((n > 0) | (col >= L))
    distf = dist.astype(f32)
    lane = lax.broadcasted_iota(jnp.int32, (L, LANES), 1)
    lane_lo = lane < HEAD_DIM

    heads_per_kv = SWA_Q_HEADS // SWA_KV_HEADS
    for pp in range(SWA_Q_HEADS // 2):
        g = (2 * pp) // heads_per_kv
        qn = _pair_rms(q_ref[:, pp * LANES:(pp + 1) * LANES], qn_ref[...]) * ATTN_SCALE
        qn_sw = pltpu.roll(qn, HEAD_DIM, axis=1)
        in_g = lane_lo if g == 0 else jnp.logical_not(lane_lo)
        o_pair = None
        for hh in range(2):
            h = 2 * pp + hh
            src = qn if hh == g else qn_sw
            qm = jnp.where(in_g, src, 0.0).astype(bf16)
            s = _dot_nt(qm, k2n) - slopes_ref[h] * distf
            s = jnp.where(ok, s, -jnp.inf)
            sink = sinks_ref[h]
            m = jnp.maximum(jnp.max(s, axis=-1, keepdims=True), sink)
            p = jnp.exp(s - m)
            denom = jnp.sum(p, axis=-1, keepdims=True) + jnp.exp(sink - m)
            o = jnp.dot(p.astype(bf16), v2, preferred_element_type=f32) / denom
            o = o if hh == g else pltpu.roll(o, HEAD_DIM, axis=1)
            keep = lane_lo if hh == 0 else jnp.logical_not(lane_lo)
            o_pair = jnp.where(keep, o, 0.0) if o_pair is None else jnp.where(keep, o, o_pair)
        o_ref[:, pp * LANES:(pp + 1) * LANES] = o_pair.astype(bf16)


def _swa(proj, sinks, slopes, q_norm, k_norm, B, S):
    L = SWA_BLOCK
    nb = S // L
    T = B * S
    smem = pl.BlockSpec(memory_space=pltpu.SMEM)

    def prev(b, n):
        return b * nb + jnp.maximum(n - 1, 0)

    return pl.pallas_call(
        _swa_body,
        grid=(B, nb),
        in_specs=[
            smem, smem,
            pl.BlockSpec((L, SWA_Q_DIM), lambda b, n: (b * nb + n, COL_QA // (SWA_Q_DIM // LANES))),
            pl.BlockSpec((L, LANES), lambda b, n: (prev(b, n), COL_KA)),
            pl.BlockSpec((L, LANES), lambda b, n: (b * nb + n, COL_KA)),
            pl.BlockSpec((L, LANES), lambda b, n: (prev(b, n), COL_VA)),
            pl.BlockSpec((L, LANES), lambda b, n: (b * nb + n, COL_VA)),
            pl.BlockSpec((1, LANES), lambda b, n: (0, 0)),
            pl.BlockSpec((1, LANES), lambda b, n: (0, 0)),
        ],
        out_specs=pl.BlockSpec((L, SWA_Q_DIM), lambda b, n: (b * nb + n, 0)),
        out_shape=jax.ShapeDtypeStruct((T, SWA_Q_DIM), bf16),
        compiler_params=pltpu.CompilerParams(
            dimension_semantics=("arbitrary", "arbitrary"),
            vmem_limit_bytes=VMEM_LIMIT),
        name="swa",
    )(sinks, slopes, proj, proj, proj, proj, proj,
      jnp.tile(q_norm, 2).reshape(1, LANES), jnp.tile(k_norm, 2).reshape(1, LANES))


N_BIAS_PARTS = 3
NB_ROWS = 8


def _moba_body(slopes_ref, q_ref, k_ref, v_ref, qn_ref, kn_ref, o_ref, kaug_ref, vt_ref, *, nb):
    L = MOBA_BLOCK
    S = nb * L
    p = pl.program_id(1)

    kn = _pair_rms(k_ref[...], kn_ref[...])
    lane = lax.broadcasted_iota(jnp.int32, (S, LANES), 1)
    kblk = lax.broadcasted_iota(jnp.int32, (S, LANES), 0) // L
    lane_lo = lane < HEAD_DIM
    c0 = lane - HEAD_DIM
    hot0 = ((c0 >= 0) & (c0 < NB_ROWS * N_BIAS_PARTS) & ((c0 % NB_ROWS) == kblk)).astype(f32)
    hot1 = ((lane < NB_ROWS * N_BIAS_PARTS) & ((lane % NB_ROWS) == kblk)).astype(f32)
    kaug_ref[0] = jnp.where(lane_lo, kn, hot0).astype(bf16)
    kaug_ref[1] = jnp.where(lane_lo, hot1, kn).astype(bf16)
    vt_ref[...] = v_ref[...].T.astype(bf16)
    means = [jnp.mean(kn[n * L:(n + 1) * L, :], axis=0, keepdims=True) for n in range(nb)]
    means += [jnp.zeros((1, LANES), f32)] * (NB_ROWS - nb)
    km = jnp.concatenate(means, axis=0)
    lane8 = lax.broadcasted_iota(jnp.int32, (NB_ROWS, LANES), 1)
    km_pair = jnp.concatenate(
        [jnp.where(lane8 < HEAD_DIM, km, 0.0), jnp.where(lane8 < HEAD_DIM, 0.0, km)], axis=0)
    km_hi, km_lo = _split2(km_pair)

    qs = _pair_rms(q_ref[...], qn_ref[...]) * ATTN_SCALE
    qt = qs.T
    qt_hi, qt_lo = _split2(qt)
    gate = (jnp.dot(km_hi, qt_hi, preferred_element_type=f32)
            + (jnp.dot(km_hi, qt_lo, preferred_element_type=f32)
               + jnp.dot(km_lo, qt_hi, preferred_element_type=f32)))

    blk = lax.broadcasted_iota(jnp.int32, (NB_ROWS, S), 0)
    qblk = lax.broadcasted_iota(jnp.int32, (NB_ROWS, S), 1) // L
    past = blk < qblk
    r = lax.broadcasted_iota(jnp.int32, (L, L), 1)
    c = lax.broadcasted_iota(jnp.int32, (L, L), 0)
    d = (r - c).astype(f32)

    qa, tab_past, tab_own = [], [], []
    for hh in range(2):
        g = jnp.where(past, gate[hh * NB_ROWS:(hh + 1) * NB_ROWS, :], -jnp.inf)
        rank = jnp.zeros((NB_ROWS, S), jnp.int32)
        for m in range(nb):
            gm = g[m:m + 1, :]
            ahead = (gm > g) | ((gm == g) & (m < blk))
            rank = rank + ahead.astype(jnp.int32)
        sel = past & (rank < MOBA_TOPK)
        slope = slopes_ref[2 * p + hh]
        bias = jnp.where(sel, (-slope * L) * (qblk - blk).astype(f32), NEG)
        bias = jnp.where(blk == qblk, 0.0, bias)
        b0 = bias.astype(bf16).astype(f32)
        r1 = bias - b0
        b1 = r1.astype(bf16).astype(f32)
        b2 = (r1 - b1).astype(bf16).astype(f32)
        extra = jnp.concatenate(
            [b0, b1, b2, jnp.zeros((HEAD_DIM - NB_ROWS * N_BIAS_PARTS, S), f32)], axis=0)
        if hh == 0:
            qa.append(jnp.concatenate([qt[:HEAD_DIM], extra], axis=0).astype(bf16))
        else:
            qa.append(jnp.concatenate([extra, qt[HEAD_DIM:]], axis=0).astype(bf16))
        t = -slope * d
        tab_past.append(t)
        tab_own.append(jnp.where(r >= c, t, NEG))

    for i in range(nb):
        outs = []
        for hh in range(2):
            nk = (i + 1) * L
            s = jnp.dot(kaug_ref[hh, 0:nk, :], qa[hh][:, i * L:(i + 1) * L],
                        preferred_element_type=f32)
            tiles = [s[n * L:(n + 1) * L, :] + (tab_own[hh] if n == i else tab_past[hh])
                     for n in range(i + 1)]
            m = functools.reduce(jnp.maximum, [jnp.max(t, axis=0, keepdims=True) for t in tiles])
            es = [jnp.exp(t - m) for t in tiles]
            l = functools.reduce(lambda a, b: a + b, [jnp.sum(e, axis=0, keepdims=True) for e in es])
            e_all = jnp.concatenate([e.astype(bf16) for e in es], axis=0)
            acc = jnp.dot(vt_ref[hh * HEAD_DIM:(hh + 1) * HEAD_DIM, 0:nk], e_all,
                          preferred_element_type=f32)
            outs.append(acc / l)
        o_ref[i * L:(i + 1) * L, :] = jnp.concatenate(outs, axis=0).T.astype(bf16)


def _moba(proj, slopes, q_norm, k_norm, B, S):
    L = MOBA_BLOCK
    nb = S // L
    assert nb <= NB_ROWS
    T = B * S
    n_pairs = MOBA_HEADS // 2
    return pl.pallas_call(
        functools.partial(_moba_body, nb=nb),
        grid=(B, n_pairs),
        in_specs=[
            pl.BlockSpec(memory_space=pltpu.SMEM),
            pl.BlockSpec((S, LANES), lambda b, p: (b, COL_QB + p)),
            pl.BlockSpec((S, LANES), lambda b, p: (b, COL_KB + p)),
            pl.BlockSpec((S, LANES), lambda b, p: (b, COL_VB + p)),
            pl.BlockSpec((1, LANES), lambda b, p: (0, 0)),
            pl.BlockSpec((1, LANES), lambda b, p: (0, 0)),
        ],
        out_specs=pl.BlockSpec((S, LANES), lambda b, p: (b, p)),
        out_shape=jax.ShapeDtypeStruct((T, MOBA_DIM), bf16),
        scratch_shapes=[
            pltpu.VMEM((2, S, LANES), bf16),
            pltpu.VMEM((LANES, S), bf16),
        ],
        compiler_params=pltpu.CompilerParams(
            dimension_semantics=("arbitrary", "arbitrary"),
            vmem_limit_bytes=VMEM_LIMIT),
        name="moba",
    )(slopes, proj, proj, proj,
      jnp.tile(q_norm, 2).reshape(1, LANES), jnp.tile(k_norm, 2).reshape(1, LANES))


ROUTER_ROWS = 32


def _mix_body(oa_ref, ob_ref, ga_ref, gb_ref, x_ref, wua_ref, wub_ref, wout_ref,
              gffn_ref, wr_ref, x1_ref, h2_ref, eid_ref, wts_ref):
    ya = jnp.dot(oa_ref[...], wua_ref[...], preferred_element_type=f32)
    yb = jnp.dot(ob_ref[...], wub_ref[...], preferred_element_type=f32)
    merged = _sigmoid(ga_ref[...]) * ya + _sigmoid(gb_ref[...]) * yb
    x1 = x_ref[...] + jnp.dot(merged.astype(bf16), wout_ref[...], preferred_element_type=f32)
    x1_ref[...] = x1
    ms = jnp.mean(x1 * x1, axis=-1, keepdims=True)
    h2 = x1 * lax.rsqrt(ms + EPS) * gffn_ref[...]
    tm = h2.shape[0]
    for c in range(ROW_SLAB):
        h2_ref[pl.ds(c, tm, stride=ROW_SLAB), :] = h2[:, c * LANES:(c + 1) * LANES]

    h_hi, h_lo = _split2(h2)
    lt = _dot_nt(wr_ref[0], h_hi) + (_dot_nt(wr_ref[0], h_lo) + _dot_nt(wr_ref[1], h_hi))
    gl = [lt[g:g + 1, :] for g in range(N_GROUPS)]
    gmax = functools.reduce(jnp.maximum, gl)
    gsum = functools.reduce(lambda a, b: a + b, [jnp.exp(v - gmax) for v in gl])
    g_p = 1.0 / gsum
    g_i = jnp.full((1, tm), N_GROUPS - 1, jnp.int32)
    for g in reversed(range(N_GROUPS)):
        g_i = jnp.where(gl[g] == gmax, g, g_i)

    el = []
    for e in range(EXPERTS_PER_GROUP):
        v = jnp.zeros((1, tm), f32)
        for g in range(N_GROUPS):
            r = N_GROUPS + g * EXPERTS_PER_GROUP + e
            v = jnp.where(g_i == g, lt[r:r + 1, :], v)
        el.append(v)
    emax = functools.reduce(jnp.maximum, el)
    ex = [jnp.exp(v - emax) for v in el]
    esum = functools.reduce(lambda a, b: a + b, ex)
    ep = [v / esum for v in ex]
    p1 = functools.reduce(jnp.maximum, ep)
    i1 = jnp.full((1, tm), EXPERTS_PER_GROUP - 1, jnp.int32)
    for e in reversed(range(EXPERTS_PER_GROUP)):
        i1 = jnp.where(ep[e] == p1, e, i1)
    rest = [jnp.where(i1 == e, -1.0, ep[e]) for e in range(EXPERTS_PER_GROUP)]
    p2 = functools.reduce(jnp.maximum, rest)
    i2 = jnp.full((1, tm), EXPERTS_PER_GROUP - 1, jnp.int32)
    for e in reversed(range(EXPERTS_PER_GROUP)):
        i2 = jnp.where(rest[e] == p2, e, i2)
    w1 = g_p * (p1 / (p1 + p2))
    w2 = g_p * (p2 / (p1 + p2))
    e1 = g_i * EXPERTS_PER_GROUP + i1
    e2 = g_i * EXPERTS_PER_GROUP + i2
    eid_ref[...] = jnp.concatenate([e1, e2, jnp.zeros((6, tm), jnp.int32)], axis=0)
    wts_ref[...] = jnp.concatenate([w1, w2, jnp.zeros((LANES - 2, tm), f32)], axis=0).T


def _mix(oa, ob, proj, x2, wua, wub, wout, g_ffn, wr, tm=256):
    T = x2.shape[0]
    const = lambda i: (0, 0)
    single = pl.Buffered(1)
    return pl.pallas_call(
        _mix_body,
        grid=(T // tm,),
        in_specs=[
            pl.BlockSpec((tm, SWA_Q_DIM), lambda i: (i, 0)),
            pl.BlockSpec((tm, MOBA_DIM), lambda i: (i, 0)),
            pl.BlockSpec((tm, D_MODEL), lambda i: (i, 0)),
            pl.BlockSpec((tm, D_MODEL), lambda i: (i, 1)),
            pl.BlockSpec((tm, D_MODEL), lambda i: (i, 0)),
            pl.BlockSpec((SWA_Q_DIM, D_MODEL), const, pipeline_mode=single),
            pl.BlockSpec((MOBA_DIM, D_MODEL), const, pipeline_mode=single),
            pl.BlockSpec((D_MODEL, D_MODEL), const, pipeline_mode=single),
            pl.BlockSpec((1, D_MODEL), const),
            pl.BlockSpec((2, ROUTER_ROWS, D_MODEL), lambda i: (0, 0, 0)),
        ],
        out_specs=[
            pl.BlockSpec((tm, D_MODEL), lambda i: (i, 0)),
            pl.BlockSpec((tm * ROW_SLAB, LANES), lambda i: (i, 0)),
            pl.BlockSpec((8, tm), lambda i: (0, i)),
            pl.BlockSpec((tm, LANES), lambda i: (i, 0)),
        ],
        out_shape=[
            jax.ShapeDtypeStruct((T, D_MODEL), f32),
            jax.ShapeDtypeStruct((T * ROW_SLAB, LANES), f32),
            jax.ShapeDtypeStruct((8, T), jnp.int32),
            jax.ShapeDtypeStruct((T, LANES), f32),
        ],
        compiler_params=pltpu.CompilerParams(
            dimension_semantics=("arbitrary",),
            vmem_limit_bytes=VMEM_LIMIT),
        name="mix",
    )(oa, ob, proj, proj, x2, wua, wub, wout, g_ffn.reshape(1, D_MODEL), wr)


def _route_tables(eid, T, tm):
    nt = 2 * T // tm + N_EXPERTS
    e_flat = eid[:2].reshape(2 * T)
    onehot = (e_flat[:, None] == jnp.arange(N_EXPERTS, dtype=jnp.int32)[None, :]).astype(jnp.int32)
    incl = jnp.cumsum(onehot, axis=0)
    pos = jnp.sum(onehot * incl, axis=1) - 1
    counts = incl[-1]
    padded = (counts + tm - 1) // tm * tm
    seg_end = jnp.cumsum(padded)
    dest = jnp.sum(onehot * (seg_end - padded)[None, :], axis=1) + pos
    tile_start = jnp.arange(nt, dtype=jnp.int32) * tm
    tile_e = jnp.sum((tile_start[:, None] >= seg_end[None, :]).astype(jnp.int32), axis=1)
    tile_e = jnp.minimum(tile_e, N_EXPERTS - 1)
    nvalid = (seg_end[-1] // tm).astype(jnp.int32).reshape(1)
    return nvalid, tile_e.astype(jnp.int32), dest.astype(jnp.int32)


def _moe_body(nvalid_ref, te_ref, dest_ref, wg_ref, wu_ref, wd_ref, h2r_hbm, ytok_hbm,
              asg, gbuf, ybuf, gsem, ssem, *, tm, nt, n_tok):
    t = pl.program_id(0)
    slot = lax.rem(t, 2)
    other = 1 - slot

    def for_rows(tile, fn):
        def body(r, carry):
            a = asg[tile * tm + r]

            @pl.when(a >= 0)
            def _():
                fn(a, r)
            return carry
        lax.fori_loop(0, tm, body, 0, unroll=8)

    def gather_copy(a, s, r):
        tok = jnp.where(a >= n_tok, a - n_tok, a)
        return pltpu.make_async_copy(
            h2r_hbm.at[tok], gbuf.at[pl.ds((s * tm + r) * ROW_SLAB, ROW_SLAB), :], gsem.at[s])

    def scatter_copy(a, s, r):
        return pltpu.make_async_copy(
            ybuf.at[pl.ds((s * tm + r) * ROW_SLAB, ROW_SLAB), :], ytok_hbm.at[a], ssem.at[s])

    @pl.when(t == 0)
    def _():
        def clear(p, carry):
            asg[p] = -1
            return carry
        lax.fori_loop(0, nt * tm, clear, 0, unroll=16)

        def place(a, carry):
            asg[dest_ref[a]] = a
            return carry
        lax.fori_loop(0, 2 * n_tok, place, 0, unroll=8)
        gbuf[...] = jnp.zeros(gbuf.shape, f32)
        for_rows(0, lambda a, r: gather_copy(a, 0, r).start())

    @pl.when(t + 1 < nt)
    def _():
        for_rows(t + 1, lambda a, r: gather_copy(a, other, r).start())

    for_rows(t, lambda a, r: gather_copy(a, slot, r).wait())

    @pl.when(t >= 2)
    def _():
        for_rows(t - 2, lambda a, r: scatter_copy(a, slot, r).wait())

    @pl.when(t < nvalid_ref[0])
    def _():
        base = slot * tm * ROW_SLAB
        x = jnp.concatenate(
            [gbuf[pl.ds(base + c, tm, stride=ROW_SLAB), :] for c in range(ROW_SLAB)],
            axis=1).astype(bf16)
        gte = jnp.dot(x, wg_ref[...], preferred_element_type=f32)
        up = jnp.dot(x, wu_ref[...], preferred_element_type=f32)
        hid = (gte * _sigmoid(gte)) * up
        y = jnp.dot(hid.astype(bf16), wd_ref[...], preferred_element_type=f32)
        for c in range(ROW_SLAB):
            ybuf[pl.ds(base + c, tm, stride=ROW_SLAB), :] = y[:, c * LANES:(c + 1) * LANES]

    for_rows(t, lambda a, r: scatter_copy(a, slot, r).start())

    @pl.when(t == nt - 1)
    def _():
        for_rows(t - 1, lambda a, r: scatter_copy(a, other, r).wait())
        for_rows(t, lambda a, r: scatter_copy(a, slot, r).wait())


def _moe(h2r, tables, wg, wu, wd, T, tm):
    nvalid, tile_e, dest = tables
    nt = tile_e.shape[0]
    assert nt >= 2
    grid_spec = pltpu.PrefetchScalarGridSpec(
        num_scalar_prefetch=3,
        grid=(nt,),
        in_specs=[
            pl.BlockSpec((None, D_MODEL, D_EXPERT), lambda t, nv, te, de: (te[t], 0, 0)),
            pl.BlockSpec((None, D_MODEL, D_EXPERT), lambda t, nv, te, de: (te[t], 0, 0)),
            pl.BlockSpec((None, D_EXPERT, D_MODEL), lambda t, nv, te, de: (te[t], 0, 0)),
            pl.BlockSpec(memory_space=pl.ANY),
        ],
        out_specs=pl.BlockSpec(memory_space=pl.ANY),
        scratch_shapes=[
            pltpu.SMEM((nt * tm,), jnp.int32),
            pltpu.VMEM((2 * tm * ROW_SLAB, LANES), f32),
            pltpu.VMEM((2 * tm * ROW_SLAB, LANES), f32),
            pltpu.SemaphoreType.DMA((2,)),
            pltpu.SemaphoreType.DMA((2,)),
        ],
    )
    return pl.pallas_call(
        functools.partial(_moe_body, tm=tm, nt=nt, n_tok=T),
        grid_spec=grid_spec,
        out_shape=jax.ShapeDtypeStruct((2 * T, ROW_SLAB, LANES), f32),
        compiler_params=pltpu.CompilerParams(
            dimension_semantics=("arbitrary",),
            vmem_limit_bytes=VMEM_LIMIT),
        name="moe",
    )(nvalid, tile_e, dest, wg, wu, wd, h2r)


def _combine_body(x1_ref, w_ref, ya_ref, yb_ref, o_ref):
    tm = x1_ref.shape[0]
    w1 = w_ref[:, 0:1]
    w2 = w_ref[:, 1:2]
    for c in range(ROW_SLAB):
        cols = slice(c * LANES, (c + 1) * LANES)
        rows = pl.ds(c, tm, stride=ROW_SLAB)
        o_ref[:, cols] = x1_ref[:, cols] + (w1 * ya_ref[rows, :] + w2 * yb_ref[rows, :])


def _combine(x1, wcol, ytok2d, tm=256):
    T = x1.shape[0]
    nblk = T // tm
    return pl.pallas_call(
        _combine_body,
        grid=(nblk,),
        in_specs=[
            pl.BlockSpec((tm, D_MODEL), lambda i: (i, 0)),
            pl.BlockSpec((tm, LANES), lambda i: (i, 0)),
            pl.BlockSpec((tm * ROW_SLAB, LANES), lambda i: (i, 0)),
            pl.BlockSpec((tm * ROW_SLAB, LANES), lambda i: (nblk + i, 0)),
        ],
        out_specs=pl.BlockSpec((tm, D_MODEL), lambda i: (i, 0)),
        out_shape=jax.ShapeDtypeStruct((T, D_MODEL), f32),
        compiler_params=pltpu.CompilerParams(
            dimension_semantics=("arbitrary",),
            vmem_limit_bytes=VMEM_LIMIT),
        name="combine",
    )(x1, wcol, ytok2d, ytok2d)


def _alibi_slopes(n):
    return jnp.exp2(-8.0 * jnp.arange(1, n + 1, dtype=f32) / n)


def kernel(x, g_mix, w_in, q_norm_swa, k_norm_swa, sinks, q_norm_moba, k_norm_moba,
           w_up_swa, w_up_moba, w_out, g_ffn, w_router_group, w_router_expert,
           w_gate_e, w_up_e, w_down_e):
    B, S, D = x.shape
    assert D == D_MODEL and S % MOBA_BLOCK == 0 and S % SWA_BLOCK == 0
    T = B * S
    x2 = x.reshape(T, D)

    n_qkv = IN_COLS - 2 * D_MODEL
    w_in_p = jnp.concatenate([w_in[:, n_qkv:], w_in[:, :n_qkv]], axis=1).astype(bf16)
    proj = _inproj(x2, g_mix, w_in_p)

    oa = _swa(proj, sinks.astype(f32), _alibi_slopes(SWA_Q_HEADS), q_norm_swa, k_norm_swa, B, S)
    ob = _moba(proj, _alibi_slopes(MOBA_HEADS), q_norm_moba, k_norm_moba, B, S)

    wr = jnp.concatenate(
        [w_router_group.T,
         w_router_expert.transpose(0, 2, 1).reshape(N_EXPERTS, D),
         jnp.zeros((ROUTER_ROWS - N_GROUPS - N_EXPERTS, D), f32)], axis=0)
    wr_hi = wr.astype(bf16)
    wr_lo = (wr - wr_hi.astype(f32)).astype(bf16)
    x1, h2s, eid, wts = _mix(oa, ob, proj, x2, w_up_swa.astype(bf16), w_up_moba.astype(bf16),
                             w_out.astype(bf16), g_ffn, jnp.stack([wr_hi, wr_lo]))

    tables = _route_tables(eid, T, MOE_TILE)
    ytok = _moe(h2s.reshape(T, ROW_SLAB, LANES), tables,
                w_gate_e.astype(bf16), w_up_e.astype(bf16), w_down_e.astype(bf16), T, MOE_TILE)
    y = _combine(x1, wts, ytok.reshape(-1, LANES))
    return y.reshape(B, S, D)
```

```python
import functools

import jax
import jax.numpy as jnp
from jax import lax
from jax.experimental import pallas as pl
from jax.experimental.pallas import tpu as pltpu

D_MODEL = 2048
HEAD_DIM = 64
ATTN_SCALE = HEAD_DIM ** -0.5
SWA_Q_HEADS = 16
SWA_KV_HEADS = 2
SWA_WINDOW = 128
SWA_BLOCK = 128
MOBA_HEADS = 16
MOBA_BLOCK = 256
MOBA_TOPK = 3
N_GROUPS = 4
EXPERTS_PER_GROUP = 4
N_EXPERTS = N_GROUPS * EXPERTS_PER_GROUP
D_EXPERT = 512
EPS = 1e-6

SWA_Q_DIM = SWA_Q_HEADS * HEAD_DIM
SWA_KV_DIM = SWA_KV_HEADS * HEAD_DIM
MOBA_DIM = MOBA_HEADS * HEAD_DIM
IN_COLS = SWA_Q_DIM + 2 * SWA_KV_DIM + 3 * MOBA_DIM + 2 * D_MODEL

LANES = 128
ROW_SLAB = D_MODEL // LANES
SLAB_IN = ROW_SLAB + 8
MOE_TILE = 512
VMEM_LIMIT = 56 * 1024 * 1024
NEG = -1e30

COL_GATE_A = 0
COL_GATE_B = D_MODEL // LANES
COL_QA = 2 * D_MODEL // LANES
COL_KA = COL_QA + SWA_Q_DIM // LANES
COL_VA = COL_KA + 1
COL_QB = COL_VA + 1
COL_KB = COL_QB + MOBA_DIM // LANES
COL_VB = COL_KB + MOBA_DIM // LANES

bf16 = jnp.bfloat16
f32 = jnp.float32


def _sigmoid(x):
    return 1.0 / (1.0 + jnp.exp(-x))


def _pair_rms(x, gain):
    lane = lax.broadcasted_iota(jnp.int32, x.shape, 1)
    lo = lane < HEAD_DIM
    sq = x * x
    s0 = jnp.sum(jnp.where(lo, sq, 0.0), axis=-1, keepdims=True)
    s1 = jnp.sum(jnp.where(lo, 0.0, sq), axis=-1, keepdims=True)
    r0 = lax.rsqrt(s0 * (1.0 / HEAD_DIM) + EPS)
    r1 = lax.rsqrt(s1 * (1.0 / HEAD_DIM) + EPS)
    return x * jnp.where(lo, r0, r1) * gain


def _dot_nt(a, b):
    return lax.dot_general(a, b, (((1,), (1,)), ((), ())), preferred_element_type=f32)


def _split2(x):
    hi = x.astype(bf16)
    lo = (x - hi.astype(f32)).astype(bf16)
    return hi, lo


def _inproj_body(x_ref, g_ref, w_ref, o_ref, h_ref):
    @pl.when(pl.program_id(1) == 0)
    def _():
        x = x_ref[...]
        ms = jnp.mean(x * x, axis=-1, keepdims=True)
        h_ref[...] = (x * lax.rsqrt(ms + EPS) * g_ref[...]).astype(bf16)

    o_ref[...] = jnp.dot(h_ref[...], w_ref[...], preferred_element_type=f32)


def _inproj(x2, g, w, tm=1024, tn=768):
    T = x2.shape[0]
    N = w.shape[1]
    return pl.pallas_call(
        _inproj_body,
        grid=(T // tm, N // tn),
        in_specs=[
            pl.BlockSpec((tm, D_MODEL), lambda i, j: (i, 0)),
            pl.BlockSpec((1, D_MODEL), lambda i, j: (0, 0)),
            pl.BlockSpec((D_MODEL, tn), lambda i, j: (0, j)),
        ],
        out_specs=pl.BlockSpec((tm, tn), lambda i, j: (i, j)),
        out_shape=jax.ShapeDtypeStruct((T, N), f32),
        scratch_shapes=[pltpu.VMEM((tm, D_MODEL), bf16)],
        compiler_params=pltpu.CompilerParams(
            dimension_semantics=("arbitrary", "arbitrary"),
            vmem_limit_bytes=VMEM_LIMIT),
        name="inproj",
    )(x2, g.reshape(1, D_MODEL), w)


def _swa_body(sinks_ref, slopes_ref, q_ref, kp_ref, kc_ref, vp_ref, vc_ref,
              qn_ref, kn_ref, o_ref):
    L = SWA_BLOCK
    n = pl.program_id(1)
    k2 = jnp.concatenate([kp_ref[...], kc_ref[...]], axis=0)
    v2 = jnp.concatenate([vp_ref[...], vc_ref[...]], axis=0).astype(bf16)
    k2n = _pair_rms(k2, kn_ref[...]).astype(bf16)

    row = lax.broadcasted_iota(jnp.int32, (L, 2 * L), 0)
    col = lax.broadcasted_iota(jnp.int32, (L, 2 * L), 1)
    dist = row + L - col
    ok = (dist >= 0) & (dist < SWA_WINDOW) & ((n > 0) | (col >= L))
    distf = dist.astype(f32)
    lane = lax.broadcasted_iota(jnp.int32, (L, LANES), 1)
    lane_lo = lane < HEAD_DIM

    heads_per_kv = SWA_Q_HEADS // SWA_KV_HEADS
    for pp in range(SWA_Q_HEADS // 2):
        g = (2 * pp) // heads_per_kv
        qn = _pair_rms(q_ref[:, pp * LANES:(pp + 1) * LANES], qn_ref[...]) * ATTN_SCALE
        qn_sw = pltpu.roll(qn, HEAD_DIM, axis=1)
        in_g = lane_lo if g == 0 else jnp.logical_not(lane_lo)
        o_pair = None
        for hh in range(2):
            h = 2 * pp + hh
            src = qn if hh == g else qn_sw
            qm = jnp.where(in_g, src, 0.0).astype(bf16)
            s = _dot_nt(qm, k2n) - slopes_ref[h] * distf
            s = jnp.where(ok, s, -jnp.inf)
            sink = sinks_ref[h]
            m = jnp.maximum(jnp.max(s, axis=-1, keepdims=True), sink)
            p = jnp.exp(s - m)
            denom = jnp.sum(p, axis=-1, keepdims=True) + jnp.exp(sink - m)
            o = jnp.dot(p.astype(bf16), v2, preferred_element_type=f32) / denom
            o = o if hh == g else pltpu.roll(o, HEAD_DIM, axis=1)
            keep = lane_lo if hh == 0 else jnp.logical_not(lane_lo)
            o_pair = jnp.where(keep, o, 0.0) if o_pair is None else jnp.where(keep, o, o_pair)
        o_ref[:, pp * LANES:(pp + 1) * LANES] = o_pair.astype(bf16)


def _swa(proj, sinks, slopes, q_norm, k_norm, B, S):
    L = SWA_BLOCK
    nb = S // L
    T = B * S
    smem = pl.BlockSpec(memory_space=pltpu.SMEM)

    def prev(b, n):
        return b * nb + jnp.maximum(n - 1, 0)

    return pl.pallas_call(
        _swa_body,
        grid=(B, nb),
        in_specs=[
            smem, smem,
            pl.BlockSpec((L, SWA_Q_DIM), lambda b, n: (b * nb + n, COL_QA // (SWA_Q_DIM // LANES))),
            pl.BlockSpec((L, LANES), lambda b, n: (prev(b, n), COL_KA)),
            pl.BlockSpec((L, LANES), lambda b, n: (b * nb + n, COL_KA)),
            pl.BlockSpec((L, LANES), lambda b, n: (prev(b, n), COL_VA)),
            pl.BlockSpec((L, LANES), lambda b, n: (b * nb + n, COL_VA)),
            pl.BlockSpec((1, LANES), lambda b, n: (0, 0)),
            pl.BlockSpec((1, LANES), lambda b, n: (0, 0)),
        ],
        out_specs=pl.BlockSpec((L, SWA_Q_DIM), lambda b, n: (b * nb + n, 0)),
        out_shape=jax.ShapeDtypeStruct((T, SWA_Q_DIM), bf16),
        compiler_params=pltpu.CompilerParams(
            dimension_semantics=("arbitrary", "arbitrary"),
            vmem_limit_bytes=VMEM_LIMIT),
        name="swa",
    )(sinks, slopes, proj, proj, proj, proj, proj,
      jnp.tile(q_norm, 2).reshape(1, LANES), jnp.tile(k_norm, 2).reshape(1, LANES))


N_BIAS_PARTS = 3
NB_ROWS = 8


def _moba_body(slopes_ref, q_ref, k_ref, v_ref, qn_ref, kn_ref, o_ref, kaug_ref, vt_ref, *, nb):
    L = MOBA_BLOCK
    S = nb * L
    p = pl.program_id(1)

    kn = _pair_rms(k_ref[...], kn_ref[...])
    lane = lax.broadcasted_iota(jnp.int32, (S, LANES), 1)
    kblk = lax.broadcasted_iota(jnp.int32, (S, LANES), 0) // L
    lane_lo = lane < HEAD_DIM
    c0 = lane - HEAD_DIM
    hot0 = ((c0 >= 0) & (c0 < NB_ROWS * N_BIAS_PARTS) & ((c0 % NB_ROWS) == kblk)).astype(f32)
    hot1 = ((lane < NB_ROWS * N_BIAS_PARTS) & ((lane % NB_ROWS) == kblk)).astype(f32)
    kaug_ref[0] = jnp.where(lane_lo, kn, hot0).astype(bf16)
    kaug_ref[1] = jnp.where(lane_lo, hot1, kn).astype(bf16)
    vt_ref[...] = v_ref[...].T.astype(bf16)
    means = [jnp.mean(kn[n * L:(n + 1) * L, :], axis=0, keepdims=True) for n in range(nb)]
    means += [jnp.zeros((1, LANES), f32)] * (NB_ROWS - nb)
    km = jnp.concatenate(means, axis=0)
    lane8 = lax.broadcasted_iota(jnp.int32, (NB_ROWS, LANES), 1)
    km_pair = jnp.concatenate(
        [jnp.where(lane8 < HEAD_DIM, km, 0.0), jnp.where(lane8 < HEAD_DIM, 0.0, km)], axis=0)
    km_hi, km_lo = _split2(km_pair)

    qs = _pair_rms(q_ref[...], qn_ref[...]) * ATTN_SCALE
    qt = qs.T
    qt_hi, qt_lo = _split2(qt)
    gate = (jnp.dot(km_hi, qt_hi, preferred_element_type=f32)
            + (jnp.dot(km_hi, qt_lo, preferred_element_type=f32)
               + jnp.dot(km_lo, qt_hi, preferred_element_type=f32)))

    blk = lax.broadcasted_iota(jnp.int32, (NB_ROWS, S), 0)
    qblk = lax.broadcasted_iota(jnp.int32, (NB_ROWS, S), 1) // L
    past = blk < qblk
    r = lax.broadcasted_iota(jnp.int32, (L, L), 1)
    c = lax.broadcasted_iota(jnp.int32, (L, L), 0)
    d = (r - c).astype(f32)

    qa, tab_past, tab_own = [], [], []
    for hh in range(2):
        g = jnp.where(past, gate[hh * NB_ROWS:(hh + 1) * NB_ROWS, :], -jnp.inf)
        rank = jnp.zeros((NB_ROWS, S), jnp.int32)
        for m in range(nb):
            gm = g[m:m + 1, :]
            ahead = (gm > g) | ((gm == g) & (m < blk))
            rank = rank + ahead.astype(jnp.int32)
        sel = past & (rank < MOBA_TOPK)
        slope = slopes_ref[2 * p + hh]
        bias = jnp.where(sel, (-slope * L) * (qblk - blk).astype(f32), NEG)
        bias = jnp.where(blk == qblk, 0.0, bias)
        b0 = bias.astype(bf16).astype(f32)
        r1 = bias - b0
        b1 = r1.astype(bf16).astype(f32)
        b2 = (r1 - b1).astype(bf16).astype(f32)
        extra = jnp.concatenate(
            [b0, b1, b2, jnp.zeros((HEAD_DIM - NB_ROWS * N_BIAS_PARTS, S), f32)], axis=0)
        if hh == 0:
            qa.append(jnp.concatenate([qt[:HEAD_DIM], extra], axis=0).astype(bf16))
        else:
            qa.append(jnp.concatenate([extra, qt[HEAD_DIM:]], axis=0).astype(bf16))
        t = -slope * d
        tab_past.append(t)
        tab_own.append(jnp.where(r >= c, t, NEG))

    for i in range(nb):
        outs = []
        for hh in range(2):
            nk = (i + 1) * L
            s = jnp.dot(kaug_ref[hh, 0:nk, :], qa[hh][:, i * L:(i + 1) * L],
                        preferred_element_type=f32)
            tiles = [s[n * L:(n + 1) * L, :] + (tab_own[hh] if n == i else tab_past[hh])
                     for n in range(i + 1)]
            m = functools.reduce(jnp.maximum, [jnp.max(t, axis=0, keepdims=True) for t in tiles])
            es = [jnp.exp(t - m) for t in tiles]
            l = functools.reduce(lambda a, b: a + b, [jnp.sum(e, axis=0, keepdims=True) for e in es])
            e_all = jnp.concatenate([e.astype(bf16) for e in es], axis=0)
            acc = jnp.dot(vt_ref[hh * HEAD_DIM:(hh + 1) * HEAD_DIM, 0:nk], e_all,
                          preferred_element_type=f32)
            outs.append(acc / l)
        o_ref[i * L:(i + 1) * L, :] = jnp.concatenate(outs, axis=0).T.astype(bf16)


def _moba(proj, slopes, q_norm, k_norm, B, S):
    L = MOBA_BLOCK
    nb = S // L
    assert nb <= NB_ROWS
    T = B * S
    n_pairs = MOBA_HEADS // 2
    return pl.pallas_call(
        functools.partial(_moba_body, nb=nb),
        grid=(B, n_pairs),
        in_specs=[
            pl.BlockSpec(memory_space=pltpu.SMEM),
            pl.BlockSpec((S, LANES), lambda b, p: (b, COL_QB + p)),
            pl.BlockSpec((S, LANES), lambda b, p: (b, COL_KB + p)),
            pl.BlockSpec((S, LANES), lambda b, p: (b, COL_VB + p)),
            pl.BlockSpec((1, LANES), lambda b, p: (0, 0)),
            pl.BlockSpec((1, LANES), lambda b, p: (0, 0)),
        ],
        out_specs=pl.BlockSpec((S, LANES), lambda b, p: (b, p)),
        out_shape=jax.ShapeDtypeStruct((T, MOBA_DIM), bf16),
        scratch_shapes=[
            pltpu.VMEM((2, S, LANES), bf16),
            pltpu.VMEM((LANES, S), bf16),
        ],
        compiler_params=pltpu.CompilerParams(
            dimension_semantics=("arbitrary", "arbitrary"),
            vmem_limit_bytes=VMEM_LIMIT),
        name="moba",
    )(slopes, proj, proj, proj,
      jnp.tile(q_norm, 2).reshape(1, LANES), jnp.tile(k_norm, 2).reshape(1, LANES))


ROUTER_ROWS = 32


def _mix_body(oa_ref, ob_ref, ga_ref, gb_ref, x_ref, wua_ref, wub_ref, wout_ref,
              gffn_ref, wr_ref, x1_ref, h2_ref, gid_ref):
    ya = jnp.dot(oa_ref[...], wua_ref[...], preferred_element_type=f32)
    yb = jnp.dot(ob_ref[...], wub_ref[...], preferred_element_type=f32)
    merged = _sigmoid(ga_ref[...]) * ya + _sigmoid(gb_ref[...]) * yb
    x1 = x_ref[...] + jnp.dot(merged.astype(bf16), wout_ref[...], preferred_element_type=f32)
    x1_ref[...] = x1
    ms = jnp.mean(x1 * x1, axis=-1, keepdims=True)
    h2 = x1 * lax.rsqrt(ms + EPS) * gffn_ref[...]
    tm = h2.shape[0]
    for c in range(ROW_SLAB):
        h2_ref[pl.ds(c, tm, stride=SLAB_IN), :] = h2[:, c * LANES:(c + 1) * LANES]

    h_hi, h_lo = _split2(h2)
    lt = _dot_nt(wr_ref[0], h_hi) + (_dot_nt(wr_ref[0], h_lo) + _dot_nt(wr_ref[1], h_hi))
    gl = [lt[g:g + 1, :] for g in range(N_GROUPS)]
    gmax = functools.reduce(jnp.maximum, gl)
    gsum = functools.reduce(lambda a, b: a + b, [jnp.exp(v - gmax) for v in gl])
    g_p = 1.0 / gsum
    g_i = jnp.full((1, tm), N_GROUPS - 1, jnp.int32)
    for g in reversed(range(N_GROUPS)):
        g_i = jnp.where(gl[g] == gmax, g, g_i)

    el = []
    for e in range(EXPERTS_PER_GROUP):
        v = jnp.zeros((1, tm), f32)
        for g in range(N_GROUPS):
            r = N_GROUPS + g * EXPERTS_PER_GROUP + e
            v = jnp.where(g_i == g, lt[r:r + 1, :], v)
        el.append(v)
    emax = functools.reduce(jnp.maximum, el)
    ex = [jnp.exp(v - emax) for v in el]
    esum = functools.reduce(lambda a, b: a + b, ex)
    ep = [v / esum for v in ex]
    p1 = functools.reduce(jnp.maximum, ep)
    i1 = jnp.full((1, tm), EXPERTS_PER_GROUP - 1, jnp.int32)
    for e in reversed(range(EXPERTS_PER_GROUP)):
        i1 = jnp.where(ep[e] == p1, e, i1)
    rest = [jnp.where(i1 == e, -1.0, ep[e]) for e in range(EXPERTS_PER_GROUP)]
    p2 = functools.reduce(jnp.maximum, rest)
    i2 = jnp.full((1, tm), EXPERTS_PER_GROUP - 1, jnp.int32)
    for e in reversed(range(EXPERTS_PER_GROUP)):
        i2 = jnp.where(rest[e] == p2, e, i2)
    w1 = g_p * (p1 / (p1 + p2))
    w2 = g_p * (p2 / (p1 + p2))
    gid_ref[...] = jnp.concatenate([g_i, jnp.zeros((7, tm), jnp.int32)], axis=0)
    cw = [jnp.where(i1 == e, w1, 0.0) + jnp.where(i2 == e, w2, 0.0) for e in range(EXPERTS_PER_GROUP)]
    cw_t = jnp.concatenate(cw + [jnp.zeros((LANES - EXPERTS_PER_GROUP, tm), f32)], axis=0).T
    for e in range(SLAB_IN - ROW_SLAB):
        if e < EXPERTS_PER_GROUP:
            row = jnp.broadcast_to(cw_t[:, e:e + 1], (tm, LANES))
        else:
            row = jnp.zeros((tm, LANES), f32)
        h2_ref[pl.ds(ROW_SLAB + e, tm, stride=SLAB_IN), :] = row


def _mix(oa, ob, proj, x2, wua, wub, wout, g_ffn, wr, tm=256):
    T = x2.shape[0]
    const = lambda i: (0, 0)
    single = pl.Buffered(1)
    return pl.pallas_call(
        _mix_body,
        grid=(T // tm,),
        in_specs=[
            pl.BlockSpec((tm, SWA_Q_DIM), lambda i: (i, 0)),
            pl.BlockSpec((tm, MOBA_DIM), lambda i: (i, 0)),
            pl.BlockSpec((tm, D_MODEL), lambda i: (i, 0)),
            pl.BlockSpec((tm, D_MODEL), lambda i: (i, 1)),
            pl.BlockSpec((tm, D_MODEL), lambda i: (i, 0)),
            pl.BlockSpec((SWA_Q_DIM, D_MODEL), const, pipeline_mode=single),
            pl.BlockSpec((MOBA_DIM, D_MODEL), const, pipeline_mode=single),
            pl.BlockSpec((D_MODEL, D_MODEL), const, pipeline_mode=single),
            pl.BlockSpec((1, D_MODEL), const),
            pl.BlockSpec((2, ROUTER_ROWS, D_MODEL), lambda i: (0, 0, 0)),
        ],
        out_specs=[
            pl.BlockSpec((tm, D_MODEL), lambda i: (i, 0)),
            pl.BlockSpec((tm * SLAB_IN, LANES), lambda i: (i, 0)),
            pl.BlockSpec((8, tm), lambda i: (0, i)),
        ],
        out_shape=[
            jax.ShapeDtypeStruct((T, D_MODEL), f32),
            jax.ShapeDtypeStruct((T * SLAB_IN, LANES), f32),
            jax.ShapeDtypeStruct((8, T), jnp.int32),
        ],
        compiler_params=pltpu.CompilerParams(
            dimension_semantics=("arbitrary",),
            vmem_limit_bytes=VMEM_LIMIT),
        name="mix",
    )(oa, ob, proj, proj, x2, wua, wub, wout, g_ffn.reshape(1, D_MODEL), wr)


def _route_tables(gid, T, tm):
    nt = T // tm + N_GROUPS
    g = gid[0]
    onehot = (g[:, None] == jnp.arange(N_GROUPS, dtype=jnp.int32)[None, :]).astype(jnp.int32)
    incl = jnp.cumsum(onehot, axis=0)
    pos = jnp.sum(onehot * incl, axis=1) - 1
    counts = incl[-1]
    padded = (counts + tm - 1) // tm * tm
    seg_end = jnp.cumsum(padded)
    dest = jnp.sum(onehot * (seg_end - padded)[None, :], axis=1) + pos
    tile_start = jnp.arange(nt, dtype=jnp.int32) * tm
    tile_g = jnp.sum((tile_start[:, None] >= seg_end[None, :]).astype(jnp.int32), axis=1)
    tile_g = jnp.minimum(tile_g, N_GROUPS - 1)
    nvalid = (seg_end[-1] // tm).astype(jnp.int32).reshape(1)
    return nvalid, tile_g.astype(jnp.int32), dest.astype(jnp.int32)


def _moe_body(nvalid_ref, tg_ref, dest_ref, wg_ref, wu_ref, wd_ref, h2s_hbm, ytok_hbm,
              asg, gbuf, ybuf, xbuf, yacc, gsem, ssem, *, tm, nt, n_tok):
    t = pl.program_id(0)
    e = pl.program_id(1)
    last_e = EXPERTS_PER_GROUP - 1
    nvalid = nvalid_ref[0]
    slot = lax.rem(t, 2)
    other = 1 - slot

    def for_rows(fn):
        def body(r, carry):
            fn(r)
            return carry
        lax.fori_loop(0, tm, body, 0, unroll=8)

    def gather_copy(tok, s, r):
        return pltpu.make_async_copy(
            h2s_hbm.at[pl.ds(pl.multiple_of(tok * SLAB_IN, 8), SLAB_IN), :],
            gbuf.at[pl.ds(pl.multiple_of((s * tm + r) * SLAB_IN, 8), SLAB_IN), :], gsem.at[s])

    def scatter_copy(row, s, r):
        return pltpu.make_async_copy(
            ybuf.at[pl.ds(pl.multiple_of((s * tm + r) * ROW_SLAB, 8), ROW_SLAB), :],
            ytok_hbm.at[pl.ds(pl.multiple_of(row * ROW_SLAB, 8), ROW_SLAB), :], ssem.at[s])

    def start_gathers(tile, s):
        for_rows(lambda r: gather_copy(jnp.maximum(asg[tile * tm + r], 0), s, r).start())

    def start_scatters(tile, s):
        def one(r):
            a = asg[tile * tm + r]
            scatter_copy(jnp.where(a >= 0, a, n_tok - 1 - a), s, r).start()
        for_rows(one)

    def wait_gathers(s):
        for_rows(lambda r: gather_copy(0, s, r).wait())

    def wait_scatters(s):
        for_rows(lambda r: scatter_copy(0, s, r).wait())

    @pl.when((t == 0) & (e == 0))
    def _():
        def clear(p, carry):
            asg[p] = -1
            return carry
        lax.fori_loop(0, nt * tm, clear, 0, unroll=16)

        def place(tok, carry):
            asg[dest_ref[tok]] = tok
            return carry
        lax.fori_loop(0, n_tok, place, 0, unroll=8)

        def number_pads(p, k):
            a = asg[p]
            asg[p] = jnp.where(a < 0, -1 - k, a)
            return k + (a < 0).astype(jnp.int32)
        lax.fori_loop(0, nt * tm, number_pads, jnp.int32(0), unroll=8)
        start_gathers(0, 0)

    @pl.when((e == 0) & (t < nvalid))
    def _():
        wait_gathers(slot)
        base = slot * tm * SLAB_IN
        xbuf[...] = jnp.concatenate(
            [gbuf[pl.ds(base + c, tm, stride=SLAB_IN), :] for c in range(ROW_SLAB)],
            axis=1).astype(bf16)

    @pl.when((e == 0) & (t + 1 < nvalid))
    def _():
        start_gathers(t + 1, other)

    @pl.when(t < nvalid)
    def _():
        x = xbuf[...]
        gte = jnp.dot(x, wg_ref[...], preferred_element_type=f32)
        up = jnp.dot(x, wu_ref[...], preferred_element_type=f32)
        cw = gbuf[pl.ds(slot * tm * SLAB_IN + ROW_SLAB + e, tm, stride=SLAB_IN), :]
        hid = (gte * _sigmoid(gte)) * up * jnp.concatenate([cw] * (D_EXPERT // LANES), axis=1)
        y = jnp.dot(hid.astype(bf16), wd_ref[...], preferred_element_type=f32)

        @pl.when(e == 0)
        def _():
            yacc[...] = y

        @pl.when(e > 0)
        def _():
            yacc[...] += y

    @pl.when((e == last_e) & (t < nvalid))
    def _():
        @pl.when(t >= 2)
        def _():
            wait_scatters(slot)
        base = slot * tm * ROW_SLAB
        for c in range(ROW_SLAB):
            ybuf[pl.ds(base + c, tm, stride=ROW_SLAB), :] = yacc[:, c * LANES:(c + 1) * LANES]
        start_scatters(t, slot)

        @pl.when(t == nvalid - 1)
        def _():
            @pl.when(t >= 1)
            def _():
                wait_scatters(other)
            wait_scatters(slot)

    @pl.when((e == last_e) & (t == nt - 1))
    def _():
        ybuf[0:tm * ROW_SLAB, :] = jnp.zeros((tm * ROW_SLAB, LANES), f32)
        for j in range(n_tok // tm, nt):
            @pl.when(j >= nvalid)
            def _():
                fill = pltpu.make_async_copy(
                    ybuf.at[0:tm * ROW_SLAB, :],
                    ytok_hbm.at[j * tm * ROW_SLAB:(j + 1) * tm * ROW_SLAB, :], ssem.at[0])
                fill.start()
                fill.wait()


def _moe(h2s, tables, wg, wu, wd, T, tm):
    nvalid, tile_g, dest = tables
    nt = tile_g.shape[0]

    def w_index(t, e, nv, tg, de):
        return (tg[t] * EXPERTS_PER_GROUP + jnp.where(t < nv[0], e, EXPERTS_PER_GROUP - 1), 0, 0)

    grid_spec = pltpu.PrefetchScalarGridSpec(
        num_scalar_prefetch=3,
        grid=(nt, EXPERTS_PER_GROUP),
        in_specs=[
            pl.BlockSpec((None, D_MODEL, D_EXPERT), w_index),
            pl.BlockSpec((None, D_MODEL, D_EXPERT), w_index),
            pl.BlockSpec((None, D_EXPERT, D_MODEL), w_index),
            pl.BlockSpec(memory_space=pl.ANY),
        ],
        out_specs=pl.BlockSpec(memory_space=pl.ANY),
        scratch_shapes=[
            pltpu.SMEM((nt * tm,), jnp.int32),
            pltpu.VMEM((2 * tm * SLAB_IN, LANES), f32),
            pltpu.VMEM((2 * tm * ROW_SLAB, LANES), f32),
            pltpu.VMEM((tm, D_MODEL), bf16),
            pltpu.VMEM((tm, D_MODEL), f32),
            pltpu.SemaphoreType.DMA((2,)),
            pltpu.SemaphoreType.DMA((2,)),
        ],
    )
    return pl.pallas_call(
        functools.partial(_moe_body, tm=tm, nt=nt, n_tok=T),
        grid_spec=grid_spec,
        out_shape=jax.ShapeDtypeStruct((nt * tm * ROW_SLAB, LANES), f32),
        compiler_params=pltpu.CompilerParams(
            dimension_semantics=("arbitrary", "arbitrary"),
            vmem_limit_bytes=VMEM_LIMIT),
        name="moe",
    )(nvalid, tile_g, dest, wg, wu, wd, h2s)


def _combine_body(x1_ref, y_ref, o_ref):
    tm = x1_ref.shape[0]
    for c in range(ROW_SLAB):
        cols = slice(c * LANES, (c + 1) * LANES)
        o_ref[:, cols] = x1_ref[:, cols] + y_ref[pl.ds(c, tm, stride=ROW_SLAB), :]


def _combine(x1, ytok, tm=256):
    T = x1.shape[0]
    return pl.pallas_call(
        _combine_body,
        grid=(T // tm,),
        in_specs=[
            pl.BlockSpec((tm, D_MODEL), lambda i: (i, 0)),
            pl.BlockSpec((tm * ROW_SLAB, LANES), lambda i: (i, 0)),
        ],
        out_specs=pl.BlockSpec((tm, D_MODEL), lambda i: (i, 0)),
        out_shape=jax.ShapeDtypeStruct((T, D_MODEL), f32),
        compiler_params=pltpu.CompilerParams(
            dimension_semantics=("arbitrary",),
            vmem_limit_bytes=VMEM_LIMIT),
        name="combine",
    )(x1, ytok)


def _alibi_slopes(n):
    return jnp.exp2(-8.0 * jnp.arange(1, n + 1, dtype=f32) / n)


def kernel(x, g_mix, w_in, q_norm_swa, k_norm_swa, sinks, q_norm_moba, k_norm_moba,
           w_up_swa, w_up_moba, w_out, g_ffn, w_router_group, w_router_expert,
           w_gate_e, w_up_e, w_down_e):
    B, S, D = x.shape
    assert D == D_MODEL and S % MOBA_BLOCK == 0 and S % SWA_BLOCK == 0
    T = B * S
    x2 = x.reshape(T, D)

    n_qkv = IN_COLS - 2 * D_MODEL
    w_in_p = jnp.concatenate([w_in[:, n_qkv:], w_in[:, :n_qkv]], axis=1).astype(bf16)
    proj = _inproj(x2, g_mix, w_in_p)

    oa = _swa(proj, sinks.astype(f32), _alibi_slopes(SWA_Q_HEADS), q_norm_swa, k_norm_swa, B, S)
    ob = _moba(proj, _alibi_slopes(MOBA_HEADS), q_norm_moba, k_norm_moba, B, S)

    wr = jnp.concatenate(
        [w_router_group.T,
         w_router_expert.transpose(0, 2, 1).reshape(N_EXPERTS, D),
         jnp.zeros((ROUTER_ROWS - N_GROUPS - N_EXPERTS, D), f32)], axis=0)
    wr_hi = wr.astype(bf16)
    wr_lo = (wr - wr_hi.astype(f32)).astype(bf16)
    x1, h2s, gid = _mix(oa, ob, proj, x2, w_up_swa.astype(bf16), w_up_moba.astype(bf16),
                        w_out.astype(bf16), g_ffn, jnp.stack([wr_hi, wr_lo]))

    tables = _route_tables(gid, T, MOE_TILE)
    ytok = _moe(h2s, tables, w_gate_e.astype(bf16), w_up_e.astype(bf16), w_down_e.astype(bf16),
                T, MOE_TILE)
    y = _combine(x1, ytok)
    return y.reshape(B, S, D)
```

```python
import functools

import jax
import jax.numpy as jnp
from jax import lax
from jax.experimental import pallas as pl
from jax.experimental.pallas import tpu as pltpu

D_MODEL = 2048
HEAD_DIM = 64
ATTN_SCALE = HEAD_DIM ** -0.5
SWA_Q_HEADS = 16
SWA_KV_HEADS = 2
SWA_WINDOW = 128
SWA_BLOCK = 128
MOBA_HEADS = 16
MOBA_BLOCK = 256
MOBA_TOPK = 3
N_GROUPS = 4
EXPERTS_PER_GROUP = 4
N_EXPERTS = N_GROUPS * EXPERTS_PER_GROUP
D_EXPERT = 512
EPS = 1e-6

SWA_Q_DIM = SWA_Q_HEADS * HEAD_DIM
SWA_KV_DIM = SWA_KV_HEADS * HEAD_DIM
MOBA_DIM = MOBA_HEADS * HEAD_DIM
IN_COLS = SWA_Q_DIM + 2 * SWA_KV_DIM + 3 * MOBA_DIM + 2 * D_MODEL

LANES = 128
ROW_SLAB = D_MODEL // LANES
SLAB_IN = ROW_SLAB + 8
MOE_TILE = 512
VMEM_LIMIT = 56 * 1024 * 1024
NEG = -1e30

COL_QA = 0
COL_KA = COL_QA + SWA_Q_DIM // LANES
COL_VA = COL_KA + 1
COL_QB = COL_VA + 1
COL_KB = COL_QB + MOBA_DIM // LANES
COL_VB = COL_KB + MOBA_DIM // LANES
COL_GATE_A = COL_VB + MOBA_DIM // LANES
COL_GATE_B = COL_GATE_A + D_MODEL // LANES
GATE_BLOCK = 2 * LANES
assert (COL_GATE_A * LANES) % GATE_BLOCK == 0 and (COL_GATE_B * LANES) % GATE_BLOCK == 0

bf16 = jnp.bfloat16
f32 = jnp.float32


def _sigmoid(x):
    return 1.0 / (1.0 + jnp.exp(-x))


def _pair_rms(x, gain):
    lane = lax.broadcasted_iota(jnp.int32, x.shape, 1)
    lo = lane < HEAD_DIM
    sq = x * x
    s0 = jnp.sum(jnp.where(lo, sq, 0.0), axis=-1, keepdims=True)
    s1 = jnp.sum(jnp.where(lo, 0.0, sq), axis=-1, keepdims=True)
    r0 = lax.rsqrt(s0 * (1.0 / HEAD_DIM) + EPS)
    r1 = lax.rsqrt(s1 * (1.0 / HEAD_DIM) + EPS)
    return x * jnp.where(lo, r0, r1) * gain


def _dot_nt(a, b):
    return lax.dot_general(a, b, (((1,), (1,)), ((), ())), preferred_element_type=f32)


def _split2(x):
    hi = x.astype(bf16)
    lo = (x - hi.astype(f32)).astype(bf16)
    return hi, lo


def _inproj_body(x_ref, g_ref, w_ref, o_ref, h_ref):
    @pl.when(pl.program_id(1) == 0)
    def _():
        x = x_ref[...]
        ms = jnp.mean(x * x, axis=-1, keepdims=True)
        h_ref[...] = (x * lax.rsqrt(ms + EPS) * g_ref[...]).astype(bf16)

    o_ref[...] = jnp.dot(h_ref[...], w_ref[...].astype(bf16), preferred_element_type=f32)


def _inproj(x2, g, w, tm=1024, tn=768):
    T = x2.shape[0]
    N = w.shape[1]
    return pl.pallas_call(
        _inproj_body,
        grid=(T // tm, N // tn),
        in_specs=[
            pl.BlockSpec((tm, D_MODEL), lambda i, j: (i, 0)),
            pl.BlockSpec((1, D_MODEL), lambda i, j: (0, 0)),
            pl.BlockSpec((D_MODEL, tn), lambda i, j: (0, j)),
        ],
        out_specs=pl.BlockSpec((tm, tn), lambda i, j: (i, j)),
        out_shape=jax.ShapeDtypeStruct((T, N), f32),
        scratch_shapes=[pltpu.VMEM((tm, D_MODEL), bf16)],
        compiler_params=pltpu.CompilerParams(
            dimension_semantics=("arbitrary", "arbitrary"),
            vmem_limit_bytes=VMEM_LIMIT),
        name="inproj",
    )(x2, g.reshape(1, D_MODEL), w)


def _swa_body(sinks_ref, slopes_ref, q_ref, kp_ref, kc_ref, vp_ref, vc_ref,
              qn_ref, kn_ref, o_ref):
    L = SWA_BLOCK
    n = pl.program_id(1)
    k2 = jnp.concatenate([kp_ref[...], kc_ref[...]], axis=0)
    v2 = jnp.concatenate([vp_ref[...], vc_ref[...]], axis=0).astype(bf16)
    k2n = _pair_rms(k2, kn_ref[...]).astype(bf16)

    row = lax.broadcasted_iota(jnp.int32, (L, 2 * L), 0)
    col = lax.broadcasted_iota(jnp.int32, (L, 2 * L), 1)
    dist = row + L - col
    ok = (dist >= 0) & (dist < SWA_WINDOW) & ((n > 0) | (col >= L))
    distf = dist.astype(f32)
    lane = lax.broadcasted_iota(jnp.int32, (L, LANES), 1)
    lane_lo = lane < HEAD_DIM

    heads_per_kv = SWA_Q_HEADS // SWA_KV_HEADS
    for pp in range(SWA_Q_HEADS // 2):
        g = (2 * pp) // heads_per_kv
        qn = _pair_rms(q_ref[:, pp * LANES:(pp + 1) * LANES], qn_ref[...]) * ATTN_SCALE
        qn_sw = pltpu.roll(qn, HEAD_DIM, axis=1)
        in_g = lane_lo if g == 0 else jnp.logical_not(lane_lo)
        o_pair = None
        for hh in range(2):
            h = 2 * pp + hh
            src = qn if hh == g else qn_sw
            qm = jnp.where(in_g, src, 0.0).astype(bf16)
            s = _dot_nt(qm, k2n) - slopes_ref[h] * distf
            s = jnp.where(ok, s, -jnp.inf)
            sink = sinks_ref[h]
            m = jnp.maximum(jnp.max(s, axis=-1, keepdims=True), sink)
            p = jnp.exp(s - m)
            denom = jnp.sum(p, axis=-1, keepdims=True) + jnp.exp(sink - m)
            o = jnp.dot(p.astype(bf16), v2, preferred_element_type=f32) / denom
            o = o if hh == g else pltpu.roll(o, HEAD_DIM, axis=1)
            keep = lane_lo if hh == 0 else jnp.logical_not(lane_lo)
            o_pair = jnp.where(keep, o, 0.0) if o_pair is None else jnp.where(keep, o, o_pair)
        o_ref[:, pp * LANES:(pp + 1) * LANES] = o_pair.astype(bf16)


def _swa(proj, sinks, slopes, q_norm, k_norm, B, S):
    L = SWA_BLOCK
    nb = S // L
    T = B * S
    smem = pl.BlockSpec(memory_space=pltpu.SMEM)

    def prev(b, n):
        return b * nb + jnp.maximum(n - 1, 0)

    return pl.pallas_call(
        _swa_body,
        grid=(B, nb),
        in_specs=[
            smem, smem,
            pl.BlockSpec((L, SWA_Q_DIM), lambda b, n: (b * nb + n, COL_QA // (SWA_Q_DIM // LANES))),
            pl.BlockSpec((L, LANES), lambda b, n: (prev(b, n), COL_KA)),
            pl.BlockSpec((L, LANES), lambda b, n: (b * nb + n, COL_KA)),
            pl.BlockSpec((L, LANES), lambda b, n: (prev(b, n), COL_VA)),
            pl.BlockSpec((L, LANES), lambda b, n: (b * nb + n, COL_VA)),
            pl.BlockSpec((1, LANES), lambda b, n: (0, 0)),
            pl.BlockSpec((1, LANES), lambda b, n: (0, 0)),
        ],
        out_specs=pl.BlockSpec((L, SWA_Q_DIM), lambda b, n: (b * nb + n, 0)),
        out_shape=jax.ShapeDtypeStruct((T, SWA_Q_DIM), bf16),
        compiler_params=pltpu.CompilerParams(
            dimension_semantics=("arbitrary", "arbitrary"),
            vmem_limit_bytes=VMEM_LIMIT),
        name="swa",
    )(sinks, slopes, proj, proj, proj, proj, proj,
      jnp.tile(q_norm, 2).reshape(1, LANES), jnp.tile(k_norm, 2).reshape(1, LANES))


N_BIAS_PARTS = 3
NB_ROWS = 8


def _moba_body(slopes_ref, q_ref, k_ref, v_ref, qn_ref, kn_ref, o_ref, kaug_ref, vt_ref, *, nb):
    L = MOBA_BLOCK
    S = nb * L
    p = pl.program_id(1)

    kn = _pair_rms(k_ref[...], kn_ref[...])
    lane = lax.broadcasted_iota(jnp.int32, (S, LANES), 1)
    kblk = lax.broadcasted_iota(jnp.int32, (S, LANES), 0) // L
    lane_lo = lane < HEAD_DIM
    c0 = lane - HEAD_DIM
    hot0 = ((c0 >= 0) & (c0 < NB_ROWS * N_BIAS_PARTS) & ((c0 % NB_ROWS) == kblk)).astype(f32)
    hot1 = ((lane < NB_ROWS * N_BIAS_PARTS) & ((lane % NB_ROWS) == kblk)).astype(f32)
    kaug_ref[0] = jnp.where(lane_lo, kn, hot0).astype(bf16)
    kaug_ref[1] = jnp.where(lane_lo, hot1, kn).astype(bf16)
    vt_ref[...] = v_ref[...].T.astype(bf16)
    means = [jnp.mean(kn[n * L:(n + 1) * L, :], axis=0, keepdims=True) for n in range(nb)]
    means += [jnp.zeros((1, LANES), f32)] * (NB_ROWS - nb)
    km = jnp.concatenate(means, axis=0)
    lane8 = lax.broadcasted_iota(jnp.int32, (NB_ROWS, LANES), 1)
    km_pair = jnp.concatenate(
        [jnp.where(lane8 < HEAD_DIM, km, 0.0), jnp.where(lane8 < HEAD_DIM, 0.0, km)], axis=0)
    km_hi, km_lo = _split2(km_pair)

    qs = _pair_rms(q_ref[...], qn_ref[...]) * ATTN_SCALE
    qt = qs.T
    qt_hi, qt_lo = _split2(qt)
    gate = (jnp.dot(km_hi, qt_hi, preferred_element_type=f32)
            + (jnp.dot(km_hi, qt_lo, preferred_element_type=f32)
               + jnp.dot(km_lo, qt_hi, preferred_element_type=f32)))

    blk = lax.broadcasted_iota(jnp.int32, (NB_ROWS, S), 0)
    qblk = lax.broadcasted_iota(jnp.int32, (NB_ROWS, S), 1) // L
    past = blk < qblk
    r = lax.broadcasted_iota(jnp.int32, (L, L), 1)
    c = lax.broadcasted_iota(jnp.int32, (L, L), 0)
    d = (r - c).astype(f32)

    qa, tab_past, tab_own = [], [], []
    for hh in range(2):
        g = jnp.where(past, gate[hh * NB_ROWS:(hh + 1) * NB_ROWS, :], -jnp.inf)
        rank = jnp.zeros((NB_ROWS, S), jnp.int32)
        for m in range(nb):
            gm = g[m:m + 1, :]
            ahead = (gm > g) | ((gm == g) & (m < blk))
            rank = rank + ahead.astype(jnp.int32)
        sel = past & (rank < MOBA_TOPK)
        slope = slopes_ref[2 * p + hh]
        bias = jnp.where(sel, (-slope * L) * (qblk - blk).astype(f32), NEG)
        bias = jnp.where(blk == qblk, 0.0, bias)
        b0 = bias.astype(bf16).astype(f32)
        r1 = bias - b0
        b1 = r1.astype(bf16).astype(f32)
        b2 = (r1 - b1).astype(bf16).astype(f32)
        extra = jnp.concatenate(
            [b0, b1, b2, jnp.zeros((HEAD_DIM - NB_ROWS * N_BIAS_PARTS, S), f32)], axis=0)
        if hh == 0:
            qa.append(jnp.concatenate([qt[:HEAD_DIM], extra], axis=0).astype(bf16))
        else:
            qa.append(jnp.concatenate([extra, qt[HEAD_DIM:]], axis=0).astype(bf16))
        t = -slope * d
        tab_past.append(t)
        tab_own.append(jnp.where(r >= c, t, NEG))

    for i in range(nb):
        outs = []
        for hh in range(2):
            nk = (i + 1) * L
            s = jnp.dot(kaug_ref[hh, 0:nk, :], qa[hh][:, i * L:(i + 1) * L],
                        preferred_element_type=f32)
            tiles = [s[n * L:(n + 1) * L, :] + (tab_own[hh] if n == i else tab_past[hh])
                     for n in range(i + 1)]
            m = functools.reduce(jnp.maximum, [jnp.max(t, axis=0, keepdims=True) for t in tiles])
            es = [jnp.exp(t - m) for t in tiles]
            l = functools.reduce(lambda a, b: a + b, [jnp.sum(e, axis=0, keepdims=True) for e in es])
            e_all = jnp.concatenate([e.astype(bf16) for e in es], axis=0)
            acc = jnp.dot(vt_ref[hh * HEAD_DIM:(hh + 1) * HEAD_DIM, 0:nk], e_all,
                          preferred_element_type=f32)
            outs.append(acc / l)
        o_ref[i * L:(i + 1) * L, :] = jnp.concatenate(outs, axis=0).T.astype(bf16)


def _moba(proj, slopes, q_norm, k_norm, B, S):
    L = MOBA_BLOCK
    nb = S // L
    assert nb <= NB_ROWS
    T = B * S
    n_pairs = MOBA_HEADS // 2
    return pl.pallas_call(
        functools.partial(_moba_body, nb=nb),
        grid=(B, n_pairs),
        in_specs=[
            pl.BlockSpec(memory_space=pltpu.SMEM),
            pl.BlockSpec((S, LANES), lambda b, p: (b, COL_QB + p)),
            pl.BlockSpec((S, LANES), lambda b, p: (b, COL_KB + p)),
            pl.BlockSpec((S, LANES), lambda b, p: (b, COL_VB + p)),
            pl.BlockSpec((1, LANES), lambda b, p: (0, 0)),
            pl.BlockSpec((1, LANES), lambda b, p: (0, 0)),
        ],
        out_specs=pl.BlockSpec((S, LANES), lambda b, p: (b, p)),
        out_shape=jax.ShapeDtypeStruct((T, MOBA_DIM), bf16),
        scratch_shapes=[
            pltpu.VMEM((2, S, LANES), bf16),
            pltpu.VMEM((LANES, S), bf16),
        ],
        compiler_params=pltpu.CompilerParams(
            dimension_semantics=("arbitrary", "arbitrary"),
            vmem_limit_bytes=VMEM_LIMIT),
        name="moba",
    )(slopes, proj, proj, proj,
      jnp.tile(q_norm, 2).reshape(1, LANES), jnp.tile(k_norm, 2).reshape(1, LANES))


ROUTER_ROWS = 32


N_GATE_BLOCKS = D_MODEL // GATE_BLOCK


def _mix_body(*refs):
    ga_refs = refs[:N_GATE_BLOCKS]
    gb_refs = refs[N_GATE_BLOCKS:2 * N_GATE_BLOCKS]
    (oa_ref, ob_ref, x_ref, wua_ref, wub_ref, wout_ref, gffn_ref, wr_ref,
     x1_ref, h2_ref, gid_ref) = refs[2 * N_GATE_BLOCKS:]
    ya = jnp.dot(oa_ref[...], wua_ref[...], preferred_element_type=f32)
    yb = jnp.dot(ob_ref[...], wub_ref[...], preferred_element_type=f32)
    gate_a = jnp.concatenate([r[...] for r in ga_refs], axis=1)
    gate_b = jnp.concatenate([r[...] for r in gb_refs], axis=1)
    merged = _sigmoid(gate_a) * ya + _sigmoid(gate_b) * yb
    x1 = x_ref[...] + jnp.dot(merged.astype(bf16), wout_ref[...], preferred_element_type=f32)
    x1_ref[...] = x1
    ms = jnp.mean(x1 * x1, axis=-1, keepdims=True)
    h2 = x1 * lax.rsqrt(ms + EPS) * gffn_ref[...]
    tm = h2.shape[0]
    for c in range(ROW_SLAB):
        h2_ref[pl.ds(c, tm, stride=SLAB_IN), :] = h2[:, c * LANES:(c + 1) * LANES]

    h_hi, h_lo = _split2(h2)
    lt = _dot_nt(wr_ref[0], h_hi) + (_dot_nt(wr_ref[0], h_lo) + _dot_nt(wr_ref[1], h_hi))
    gl = [lt[g:g + 1, :] for g in range(N_GROUPS)]
    gmax = functools.reduce(jnp.maximum, gl)
    gsum = functools.reduce(lambda a, b: a + b, [jnp.exp(v - gmax) for v in gl])
    g_p = 1.0 / gsum
    g_i = jnp.full((1, tm), N_GROUPS - 1, jnp.int32)
    for g in reversed(range(N_GROUPS)):
        g_i = jnp.where(gl[g] == gmax, g, g_i)

    el = []
    for e in range(EXPERTS_PER_GROUP):
        v = jnp.zeros((1, tm), f32)
        for g in range(N_GROUPS):
            r = N_GROUPS + g * EXPERTS_PER_GROUP + e
            v = jnp.where(g_i == g, lt[r:r + 1, :], v)
        el.append(v)
    emax = functools.reduce(jnp.maximum, el)
    ex = [jnp.exp(v - emax) for v in el]
    esum = functools.reduce(lambda a, b: a + b, ex)
    ep = [v / esum for v in ex]
    p1 = functools.reduce(jnp.maximum, ep)
    i1 = jnp.full((1, tm), EXPERTS_PER_GROUP - 1, jnp.int32)
    for e in reversed(range(EXPERTS_PER_GROUP)):
        i1 = jnp.where(ep[e] == p1, e, i1)
    rest = [jnp.where(i1 == e, -1.0, ep[e]) for e in range(EXPERTS_PER_GROUP)]
    p2 = functools.reduce(jnp.maximum, rest)
    i2 = jnp.full((1, tm), EXPERTS_PER_GROUP - 1, jnp.int32)
    for e in reversed(range(EXPERTS_PER_GROUP)):
        i2 = jnp.where(rest[e] == p2, e, i2)
    w1 = g_p * (p1 / (p1 + p2))
    w2 = g_p * (p2 / (p1 + p2))
    gid_ref[...] = jnp.concatenate([g_i, jnp.zeros((7, tm), jnp.int32)], axis=0)
    cw = [jnp.where(i1 == e, w1, 0.0) + jnp.where(i2 == e, w2, 0.0) for e in range(EXPERTS_PER_GROUP)]
    cw_t = jnp.concatenate(cw + [jnp.zeros((LANES - EXPERTS_PER_GROUP, tm), f32)], axis=0).T
    for e in range(SLAB_IN - ROW_SLAB):
        if e < EXPERTS_PER_GROUP:
            row = jnp.broadcast_to(cw_t[:, e:e + 1], (tm, LANES))
        else:
            row = jnp.zeros((tm, LANES), f32)
        h2_ref[pl.ds(ROW_SLAB + e, tm, stride=SLAB_IN), :] = row


def _mix(oa, ob, proj, x2, wua, wub, wout, g_ffn, wr, tm=256):
    T = x2.shape[0]
    const = lambda i: (0, 0)
    single = pl.Buffered(1)

    def gate_specs(first_col):
        first = first_col * LANES // GATE_BLOCK
        return [pl.BlockSpec((tm, GATE_BLOCK), functools.partial(lambda i, j: (i, j), j=first + k))
                for k in range(N_GATE_BLOCKS)]

    return pl.pallas_call(
        _mix_body,
        grid=(T // tm,),
        in_specs=gate_specs(COL_GATE_A) + gate_specs(COL_GATE_B) + [
            pl.BlockSpec((tm, SWA_Q_DIM), lambda i: (i, 0)),
            pl.BlockSpec((tm, MOBA_DIM), lambda i: (i, 0)),
            pl.BlockSpec((tm, D_MODEL), lambda i: (i, 0)),
            pl.BlockSpec((SWA_Q_DIM, D_MODEL), const, pipeline_mode=single),
            pl.BlockSpec((MOBA_DIM, D_MODEL), const, pipeline_mode=single),
            pl.BlockSpec((D_MODEL, D_MODEL), const, pipeline_mode=single),
            pl.BlockSpec((1, D_MODEL), const),
            pl.BlockSpec((2, ROUTER_ROWS, D_MODEL), lambda i: (0, 0, 0)),
        ],
        out_specs=[
            pl.BlockSpec((tm, D_MODEL), lambda i: (i, 0)),
            pl.BlockSpec((tm * SLAB_IN, LANES), lambda i: (i, 0)),
            pl.BlockSpec((8, tm), lambda i: (0, i)),
        ],
        out_shape=[
            jax.ShapeDtypeStruct((T, D_MODEL), f32),
            jax.ShapeDtypeStruct((T * SLAB_IN, LANES), f32),
            jax.ShapeDtypeStruct((8, T), jnp.int32),
        ],
        compiler_params=pltpu.CompilerParams(
            dimension_semantics=("arbitrary",),
            vmem_limit_bytes=VMEM_LIMIT),
        name="mix",
    )(*([proj] * (2 * N_GATE_BLOCKS)), oa, ob, x2, wua, wub, wout, g_ffn.reshape(1, D_MODEL), wr)


def _route_tables(gid, T, tm):
    nt = T // tm + N_GROUPS
    g = gid[0]
    onehot = (g[:, None] == jnp.arange(N_GROUPS, dtype=jnp.int32)[None, :]).astype(jnp.int32)
    incl = jnp.cumsum(onehot, axis=0)
    pos = jnp.sum(onehot * incl, axis=1) - 1
    counts = incl[-1]
    padded = (counts + tm - 1) // tm * tm
    seg_end = jnp.cumsum(padded)
    seg_start = seg_end - padded
    dest = jnp.sum(onehot * seg_start[None, :], axis=1) + pos
    tile_start = jnp.arange(nt, dtype=jnp.int32) * tm
    tile_g = jnp.sum((tile_start[:, None] >= seg_end[None, :]).astype(jnp.int32), axis=1)
    tile_g = jnp.minimum(tile_g, N_GROUPS - 1)
    nvalid = (seg_end[-1] // tm).astype(jnp.int32).reshape(1)
    n_pad = padded - counts
    pads = jnp.concatenate([seg_start + counts, seg_end, jnp.cumsum(n_pad) - n_pad])
    return (nvalid, tile_g.astype(jnp.int32), dest.astype(jnp.int32), pads.astype(jnp.int32))


def _moe_body(nvalid_ref, tg_ref, dest_ref, pads_ref, wg_ref, wu_ref, wd_ref, h2s_hbm, ytok_hbm,
              asg, gbuf, ybuf, xbuf, yacc, gsem, ssem, *, tm, nt, n_tok):
    t = pl.program_id(0)
    e = pl.program_id(1)
    last_e = EXPERTS_PER_GROUP - 1
    nvalid = nvalid_ref[0]
    slot = lax.rem(t, 2)
    other = 1 - slot

    def for_rows(fn):
        def body(r, carry):
            fn(r)
            return carry
        lax.fori_loop(0, tm, body, 0, unroll=8)

    def gather_copy(tok, s, r):
        return pltpu.make_async_copy(
            h2s_hbm.at[pl.ds(pl.multiple_of(tok * SLAB_IN, 8), SLAB_IN), :],
            gbuf.at[pl.ds(pl.multiple_of((s * tm + r) * SLAB_IN, 8), SLAB_IN), :], gsem.at[s])

    def scatter_copy(row, s, r):
        return pltpu.make_async_copy(
            ybuf.at[pl.ds(pl.multiple_of((s * tm + r) * ROW_SLAB, 8), ROW_SLAB), :],
            ytok_hbm.at[pl.ds(pl.multiple_of(row * ROW_SLAB, 8), ROW_SLAB), :], ssem.at[s])

    def start_gathers(tile, s):
        for_rows(lambda r: gather_copy(jnp.maximum(asg[tile * tm + r], 0), s, r).start())

    def start_scatters(tile, s):
        def one(r):
            a = asg[tile * tm + r]
            scatter_copy(jnp.where(a >= 0, a, n_tok - 1 - a), s, r).start()
        for_rows(one)

    def wait_gathers(s):
        for_rows(lambda r: gather_copy(0, s, r).wait())

    def wait_scatters(s):
        for_rows(lambda r: scatter_copy(0, s, r).wait())

    @pl.when((t == 0) & (e == 0))
    def _():
        def place(tok, carry):
            asg[dest_ref[tok]] = tok
            return carry
        lax.fori_loop(0, n_tok, place, 0, unroll=8)

        for g in range(N_GROUPS):
            first = pads_ref[g]
            code = -1 - pads_ref[2 * N_GROUPS + g] + first

            def mark(p, carry):
                asg[p] = code - p
                return carry
            lax.fori_loop(first, pads_ref[N_GROUPS + g], mark, 0)
        start_gathers(0, 0)

    @pl.when((e == 0) & (t < nvalid))
    def _():
        wait_gathers(slot)
        base = slot * tm * SLAB_IN
        xbuf[...] = jnp.concatenate(
            [gbuf[pl.ds(base + c, tm, stride=SLAB_IN), :] for c in range(ROW_SLAB)],
            axis=1).astype(bf16)

    @pl.when((e == 0) & (t + 1 < nvalid))
    def _():
        start_gathers(t + 1, other)

    @pl.when(t < nvalid)
    def _():
        x = xbuf[...]
        gte = jnp.dot(x, wg_ref[...], preferred_element_type=f32)
        up = jnp.dot(x, wu_ref[...], preferred_element_type=f32)
        cw = gbuf[pl.ds(slot * tm * SLAB_IN + ROW_SLAB + e, tm, stride=SLAB_IN), :]
        hid = (gte * _sigmoid(gte)) * up * jnp.concatenate([cw] * (D_EXPERT // LANES), axis=1)
        y = jnp.dot(hid.astype(bf16), wd_ref[...], preferred_element_type=f32)

        @pl.when(e == 0)
        def _():
            yacc[...] = y

        @pl.when(e > 0)
        def _():
            yacc[...] += y

    @pl.when((e == last_e) & (t < nvalid))
    def _():
        @pl.when(t >= 2)
        def _():
            wait_scatters(slot)
        base = slot * tm * ROW_SLAB
        for c in range(ROW_SLAB):
            ybuf[pl.ds(base + c, tm, stride=ROW_SLAB), :] = yacc[:, c * LANES:(c + 1) * LANES]
        start_scatters(t, slot)

        @pl.when(t == nvalid - 1)
        def _():
            @pl.when(t >= 1)
            def _():
                wait_scatters(other)
            wait_scatters(slot)

    @pl.when((e == last_e) & (t == nt - 1))
    def _():
        ybuf[0:tm * ROW_SLAB, :] = jnp.zeros((tm * ROW_SLAB, LANES), f32)
        for j in range(n_tok // tm, nt):
            @pl.when(j >= nvalid)
            def _():
                fill = pltpu.make_async_copy(
                    ybuf.at[0:tm * ROW_SLAB, :],
                    ytok_hbm.at[j * tm * ROW_SLAB:(j + 1) * tm * ROW_SLAB, :], ssem.at[0])
                fill.start()
                fill.wait()


def _moe(h2s, tables, wg, wu, wd, T, tm):
    nvalid, tile_g, dest, pads = tables
    nt = tile_g.shape[0]

    def w_index(t, e, nv, tg, de, pd):
        return (tg[t] * EXPERTS_PER_GROUP + jnp.where(t < nv[0], e, EXPERTS_PER_GROUP - 1), 0, 0)

    grid_spec = pltpu.PrefetchScalarGridSpec(
        num_scalar_prefetch=4,
        grid=(nt, EXPERTS_PER_GROUP),
        in_specs=[
            pl.BlockSpec((None, D_MODEL, D_EXPERT), w_index),
            pl.BlockSpec((None, D_MODEL, D_EXPERT), w_index),
            pl.BlockSpec((None, D_EXPERT, D_MODEL), w_index),
            pl.BlockSpec(memory_space=pl.ANY),
        ],
        out_specs=pl.BlockSpec(memory_space=pl.ANY),
        scratch_shapes=[
            pltpu.SMEM((nt * tm,), jnp.int32),
            pltpu.VMEM((2 * tm * SLAB_IN, LANES), f32),
            pltpu.VMEM((2 * tm * ROW_SLAB, LANES), f32),
            pltpu.VMEM((tm, D_MODEL), bf16),
            pltpu.VMEM((tm, D_MODEL), f32),
            pltpu.SemaphoreType.DMA((2,)),
            pltpu.SemaphoreType.DMA((2,)),
        ],
    )
    return pl.pallas_call(
        functools.partial(_moe_body, tm=tm, nt=nt, n_tok=T),
        grid_spec=grid_spec,
        out_shape=jax.ShapeDtypeStruct((nt * tm * ROW_SLAB, LANES), f32),
        compiler_params=pltpu.CompilerParams(
            dimension_semantics=("arbitrary", "arbitrary"),
            vmem_limit_bytes=VMEM_LIMIT),
        name="moe",
    )(nvalid, tile_g, dest, pads, wg, wu, wd, h2s)


def _combine_body(x1_ref, y_ref, o_ref):
    tm = x1_ref.shape[0]
    for c in range(ROW_SLAB):
        cols = slice(c * LANES, (c + 1) * LANES)
        o_ref[:, cols] = x1_ref[:, cols] + y_ref[pl.ds(c, tm, stride=ROW_SLAB), :]


def _combine(x1, ytok, tm=256):
    T = x1.shape[0]
    return pl.pallas_call(
        _combine_body,
        grid=(T // tm,),
        in_specs=[
            pl.BlockSpec((tm, D_MODEL), lambda i: (i, 0)),
            pl.BlockSpec((tm * ROW_SLAB, LANES), lambda i: (i, 0)),
        ],
        out_specs=pl.BlockSpec((tm, D_MODEL), lambda i: (i, 0)),
        out_shape=jax.ShapeDtypeStruct((T, D_MODEL), f32),
        compiler_params=pltpu.CompilerParams(
            dimension_semantics=("arbitrary",),
            vmem_limit_bytes=VMEM_LIMIT),
        name="combine",
    )(x1, ytok)


def _alibi_slopes(n):
    return jnp.exp2(-8.0 * jnp.arange(1, n + 1, dtype=f32) / n)


def kernel(x, g_mix, w_in, q_norm_swa, k_norm_swa, sinks, q_norm_moba, k_norm_moba,
           w_up_swa, w_up_moba, w_out, g_ffn, w_router_group, w_router_expert,
           w_gate_e, w_up_e, w_down_e):
    B, S, D = x.shape
    assert D == D_MODEL and S % MOBA_BLOCK == 0 and S % SWA_BLOCK == 0
    T = B * S
    x2 = x.reshape(T, D)

    proj = _inproj(x2, g_mix, w_in)

    oa = _swa(proj, sinks.astype(f32), _alibi_slopes(SWA_Q_HEADS), q_norm_swa, k_norm_swa, B, S)
    ob = _moba(proj, _alibi_slopes(MOBA_HEADS), q_norm_moba, k_norm_moba, B, S)

    wr = jnp.concatenate(
        [w_router_group.T,
         w_router_expert.transpose(0, 2, 1).reshape(N_EXPERTS, D),
         jnp.zeros((ROUTER_ROWS - N_GROUPS - N_EXPERTS, D), f32)], axis=0)
    wr_hi = wr.astype(bf16)
    wr_lo = (wr - wr_hi.astype(f32)).astype(bf16)
    x1, h2s, gid = _mix(oa, ob, proj, x2, w_up_swa.astype(bf16), w_up_moba.astype(bf16),
                        w_out.astype(bf16), g_ffn, jnp.stack([wr_hi, wr_lo]))

    tables = _route_tables(gid, T, MOE_TILE)
    ytok = _moe(h2s, tables, w_gate_e.astype(bf16), w_up_e.astype(bf16), w_down_e.astype(bf16),
                T, MOE_TILE)
    y = _combine(x1, ytok)
    return y.reshape(B, S, D)
```

```python
import functools

import jax
import jax.numpy as jnp
from jax import lax
from jax.experimental import pallas as pl
from jax.experimental.pallas import tpu as pltpu

D_MODEL = 2048
HEAD_DIM = 64
ATTN_SCALE = HEAD_DIM ** -0.5
SWA_Q_HEADS = 16
SWA_KV_HEADS = 2
SWA_WINDOW = 128
SWA_BLOCK = 128
MOBA_HEADS = 16
MOBA_BLOCK = 256
MOBA_TOPK = 3
N_GROUPS = 4
EXPERTS_PER_GROUP = 4
N_EXPERTS = N_GROUPS * EXPERTS_PER_GROUP
D_EXPERT = 512
EPS = 1e-6

SWA_Q_DIM = SWA_Q_HEADS * HEAD_DIM
SWA_KV_DIM = SWA_KV_HEADS * HEAD_DIM
MOBA_DIM = MOBA_HEADS * HEAD_DIM
IN_COLS = SWA_Q_DIM + 2 * SWA_KV_DIM + 3 * MOBA_DIM + 2 * D_MODEL

LANES = 128
ROW_SLAB = D_MODEL // LANES
SLAB_IN = ROW_SLAB + 8
MOE_TILE = 512
VMEM_LIMIT = 56 * 1024 * 1024
NEG = -1e30

COL_QA = 0
COL_KA = COL_QA + SWA_Q_DIM // LANES
COL_VA = COL_KA + 1
COL_QB = COL_VA + 1
COL_KB = COL_QB + MOBA_DIM // LANES
COL_VB = COL_KB + MOBA_DIM // LANES
COL_GATE_A = COL_VB + MOBA_DIM // LANES
COL_GATE_B = COL_GATE_A + D_MODEL // LANES
GATE_BLOCK = 2 * LANES
assert (COL_GATE_A * LANES) % GATE_BLOCK == 0 and (COL_GATE_B * LANES) % GATE_BLOCK == 0

bf16 = jnp.bfloat16
f32 = jnp.float32


def _sigmoid(x):
    return 1.0 / (1.0 + jnp.exp(-x))


def _pair_rms(x, gain):
    lane = lax.broadcasted_iota(jnp.int32, x.shape, 1)
    lo = lane < HEAD_DIM
    sq = x * x
    s0 = jnp.sum(jnp.where(lo, sq, 0.0), axis=-1, keepdims=True)
    s1 = jnp.sum(jnp.where(lo, 0.0, sq), axis=-1, keepdims=True)
    r0 = lax.rsqrt(s0 * (1.0 / HEAD_DIM) + EPS)
    r1 = lax.rsqrt(s1 * (1.0 / HEAD_DIM) + EPS)
    return x * jnp.where(lo, r0, r1) * gain


def _dot_nt(a, b):
    return lax.dot_general(a, b, (((1,), (1,)), ((), ())), preferred_element_type=f32)


def _split2(x):
    hi = x.astype(bf16)
    lo = (x - hi.astype(f32)).astype(bf16)
    return hi, lo


def _inproj_body(x_ref, g_ref, w_ref, o_ref, h_ref):
    @pl.when(pl.program_id(1) == 0)
    def _():
        x = x_ref[...]
        ms = jnp.mean(x * x, axis=-1, keepdims=True)
        h_ref[...] = (x * lax.rsqrt(ms + EPS) * g_ref[...]).astype(bf16)

    o_ref[...] = jnp.dot(h_ref[...], w_ref[...].astype(bf16), preferred_element_type=f32)


def _inproj(x2, g, w, tm=1024, tn=768):
    T = x2.shape[0]
    N = w.shape[1]
    return pl.pallas_call(
        _inproj_body,
        grid=(T // tm, N // tn),
        in_specs=[
            pl.BlockSpec((tm, D_MODEL), lambda i, j: (i, 0)),
            pl.BlockSpec((1, D_MODEL), lambda i, j: (0, 0)),
            pl.BlockSpec((D_MODEL, tn), lambda i, j: (0, j)),
        ],
        out_specs=pl.BlockSpec((tm, tn), lambda i, j: (i, j)),
        out_shape=jax.ShapeDtypeStruct((T, N), f32),
        scratch_shapes=[pltpu.VMEM((tm, D_MODEL), bf16)],
        compiler_params=pltpu.CompilerParams(
            dimension_semantics=("arbitrary", "arbitrary"),
            vmem_limit_bytes=VMEM_LIMIT),
        name="inproj",
    )(x2, g.reshape(1, D_MODEL), w)


def _swa_body(sinks_ref, slopes_ref, q_ref, kp_ref, kc_ref, vp_ref, vc_ref,
              qn_ref, kn_ref, o_ref):
    L = SWA_BLOCK
    n = pl.program_id(1)
    k2 = jnp.concatenate([kp_ref[...], kc_ref[...]], axis=0)
    v2 = jnp.concatenate([vp_ref[...], vc_ref[...]], axis=0).astype(bf16)
    k2n = _pair_rms(k2, kn_ref[...]).astype(bf16)

    row = lax.broadcasted_iota(jnp.int32, (L, 2 * L), 0)
    col = lax.broadcasted_iota(jnp.int32, (L, 2 * L), 1)
    dist = row + L - col
    ok = (dist >= 0) & (dist < SWA_WINDOW) & ((n > 0) | (col >= L))
    distf = dist.astype(f32)
    lane = lax.broadcasted_iota(jnp.int32, (L, LANES), 1)
    lane_lo = lane < HEAD_DIM

    heads_per_kv = SWA_Q_HEADS // SWA_KV_HEADS
    for pp in range(SWA_Q_HEADS // 2):
        g = (2 * pp) // heads_per_kv
        qn = _pair_rms(q_ref[:, pp * LANES:(pp + 1) * LANES], qn_ref[...]) * ATTN_SCALE
        qn_sw = pltpu.roll(qn, HEAD_DIM, axis=1)
        in_g = lane_lo if g == 0 else jnp.logical_not(lane_lo)
        o_pair = None
        for hh in range(2):
            h = 2 * pp + hh
            src = qn if hh == g else qn_sw
            qm = jnp.where(in_g, src, 0.0).astype(bf16)
            s = _dot_nt(qm, k2n) - slopes_ref[h] * distf
            s = jnp.where(ok, s, -jnp.inf)
            sink = sinks_ref[h]
            m = jnp.maximum(jnp.max(s, axis=-1, keepdims=True), sink)
            p = jnp.exp(s - m)
            denom = jnp.sum(p, axis=-1, keepdims=True) + jnp.exp(sink - m)
            o = jnp.dot(p.astype(bf16), v2, preferred_element_type=f32) / denom
            o = o if hh == g else pltpu.roll(o, HEAD_DIM, axis=1)
            keep = lane_lo if hh == 0 else jnp.logical_not(lane_lo)
            o_pair = jnp.where(keep, o, 0.0) if o_pair is None else jnp.where(keep, o, o_pair)
        o_ref[:, pp * LANES:(pp + 1) * LANES] = o_pair.astype(bf16)


def _swa(proj, sinks, slopes, q_norm, k_norm, B, S):
    L = SWA_BLOCK
    nb = S // L
    T = B * S
    smem = pl.BlockSpec(memory_space=pltpu.SMEM)

    def prev(b, n):
        return b * nb + jnp.maximum(n - 1, 0)

    return pl.pallas_call(
        _swa_body,
        grid=(B, nb),
        in_specs=[
            smem, smem,
            pl.BlockSpec((L, SWA_Q_DIM), lambda b, n: (b * nb + n, COL_QA // (SWA_Q_DIM // LANES))),
            pl.BlockSpec((L, LANES), lambda b, n: (prev(b, n), COL_KA)),
            pl.BlockSpec((L, LANES), lambda b, n: (b * nb + n, COL_KA)),
            pl.BlockSpec((L, LANES), lambda b, n: (prev(b, n), COL_VA)),
            pl.BlockSpec((L, LANES), lambda b, n: (b * nb + n, COL_VA)),
            pl.BlockSpec((1, LANES), lambda b, n: (0, 0)),
            pl.BlockSpec((1, LANES), lambda b, n: (0, 0)),
        ],
        out_specs=pl.BlockSpec((L, SWA_Q_DIM), lambda b, n: (b * nb + n, 0)),
        out_shape=jax.ShapeDtypeStruct((T, SWA_Q_DIM), bf16),
        compiler_params=pltpu.CompilerParams(
            dimension_semantics=("arbitrary", "arbitrary"),
            vmem_limit_bytes=VMEM_LIMIT),
        name="swa",
    )(sinks, slopes, proj, proj, proj, proj, proj,
      jnp.tile(q_norm, 2).reshape(1, LANES), jnp.tile(k_norm, 2).reshape(1, LANES))


N_BIAS_PARTS = 3
NB_ROWS = 8
N_ALIBI_COL = NB_ROWS * N_BIAS_PARTS
SCORE_AHEAD = 3


def _split3(x):
    p0 = x.astype(bf16).astype(f32)
    r1 = x - p0
    p1 = r1.astype(bf16).astype(f32)
    p2 = (r1 - p1).astype(bf16).astype(f32)
    return p0, p1, p2


def _moba_body(slopes_ref, q_ref, k_ref, v_ref, qn_ref, kn_ref, o_ref, kaug_ref, vt_ref, *, nb):
    L = MOBA_BLOCK
    S = nb * L
    p = pl.program_id(1)

    kn = _pair_rms(k_ref[...], kn_ref[...])
    lane = lax.broadcasted_iota(jnp.int32, (S, LANES), 1)
    krow = lax.broadcasted_iota(jnp.int32, (S, LANES), 0)
    kblk = krow // L
    kpos = (krow % L).astype(f32)
    lane_lo = lane < HEAD_DIM

    def aug_cols(cc, slope):
        hot = ((cc >= 0) & (cc < N_ALIBI_COL) & ((cc % NB_ROWS) == kblk)).astype(f32)
        a0, a1, a2 = _split3(slope * kpos)
        return jnp.where(cc == N_ALIBI_COL, a0,
                         jnp.where(cc == N_ALIBI_COL + 1, a1,
                                   jnp.where(cc == N_ALIBI_COL + 2, a2, hot)))

    kaug_ref[0] = jnp.where(lane_lo, kn, aug_cols(lane - HEAD_DIM, slopes_ref[2 * p])).astype(bf16)
    kaug_ref[1] = jnp.where(lane_lo, aug_cols(lane, slopes_ref[2 * p + 1]), kn).astype(bf16)
    vt_ref[...] = v_ref[...].T.astype(bf16)
    means = [jnp.mean(kn[n * L:(n + 1) * L, :], axis=0, keepdims=True) for n in range(nb)]
    means += [jnp.zeros((1, LANES), f32)] * (NB_ROWS - nb)
    km = jnp.concatenate(means, axis=0)
    lane8 = lax.broadcasted_iota(jnp.int32, (NB_ROWS, LANES), 1)
    km_pair = jnp.concatenate(
        [jnp.where(lane8 < HEAD_DIM, km, 0.0), jnp.where(lane8 < HEAD_DIM, 0.0, km)], axis=0)
    km_hi, km_lo = _split2(km_pair)

    qs = _pair_rms(q_ref[...], qn_ref[...]) * ATTN_SCALE
    qt = qs.T
    qt_hi, qt_lo = _split2(qt)
    gate = (jnp.dot(km_hi, qt_hi, preferred_element_type=f32)
            + (jnp.dot(km_hi, qt_lo, preferred_element_type=f32)
               + jnp.dot(km_lo, qt_hi, preferred_element_type=f32)))

    blk = lax.broadcasted_iota(jnp.int32, (NB_ROWS, S), 0)
    qblk = lax.broadcasted_iota(jnp.int32, (NB_ROWS, S), 1) // L
    past = blk < qblk
    r = lax.broadcasted_iota(jnp.int32, (L, L), 1)
    c = lax.broadcasted_iota(jnp.int32, (L, L), 0)
    causal = jnp.where(r >= c, 0.0, NEG)
    ones_rows = (blk < 3).astype(f32)

    qa = []
    for hh in range(2):
        g = jnp.where(past, gate[hh * NB_ROWS:(hh + 1) * NB_ROWS, :], -jnp.inf)
        rank = jnp.zeros((NB_ROWS, S), jnp.int32)
        for m in range(nb):
            gm = g[m:m + 1, :]
            ahead = (gm > g) | ((gm == g) & (m < blk))
            rank = rank + ahead.astype(jnp.int32)
        sel = past & (rank < MOBA_TOPK)
        slope = slopes_ref[2 * p + hh]
        bias = jnp.where(sel, (-slope * L) * (qblk - blk).astype(f32), NEG)
        bias = jnp.where(blk == qblk, 0.0, bias)
        b0, b1, b2 = _split3(bias)
        extra = jnp.concatenate(
            [b0, b1, b2, ones_rows,
             jnp.zeros((HEAD_DIM - N_ALIBI_COL - NB_ROWS, S), f32)], axis=0)
        if hh == 0:
            qa.append(jnp.concatenate([qt[:HEAD_DIM], extra], axis=0).astype(bf16))
        else:
            qa.append(jnp.concatenate([extra, qt[HEAD_DIM:]], axis=0).astype(bf16))

    def scores(i, hh):
        return jnp.dot(kaug_ref[hh, 0:(i + 1) * L, :], qa[hh][:, i * L:(i + 1) * L],
                       preferred_element_type=f32)

    units = [(i, hh) for i in range(nb) for hh in range(2)]
    pending = [scores(*u) for u in units[:SCORE_AHEAD]]
    outs = {}

    def finish(i, hh, e_all, l):
        acc = jnp.dot(vt_ref[hh * HEAD_DIM:(hh + 1) * HEAD_DIM, 0:(i + 1) * L], e_all,
                      preferred_element_type=f32)
        outs[(i, hh)] = acc / l
        if hh == 1:
            o_ref[i * L:(i + 1) * L, :] = jnp.concatenate(
                [outs.pop((i, 0)), outs.pop((i, 1))], axis=0).T.astype(bf16)

    deferred = None
    for idx, (i, hh) in enumerate(units):
        s = pending.pop(0)
        if idx + SCORE_AHEAD < len(units):
            pending.append(scores(*units[idx + SCORE_AHEAD]))
        tiles = [s[n * L:(n + 1) * L, :] for n in range(i)] + [s[i * L:(i + 1) * L, :] + causal]
        m = functools.reduce(jnp.maximum, [jnp.max(t, axis=0, keepdims=True) for t in tiles])
        es = [jnp.exp(t - m) for t in tiles]
        l = functools.reduce(lambda a, b: a + b, [jnp.sum(e, axis=0, keepdims=True) for e in es])
        e_all = jnp.concatenate([e.astype(bf16) for e in es], axis=0)
        if deferred is not None:
            finish(*deferred)
        deferred = (i, hh, e_all, l)
    finish(*deferred)


def _moba(proj, slopes, q_norm, k_norm, B, S):
    L = MOBA_BLOCK
    nb = S // L
    assert nb <= NB_ROWS
    T = B * S
    n_pairs = MOBA_HEADS // 2
    return pl.pallas_call(
        functools.partial(_moba_body, nb=nb),
        grid=(B, n_pairs),
        in_specs=[
            pl.BlockSpec(memory_space=pltpu.SMEM),
            pl.BlockSpec((S, LANES), lambda b, p: (b, COL_QB + p)),
            pl.BlockSpec((S, LANES), lambda b, p: (b, COL_KB + p)),
            pl.BlockSpec((S, LANES), lambda b, p: (b, COL_VB + p)),
            pl.BlockSpec((1, LANES), lambda b, p: (0, 0)),
            pl.BlockSpec((1, LANES), lambda b, p: (0, 0)),
        ],
        out_specs=pl.BlockSpec((S, LANES), lambda b, p: (b, p)),
        out_shape=jax.ShapeDtypeStruct((T, MOBA_DIM), bf16),
        scratch_shapes=[
            pltpu.VMEM((2, S, LANES), bf16),
            pltpu.VMEM((LANES, S), bf16),
        ],
        compiler_params=pltpu.CompilerParams(
            dimension_semantics=("arbitrary", "arbitrary"),
            vmem_limit_bytes=VMEM_LIMIT),
        name="moba",
    )(slopes, proj, proj, proj,
      jnp.tile(q_norm, 2).reshape(1, LANES), jnp.tile(k_norm, 2).reshape(1, LANES))


ROUTER_ROWS = 32


N_GATE_BLOCKS = D_MODEL // GATE_BLOCK


def _mix_body(*refs):
    ga_refs = refs[:N_GATE_BLOCKS]
    gb_refs = refs[N_GATE_BLOCKS:2 * N_GATE_BLOCKS]
    (oa_ref, ob_ref, x_ref, wua_ref, wub_ref, wout_ref, gffn_ref, wr_ref,
     x1_ref, h2_ref, gid_ref) = refs[2 * N_GATE_BLOCKS:]
    ya = jnp.dot(oa_ref[...], wua_ref[...], preferred_element_type=f32)
    yb = jnp.dot(ob_ref[...], wub_ref[...], preferred_element_type=f32)
    gate_a = jnp.concatenate([r[...] for r in ga_refs], axis=1)
    gate_b = jnp.concatenate([r[...] for r in gb_refs], axis=1)
    merged = _sigmoid(gate_a) * ya + _sigmoid(gate_b) * yb
    x1 = x_ref[...] + jnp.dot(merged.astype(bf16), wout_ref[...], preferred_element_type=f32)
    x1_ref[...] = x1
    ms = jnp.mean(x1 * x1, axis=-1, keepdims=True)
    h2 = x1 * lax.rsqrt(ms + EPS) * gffn_ref[...]
    tm = h2.shape[0]
    for c in range(ROW_SLAB):
        h2_ref[pl.ds(c, tm, stride=SLAB_IN), :] = h2[:, c * LANES:(c + 1) * LANES]

    h_hi, h_lo = _split2(h2)
    lt = _dot_nt(wr_ref[0], h_hi) + (_dot_nt(wr_ref[0], h_lo) + _dot_nt(wr_ref[1], h_hi))
    gl = [lt[g:g + 1, :] for g in range(N_GROUPS)]
    gmax = functools.reduce(jnp.maximum, gl)
    gsum = functools.reduce(lambda a, b: a + b, [jnp.exp(v - gmax) for v in gl])
    g_p = 1.0 / gsum
    g_i = jnp.full((1, tm), N_GROUPS - 1, jnp.int32)
    for g in reversed(range(N_GROUPS)):
        g_i = jnp.where(gl[g] == gmax, g, g_i)

    el = []
    for e in range(EXPERTS_PER_GROUP):
        v = jnp.zeros((1, tm), f32)
        for g in range(N_GROUPS):
            r = N_GROUPS + g * EXPERTS_PER_GROUP + e
            v = jnp.where(g_i == g, lt[r:r + 1, :], v)
        el.append(v)
    emax = functools.reduce(jnp.maximum, el)
    ex = [jnp.exp(v - emax) for v in el]
    esum = functools.reduce(lambda a, b: a + b, ex)
    ep = [v / esum for v in ex]
    p1 = functools.reduce(jnp.maximum, ep)
    i1 = jnp.full((1, tm), EXPERTS_PER_GROUP - 1, jnp.int32)
    for e in reversed(range(EXPERTS_PER_GROUP)):
        i1 = jnp.where(ep[e] == p1, e, i1)
    rest = [jnp.where(i1 == e, -1.0, ep[e]) for e in range(EXPERTS_PER_GROUP)]
    p2 = functools.reduce(jnp.maximum, rest)
    i2 = jnp.full((1, tm), EXPERTS_PER_GROUP - 1, jnp.int32)
    for e in reversed(range(EXPERTS_PER_GROUP)):
        i2 = jnp.where(rest[e] == p2, e, i2)
    w1 = g_p * (p1 / (p1 + p2))
    w2 = g_p * (p2 / (p1 + p2))
    gid_ref[...] = jnp.concatenate([g_i, jnp.zeros((7, tm), jnp.int32)], axis=0)
    cw = [jnp.where(i1 == e, w1, 0.0) + jnp.where(i2 == e, w2, 0.0) for e in range(EXPERTS_PER_GROUP)]
    cw_t = jnp.concatenate(cw + [jnp.zeros((LANES - EXPERTS_PER_GROUP, tm), f32)], axis=0).T
    for e in range(SLAB_IN - ROW_SLAB):
        if e < EXPERTS_PER_GROUP:
            row = jnp.broadcast_to(cw_t[:, e:e + 1], (tm, LANES))
        else:
            row = jnp.zeros((tm, LANES), f32)
        h2_ref[pl.ds(ROW_SLAB + e, tm, stride=SLAB_IN), :] = row


def _mix(oa, ob, proj, x2, wua, wub, wout, g_ffn, wr, tm=256):
    T = x2.shape[0]
    const = lambda i: (0, 0)
    single = pl.Buffered(1)

    def gate_specs(first_col):
        first = first_col * LANES // GATE_BLOCK
        return [pl.BlockSpec((tm, GATE_BLOCK), functools.partial(lambda i, j: (i, j), j=first + k))
                for k in range(N_GATE_BLOCKS)]

    return pl.pallas_call(
        _mix_body,
        grid=(T // tm,),
        in_specs=gate_specs(COL_GATE_A) + gate_specs(COL_GATE_B) + [
            pl.BlockSpec((tm, SWA_Q_DIM), lambda i: (i, 0)),
            pl.BlockSpec((tm, MOBA_DIM), lambda i: (i, 0)),
            pl.BlockSpec((tm, D_MODEL), lambda i: (i, 0)),
            pl.BlockSpec((SWA_Q_DIM, D_MODEL), const, pipeline_mode=single),
            pl.BlockSpec((MOBA_DIM, D_MODEL), const, pipeline_mode=single),
            pl.BlockSpec((D_MODEL, D_MODEL), const, pipeline_mode=single),
            pl.BlockSpec((1, D_MODEL), const),
            pl.BlockSpec((2, ROUTER_ROWS, D_MODEL), lambda i: (0, 0, 0)),
        ],
        out_specs=[
            pl.BlockSpec((tm, D_MODEL), lambda i: (i, 0)),
            pl.BlockSpec((tm * SLAB_IN, LANES), lambda i: (i, 0)),
            pl.BlockSpec((8, tm), lambda i: (0, i)),
        ],
        out_shape=[
            jax.ShapeDtypeStruct((T, D_MODEL), f32),
            jax.ShapeDtypeStruct((T * SLAB_IN, LANES), f32),
            jax.ShapeDtypeStruct((8, T), jnp.int32),
        ],
        compiler_params=pltpu.CompilerParams(
            dimension_semantics=("arbitrary",),
            vmem_limit_bytes=VMEM_LIMIT),
        name="mix",
    )(*([proj] * (2 * N_GATE_BLOCKS)), oa, ob, x2, wua, wub, wout, g_ffn.reshape(1, D_MODEL), wr)


def _route_tables(gid, T, tm):
    nt = T // tm + N_GROUPS
    g = gid[0]
    onehot = (g[:, None] == jnp.arange(N_GROUPS, dtype=jnp.int32)[None, :]).astype(jnp.int32)
    incl = jnp.cumsum(onehot, axis=0)
    pos = jnp.sum(onehot * incl, axis=1) - 1
    counts = incl[-1]
    padded = (counts + tm - 1) // tm * tm
    seg_end = jnp.cumsum(padded)
    seg_start = seg_end - padded
    dest = jnp.sum(onehot * seg_start[None, :], axis=1) + pos
    tile_start = jnp.arange(nt, dtype=jnp.int32) * tm
    tile_g = jnp.sum((tile_start[:, None] >= seg_end[None, :]).astype(jnp.int32), axis=1)
    tile_g = jnp.minimum(tile_g, N_GROUPS - 1)
    nvalid = (seg_end[-1] // tm).astype(jnp.int32).reshape(1)
    n_pad = padded - counts
    pads = jnp.concatenate([seg_start + counts, seg_end, jnp.cumsum(n_pad) - n_pad])
    return (nvalid, tile_g.astype(jnp.int32), dest.astype(jnp.int32), pads.astype(jnp.int32))


def _moe_body(nvalid_ref, tg_ref, dest_ref, pads_ref, wg_ref, wu_ref, wd_ref, h2s_hbm, ytok_hbm,
              asg, gbuf, ybuf, xbuf, yacc, gsem, ssem, *, tm, nt, n_tok):
    t = pl.program_id(0)
    e = pl.program_id(1)
    last_e = EXPERTS_PER_GROUP - 1
    nvalid = nvalid_ref[0]
    slot = lax.rem(t, 2)
    other = 1 - slot

    def for_rows(fn):
        def body(r, carry):
            fn(r)
            return carry
        lax.fori_loop(0, tm, body, 0, unroll=8)

    def gather_copy(tok, s, r):
        return pltpu.make_async_copy(
            h2s_hbm.at[pl.ds(pl.multiple_of(tok * SLAB_IN, 8), SLAB_IN), :],
            gbuf.at[pl.ds(pl.multiple_of((s * tm + r) * SLAB_IN, 8), SLAB_IN), :], gsem.at[s])

    def scatter_copy(row, s, r):
        return pltpu.make_async_copy(
            ybuf.at[pl.ds(pl.multiple_of((s * tm + r) * ROW_SLAB, 8), ROW_SLAB), :],
            ytok_hbm.at[pl.ds(pl.multiple_of(row * ROW_SLAB, 8), ROW_SLAB), :], ssem.at[s])

    def start_gathers(tile, s):
        for_rows(lambda r: gather_copy(jnp.maximum(asg[tile * tm + r], 0), s, r).start())

    def start_scatters(tile, s):
        def one(r):
            a = asg[tile * tm + r]
            scatter_copy(jnp.where(a >= 0, a, n_tok - 1 - a), s, r).start()
        for_rows(one)

    def wait_gathers(s):
        for_rows(lambda r: gather_copy(0, s, r).wait())

    def wait_scatters(s):
        for_rows(lambda r: scatter_copy(0, s, r).wait())

    @pl.when((t == 0) & (e == 0))
    def _():
        def place(tok, carry):
            asg[dest_ref[tok]] = tok
            return carry
        lax.fori_loop(0, n_tok, place, 0, unroll=8)

        for g in range(N_GROUPS):
            first = pads_ref[g]
            code = -1 - pads_ref[2 * N_GROUPS + g] + first

            def mark(p, carry):
                asg[p] = code - p
                return carry
            lax.fori_loop(first, pads_ref[N_GROUPS + g], mark, 0)
        start_gathers(0, 0)

    @pl.when((e == 0) & (t < nvalid))
    def _():
        wait_gathers(slot)
        base = slot * tm * SLAB_IN
        xbuf[...] = jnp.concatenate(
            [gbuf[pl.ds(base + c, tm, stride=SLAB_IN), :] for c in range(ROW_SLAB)],
            axis=1).astype(bf16)

    @pl.when((e == 0) & (t + 1 < nvalid))
    def _():
        start_gathers(t + 1, other)

    @pl.when(t < nvalid)
    def _():
        x = xbuf[...]
        gte = jnp.dot(x, wg_ref[...], preferred_element_type=f32)
        up = jnp.dot(x, wu_ref[...], preferred_element_type=f32)
        cw = gbuf[pl.ds(slot * tm * SLAB_IN + ROW_SLAB + e, tm, stride=SLAB_IN), :]
        hid = (gte * _sigmoid(gte)) * up * jnp.concatenate([cw] * (D_EXPERT // LANES), axis=1)
        y = jnp.dot(hid.astype(bf16), wd_ref[...], preferred_element_type=f32)

        @pl.when(e == 0)
        def _():
            yacc[...] = y

        @pl.when(e > 0)
        def _():
            yacc[...] += y

    @pl.when((e == last_e) & (t < nvalid))
    def _():
        @pl.when(t >= 2)
        def _():
            wait_scatters(slot)
        base = slot * tm * ROW_SLAB
        for c in range(ROW_SLAB):
            ybuf[pl.ds(base + c, tm, stride=ROW_SLAB), :] = yacc[:, c * LANES:(c + 1) * LANES]
        start_scatters(t, slot)

        @pl.when(t == nvalid - 1)
        def _():
            @pl.when(t >= 1)
            def _():
                wait_scatters(other)
            wait_scatters(slot)

    @pl.when((e == last_e) & (t == nt - 1))
    def _():
        ybuf[0:tm * ROW_SLAB, :] = jnp.zeros((tm * ROW_SLAB, LANES), f32)
        for j in range(n_tok // tm, nt):
            @pl.when(j >= nvalid)
            def _():
                fill = pltpu.make_async_copy(
                    ybuf.at[0:tm * ROW_SLAB, :],
                    ytok_hbm.at[j * tm * ROW_SLAB:(j + 1) * tm * ROW_SLAB, :], ssem.at[0])
                fill.start()
                fill.wait()


def _moe(h2s, tables, wg, wu, wd, T, tm):
    nvalid, tile_g, dest, pads = tables
    nt = tile_g.shape[0]

    def w_index(t, e, nv, tg, de, pd):
        return (tg[t] * EXPERTS_PER_GROUP + jnp.where(t < nv[0], e, EXPERTS_PER_GROUP - 1), 0, 0)

    grid_spec = pltpu.PrefetchScalarGridSpec(
        num_scalar_prefetch=4,
        grid=(nt, EXPERTS_PER_GROUP),
        in_specs=[
            pl.BlockSpec((None, D_MODEL, D_EXPERT), w_index),
            pl.BlockSpec((None, D_MODEL, D_EXPERT), w_index),
            pl.BlockSpec((None, D_EXPERT, D_MODEL), w_index),
            pl.BlockSpec(memory_space=pl.ANY),
        ],
        out_specs=pl.BlockSpec(memory_space=pl.ANY),
        scratch_shapes=[
            pltpu.SMEM((nt * tm,), jnp.int32),
            pltpu.VMEM((2 * tm * SLAB_IN, LANES), f32),
            pltpu.VMEM((2 * tm * ROW_SLAB, LANES), f32),
            pltpu.VMEM((tm, D_MODEL), bf16),
            pltpu.VMEM((tm, D_MODEL), f32),
            pltpu.SemaphoreType.DMA((2,)),
            pltpu.SemaphoreType.DMA((2,)),
        ],
    )
    return pl.pallas_call(
        functools.partial(_moe_body, tm=tm, nt=nt, n_tok=T),
        grid_spec=grid_spec,
        out_shape=jax.ShapeDtypeStruct((nt * tm * ROW_SLAB, LANES), f32),
        compiler_params=pltpu.CompilerParams(
            dimension_semantics=("arbitrary", "arbitrary"),
            vmem_limit_bytes=VMEM_LIMIT),
        name="moe",
    )(nvalid, tile_g, dest, pads, wg, wu, wd, h2s)


def _combine_body(x1_ref, y_ref, o_ref):
    tm = x1_ref.shape[0]
    for c in range(ROW_SLAB):
        cols = slice(c * LANES, (c + 1) * LANES)
        o_ref[:, cols] = x1_ref[:, cols] + y_ref[pl.ds(c, tm, stride=ROW_SLAB), :]


def _combine(x1, ytok, tm=256):
    T = x1.shape[0]
    return pl.pallas_call(
        _combine_body,
        grid=(T // tm,),
        in_specs=[
            pl.BlockSpec((tm, D_MODEL), lambda i: (i, 0)),
            pl.BlockSpec((tm * ROW_SLAB, LANES), lambda i: (i, 0)),
        ],
        out_specs=pl.BlockSpec((tm, D_MODEL), lambda i: (i, 0)),
        out_shape=jax.ShapeDtypeStruct((T, D_MODEL), f32),
        compiler_params=pltpu.CompilerParams(
            dimension_semantics=("arbitrary",),
            vmem_limit_bytes=VMEM_LIMIT),
        name="combine",
    )(x1, ytok)


def _alibi_slopes(n):
    return jnp.exp2(-8.0 * jnp.arange(1, n + 1, dtype=f32) / n)


def kernel(x, g_mix, w_in, q_norm_swa, k_norm_swa, sinks, q_norm_moba, k_norm_moba,
           w_up_swa, w_up_moba, w_out, g_ffn, w_router_group, w_router_expert,
           w_gate_e, w_up_e, w_down_e):
    B, S, D = x.shape
    assert D == D_MODEL and S % MOBA_BLOCK == 0 and S % SWA_BLOCK == 0
    T = B * S
    x2 = x.reshape(T, D)

    proj = _inproj(x2, g_mix, w_in)

    oa = _swa(proj, sinks.astype(f32), _alibi_slopes(SWA_Q_HEADS), q_norm_swa, k_norm_swa, B, S)
    ob = _moba(proj, _alibi_slopes(MOBA_HEADS), q_norm_moba, k_norm_moba, B, S)

    wr = jnp.concatenate(
        [w_router_group.T,
         w_router_expert.transpose(0, 2, 1).reshape(N_EXPERTS, D),
         jnp.zeros((ROUTER_ROWS - N_GROUPS - N_EXPERTS, D), f32)], axis=0)
    wr_hi = wr.astype(bf16)
    wr_lo = (wr - wr_hi.astype(f32)).astype(bf16)
    x1, h2s, gid = _mix(oa, ob, proj, x2, w_up_swa.astype(bf16), w_up_moba.astype(bf16),
                        w_out.astype(bf16), g_ffn, jnp.stack([wr_hi, wr_lo]))

    tables = _route_tables(gid, T, MOE_TILE)
    ytok = _moe(h2s, tables, w_gate_e.astype(bf16), w_up_e.astype(bf16), w_down_e.astype(bf16),
                T, MOE_TILE)
    y = _combine(x1, ytok)
    return y.reshape(B, S, D)
```

```python
import functools

import jax
import jax.numpy as jnp
from jax import lax
from jax.experimental import pallas as pl
from jax.experimental.pallas import tpu as pltpu

D_MODEL = 2048
HEAD_DIM = 64
ATTN_SCALE = HEAD_DIM ** -0.5
SWA_Q_HEADS = 16
SWA_KV_HEADS = 2
SWA_WINDOW = 128
SWA_BLOCK = 128
MOBA_HEADS = 16
MOBA_BLOCK = 256
MOBA_TOPK = 3
N_GROUPS = 4
EXPERTS_PER_GROUP = 4
N_EXPERTS = N_GROUPS * EXPERTS_PER_GROUP
D_EXPERT = 512
EPS = 1e-6

SWA_Q_DIM = SWA_Q_HEADS * HEAD_DIM
SWA_KV_DIM = SWA_KV_HEADS * HEAD_DIM
MOBA_DIM = MOBA_HEADS * HEAD_DIM
IN_COLS = SWA_Q_DIM + 2 * SWA_KV_DIM + 3 * MOBA_DIM + 2 * D_MODEL

LANES = 128
ROW_SLAB = D_MODEL // LANES
SLAB_IN = ROW_SLAB + 8
MOE_TILE = 512
VMEM_LIMIT = 56 * 1024 * 1024
NEG = -1e30

COL_QA = 0
COL_KA = COL_QA + SWA_Q_DIM // LANES
COL_VA = COL_KA + 1
COL_QB = COL_VA + 1
COL_KB = COL_QB + MOBA_DIM // LANES
COL_VB = COL_KB + MOBA_DIM // LANES
COL_GATE_A = COL_VB + MOBA_DIM // LANES
COL_GATE_B = COL_GATE_A + D_MODEL // LANES
GATE_BLOCK = 2 * LANES
assert (COL_GATE_A * LANES) % GATE_BLOCK == 0 and (COL_GATE_B * LANES) % GATE_BLOCK == 0

bf16 = jnp.bfloat16
f32 = jnp.float32


def _sigmoid(x):
    return 1.0 / (1.0 + jnp.exp(-x))


def _pair_rms(x, gain):
    lane = lax.broadcasted_iota(jnp.int32, x.shape, 1)
    lo = lane < HEAD_DIM
    sq = x * x
    s0 = jnp.sum(jnp.where(lo, sq, 0.0), axis=-1, keepdims=True)
    s1 = jnp.sum(jnp.where(lo, 0.0, sq), axis=-1, keepdims=True)
    r0 = lax.rsqrt(s0 * (1.0 / HEAD_DIM) + EPS)
    r1 = lax.rsqrt(s1 * (1.0 / HEAD_DIM) + EPS)
    return x * jnp.where(lo, r0, r1) * gain


def _dot_nt(a, b):
    return lax.dot_general(a, b, (((1,), (1,)), ((), ())), preferred_element_type=f32)


def _split2(x):
    hi = x.astype(bf16)
    lo = (x - hi.astype(f32)).astype(bf16)
    return hi, lo


CAST_CHUNK = 2 * LANES
NORM_ROWS = 512


def _inproj_body(x_ref, g_ref, w_ref, o_ref, h_ref):
    @pl.when(pl.program_id(1) == 0)
    def _():
        for r in range(0, x_ref.shape[0], NORM_ROWS):
            x = x_ref[r:r + NORM_ROWS, :]
            ms = jnp.mean(x * x, axis=-1, keepdims=True)
            h_ref[r:r + NORM_ROWS, :] = (x * lax.rsqrt(ms + EPS) * g_ref[...]).astype(bf16)

    n_chunks = w_ref.shape[1] // CAST_CHUNK
    cast = lambda c: w_ref[:, c * CAST_CHUNK:(c + 1) * CAST_CHUNK].astype(bf16)
    w_next = cast(0)
    for c in range(n_chunks):
        w_cur = w_next
        if c + 1 < n_chunks:
            w_next = cast(c + 1)
        o_ref[:, c * CAST_CHUNK:(c + 1) * CAST_CHUNK] = jnp.dot(
            h_ref[...], w_cur, preferred_element_type=f32)


def _inproj(x2, g, w, tm=2048, tn=768):
    T = x2.shape[0]
    N = w.shape[1]
    return pl.pallas_call(
        _inproj_body,
        grid=(T // tm, N // tn),
        in_specs=[
            pl.BlockSpec((tm, D_MODEL), lambda i, j: (i, 0), pipeline_mode=pl.Buffered(1)),
            pl.BlockSpec((1, D_MODEL), lambda i, j: (0, 0)),
            pl.BlockSpec((D_MODEL, tn), lambda i, j: (0, j)),
        ],
        out_specs=pl.BlockSpec((tm, tn), lambda i, j: (i, j)),
        out_shape=jax.ShapeDtypeStruct((T, N), f32),
        scratch_shapes=[pltpu.VMEM((tm, D_MODEL), bf16)],
        compiler_params=pltpu.CompilerParams(
            dimension_semantics=("arbitrary", "arbitrary"),
            vmem_limit_bytes=VMEM_LIMIT),
        name="inproj",
    )(x2, g.reshape(1, D_MODEL), w)


SWA_AHEAD = 2


def _swa_body(sinks_ref, slopes_ref, q_ref, kp_ref, kc_ref, vp_ref, vc_ref,
              qn_ref, kn_ref, o_ref):
    L = SWA_BLOCK
    n = pl.program_id(1)
    k2 = jnp.concatenate([kp_ref[...], kc_ref[...]], axis=0)
    k2n = _pair_rms(k2, kn_ref[...]).astype(bf16)
    v2t = jnp.concatenate([vp_ref[...], vc_ref[...]], axis=0).T.astype(bf16)
    qn = jnp.concatenate(
        [_pair_rms(q_ref[:, pp * LANES:(pp + 1) * LANES], qn_ref[...])
         for pp in range(SWA_Q_HEADS // 2)], axis=1) * ATTN_SCALE
    qt = qn.T

    key = lax.broadcasted_iota(jnp.int32, (2 * L, L), 0)
    qry = lax.broadcasted_iota(jnp.int32, (2 * L, L), 1)
    dist = qry + L - key
    ok = (dist >= 0) & (dist < SWA_WINDOW) & ((n > 0) | (key >= L))
    distf = dist.astype(f32)
    zeros = jnp.zeros((HEAD_DIM, L), f32)

    heads_per_kv = SWA_Q_HEADS // SWA_KV_HEADS

    def scores(h):
        g = h // heads_per_kv
        qh = qt[h * HEAD_DIM:(h + 1) * HEAD_DIM, :]
        qa = jnp.concatenate([qh, zeros] if g == 0 else [zeros, qh], axis=0).astype(bf16)
        return jnp.dot(k2n, qa, preferred_element_type=f32)

    def softmax(h, s):
        s = jnp.where(ok, s - slopes_ref[h] * distf, -jnp.inf)
        sink = sinks_ref[h]
        m = jnp.maximum(jnp.max(s, axis=0, keepdims=True), sink)
        p = jnp.exp(s - m)
        denom = jnp.sum(p, axis=0, keepdims=True) + jnp.exp(sink - m)
        return p.astype(bf16), denom

    outs = []

    def finish(h, p, denom):
        g = h // heads_per_kv
        o = jnp.dot(v2t, p, preferred_element_type=f32)
        outs.append(o[g * HEAD_DIM:(g + 1) * HEAD_DIM, :] / denom)

    pending = [scores(h) for h in range(SWA_AHEAD)]
    deferred = None
    for h in range(SWA_Q_HEADS):
        s = pending.pop(0)
        if h + SWA_AHEAD < SWA_Q_HEADS:
            pending.append(scores(h + SWA_AHEAD))
        p, denom = softmax(h, s)
        if deferred is not None:
            finish(*deferred)
        deferred = (h, p, denom)
    finish(*deferred)
    o_ref[...] = jnp.concatenate(outs, axis=0).T.astype(bf16)


def _swa(proj, sinks, slopes, q_norm, k_norm, B, S):
    L = SWA_BLOCK
    nb = S // L
    T = B * S
    smem = pl.BlockSpec(memory_space=pltpu.SMEM)

    def prev(b, n):
        return b * nb + jnp.maximum(n - 1, 0)

    return pl.pallas_call(
        _swa_body,
        grid=(B, nb),
        in_specs=[
            smem, smem,
            pl.BlockSpec((L, SWA_Q_DIM), lambda b, n: (b * nb + n, COL_QA // (SWA_Q_DIM // LANES))),
            pl.BlockSpec((L, LANES), lambda b, n: (prev(b, n), COL_KA)),
            pl.BlockSpec((L, LANES), lambda b, n: (b * nb + n, COL_KA)),
            pl.BlockSpec((L, LANES), lambda b, n: (prev(b, n), COL_VA)),
            pl.BlockSpec((L, LANES), lambda b, n: (b * nb + n, COL_VA)),
            pl.BlockSpec((1, LANES), lambda b, n: (0, 0)),
            pl.BlockSpec((1, LANES), lambda b, n: (0, 0)),
        ],
        out_specs=pl.BlockSpec((L, SWA_Q_DIM), lambda b, n: (b * nb + n, 0)),
        out_shape=jax.ShapeDtypeStruct((T, SWA_Q_DIM), bf16),
        compiler_params=pltpu.CompilerParams(
            dimension_semantics=("arbitrary", "arbitrary"),
            vmem_limit_bytes=VMEM_LIMIT),
        name="swa",
    )(sinks, slopes, proj, proj, proj, proj, proj,
      jnp.tile(q_norm, 2).reshape(1, LANES), jnp.tile(k_norm, 2).reshape(1, LANES))


N_BIAS_PARTS = 3
NB_ROWS = 8
N_ALIBI_COL = NB_ROWS * N_BIAS_PARTS
SCORE_AHEAD = 3


def _split3(x):
    p0 = x.astype(bf16).astype(f32)
    r1 = x - p0
    p1 = r1.astype(bf16).astype(f32)
    p2 = (r1 - p1).astype(bf16).astype(f32)
    return p0, p1, p2


def _moba_body(slopes_ref, q_ref, k_ref, v_ref, qn_ref, kn_ref, o_ref, kaug_ref, vt_ref, *, nb):
    L = MOBA_BLOCK
    S = nb * L
    p = pl.program_id(1)

    kn = _pair_rms(k_ref[...], kn_ref[...])
    lane = lax.broadcasted_iota(jnp.int32, (S, LANES), 1)
    krow = lax.broadcasted_iota(jnp.int32, (S, LANES), 0)
    kblk = krow // L
    kpos = (krow % L).astype(f32)
    lane_lo = lane < HEAD_DIM

    def aug_cols(cc, slope):
        hot = ((cc >= 0) & (cc < N_ALIBI_COL) & ((cc % NB_ROWS) == kblk)).astype(f32)
        a0, a1, a2 = _split3(slope * kpos)
        return jnp.where(cc == N_ALIBI_COL, a0,
                         jnp.where(cc == N_ALIBI_COL + 1, a1,
                                   jnp.where(cc == N_ALIBI_COL + 2, a2, hot)))

    kaug_ref[0] = jnp.where(lane_lo, kn, aug_cols(lane - HEAD_DIM, slopes_ref[2 * p])).astype(bf16)
    kaug_ref[1] = jnp.where(lane_lo, aug_cols(lane, slopes_ref[2 * p + 1]), kn).astype(bf16)
    vt_ref[...] = v_ref[...].T.astype(bf16)
    means = [jnp.mean(kn[n * L:(n + 1) * L, :], axis=0, keepdims=True) for n in range(nb)]
    means += [jnp.zeros((1, LANES), f32)] * (NB_ROWS - nb)
    km = jnp.concatenate(means, axis=0)
    lane8 = lax.broadcasted_iota(jnp.int32, (NB_ROWS, LANES), 1)
    km_pair = jnp.concatenate(
        [jnp.where(lane8 < HEAD_DIM, km, 0.0), jnp.where(lane8 < HEAD_DIM, 0.0, km)], axis=0)
    km_hi, km_lo = _split2(km_pair)

    qs = _pair_rms(q_ref[...], qn_ref[...]) * ATTN_SCALE
    qt = qs.T
    qt_hi, qt_lo = _split2(qt)
    gate = (jnp.dot(km_hi, qt_hi, preferred_element_type=f32)
            + (jnp.dot(km_hi, qt_lo, preferred_element_type=f32)
               + jnp.dot(km_lo, qt_hi, preferred_element_type=f32)))

    blk = lax.broadcasted_iota(jnp.int32, (NB_ROWS, S), 0)
    qblk = lax.broadcasted_iota(jnp.int32, (NB_ROWS, S), 1) // L
    past = blk < qblk
    r = lax.broadcasted_iota(jnp.int32, (L, L), 1)
    c = lax.broadcasted_iota(jnp.int32, (L, L), 0)
    causal = jnp.where(r >= c, 0.0, NEG)
    ones_rows = (blk < 3).astype(f32)

    qa = []
    for hh in range(2):
        g = jnp.where(past, gate[hh * NB_ROWS:(hh + 1) * NB_ROWS, :], -jnp.inf)
        rank = jnp.zeros((NB_ROWS, S), jnp.int32)
        for m in range(nb):
            gm = g[m:m + 1, :]
            ahead = (gm > g) | ((gm == g) & (m < blk))
            rank = rank + ahead.astype(jnp.int32)
        sel = past & (rank < MOBA_TOPK)
        slope = slopes_ref[2 * p + hh]
        bias = jnp.where(sel, (-slope * L) * (qblk - blk).astype(f32), NEG)
        bias = jnp.where(blk == qblk, 0.0, bias)
        b0, b1, b2 = _split3(bias)
        extra = jnp.concatenate(
            [b0, b1, b2, ones_rows,
             jnp.zeros((HEAD_DIM - N_ALIBI_COL - NB_ROWS, S), f32)], axis=0)
        if hh == 0:
            qa.append(jnp.concatenate([qt[:HEAD_DIM], extra], axis=0).astype(bf16))
        else:
            qa.append(jnp.concatenate([extra, qt[HEAD_DIM:]], axis=0).astype(bf16))

    def scores(i, hh):
        return jnp.dot(kaug_ref[hh, 0:(i + 1) * L, :], qa[hh][:, i * L:(i + 1) * L],
                       preferred_element_type=f32)

    units = [(i, hh) for i in range(nb) for hh in range(2)]
    pending = [scores(*u) for u in units[:SCORE_AHEAD]]
    outs = {}

    def finish(i, hh, e_all, l):
        acc = jnp.dot(vt_ref[hh * HEAD_DIM:(hh + 1) * HEAD_DIM, 0:(i + 1) * L], e_all,
                      preferred_element_type=f32)
        outs[(i, hh)] = acc / l
        if hh == 1:
            o_ref[i * L:(i + 1) * L, :] = jnp.concatenate(
                [outs.pop((i, 0)), outs.pop((i, 1))], axis=0).T.astype(bf16)

    deferred = None
    for idx, (i, hh) in enumerate(units):
        s = pending.pop(0)
        if idx + SCORE_AHEAD < len(units):
            pending.append(scores(*units[idx + SCORE_AHEAD]))
        tiles = [s[n * L:(n + 1) * L, :] for n in range(i)] + [s[i * L:(i + 1) * L, :] + causal]
        m = functools.reduce(jnp.maximum, [jnp.max(t, axis=0, keepdims=True) for t in tiles])
        es = [jnp.exp(t - m) for t in tiles]
        l = functools.reduce(lambda a, b: a + b, [jnp.sum(e, axis=0, keepdims=True) for e in es])
        e_all = jnp.concatenate([e.astype(bf16) for e in es], axis=0)
        if deferred is not None:
            finish(*deferred)
        deferred = (i, hh, e_all, l)
    finish(*deferred)


def _moba(proj, slopes, q_norm, k_norm, B, S):
    L = MOBA_BLOCK
    nb = S // L
    assert nb <= NB_ROWS
    T = B * S
    n_pairs = MOBA_HEADS // 2
    return pl.pallas_call(
        functools.partial(_moba_body, nb=nb),
        grid=(B, n_pairs),
        in_specs=[
            pl.BlockSpec(memory_space=pltpu.SMEM),
            pl.BlockSpec((S, LANES), lambda b, p: (b, COL_QB + p)),
            pl.BlockSpec((S, LANES), lambda b, p: (b, COL_KB + p)),
            pl.BlockSpec((S, LANES), lambda b, p: (b, COL_VB + p)),
            pl.BlockSpec((1, LANES), lambda b, p: (0, 0)),
            pl.BlockSpec((1, LANES), lambda b, p: (0, 0)),
        ],
        out_specs=pl.BlockSpec((S, LANES), lambda b, p: (b, p)),
        out_shape=jax.ShapeDtypeStruct((T, MOBA_DIM), bf16),
        scratch_shapes=[
            pltpu.VMEM((2, S, LANES), bf16),
            pltpu.VMEM((LANES, S), bf16),
        ],
        compiler_params=pltpu.CompilerParams(
            dimension_semantics=("arbitrary", "arbitrary"),
            vmem_limit_bytes=VMEM_LIMIT),
        name="moba",
    )(slopes, proj, proj, proj,
      jnp.tile(q_norm, 2).reshape(1, LANES), jnp.tile(k_norm, 2).reshape(1, LANES))


ROUTER_COLS = LANES


N_GATE_BLOCKS = D_MODEL // GATE_BLOCK
MIX_CHUNK = 2 * GATE_BLOCK


def _mix_body(*refs):
    ga_refs = refs[:N_GATE_BLOCKS]
    gb_refs = refs[N_GATE_BLOCKS:2 * N_GATE_BLOCKS]
    (oa_ref, ob_ref, x_ref, wua_ref, wub_ref, wout_ref, gffn_ref, wr_ref,
     x1_ref, h2_ref, gid_ref) = refs[2 * N_GATE_BLOCKS:]
    oa = oa_ref[...]
    ob = ob_ref[...]
    gpc = MIX_CHUNK // GATE_BLOCK

    def up(c):
        cols = slice(c * MIX_CHUNK, (c + 1) * MIX_CHUNK)
        return (jnp.dot(oa, wua_ref[:, cols], preferred_element_type=f32),
                jnp.dot(ob, wub_ref[:, cols], preferred_element_type=f32))

    x1 = x_ref[...]
    pending = up(0)
    for c in range(D_MODEL // MIX_CHUNK):
        ya, yb = pending
        if (c + 1) * MIX_CHUNK < D_MODEL:
            pending = up(c + 1)
        gate_a = jnp.concatenate([r[...] for r in ga_refs[c * gpc:(c + 1) * gpc]], axis=1)
        gate_b = jnp.concatenate([r[...] for r in gb_refs[c * gpc:(c + 1) * gpc]], axis=1)
        merged = (_sigmoid(gate_a) * ya + _sigmoid(gate_b) * yb).astype(bf16)
        x1 = x1 + jnp.dot(merged, wout_ref[c * MIX_CHUNK:(c + 1) * MIX_CHUNK, :],
                          preferred_element_type=f32)
    x1_ref[...] = x1
    ms = jnp.mean(x1 * x1, axis=-1, keepdims=True)
    h2 = x1 * lax.rsqrt(ms + EPS) * gffn_ref[...]
    tm = h2.shape[0]
    for c in range(ROW_SLAB):
        h2_ref[pl.ds(c, tm, stride=SLAB_IN), :] = h2[:, c * LANES:(c + 1) * LANES]

    h_hi, h_lo = _split2(h2)
    lt = (jnp.dot(h_hi, wr_ref[0], preferred_element_type=f32)
          + (jnp.dot(h_lo, wr_ref[0], preferred_element_type=f32)
             + jnp.dot(h_hi, wr_ref[1], preferred_element_type=f32))).T
    gl = [lt[g:g + 1, :] for g in range(N_GROUPS)]
    gmax = functools.reduce(jnp.maximum, gl)
    gsum = functools.reduce(lambda a, b: a + b, [jnp.exp(v - gmax) for v in gl])
    g_p = 1.0 / gsum
    g_i = jnp.full((1, tm), N_GROUPS - 1, jnp.int32)
    for g in reversed(range(N_GROUPS)):
        g_i = jnp.where(gl[g] == gmax, g, g_i)

    el = []
    for e in range(EXPERTS_PER_GROUP):
        v = jnp.zeros((1, tm), f32)
        for g in range(N_GROUPS):
            r = N_GROUPS + g * EXPERTS_PER_GROUP + e
            v = jnp.where(g_i == g, lt[r:r + 1, :], v)
        el.append(v)
    emax = functools.reduce(jnp.maximum, el)
    ex = [jnp.exp(v - emax) for v in el]
    esum = functools.reduce(lambda a, b: a + b, ex)
    ep = [v / esum for v in ex]
    p1 = functools.reduce(jnp.maximum, ep)
    i1 = jnp.full((1, tm), EXPERTS_PER_GROUP - 1, jnp.int32)
    for e in reversed(range(EXPERTS_PER_GROUP)):
        i1 = jnp.where(ep[e] == p1, e, i1)
    rest = [jnp.where(i1 == e, -1.0, ep[e]) for e in range(EXPERTS_PER_GROUP)]
    p2 = functools.reduce(jnp.maximum, rest)
    i2 = jnp.full((1, tm), EXPERTS_PER_GROUP - 1, jnp.int32)
    for e in reversed(range(EXPERTS_PER_GROUP)):
        i2 = jnp.where(rest[e] == p2, e, i2)
    w1 = g_p * (p1 / (p1 + p2))
    w2 = g_p * (p2 / (p1 + p2))
    gid_ref[...] = jnp.concatenate([g_i, jnp.zeros((7, tm), jnp.int32)], axis=0)
    cw = [jnp.where(i1 == e, w1, 0.0) + jnp.where(i2 == e, w2, 0.0) for e in range(EXPERTS_PER_GROUP)]
    cw_t = jnp.concatenate(cw + [jnp.zeros((LANES - EXPERTS_PER_GROUP, tm), f32)], axis=0).T
    for e in range(SLAB_IN - ROW_SLAB):
        if e < EXPERTS_PER_GROUP:
            row = jnp.broadcast_to(cw_t[:, e:e + 1], (tm, LANES))
        else:
            row = jnp.zeros((tm, LANES), f32)
        h2_ref[pl.ds(ROW_SLAB + e, tm, stride=SLAB_IN), :] = row


def _mix(oa, ob, proj, x2, wua, wub, wout, g_ffn, wr, tm=256):
    T = x2.shape[0]
    const = lambda i: (0, 0)
    single = pl.Buffered(1)

    def gate_specs(first_col):
        first = first_col * LANES // GATE_BLOCK
        return [pl.BlockSpec((tm, GATE_BLOCK), functools.partial(lambda i, j: (i, j), j=first + k))
                for k in range(N_GATE_BLOCKS)]

    return pl.pallas_call(
        _mix_body,
        grid=(T // tm,),
        in_specs=gate_specs(COL_GATE_A) + gate_specs(COL_GATE_B) + [
            pl.BlockSpec((tm, SWA_Q_DIM), lambda i: (i, 0)),
            pl.BlockSpec((tm, MOBA_DIM), lambda i: (i, 0)),
            pl.BlockSpec((tm, D_MODEL), lambda i: (i, 0)),
            pl.BlockSpec((SWA_Q_DIM, D_MODEL), const, pipeline_mode=single),
            pl.BlockSpec((MOBA_DIM, D_MODEL), const, pipeline_mode=single),
            pl.BlockSpec((D_MODEL, D_MODEL), const, pipeline_mode=single),
            pl.BlockSpec((1, D_MODEL), const),
            pl.BlockSpec((2, D_MODEL, ROUTER_COLS), lambda i: (0, 0, 0)),
        ],
        out_specs=[
            pl.BlockSpec((tm, D_MODEL), lambda i: (i, 0)),
            pl.BlockSpec((tm * SLAB_IN, LANES), lambda i: (i, 0)),
            pl.BlockSpec((8, tm), lambda i: (0, i)),
        ],
        out_shape=[
            jax.ShapeDtypeStruct((T, D_MODEL), f32),
            jax.ShapeDtypeStruct((T * SLAB_IN, LANES), f32),
            jax.ShapeDtypeStruct((8, T), jnp.int32),
        ],
        compiler_params=pltpu.CompilerParams(
            dimension_semantics=("arbitrary",),
            vmem_limit_bytes=VMEM_LIMIT),
        name="mix",
    )(*([proj] * (2 * N_GATE_BLOCKS)), oa, ob, x2, wua, wub, wout, g_ffn.reshape(1, D_MODEL), wr)


def _route_tables(gid, T, tm):
    nt = T // tm + N_GROUPS
    g = gid[0]
    onehot = (g[:, None] == jnp.arange(N_GROUPS, dtype=jnp.int32)[None, :]).astype(jnp.int32)
    incl = jnp.cumsum(onehot, axis=0)
    pos = jnp.sum(onehot * incl, axis=1) - 1
    counts = incl[-1]
    padded = (counts + tm - 1) // tm * tm
    seg_end = jnp.cumsum(padded)
    seg_start = seg_end - padded
    dest = jnp.sum(onehot * seg_start[None, :], axis=1) + pos
    tile_start = jnp.arange(nt, dtype=jnp.int32) * tm
    tile_g = jnp.sum((tile_start[:, None] >= seg_end[None, :]).astype(jnp.int32), axis=1)
    tile_g = jnp.minimum(tile_g, N_GROUPS - 1)
    nvalid = (seg_end[-1] // tm).astype(jnp.int32).reshape(1)
    n_pad = padded - counts
    pads = jnp.concatenate([seg_start + counts, seg_end, jnp.cumsum(n_pad) - n_pad])
    return (nvalid, tile_g.astype(jnp.int32), dest.astype(jnp.int32), pads.astype(jnp.int32))


def _moe_body(nvalid_ref, tg_ref, dest_ref, pads_ref, wg_ref, wu_ref, wd_ref, h2s_hbm, ytok_hbm,
              asg, gbuf, ybuf, xbuf, yacc, gsem, ssem, *, tm, nt, n_tok):
    t = pl.program_id(0)
    e = pl.program_id(1)
    last_e = EXPERTS_PER_GROUP - 1
    nvalid = nvalid_ref[0]
    slot = lax.rem(t, 2)
    other = 1 - slot

    def for_rows(fn):
        def body(r, carry):
            fn(r)
            return carry
        lax.fori_loop(0, tm, body, 0, unroll=8)

    def gather_copy(tok, s, r):
        return pltpu.make_async_copy(
            h2s_hbm.at[pl.ds(pl.multiple_of(tok * SLAB_IN, 8), SLAB_IN), :],
            gbuf.at[pl.ds(pl.multiple_of((s * tm + r) * SLAB_IN, 8), SLAB_IN), :], gsem.at[s])

    def scatter_copy(row, s, r):
        return pltpu.make_async_copy(
            ybuf.at[pl.ds(pl.multiple_of((s * tm + r) * ROW_SLAB, 8), ROW_SLAB), :],
            ytok_hbm.at[pl.ds(pl.multiple_of(row * ROW_SLAB, 8), ROW_SLAB), :], ssem.at[s])

    def start_gathers(tile, s):
        for_rows(lambda r: gather_copy(jnp.maximum(asg[tile * tm + r], 0), s, r).start())

    def start_scatters(tile, s):
        def one(r):
            a = asg[tile * tm + r]
            scatter_copy(jnp.where(a >= 0, a, n_tok - 1 - a), s, r).start()
        for_rows(one)

    def wait_gathers(s):
        for_rows(lambda r: gather_copy(0, s, r).wait())

    def wait_scatters(s):
        for_rows(lambda r: scatter_copy(0, s, r).wait())

    @pl.when((t == 0) & (e == 0))
    def _():
        def place(tok, carry):
            asg[dest_ref[tok]] = tok
            return carry
        lax.fori_loop(0, n_tok, place, 0, unroll=8)

        for g in range(N_GROUPS):
            first = pads_ref[g]
            code = -1 - pads_ref[2 * N_GROUPS + g] + first

            def mark(p, carry):
                asg[p] = code - p
                return carry
            lax.fori_loop(first, pads_ref[N_GROUPS + g], mark, 0)
        start_gathers(0, 0)

    @pl.when((e == 0) & (t < nvalid))
    def _():
        wait_gathers(slot)
        base = slot * tm * SLAB_IN
        xbuf[...] = jnp.concatenate(
            [gbuf[pl.ds(base + c, tm, stride=SLAB_IN), :] for c in range(ROW_SLAB)],
            axis=1).astype(bf16)

    @pl.when((e == 0) & (t + 1 < nvalid))
    def _():
        start_gathers(t + 1, other)

    @pl.when(t < nvalid)
    def _():
        x = xbuf[...]
        gte = jnp.dot(x, wg_ref[...], preferred_element_type=f32)
        up = jnp.dot(x, wu_ref[...], preferred_element_type=f32)
        cw = gbuf[pl.ds(slot * tm * SLAB_IN + ROW_SLAB + e, tm, stride=SLAB_IN), :]
        hid = (gte * _sigmoid(gte)) * up * jnp.concatenate([cw] * (D_EXPERT // LANES), axis=1)
        y = jnp.dot(hid.astype(bf16), wd_ref[...], preferred_element_type=f32)

        @pl.when(e == 0)
        def _():
            yacc[...] = y

        @pl.when(e > 0)
        def _():
            yacc[...] += y

    @pl.when((e == last_e) & (t < nvalid))
    def _():
        @pl.when(t >= 2)
        def _():
            wait_scatters(slot)
        base = slot * tm * ROW_SLAB
        for c in range(ROW_SLAB):
            ybuf[pl.ds(base + c, tm, stride=ROW_SLAB), :] = yacc[:, c * LANES:(c + 1) * LANES]
        start_scatters(t, slot)

        @pl.when(t == nvalid - 1)
        def _():
            @pl.when(t >= 1)
            def _():
                wait_scatters(other)
            wait_scatters(slot)

    @pl.when((e == last_e) & (t == nt - 1))
    def _():
        ybuf[0:tm * ROW_SLAB, :] = jnp.zeros((tm * ROW_SLAB, LANES), f32)
        for j in range(n_tok // tm, nt):
            @pl.when(j >= nvalid)
            def _():
                fill = pltpu.make_async_copy(
                    ybuf.at[0:tm * ROW_SLAB, :],
                    ytok_hbm.at[j * tm * ROW_SLAB:(j + 1) * tm * ROW_SLAB, :], ssem.at[0])
                fill.start()
                fill.wait()


def _moe(h2s, tables, wg, wu, wd, T, tm):
    nvalid, tile_g, dest, pads = tables
    nt = tile_g.shape[0]

    def w_index(t, e, nv, tg, de, pd):
        return (tg[t] * EXPERTS_PER_GROUP + jnp.where(t < nv[0], e, EXPERTS_PER_GROUP - 1), 0, 0)

    grid_spec = pltpu.PrefetchScalarGridSpec(
        num_scalar_prefetch=4,
        grid=(nt, EXPERTS_PER_GROUP),
        in_specs=[
            pl.BlockSpec((None, D_MODEL, D_EXPERT), w_index),
            pl.BlockSpec((None, D_MODEL, D_EXPERT), w_index),
            pl.BlockSpec((None, D_EXPERT, D_MODEL), w_index),
            pl.BlockSpec(memory_space=pl.ANY),
        ],
        out_specs=pl.BlockSpec(memory_space=pl.ANY),
        scratch_shapes=[
            pltpu.SMEM((nt * tm,), jnp.int32),
            pltpu.VMEM((2 * tm * SLAB_IN, LANES), f32),
            pltpu.VMEM((2 * tm * ROW_SLAB, LANES), f32),
            pltpu.VMEM((tm, D_MODEL), bf16),
            pltpu.VMEM((tm, D_MODEL), f32),
            pltpu.SemaphoreType.DMA((2,)),
            pltpu.SemaphoreType.DMA((2,)),
        ],
    )
    return pl.pallas_call(
        functools.partial(_moe_body, tm=tm, nt=nt, n_tok=T),
        grid_spec=grid_spec,
        out_shape=jax.ShapeDtypeStruct((nt * tm * ROW_SLAB, LANES), f32),
        compiler_params=pltpu.CompilerParams(
            dimension_semantics=("arbitrary", "arbitrary"),
            vmem_limit_bytes=VMEM_LIMIT),
        name="moe",
    )(nvalid, tile_g, dest, pads, wg, wu, wd, h2s)


def _combine_body(x1_ref, y_ref, o_ref):
    tm = x1_ref.shape[0]
    for c in range(ROW_SLAB):
        cols = slice(c * LANES, (c + 1) * LANES)
        o_ref[:, cols] = x1_ref[:, cols] + y_ref[pl.ds(c, tm, stride=ROW_SLAB), :]


def _combine(x1, ytok, tm=256):
    T = x1.shape[0]
    return pl.pallas_call(
        _combine_body,
        grid=(T // tm,),
        in_specs=[
            pl.BlockSpec((tm, D_MODEL), lambda i: (i, 0)),
            pl.BlockSpec((tm * ROW_SLAB, LANES), lambda i: (i, 0)),
        ],
        out_specs=pl.BlockSpec((tm, D_MODEL), lambda i: (i, 0)),
        out_shape=jax.ShapeDtypeStruct((T, D_MODEL), f32),
        compiler_params=pltpu.CompilerParams(
            dimension_semantics=("arbitrary",),
            vmem_limit_bytes=VMEM_LIMIT),
        name="combine",
    )(x1, ytok)


def _alibi_slopes(n):
    return jnp.exp2(-8.0 * jnp.arange(1, n + 1, dtype=f32) / n)


def kernel(x, g_mix, w_in, q_norm_swa, k_norm_swa, sinks, q_norm_moba, k_norm_moba,
           w_up_swa, w_up_moba, w_out, g_ffn, w_router_group, w_router_expert,
           w_gate_e, w_up_e, w_down_e):
    B, S, D = x.shape
    assert D == D_MODEL and S % MOBA_BLOCK == 0 and S % SWA_BLOCK == 0
    T = B * S
    x2 = x.reshape(T, D)

    proj = _inproj(x2, g_mix, w_in)

    oa = _swa(proj, sinks.astype(f32), _alibi_slopes(SWA_Q_HEADS), q_norm_swa, k_norm_swa, B, S)
    ob = _moba(proj, _alibi_slopes(MOBA_HEADS), q_norm_moba, k_norm_moba, B, S)

    wr = jnp.concatenate(
        [w_router_group,
         w_router_expert.transpose(1, 0, 2).reshape(D, N_EXPERTS),
         jnp.zeros((D, ROUTER_COLS - N_GROUPS - N_EXPERTS), f32)], axis=1)
    wr_hi = wr.astype(bf16)
    wr_lo = (wr - wr_hi.astype(f32)).astype(bf16)
    x1, h2s, gid = _mix(oa, ob, proj, x2, w_up_swa.astype(bf16), w_up_moba.astype(bf16),
                        w_out.astype(bf16), g_ffn, jnp.stack([wr_hi, wr_lo]))

    tables = _route_tables(gid, T, MOE_TILE)
    ytok = _moe(h2s, tables, w_gate_e.astype(bf16), w_up_e.astype(bf16), w_down_e.astype(bf16),
                T, MOE_TILE)
    y = _combine(x1, ytok)
    return y.reshape(B, S, D)
```

```python
import functools

import jax
import jax.numpy as jnp
from jax import lax
from jax.experimental import pallas as pl
from jax.experimental.pallas import tpu as pltpu

D_MODEL = 2048
HEAD_DIM = 64
ATTN_SCALE = HEAD_DIM ** -0.5
SWA_Q_HEADS = 16
SWA_KV_HEADS = 2
SWA_WINDOW = 128
SWA_BLOCK = 128
MOBA_HEADS = 16
MOBA_BLOCK = 256
MOBA_TOPK = 3
N_GROUPS = 4
EXPERTS_PER_GROUP = 4
N_EXPERTS = N_GROUPS * EXPERTS_PER_GROUP
D_EXPERT = 512
EPS = 1e-6

SWA_Q_DIM = SWA_Q_HEADS * HEAD_DIM
SWA_KV_DIM = SWA_KV_HEADS * HEAD_DIM
MOBA_DIM = MOBA_HEADS * HEAD_DIM
IN_COLS = SWA_Q_DIM + 2 * SWA_KV_DIM + 3 * MOBA_DIM + 2 * D_MODEL

LANES = 128
ROW_SLAB = D_MODEL // LANES
SLAB_IN = ROW_SLAB + 8
MOE_TILE = 512
VMEM_LIMIT = 56 * 1024 * 1024
NEG = -1e30

COL_QA = 0
COL_KA = COL_QA + SWA_Q_DIM // LANES
COL_VA = COL_KA + 1
COL_QB = COL_VA + 1
COL_KB = COL_QB + MOBA_DIM // LANES
COL_VB = COL_KB + MOBA_DIM // LANES
COL_GATE_A = COL_VB + MOBA_DIM // LANES
COL_GATE_B = COL_GATE_A + D_MODEL // LANES
GATE_BLOCK = 2 * LANES
assert (COL_GATE_A * LANES) % GATE_BLOCK == 0 and (COL_GATE_B * LANES) % GATE_BLOCK == 0

bf16 = jnp.bfloat16
f32 = jnp.float32


def _sigmoid(x):
    return 1.0 / (1.0 + jnp.exp(-x))


def _pair_rms(x, gain):
    lane = lax.broadcasted_iota(jnp.int32, x.shape, 1)
    lo = lane < HEAD_DIM
    sq = x * x
    s0 = jnp.sum(jnp.where(lo, sq, 0.0), axis=-1, keepdims=True)
    s1 = jnp.sum(jnp.where(lo, 0.0, sq), axis=-1, keepdims=True)
    r0 = lax.rsqrt(s0 * (1.0 / HEAD_DIM) + EPS)
    r1 = lax.rsqrt(s1 * (1.0 / HEAD_DIM) + EPS)
    return x * jnp.where(lo, r0, r1) * gain


def _dot_nt(a, b):
    return lax.dot_general(a, b, (((1,), (1,)), ((), ())), preferred_element_type=f32)


def _split2(x):
    hi = x.astype(bf16)
    lo = (x - hi.astype(f32)).astype(bf16)
    return hi, lo


CAST_CHUNK = 2 * LANES
NORM_ROWS = 512


def _inproj_body(x_ref, g_ref, w_ref, o_ref, h_ref):
    @pl.when(pl.program_id(1) == 0)
    def _():
        for r in range(0, x_ref.shape[0], NORM_ROWS):
            x = x_ref[r:r + NORM_ROWS, :]
            ms = jnp.mean(x * x, axis=-1, keepdims=True)
            h_ref[r:r + NORM_ROWS, :] = (x * lax.rsqrt(ms + EPS) * g_ref[...]).astype(bf16)

    n_chunks = w_ref.shape[1] // CAST_CHUNK
    cast = lambda c: w_ref[:, c * CAST_CHUNK:(c + 1) * CAST_CHUNK].astype(bf16)
    w_next = cast(0)
    for c in range(n_chunks):
        w_cur = w_next
        if c + 1 < n_chunks:
            w_next = cast(c + 1)
        o_ref[:, c * CAST_CHUNK:(c + 1) * CAST_CHUNK] = jnp.dot(
            h_ref[...], w_cur, preferred_element_type=f32)


def _inproj(x2, g, w, tm=2048, tn=768):
    T = x2.shape[0]
    N = w.shape[1]
    return pl.pallas_call(
        _inproj_body,
        grid=(T // tm, N // tn),
        in_specs=[
            pl.BlockSpec((tm, D_MODEL), lambda i, j: (i, 0), pipeline_mode=pl.Buffered(1)),
            pl.BlockSpec((1, D_MODEL), lambda i, j: (0, 0)),
            pl.BlockSpec((D_MODEL, tn), lambda i, j: (0, j)),
        ],
        out_specs=pl.BlockSpec((tm, tn), lambda i, j: (i, j)),
        out_shape=jax.ShapeDtypeStruct((T, N), f32),
        scratch_shapes=[pltpu.VMEM((tm, D_MODEL), bf16)],
        compiler_params=pltpu.CompilerParams(
            dimension_semantics=("arbitrary", "arbitrary"),
            vmem_limit_bytes=VMEM_LIMIT),
        name="inproj",
    )(x2, g.reshape(1, D_MODEL), w)


SWA_AHEAD = 2


def _swa_body(sinks_ref, slopes_ref, q_ref, kp_ref, kc_ref, vp_ref, vc_ref,
              qn_ref, kn_ref, o_ref):
    L = SWA_BLOCK
    n = pl.program_id(1)
    k2 = jnp.concatenate([kp_ref[...], kc_ref[...]], axis=0)
    k2n = _pair_rms(k2, kn_ref[...]).astype(bf16)
    v2t = jnp.concatenate([vp_ref[...], vc_ref[...]], axis=0).T.astype(bf16)
    qn = jnp.concatenate(
        [_pair_rms(q_ref[:, pp * LANES:(pp + 1) * LANES], qn_ref[...])
         for pp in range(SWA_Q_HEADS // 2)], axis=1) * ATTN_SCALE
    qt = qn.T

    key = lax.broadcasted_iota(jnp.int32, (2 * L, L), 0)
    qry = lax.broadcasted_iota(jnp.int32, (2 * L, L), 1)
    dist = qry + L - key
    ok = (dist >= 0) & (dist < SWA_WINDOW) & ((n > 0) | (key >= L))
    distf = dist.astype(f32)
    zeros = jnp.zeros((HEAD_DIM, L), f32)

    heads_per_kv = SWA_Q_HEADS // SWA_KV_HEADS

    def scores(h):
        g = h // heads_per_kv
        qh = qt[h * HEAD_DIM:(h + 1) * HEAD_DIM, :]
        qa = jnp.concatenate([qh, zeros] if g == 0 else [zeros, qh], axis=0).astype(bf16)
        return jnp.dot(k2n, qa, preferred_element_type=f32)

    def softmax(h, s):
        s = jnp.where(ok, s - slopes_ref[h] * distf, -jnp.inf)
        sink = sinks_ref[h]
        m = jnp.maximum(jnp.max(s, axis=0, keepdims=True), sink)
        p = jnp.exp(s - m)
        denom = jnp.sum(p, axis=0, keepdims=True) + jnp.exp(sink - m)
        return p.astype(bf16), denom

    outs = []

    def finish(h, p, denom):
        g = h // heads_per_kv
        o = jnp.dot(v2t, p, preferred_element_type=f32)
        outs.append(o[g * HEAD_DIM:(g + 1) * HEAD_DIM, :] / denom)

    pending = [scores(h) for h in range(SWA_AHEAD)]
    deferred = None
    for h in range(SWA_Q_HEADS):
        s = pending.pop(0)
        if h + SWA_AHEAD < SWA_Q_HEADS:
            pending.append(scores(h + SWA_AHEAD))
        p, denom = softmax(h, s)
        if deferred is not None:
            finish(*deferred)
        deferred = (h, p, denom)
    finish(*deferred)
    o_ref[...] = jnp.concatenate(outs, axis=0).T.astype(bf16)


def _swa(proj, sinks, slopes, q_norm, k_norm, B, S):
    L = SWA_BLOCK
    nb = S // L
    T = B * S
    smem = pl.BlockSpec(memory_space=pltpu.SMEM)

    def prev(b, n):
        return b * nb + jnp.maximum(n - 1, 0)

    return pl.pallas_call(
        _swa_body,
        grid=(B, nb),
        in_specs=[
            smem, smem,
            pl.BlockSpec((L, SWA_Q_DIM), lambda b, n: (b * nb + n, COL_QA // (SWA_Q_DIM // LANES))),
            pl.BlockSpec((L, LANES), lambda b, n: (prev(b, n), COL_KA)),
            pl.BlockSpec((L, LANES), lambda b, n: (b * nb + n, COL_KA)),
            pl.BlockSpec((L, LANES), lambda b, n: (prev(b, n), COL_VA)),
            pl.BlockSpec((L, LANES), lambda b, n: (b * nb + n, COL_VA)),
            pl.BlockSpec((1, LANES), lambda b, n: (0, 0)),
            pl.BlockSpec((1, LANES), lambda b, n: (0, 0)),
        ],
        out_specs=pl.BlockSpec((L, SWA_Q_DIM), lambda b, n: (b * nb + n, 0)),
        out_shape=jax.ShapeDtypeStruct((T, SWA_Q_DIM), bf16),
        compiler_params=pltpu.CompilerParams(
            dimension_semantics=("arbitrary", "arbitrary"),
            vmem_limit_bytes=VMEM_LIMIT),
        name="swa",
    )(sinks, slopes, proj, proj, proj, proj, proj,
      jnp.tile(q_norm, 2).reshape(1, LANES), jnp.tile(k_norm, 2).reshape(1, LANES))


N_BIAS_PARTS = 3
NB_ROWS = 8
N_ALIBI_COL = NB_ROWS * N_BIAS_PARTS
SCORE_AHEAD = 3


def _split3(x):
    p0 = x.astype(bf16).astype(f32)
    r1 = x - p0
    p1 = r1.astype(bf16).astype(f32)
    p2 = (r1 - p1).astype(bf16).astype(f32)
    return p0, p1, p2


def _moba_body(slopes_ref, q_ref, k_ref, v_ref, qn_ref, kn_ref, o_ref, kaug_ref, vt_ref, *, nb):
    L = MOBA_BLOCK
    S = nb * L
    p = pl.program_id(1)

    kn = _pair_rms(k_ref[...], kn_ref[...])
    lane = lax.broadcasted_iota(jnp.int32, (S, LANES), 1)
    krow = lax.broadcasted_iota(jnp.int32, (S, LANES), 0)
    kblk = krow // L
    kpos = (krow % L).astype(f32)
    lane_lo = lane < HEAD_DIM

    def aug_cols(cc, slope):
        hot = ((cc >= 0) & (cc < N_ALIBI_COL) & ((cc % NB_ROWS) == kblk)).astype(f32)
        a0, a1, a2 = _split3(slope * kpos)
        return jnp.where(cc == N_ALIBI_COL, a0,
                         jnp.where(cc == N_ALIBI_COL + 1, a1,
                                   jnp.where(cc == N_ALIBI_COL + 2, a2, hot)))

    kaug_ref[0] = jnp.where(lane_lo, kn, aug_cols(lane - HEAD_DIM, slopes_ref[2 * p])).astype(bf16)
    kaug_ref[1] = jnp.where(lane_lo, aug_cols(lane, slopes_ref[2 * p + 1]), kn).astype(bf16)
    vt_ref[...] = v_ref[...].T.astype(bf16)
    means = [jnp.mean(kn[n * L:(n + 1) * L, :], axis=0, keepdims=True) for n in range(nb)]
    means += [jnp.zeros((1, LANES), f32)] * (NB_ROWS - nb)
    km = jnp.concatenate(means, axis=0)
    lane8 = lax.broadcasted_iota(jnp.int32, (NB_ROWS, LANES), 1)
    km_pair = jnp.concatenate(
        [jnp.where(lane8 < HEAD_DIM, km, 0.0), jnp.where(lane8 < HEAD_DIM, 0.0, km)], axis=0)
    km_hi, km_lo = _split2(km_pair)

    qs = _pair_rms(q_ref[...], qn_ref[...]) * ATTN_SCALE
    qt = qs.T
    qt_hi, qt_lo = _split2(qt)
    gate = (jnp.dot(km_hi, qt_hi, preferred_element_type=f32)
            + (jnp.dot(km_hi, qt_lo, preferred_element_type=f32)
               + jnp.dot(km_lo, qt_hi, preferred_element_type=f32)))

    blk = lax.broadcasted_iota(jnp.int32, (NB_ROWS, S), 0)
    qblk = lax.broadcasted_iota(jnp.int32, (NB_ROWS, S), 1) // L
    past = blk < qblk
    r = lax.broadcasted_iota(jnp.int32, (L, L), 1)
    c = lax.broadcasted_iota(jnp.int32, (L, L), 0)
    causal = jnp.where(r >= c, 0.0, NEG)
    ones_rows = (blk < 3).astype(f32)

    qa = []
    for hh in range(2):
        g = jnp.where(past, gate[hh * NB_ROWS:(hh + 1) * NB_ROWS, :], -jnp.inf)
        rank = jnp.zeros((NB_ROWS, S), jnp.int32)
        for m in range(nb):
            gm = g[m:m + 1, :]
            ahead = (gm > g) | ((gm == g) & (m < blk))
            rank = rank + ahead.astype(jnp.int32)
        sel = past & (rank < MOBA_TOPK)
        slope = slopes_ref[2 * p + hh]
        bias = jnp.where(sel, (-slope * L) * (qblk - blk).astype(f32), NEG)
        bias = jnp.where(blk == qblk, 0.0, bias)
        b0, b1, b2 = _split3(bias)
        extra = jnp.concatenate(
            [b0, b1, b2, ones_rows,
             jnp.zeros((HEAD_DIM - N_ALIBI_COL - NB_ROWS, S), f32)], axis=0)
        if hh == 0:
            qa.append(jnp.concatenate([qt[:HEAD_DIM], extra], axis=0).astype(bf16))
        else:
            qa.append(jnp.concatenate([extra, qt[HEAD_DIM:]], axis=0).astype(bf16))

    def scores(i, hh):
        return jnp.dot(kaug_ref[hh, 0:(i + 1) * L, :], qa[hh][:, i * L:(i + 1) * L],
                       preferred_element_type=f32)

    units = [(i, hh) for i in range(nb) for hh in range(2)]
    pending = [scores(*u) for u in units[:SCORE_AHEAD]]
    outs = {}

    def finish(i, hh, e_all, l):
        acc = jnp.dot(vt_ref[hh * HEAD_DIM:(hh + 1) * HEAD_DIM, 0:(i + 1) * L], e_all,
                      preferred_element_type=f32)
        outs[(i, hh)] = acc / l
        if hh == 1:
            o_ref[i * L:(i + 1) * L, :] = jnp.concatenate(
                [outs.pop((i, 0)), outs.pop((i, 1))], axis=0).T.astype(bf16)

    deferred = None
    for idx, (i, hh) in enumerate(units):
        s = pending.pop(0)
        if idx + SCORE_AHEAD < len(units):
            pending.append(scores(*units[idx + SCORE_AHEAD]))
        tiles = [s[n * L:(n + 1) * L, :] for n in range(i)] + [s[i * L:(i + 1) * L, :] + causal]
        m = functools.reduce(jnp.maximum, [jnp.max(t, axis=0, keepdims=True) for t in tiles])
        es = [jnp.exp(t - m) for t in tiles]
        l = functools.reduce(lambda a, b: a + b, [jnp.sum(e, axis=0, keepdims=True) for e in es])
        e_all = jnp.concatenate([e.astype(bf16) for e in es], axis=0)
        if deferred is not None:
            finish(*deferred)
        deferred = (i, hh, e_all, l)
    finish(*deferred)


def _moba(proj, slopes, q_norm, k_norm, B, S):
    L = MOBA_BLOCK
    nb = S // L
    assert nb <= NB_ROWS
    T = B * S
    n_pairs = MOBA_HEADS // 2
    return pl.pallas_call(
        functools.partial(_moba_body, nb=nb),
        grid=(B, n_pairs),
        in_specs=[
            pl.BlockSpec(memory_space=pltpu.SMEM),
            pl.BlockSpec((S, LANES), lambda b, p: (b, COL_QB + p)),
            pl.BlockSpec((S, LANES), lambda b, p: (b, COL_KB + p)),
            pl.BlockSpec((S, LANES), lambda b, p: (b, COL_VB + p)),
            pl.BlockSpec((1, LANES), lambda b, p: (0, 0)),
            pl.BlockSpec((1, LANES), lambda b, p: (0, 0)),
        ],
        out_specs=pl.BlockSpec((S, LANES), lambda b, p: (b, p)),
        out_shape=jax.ShapeDtypeStruct((T, MOBA_DIM), bf16),
        scratch_shapes=[
            pltpu.VMEM((2, S, LANES), bf16),
            pltpu.VMEM((LANES, S), bf16),
        ],
        compiler_params=pltpu.CompilerParams(
            dimension_semantics=("arbitrary", "arbitrary"),
            vmem_limit_bytes=VMEM_LIMIT),
        name="moba",
    )(slopes, proj, proj, proj,
      jnp.tile(q_norm, 2).reshape(1, LANES), jnp.tile(k_norm, 2).reshape(1, LANES))


ROUTER_COLS = LANES


N_GATE_BLOCKS = D_MODEL // GATE_BLOCK
MIX_CHUNK = 2 * GATE_BLOCK


def _mix_body(*refs):
    ga_refs = refs[:N_GATE_BLOCKS]
    gb_refs = refs[N_GATE_BLOCKS:2 * N_GATE_BLOCKS]
    (oa_ref, ob_ref, x_ref, wua_ref, wub_ref, wout_ref, gffn_ref, wr_ref,
     x1_ref, h2_ref, gid_ref) = refs[2 * N_GATE_BLOCKS:]
    oa = oa_ref[...]
    ob = ob_ref[...]
    gpc = MIX_CHUNK // GATE_BLOCK

    def up(c):
        cols = slice(c * MIX_CHUNK, (c + 1) * MIX_CHUNK)
        return (jnp.dot(oa, wua_ref[:, cols], preferred_element_type=f32),
                jnp.dot(ob, wub_ref[:, cols], preferred_element_type=f32))

    x1 = x_ref[...]
    pending = up(0)
    for c in range(D_MODEL // MIX_CHUNK):
        ya, yb = pending
        if (c + 1) * MIX_CHUNK < D_MODEL:
            pending = up(c + 1)
        gate_a = jnp.concatenate([r[...] for r in ga_refs[c * gpc:(c + 1) * gpc]], axis=1)
        gate_b = jnp.concatenate([r[...] for r in gb_refs[c * gpc:(c + 1) * gpc]], axis=1)
        merged = (_sigmoid(gate_a) * ya + _sigmoid(gate_b) * yb).astype(bf16)
        x1 = x1 + jnp.dot(merged, wout_ref[c * MIX_CHUNK:(c + 1) * MIX_CHUNK, :],
                          preferred_element_type=f32)
    x1_ref[...] = x1
    ms = jnp.mean(x1 * x1, axis=-1, keepdims=True)
    h2 = x1 * lax.rsqrt(ms + EPS) * gffn_ref[...]
    tm = h2.shape[0]
    for c in range(ROW_SLAB):
        h2_ref[pl.ds(c, tm, stride=SLAB_IN), :] = h2[:, c * LANES:(c + 1) * LANES]

    h_hi, h_lo = _split2(h2)
    lt = (jnp.dot(h_hi, wr_ref[0], preferred_element_type=f32)
          + (jnp.dot(h_lo, wr_ref[0], preferred_element_type=f32)
             + jnp.dot(h_hi, wr_ref[1], preferred_element_type=f32))).T
    gl = [lt[g:g + 1, :] for g in range(N_GROUPS)]
    gmax = functools.reduce(jnp.maximum, gl)
    gsum = functools.reduce(lambda a, b: a + b, [jnp.exp(v - gmax) for v in gl])
    g_p = 1.0 / gsum
    g_i = jnp.full((1, tm), N_GROUPS - 1, jnp.int32)
    for g in reversed(range(N_GROUPS)):
        g_i = jnp.where(gl[g] == gmax, g, g_i)

    el = []
    for e in range(EXPERTS_PER_GROUP):
        v = jnp.zeros((1, tm), f32)
        for g in range(N_GROUPS):
            r = N_GROUPS + g * EXPERTS_PER_GROUP + e
            v = jnp.where(g_i == g, lt[r:r + 1, :], v)
        el.append(v)
    emax = functools.reduce(jnp.maximum, el)
    ex = [jnp.exp(v - emax) for v in el]
    esum = functools.reduce(lambda a, b: a + b, ex)
    ep = [v / esum for v in ex]
    p1 = functools.reduce(jnp.maximum, ep)
    i1 = jnp.full((1, tm), EXPERTS_PER_GROUP - 1, jnp.int32)
    for e in reversed(range(EXPERTS_PER_GROUP)):
        i1 = jnp.where(ep[e] == p1, e, i1)
    rest = [jnp.where(i1 == e, -1.0, ep[e]) for e in range(EXPERTS_PER_GROUP)]
    p2 = functools.reduce(jnp.maximum, rest)
    i2 = jnp.full((1, tm), EXPERTS_PER_GROUP - 1, jnp.int32)
    for e in reversed(range(EXPERTS_PER_GROUP)):
        i2 = jnp.where(rest[e] == p2, e, i2)
    w1 = g_p * (p1 / (p1 + p2))
    w2 = g_p * (p2 / (p1 + p2))
    gid_ref[...] = jnp.concatenate([g_i, jnp.zeros((7, tm), jnp.int32)], axis=0)
    cw = [jnp.where(i1 == e, w1, 0.0) + jnp.where(i2 == e, w2, 0.0) for e in range(EXPERTS_PER_GROUP)]
    cw_t = jnp.concatenate(cw + [jnp.zeros((LANES - EXPERTS_PER_GROUP, tm), f32)], axis=0).T
    for e in range(SLAB_IN - ROW_SLAB):
        if e < EXPERTS_PER_GROUP:
            row = jnp.broadcast_to(cw_t[:, e:e + 1], (tm, LANES))
        else:
            row = jnp.zeros((tm, LANES), f32)
        h2_ref[pl.ds(ROW_SLAB + e, tm, stride=SLAB_IN), :] = row


def _mix(oa, ob, proj, x2, wua, wub, wout, g_ffn, wr, tm=256):
    T = x2.shape[0]
    const = lambda i: (0, 0)
    single = pl.Buffered(1)

    def gate_specs(first_col):
        first = first_col * LANES // GATE_BLOCK
        return [pl.BlockSpec((tm, GATE_BLOCK), functools.partial(lambda i, j: (i, j), j=first + k))
                for k in range(N_GATE_BLOCKS)]

    return pl.pallas_call(
        _mix_body,
        grid=(T // tm,),
        in_specs=gate_specs(COL_GATE_A) + gate_specs(COL_GATE_B) + [
            pl.BlockSpec((tm, SWA_Q_DIM), lambda i: (i, 0)),
            pl.BlockSpec((tm, MOBA_DIM), lambda i: (i, 0)),
            pl.BlockSpec((tm, D_MODEL), lambda i: (i, 0)),
            pl.BlockSpec((SWA_Q_DIM, D_MODEL), const, pipeline_mode=single),
            pl.BlockSpec((MOBA_DIM, D_MODEL), const, pipeline_mode=single),
            pl.BlockSpec((D_MODEL, D_MODEL), const, pipeline_mode=single),
            pl.BlockSpec((1, D_MODEL), const),
            pl.BlockSpec((2, D_MODEL, ROUTER_COLS), lambda i: (0, 0, 0)),
        ],
        out_specs=[
            pl.BlockSpec((tm, D_MODEL), lambda i: (i, 0)),
            pl.BlockSpec((tm * SLAB_IN, LANES), lambda i: (i, 0)),
            pl.BlockSpec((8, tm), lambda i: (0, i)),
        ],
        out_shape=[
            jax.ShapeDtypeStruct((T, D_MODEL), f32),
            jax.ShapeDtypeStruct((T * SLAB_IN, LANES), f32),
            jax.ShapeDtypeStruct((8, T), jnp.int32),
        ],
        compiler_params=pltpu.CompilerParams(
            dimension_semantics=("arbitrary",),
            vmem_limit_bytes=VMEM_LIMIT),
        name="mix",
    )(*([proj] * (2 * N_GATE_BLOCKS)), oa, ob, x2, wua, wub, wout, g_ffn.reshape(1, D_MODEL), wr)


def _route_tables(gid, T, tm):
    nt = T // tm + N_GROUPS
    g = gid[0]
    onehot = (g[:, None] == jnp.arange(N_GROUPS, dtype=jnp.int32)[None, :]).astype(jnp.int32)
    incl = jnp.cumsum(onehot, axis=0)
    pos = jnp.sum(onehot * incl, axis=1) - 1
    counts = incl[-1]
    padded = (counts + tm - 1) // tm * tm
    seg_end = jnp.cumsum(padded)
    seg_start = seg_end - padded
    dest = jnp.sum(onehot * seg_start[None, :], axis=1) + pos
    tile_start = jnp.arange(nt, dtype=jnp.int32) * tm
    tile_g = jnp.sum((tile_start[:, None] >= seg_end[None, :]).astype(jnp.int32), axis=1)
    tile_g = jnp.minimum(tile_g, N_GROUPS - 1)
    nvalid = (seg_end[-1] // tm).astype(jnp.int32).reshape(1)
    n_pad = padded - counts
    pads = jnp.concatenate([seg_start + counts, seg_end, jnp.cumsum(n_pad) - n_pad])
    return (nvalid, tile_g.astype(jnp.int32), dest.astype(jnp.int32), pads.astype(jnp.int32))


def _moe_body(nvalid_ref, tg_ref, dest_ref, pads_ref, wg_ref, wu_ref, wd_ref, h2s_hbm, ytok_hbm,
              asg, gbuf, ybuf, xbuf, hbuf, gsem, ssem, *, tm, nt, n_tok):
    t = pl.program_id(0)
    e = pl.program_id(1)
    last_e = EXPERTS_PER_GROUP - 1
    nvalid = nvalid_ref[0]
    slot = lax.rem(t, 2)
    other = 1 - slot

    def for_rows(fn):
        def body(r, carry):
            fn(r)
            return carry
        lax.fori_loop(0, tm, body, 0, unroll=8)

    def gather_copy(tok, s, r):
        return pltpu.make_async_copy(
            h2s_hbm.at[pl.ds(pl.multiple_of(tok * SLAB_IN, 8), SLAB_IN), :],
            gbuf.at[pl.ds(pl.multiple_of((s * tm + r) * SLAB_IN, 8), SLAB_IN), :], gsem.at[s])

    def scatter_copy(row, s, r):
        return pltpu.make_async_copy(
            ybuf.at[pl.ds(pl.multiple_of((s * tm + r) * ROW_SLAB, 8), ROW_SLAB), :],
            ytok_hbm.at[pl.ds(pl.multiple_of(row * ROW_SLAB, 8), ROW_SLAB), :], ssem.at[s])

    def start_gathers(tile, s):
        for_rows(lambda r: gather_copy(jnp.maximum(asg[tile * tm + r], 0), s, r).start())

    def start_scatters(tile, s):
        def one(r):
            a = asg[tile * tm + r]
            scatter_copy(jnp.where(a >= 0, a, n_tok - 1 - a), s, r).start()
        for_rows(one)

    def wait_gathers(s):
        for_rows(lambda r: gather_copy(0, s, r).wait())

    def wait_scatters(s):
        for_rows(lambda r: scatter_copy(0, s, r).wait())

    @pl.when((t == 0) & (e == 0))
    def _():
        def place(tok, carry):
            asg[dest_ref[tok]] = tok
            return carry
        lax.fori_loop(0, n_tok, place, 0, unroll=8)

        for g in range(N_GROUPS):
            first = pads_ref[g]
            code = -1 - pads_ref[2 * N_GROUPS + g] + first

            def mark(p, carry):
                asg[p] = code - p
                return carry
            lax.fori_loop(first, pads_ref[N_GROUPS + g], mark, 0)
        start_gathers(0, 0)

    @pl.when((e == 0) & (t < nvalid))
    def _():
        wait_gathers(slot)
        base = slot * tm * SLAB_IN
        xbuf[...] = jnp.concatenate(
            [gbuf[pl.ds(base + c, tm, stride=SLAB_IN), :] for c in range(ROW_SLAB)],
            axis=1).astype(bf16)

    @pl.when(t < nvalid)
    def _():
        x = xbuf[...]
        gte = jnp.dot(x, wg_ref[...], preferred_element_type=f32)
        up = jnp.dot(x, wu_ref[...], preferred_element_type=f32)
        cw = gbuf[pl.ds(slot * tm * SLAB_IN + ROW_SLAB + e, tm, stride=SLAB_IN), :]
        hid = (gte * _sigmoid(gte)) * up * jnp.concatenate([cw] * (D_EXPERT // LANES), axis=1)
        hbuf[e] = hid.astype(bf16)
        quarter = tm // EXPERTS_PER_GROUP
        have_next = t + 1 < nvalid
        for r in range(quarter):
            row = e * quarter + r
            tok = jnp.where(have_next, jnp.maximum(asg[(t + 1) * tm + row], 0), 0)
            gather_copy(tok, other, row).start()

    @pl.when((e == last_e) & (t < nvalid))
    def _():
        hcat = jnp.concatenate([hbuf[k] for k in range(EXPERTS_PER_GROUP)], axis=1)
        y = jnp.dot(hcat, wd_ref[...], preferred_element_type=f32)

        @pl.when(t >= 2)
        def _():
            wait_scatters(slot)
        base = slot * tm * ROW_SLAB
        for c in range(ROW_SLAB):
            ybuf[pl.ds(base + c, tm, stride=ROW_SLAB), :] = y[:, c * LANES:(c + 1) * LANES]
        start_scatters(t, slot)

        @pl.when(t == nvalid - 1)
        def _():
            wait_gathers(other)
            @pl.when(t >= 1)
            def _():
                wait_scatters(other)
            wait_scatters(slot)

    @pl.when((e == last_e) & (t == nt - 1))
    def _():
        ybuf[0:tm * ROW_SLAB, :] = jnp.zeros((tm * ROW_SLAB, LANES), f32)
        for j in range(n_tok // tm, nt):
            @pl.when(j >= nvalid)
            def _():
                fill = pltpu.make_async_copy(
                    ybuf.at[0:tm * ROW_SLAB, :],
                    ytok_hbm.at[j * tm * ROW_SLAB:(j + 1) * tm * ROW_SLAB, :], ssem.at[0])
                fill.start()
                fill.wait()


def _moe(h2s, tables, wg, wu, wd, T, tm):
    nvalid, tile_g, dest, pads = tables
    nt = tile_g.shape[0]

    def w_index(t, e, nv, tg, de, pd):
        return (tg[t] * EXPERTS_PER_GROUP + jnp.where(t < nv[0], e, EXPERTS_PER_GROUP - 1), 0, 0)

    grid_spec = pltpu.PrefetchScalarGridSpec(
        num_scalar_prefetch=4,
        grid=(nt, EXPERTS_PER_GROUP),
        in_specs=[
            pl.BlockSpec((None, D_MODEL, D_EXPERT), w_index),
            pl.BlockSpec((None, D_MODEL, D_EXPERT), w_index),
            pl.BlockSpec((None, EXPERTS_PER_GROUP * D_EXPERT, D_MODEL),
                         lambda t, e, nv, tg, de, pd: (tg[t], 0, 0)),
            pl.BlockSpec(memory_space=pl.ANY),
        ],
        out_specs=pl.BlockSpec(memory_space=pl.ANY),
        scratch_shapes=[
            pltpu.SMEM(((nt + 1) * tm,), jnp.int32),
            pltpu.VMEM((2 * tm * SLAB_IN, LANES), f32),
            pltpu.VMEM((2 * tm * ROW_SLAB, LANES), f32),
            pltpu.VMEM((tm, D_MODEL), bf16),
            pltpu.VMEM((EXPERTS_PER_GROUP, tm, D_EXPERT), bf16),
            pltpu.SemaphoreType.DMA((2,)),
            pltpu.SemaphoreType.DMA((2,)),
        ],
    )
    return pl.pallas_call(
        functools.partial(_moe_body, tm=tm, nt=nt, n_tok=T),
        grid_spec=grid_spec,
        out_shape=jax.ShapeDtypeStruct((nt * tm * ROW_SLAB, LANES), f32),
        compiler_params=pltpu.CompilerParams(
            dimension_semantics=("arbitrary", "arbitrary"),
            vmem_limit_bytes=VMEM_LIMIT),
        name="moe",
    )(nvalid, tile_g, dest, pads, wg, wu,
      wd.reshape(N_GROUPS, EXPERTS_PER_GROUP * D_EXPERT, D_MODEL), h2s)


def _combine_body(x1_ref, y_ref, o_ref):
    tm = x1_ref.shape[0]
    for c in range(ROW_SLAB):
        cols = slice(c * LANES, (c + 1) * LANES)
        o_ref[:, cols] = x1_ref[:, cols] + y_ref[pl.ds(c, tm, stride=ROW_SLAB), :]


def _combine(x1, ytok, tm=256):
    T = x1.shape[0]
    return pl.pallas_call(
        _combine_body,
        grid=(T // tm,),
        in_specs=[
            pl.BlockSpec((tm, D_MODEL), lambda i: (i, 0)),
            pl.BlockSpec((tm * ROW_SLAB, LANES), lambda i: (i, 0)),
        ],
        out_specs=pl.BlockSpec((tm, D_MODEL), lambda i: (i, 0)),
        out_shape=jax.ShapeDtypeStruct((T, D_MODEL), f32),
        compiler_params=pltpu.CompilerParams(
            dimension_semantics=("arbitrary",),
            vmem_limit_bytes=VMEM_LIMIT),
        name="combine",
    )(x1, ytok)


def _alibi_slopes(n):
    return jnp.exp2(-8.0 * jnp.arange(1, n + 1, dtype=f32) / n)


def kernel(x, g_mix, w_in, q_norm_swa, k_norm_swa, sinks, q_norm_moba, k_norm_moba,
           w_up_swa, w_up_moba, w_out, g_ffn, w_router_group, w_router_expert,
           w_gate_e, w_up_e, w_down_e):
    B, S, D = x.shape
    assert D == D_MODEL and S % MOBA_BLOCK == 0 and S % SWA_BLOCK == 0
    T = B * S
    x2 = x.reshape(T, D)

    proj = _inproj(x2, g_mix, w_in)

    oa = _swa(proj, sinks.astype(f32), _alibi_slopes(SWA_Q_HEADS), q_norm_swa, k_norm_swa, B, S)
    ob = _moba(proj, _alibi_slopes(MOBA_HEADS), q_norm_moba, k_norm_moba, B, S)

    wr = jnp.concatenate(
        [w_router_group,
         w_router_expert.transpose(1, 0, 2).reshape(D, N_EXPERTS),
         jnp.zeros((D, ROUTER_COLS - N_GROUPS - N_EXPERTS), f32)], axis=1)
    wr_hi = wr.astype(bf16)
    wr_lo = (wr - wr_hi.astype(f32)).astype(bf16)
    x1, h2s, gid = _mix(oa, ob, proj, x2, w_up_swa.astype(bf16), w_up_moba.astype(bf16),
                        w_out.astype(bf16), g_ffn, jnp.stack([wr_hi, wr_lo]))

    tables = _route_tables(gid, T, MOE_TILE)
    ytok = _moe(h2s, tables, w_gate_e.astype(bf16), w_up_e.astype(bf16), w_down_e.astype(bf16),
                T, MOE_TILE)
    y = _combine(x1, ytok)
    return y.reshape(B, S, D)
```

```python
import functools

import jax
import jax.numpy as jnp
from jax import lax
from jax.experimental import pallas as pl
from jax.experimental.pallas import tpu as pltpu

D_MODEL = 2048
HEAD_DIM = 64
ATTN_SCALE = HEAD_DIM ** -0.5
SWA_Q_HEADS = 16
SWA_KV_HEADS = 2
SWA_WINDOW = 128
SWA_BLOCK = 128
MOBA_HEADS = 16
MOBA_BLOCK = 256
MOBA_TOPK = 3
N_GROUPS = 4
EXPERTS_PER_GROUP = 4
N_EXPERTS = N_GROUPS * EXPERTS_PER_GROUP
D_EXPERT = 512
EPS = 1e-6

SWA_Q_DIM = SWA_Q_HEADS * HEAD_DIM
SWA_KV_DIM = SWA_KV_HEADS * HEAD_DIM
MOBA_DIM = MOBA_HEADS * HEAD_DIM
IN_COLS = SWA_Q_DIM + 2 * SWA_KV_DIM + 3 * MOBA_DIM + 2 * D_MODEL

LANES = 128
ROW_SLAB = D_MODEL // LANES
SLAB_IN = ROW_SLAB + 8
MOE_TILE = 512
VMEM_LIMIT = 56 * 1024 * 1024
NEG = -1e30

COL_QA = 0
COL_KA = COL_QA + SWA_Q_DIM // LANES
COL_VA = COL_KA + 1
COL_QB = COL_VA + 1
COL_KB = COL_QB + MOBA_DIM // LANES
COL_VB = COL_KB + MOBA_DIM // LANES
COL_GATE_A = COL_VB + MOBA_DIM // LANES
COL_GATE_B = COL_GATE_A + D_MODEL // LANES
GATE_BLOCK = 2 * LANES
assert (COL_GATE_A * LANES) % GATE_BLOCK == 0 and (COL_GATE_B * LANES) % GATE_BLOCK == 0

bf16 = jnp.bfloat16
f32 = jnp.float32


def _sigmoid(x):
    return 1.0 / (1.0 + jnp.exp(-x))


def _pair_rms(x, gain):
    lane = lax.broadcasted_iota(jnp.int32, x.shape, 1)
    lo = lane < HEAD_DIM
    sq = x * x
    s0 = jnp.sum(jnp.where(lo, sq, 0.0), axis=-1, keepdims=True)
    s1 = jnp.sum(jnp.where(lo, 0.0, sq), axis=-1, keepdims=True)
    r0 = lax.rsqrt(s0 * (1.0 / HEAD_DIM) + EPS)
    r1 = lax.rsqrt(s1 * (1.0 / HEAD_DIM) + EPS)
    return x * jnp.where(lo, r0, r1) * gain


def _dot_nt(a, b):
    return lax.dot_general(a, b, (((1,), (1,)), ((), ())), preferred_element_type=f32)


def _split2(x):
    hi = x.astype(bf16)
    lo = (x - hi.astype(f32)).astype(bf16)
    return hi, lo


CAST_CHUNK = 2 * LANES
NORM_ROWS = 512


def _inproj_body(x_ref, g_ref, w_ref, o_ref, h_ref):
    @pl.when(pl.program_id(1) == 0)
    def _():
        for r in range(0, x_ref.shape[0], NORM_ROWS):
            x = x_ref[r:r + NORM_ROWS, :]
            ms = jnp.mean(x * x, axis=-1, keepdims=True)
            h_ref[r:r + NORM_ROWS, :] = (x * lax.rsqrt(ms + EPS) * g_ref[...]).astype(bf16)

    n_chunks = w_ref.shape[1] // CAST_CHUNK
    cast = lambda c: w_ref[:, c * CAST_CHUNK:(c + 1) * CAST_CHUNK].astype(bf16)
    w_next = cast(0)
    for c in range(n_chunks):
        w_cur = w_next
        if c + 1 < n_chunks:
            w_next = cast(c + 1)
        o_ref[:, c * CAST_CHUNK:(c + 1) * CAST_CHUNK] = jnp.dot(
            h_ref[...], w_cur, preferred_element_type=f32)


def _inproj(x2, g, w, tm=2048, tn=768):
    T = x2.shape[0]
    N = w.shape[1]
    return pl.pallas_call(
        _inproj_body,
        grid=(T // tm, N // tn),
        in_specs=[
            pl.BlockSpec((tm, D_MODEL), lambda i, j: (i, 0), pipeline_mode=pl.Buffered(1)),
            pl.BlockSpec((1, D_MODEL), lambda i, j: (0, 0)),
            pl.BlockSpec((D_MODEL, tn), lambda i, j: (0, j)),
        ],
        out_specs=pl.BlockSpec((tm, tn), lambda i, j: (i, j)),
        out_shape=jax.ShapeDtypeStruct((T, N), f32),
        scratch_shapes=[pltpu.VMEM((tm, D_MODEL), bf16)],
        compiler_params=pltpu.CompilerParams(
            dimension_semantics=("arbitrary", "arbitrary"),
            vmem_limit_bytes=VMEM_LIMIT),
        name="inproj",
    )(x2, g.reshape(1, D_MODEL), w)


SWA_AHEAD = 2


def _swa_body(sinks_ref, slopes_ref, q_ref, kp_ref, kc_ref, vp_ref, vc_ref,
              qn_ref, kn_ref, o_ref):
    L = SWA_BLOCK
    n = pl.program_id(1)
    k2 = jnp.concatenate([kp_ref[...], kc_ref[...]], axis=0)
    k2n = _pair_rms(k2, kn_ref[...]).astype(bf16)
    v2t = jnp.concatenate([vp_ref[...], vc_ref[...]], axis=0).T.astype(bf16)
    qn = jnp.concatenate(
        [_pair_rms(q_ref[:, pp * LANES:(pp + 1) * LANES], qn_ref[...])
         for pp in range(SWA_Q_HEADS // 2)], axis=1) * ATTN_SCALE
    qt = qn.T

    key = lax.broadcasted_iota(jnp.int32, (2 * L, L), 0)
    qry = lax.broadcasted_iota(jnp.int32, (2 * L, L), 1)
    dist = qry + L - key
    ok = (dist >= 0) & (dist < SWA_WINDOW) & ((n > 0) | (key >= L))
    distf = dist.astype(f32)
    zeros = jnp.zeros((HEAD_DIM, L), f32)

    heads_per_kv = SWA_Q_HEADS // SWA_KV_HEADS

    def scores(h):
        g = h // heads_per_kv
        qh = qt[h * HEAD_DIM:(h + 1) * HEAD_DIM, :]
        qa = jnp.concatenate([qh, zeros] if g == 0 else [zeros, qh], axis=0).astype(bf16)
        return jnp.dot(k2n, qa, preferred_element_type=f32)

    def softmax(h, s):
        s = jnp.where(ok, s - slopes_ref[h] * distf, -jnp.inf)
        sink = sinks_ref[h]
        m = jnp.maximum(jnp.max(s, axis=0, keepdims=True), sink)
        p = jnp.exp(s - m)
        denom = jnp.sum(p, axis=0, keepdims=True) + jnp.exp(sink - m)
        return p.astype(bf16), denom

    outs = []

    def finish(h, p, denom):
        g = h // heads_per_kv
        o = jnp.dot(v2t, p, preferred_element_type=f32)
        outs.append(o[g * HEAD_DIM:(g + 1) * HEAD_DIM, :] / denom)

    pending = [scores(h) for h in range(SWA_AHEAD)]
    deferred = None
    for h in range(SWA_Q_HEADS):
        s = pending.pop(0)
        if h + SWA_AHEAD < SWA_Q_HEADS:
            pending.append(scores(h + SWA_AHEAD))
        p, denom = softmax(h, s)
        if deferred is not None:
            finish(*deferred)
        deferred = (h, p, denom)
    finish(*deferred)
    o_ref[...] = jnp.concatenate(outs, axis=0).T.astype(bf16)


def _swa(proj, sinks, slopes, q_norm, k_norm, B, S):
    L = SWA_BLOCK
    nb = S // L
    T = B * S
    smem = pl.BlockSpec(memory_space=pltpu.SMEM)

    def prev(b, n):
        return b * nb + jnp.maximum(n - 1, 0)

    return pl.pallas_call(
        _swa_body,
        grid=(B, nb),
        in_specs=[
            smem, smem,
            pl.BlockSpec((L, SWA_Q_DIM), lambda b, n: (b * nb + n, COL_QA // (SWA_Q_DIM // LANES))),
            pl.BlockSpec((L, LANES), lambda b, n: (prev(b, n), COL_KA)),
            pl.BlockSpec((L, LANES), lambda b, n: (b * nb + n, COL_KA)),
            pl.BlockSpec((L, LANES), lambda b, n: (prev(b, n), COL_VA)),
            pl.BlockSpec((L, LANES), lambda b, n: (b * nb + n, COL_VA)),
            pl.BlockSpec((1, LANES), lambda b, n: (0, 0)),
            pl.BlockSpec((1, LANES), lambda b, n: (0, 0)),
        ],
        out_specs=pl.BlockSpec((L, SWA_Q_DIM), lambda b, n: (b * nb + n, 0)),
        out_shape=jax.ShapeDtypeStruct((T, SWA_Q_DIM), bf16),
        compiler_params=pltpu.CompilerParams(
            dimension_semantics=("arbitrary", "arbitrary"),
            vmem_limit_bytes=VMEM_LIMIT),
        name="swa",
    )(sinks, slopes, proj, proj, proj, proj, proj,
      jnp.tile(q_norm, 2).reshape(1, LANES), jnp.tile(k_norm, 2).reshape(1, LANES))


N_BIAS_PARTS = 3
NB_ROWS = 8
N_ALIBI_COL = NB_ROWS * N_BIAS_PARTS
SCORE_AHEAD = 3


def _split3(x):
    p0 = x.astype(bf16).astype(f32)
    r1 = x - p0
    p1 = r1.astype(bf16).astype(f32)
    p2 = (r1 - p1).astype(bf16).astype(f32)
    return p0, p1, p2


def _moba_body(slopes_ref, q_ref, k_ref, v_ref, qn_ref, kn_ref, o_ref, kaug_ref, vt_ref, *, nb):
    L = MOBA_BLOCK
    S = nb * L
    p = pl.program_id(1)

    kn = _pair_rms(k_ref[...], kn_ref[...])
    lane = lax.broadcasted_iota(jnp.int32, (S, LANES), 1)
    krow = lax.broadcasted_iota(jnp.int32, (S, LANES), 0)
    kblk = krow // L
    kpos = (krow % L).astype(f32)
    lane_lo = lane < HEAD_DIM

    def aug_cols(cc, slope):
        hot = ((cc >= 0) & (cc < N_ALIBI_COL) & ((cc % NB_ROWS) == kblk)).astype(f32)
        a0, a1, a2 = _split3(slope * kpos)
        return jnp.where(cc == N_ALIBI_COL, a0,
                         jnp.where(cc == N_ALIBI_COL + 1, a1,
                                   jnp.where(cc == N_ALIBI_COL + 2, a2, hot)))

    kaug_ref[0] = jnp.where(lane_lo, kn, aug_cols(lane - HEAD_DIM, slopes_ref[2 * p])).astype(bf16)
    kaug_ref[1] = jnp.where(lane_lo, aug_cols(lane, slopes_ref[2 * p + 1]), kn).astype(bf16)
    vt_ref[...] = v_ref[...].T.astype(bf16)
    means = [jnp.mean(kn[n * L:(n + 1) * L, :], axis=0, keepdims=True) for n in range(nb)]
    means += [jnp.zeros((1, LANES), f32)] * (NB_ROWS - nb)
    km = jnp.concatenate(means, axis=0)
    lane8 = lax.broadcasted_iota(jnp.int32, (NB_ROWS, LANES), 1)
    km_pair = jnp.concatenate(
        [jnp.where(lane8 < HEAD_DIM, km, 0.0), jnp.where(lane8 < HEAD_DIM, 0.0, km)], axis=0)
    km_hi, km_lo = _split2(km_pair)

    qs = _pair_rms(q_ref[...], qn_ref[...]) * ATTN_SCALE
    qt = qs.T
    qt_hi, qt_lo = _split2(qt)
    gate = (jnp.dot(km_hi, qt_hi, preferred_element_type=f32)
            + (jnp.dot(km_hi, qt_lo, preferred_element_type=f32)
               + jnp.dot(km_lo, qt_hi, preferred_element_type=f32)))

    blk = lax.broadcasted_iota(jnp.int32, (NB_ROWS, S), 0)
    qblk = lax.broadcasted_iota(jnp.int32, (NB_ROWS, S), 1) // L
    past = blk < qblk
    r = lax.broadcasted_iota(jnp.int32, (L, L), 1)
    c = lax.broadcasted_iota(jnp.int32, (L, L), 0)
    causal = jnp.where(r >= c, 0.0, NEG)
    ones_rows = (blk < 3).astype(f32)

    qa = []
    for hh in range(2):
        g = jnp.where(past, gate[hh * NB_ROWS:(hh + 1) * NB_ROWS, :], -jnp.inf)
        rank = jnp.zeros((NB_ROWS, S), jnp.int32)
        for m in range(nb):
            gm = g[m:m + 1, :]
            ahead = (gm > g) | ((gm == g) & (m < blk))
            rank = rank + ahead.astype(jnp.int32)
        sel = past & (rank < MOBA_TOPK)
        slope = slopes_ref[2 * p + hh]
        bias = jnp.where(sel, (-slope * L) * (qblk - blk).astype(f32), NEG)
        bias = jnp.where(blk == qblk, 0.0, bias)
        b0, b1, b2 = _split3(bias)
        extra = jnp.concatenate(
            [b0, b1, b2, ones_rows,
             jnp.zeros((HEAD_DIM - N_ALIBI_COL - NB_ROWS, S), f32)], axis=0)
        if hh == 0:
            qa.append(jnp.concatenate([qt[:HEAD_DIM], extra], axis=0).astype(bf16))
        else:
            qa.append(jnp.concatenate([extra, qt[HEAD_DIM:]], axis=0).astype(bf16))

    def scores(i, hh):
        return jnp.dot(kaug_ref[hh, 0:(i + 1) * L, :], qa[hh][:, i * L:(i + 1) * L],
                       preferred_element_type=f32)

    units = [(i, hh) for i in range(nb) for hh in range(2)]
    pending = [scores(*u) for u in units[:SCORE_AHEAD]]
    outs = {}

    def finish(i, hh, e_all, l):
        acc = jnp.dot(vt_ref[hh * HEAD_DIM:(hh + 1) * HEAD_DIM, 0:(i + 1) * L], e_all,
                      preferred_element_type=f32)
        outs[(i, hh)] = acc / l
        if hh == 1:
            o_ref[i * L:(i + 1) * L, :] = jnp.concatenate(
                [outs.pop((i, 0)), outs.pop((i, 1))], axis=0).T.astype(bf16)

    deferred = None
    for idx, (i, hh) in enumerate(units):
        s = pending.pop(0)
        if idx + SCORE_AHEAD < len(units):
            pending.append(scores(*units[idx + SCORE_AHEAD]))
        tiles = [s[n * L:(n + 1) * L, :] for n in range(i)] + [s[i * L:(i + 1) * L, :] + causal]
        m = functools.reduce(jnp.maximum, [jnp.max(t, axis=0, keepdims=True) for t in tiles])
        es = [jnp.exp(t - m) for t in tiles]
        l = functools.reduce(lambda a, b: a + b, [jnp.sum(e, axis=0, keepdims=True) for e in es])
        e_all = jnp.concatenate([e.astype(bf16) for e in es], axis=0)
        if deferred is not None:
            finish(*deferred)
        deferred = (i, hh, e_all, l)
    finish(*deferred)


def _moba(proj, slopes, q_norm, k_norm, B, S):
    L = MOBA_BLOCK
    nb = S // L
    assert nb <= NB_ROWS
    T = B * S
    n_pairs = MOBA_HEADS // 2
    return pl.pallas_call(
        functools.partial(_moba_body, nb=nb),
        grid=(B, n_pairs),
        in_specs=[
            pl.BlockSpec(memory_space=pltpu.SMEM),
            pl.BlockSpec((S, LANES), lambda b, p: (b, COL_QB + p)),
            pl.BlockSpec((S, LANES), lambda b, p: (b, COL_KB + p)),
            pl.BlockSpec((S, LANES), lambda b, p: (b, COL_VB + p)),
            pl.BlockSpec((1, LANES), lambda b, p: (0, 0)),
            pl.BlockSpec((1, LANES), lambda b, p: (0, 0)),
        ],
        out_specs=pl.BlockSpec((S, LANES), lambda b, p: (b, p)),
        out_shape=jax.ShapeDtypeStruct((T, MOBA_DIM), bf16),
        scratch_shapes=[
            pltpu.VMEM((2, S, LANES), bf16),
            pltpu.VMEM((LANES, S), bf16),
        ],
        compiler_params=pltpu.CompilerParams(
            dimension_semantics=("arbitrary", "arbitrary"),
            vmem_limit_bytes=VMEM_LIMIT),
        name="moba",
    )(slopes, proj, proj, proj,
      jnp.tile(q_norm, 2).reshape(1, LANES), jnp.tile(k_norm, 2).reshape(1, LANES))


ROUTER_COLS = LANES


N_GATE_BLOCKS = D_MODEL // GATE_BLOCK
MIX_CHUNK = 2 * GATE_BLOCK


def _mix_body(*refs):
    ga_refs = refs[:N_GATE_BLOCKS]
    gb_refs = refs[N_GATE_BLOCKS:2 * N_GATE_BLOCKS]
    (oa_ref, ob_ref, x_ref, wua_ref, wub_ref, wout_ref, gffn_ref, wr_ref,
     x1_ref, h2_ref, gid_ref) = refs[2 * N_GATE_BLOCKS:]
    oa = oa_ref[...]
    ob = ob_ref[...]
    gpc = MIX_CHUNK // GATE_BLOCK

    def up(c):
        cols = slice(c * MIX_CHUNK, (c + 1) * MIX_CHUNK)
        return (jnp.dot(oa, wua_ref[:, cols], preferred_element_type=f32),
                jnp.dot(ob, wub_ref[:, cols], preferred_element_type=f32))

    x1 = x_ref[...]
    pending = up(0)
    for c in range(D_MODEL // MIX_CHUNK):
        ya, yb = pending
        if (c + 1) * MIX_CHUNK < D_MODEL:
            pending = up(c + 1)
        gate_a = jnp.concatenate([r[...] for r in ga_refs[c * gpc:(c + 1) * gpc]], axis=1)
        gate_b = jnp.concatenate([r[...] for r in gb_refs[c * gpc:(c + 1) * gpc]], axis=1)
        merged = (_sigmoid(gate_a) * ya + _sigmoid(gate_b) * yb).astype(bf16)
        x1 = x1 + jnp.dot(merged, wout_ref[c * MIX_CHUNK:(c + 1) * MIX_CHUNK, :],
                          preferred_element_type=f32)
    x1_ref[...] = x1
    ms = jnp.mean(x1 * x1, axis=-1, keepdims=True)
    h2 = x1 * lax.rsqrt(ms + EPS) * gffn_ref[...]
    tm = h2.shape[0]
    for c in range(ROW_SLAB):
        h2_ref[pl.ds(c, tm, stride=SLAB_IN), :] = h2[:, c * LANES:(c + 1) * LANES]

    h_hi, h_lo = _split2(h2)
    hi_all = jnp.dot(h_hi, wr_ref[...], preferred_element_type=f32)
    lo_hi = jnp.dot(h_lo, wr_ref[:, :ROUTER_COLS], preferred_element_type=f32)
    lt = (hi_all[:, :ROUTER_COLS] + (lo_hi + hi_all[:, ROUTER_COLS:])).T
    gl = [lt[g:g + 1, :] for g in range(N_GROUPS)]
    gmax = functools.reduce(jnp.maximum, gl)
    gsum = functools.reduce(lambda a, b: a + b, [jnp.exp(v - gmax) for v in gl])
    g_p = 1.0 / gsum
    g_i = jnp.full((1, tm), N_GROUPS - 1, jnp.int32)
    for g in reversed(range(N_GROUPS)):
        g_i = jnp.where(gl[g] == gmax, g, g_i)

    el = []
    for e in range(EXPERTS_PER_GROUP):
        v = jnp.zeros((1, tm), f32)
        for g in range(N_GROUPS):
            r = N_GROUPS + g * EXPERTS_PER_GROUP + e
            v = jnp.where(g_i == g, lt[r:r + 1, :], v)
        el.append(v)
    emax = functools.reduce(jnp.maximum, el)
    ex = [jnp.exp(v - emax) for v in el]
    esum = functools.reduce(lambda a, b: a + b, ex)
    ep = [v / esum for v in ex]
    p1 = functools.reduce(jnp.maximum, ep)
    i1 = jnp.full((1, tm), EXPERTS_PER_GROUP - 1, jnp.int32)
    for e in reversed(range(EXPERTS_PER_GROUP)):
        i1 = jnp.where(ep[e] == p1, e, i1)
    rest = [jnp.where(i1 == e, -1.0, ep[e]) for e in range(EXPERTS_PER_GROUP)]
    p2 = functools.reduce(jnp.maximum, rest)
    i2 = jnp.full((1, tm), EXPERTS_PER_GROUP - 1, jnp.int32)
    for e in reversed(range(EXPERTS_PER_GROUP)):
        i2 = jnp.where(rest[e] == p2, e, i2)
    w1 = g_p * (p1 / (p1 + p2))
    w2 = g_p * (p2 / (p1 + p2))
    gid_ref[...] = jnp.concatenate([g_i, jnp.zeros((7, tm), jnp.int32)], axis=0)
    cw = [jnp.where(i1 == e, w1, 0.0) + jnp.where(i2 == e, w2, 0.0) for e in range(EXPERTS_PER_GROUP)]
    cw_t = jnp.concatenate(cw + [jnp.zeros((LANES - EXPERTS_PER_GROUP, tm), f32)], axis=0).T
    for e in range(SLAB_IN - ROW_SLAB):
        if e < EXPERTS_PER_GROUP:
            row = jnp.broadcast_to(cw_t[:, e:e + 1], (tm, LANES))
        else:
            row = jnp.zeros((tm, LANES), f32)
        h2_ref[pl.ds(ROW_SLAB + e, tm, stride=SLAB_IN), :] = row


def _mix(oa, ob, proj, x2, wua, wub, wout, g_ffn, wr, tm=256):
    T = x2.shape[0]
    const = lambda i: (0, 0)
    single = pl.Buffered(1)

    def gate_specs(first_col):
        first = first_col * LANES // GATE_BLOCK
        return [pl.BlockSpec((tm, GATE_BLOCK), functools.partial(lambda i, j: (i, j), j=first + k))
                for k in range(N_GATE_BLOCKS)]

    return pl.pallas_call(
        _mix_body,
        grid=(T // tm,),
        in_specs=gate_specs(COL_GATE_A) + gate_specs(COL_GATE_B) + [
            pl.BlockSpec((tm, SWA_Q_DIM), lambda i: (i, 0)),
            pl.BlockSpec((tm, MOBA_DIM), lambda i: (i, 0)),
            pl.BlockSpec((tm, D_MODEL), lambda i: (i, 0)),
            pl.BlockSpec((SWA_Q_DIM, D_MODEL), const, pipeline_mode=single),
            pl.BlockSpec((MOBA_DIM, D_MODEL), const, pipeline_mode=single),
            pl.BlockSpec((D_MODEL, D_MODEL), const, pipeline_mode=single),
            pl.BlockSpec((1, D_MODEL), const),
            pl.BlockSpec((D_MODEL, 2 * ROUTER_COLS), const),
        ],
        out_specs=[
            pl.BlockSpec((tm, D_MODEL), lambda i: (i, 0)),
            pl.BlockSpec((tm * SLAB_IN, LANES), lambda i: (i, 0)),
            pl.BlockSpec((8, tm), lambda i: (0, i)),
        ],
        out_shape=[
            jax.ShapeDtypeStruct((T, D_MODEL), f32),
            jax.ShapeDtypeStruct((T * SLAB_IN, LANES), f32),
            jax.ShapeDtypeStruct((8, T), jnp.int32),
        ],
        compiler_params=pltpu.CompilerParams(
            dimension_semantics=("arbitrary",),
            vmem_limit_bytes=VMEM_LIMIT),
        name="mix",
    )(*([proj] * (2 * N_GATE_BLOCKS)), oa, ob, x2, wua, wub, wout, g_ffn.reshape(1, D_MODEL), wr)


def _route_tables(gid, T, tm):
    nt = T // tm + N_GROUPS
    g = gid[0]
    onehot = (g[:, None] == jnp.arange(N_GROUPS, dtype=jnp.int32)[None, :]).astype(jnp.int32)
    incl = jnp.cumsum(onehot, axis=0)
    pos = jnp.sum(onehot * incl, axis=1) - 1
    counts = incl[-1]
    padded = (counts + tm - 1) // tm * tm
    seg_end = jnp.cumsum(padded)
    seg_start = seg_end - padded
    dest = jnp.sum(onehot * seg_start[None, :], axis=1) + pos
    tile_start = jnp.arange(nt, dtype=jnp.int32) * tm
    tile_g = jnp.sum((tile_start[:, None] >= seg_end[None, :]).astype(jnp.int32), axis=1)
    tile_g = jnp.minimum(tile_g, N_GROUPS - 1)
    nvalid = (seg_end[-1] // tm).astype(jnp.int32).reshape(1)
    n_pad = padded - counts
    pads = jnp.concatenate([seg_start + counts, seg_end, jnp.cumsum(n_pad) - n_pad])
    return (nvalid, tile_g.astype(jnp.int32), dest.astype(jnp.int32), pads.astype(jnp.int32))


def _moe_body(nvalid_ref, tg_ref, dest_ref, pads_ref, wg_ref, wu_ref, wd_ref, h2s_hbm, ytok_hbm,
              asg, gbuf, ybuf, xbuf, hbuf, gsem, ssem, *, tm, nt, n_tok):
    t = pl.program_id(0)
    e = pl.program_id(1)
    last_e = EXPERTS_PER_GROUP - 1
    nvalid = nvalid_ref[0]
    slot = lax.rem(t, 2)
    other = 1 - slot

    def for_rows(fn):
        def body(r, carry):
            fn(r)
            return carry
        lax.fori_loop(0, tm, body, 0, unroll=8)

    def gather_copy(tok, s, r):
        return pltpu.make_async_copy(
            h2s_hbm.at[pl.ds(pl.multiple_of(tok * SLAB_IN, 8), SLAB_IN), :],
            gbuf.at[pl.ds(pl.multiple_of((s * tm + r) * SLAB_IN, 8), SLAB_IN), :], gsem.at[s])

    def scatter_copy(row, s, r):
        return pltpu.make_async_copy(
            ybuf.at[pl.ds(pl.multiple_of((s * tm + r) * ROW_SLAB, 8), ROW_SLAB), :],
            ytok_hbm.at[pl.ds(pl.multiple_of(row * ROW_SLAB, 8), ROW_SLAB), :], ssem.at[s])

    def start_gathers(tile, s):
        for_rows(lambda r: gather_copy(jnp.maximum(asg[tile * tm + r], 0), s, r).start())

    def start_scatters(tile, s):
        def one(r):
            a = asg[tile * tm + r]
            scatter_copy(jnp.where(a >= 0, a, n_tok - 1 - a), s, r).start()
        for_rows(one)

    def wait_gathers(s):
        for_rows(lambda r: gather_copy(0, s, r).wait())

    def wait_scatters(s):
        for_rows(lambda r: scatter_copy(0, s, r).wait())

    @pl.when((t == 0) & (e == 0))
    def _():
        def place(tok, carry):
            asg[dest_ref[tok]] = tok
            return carry
        lax.fori_loop(0, n_tok, place, 0, unroll=8)

        for g in range(N_GROUPS):
            first = pads_ref[g]
            code = -1 - pads_ref[2 * N_GROUPS + g] + first

            def mark(p, carry):
                asg[p] = code - p
                return carry
            lax.fori_loop(first, pads_ref[N_GROUPS + g], mark, 0)
        ybuf[...] = jnp.zeros(ybuf.shape, f32)
        start_gathers(0, 0)

    @pl.when((e == 0) & (t < nvalid))
    def _():
        wait_gathers(slot)
        base = slot * tm * SLAB_IN
        xbuf[...] = jnp.concatenate(
            [gbuf[pl.ds(base + c, tm, stride=SLAB_IN), :] for c in range(ROW_SLAB)],
            axis=1).astype(bf16)

    @pl.when(t < nvalid)
    def _():
        x = xbuf[...]
        gte = jnp.dot(x, wg_ref[...], preferred_element_type=f32)
        up = jnp.dot(x, wu_ref[...], preferred_element_type=f32)
        cw = gbuf[pl.ds(slot * tm * SLAB_IN + ROW_SLAB + e, tm, stride=SLAB_IN), :]
        hid = (gte * _sigmoid(gte)) * up * jnp.concatenate([cw] * (D_EXPERT // LANES), axis=1)
        hbuf[e] = hid.astype(bf16)
        quarter = tm // EXPERTS_PER_GROUP
        have_next = t + 1 < nvalid
        have_prev = t >= 1
        for r in range(quarter):
            row = e * quarter + r
            tok = jnp.where(have_next, jnp.maximum(asg[(t + 1) * tm + row], 0), 0)
            gather_copy(tok, other, row).start()
            a = asg[jnp.maximum(t - 1, 0) * tm + row]
            dst = jnp.where(have_prev, jnp.where(a >= 0, a, n_tok - 1 - a), nt * tm + row)
            scatter_copy(dst, other, row).start(priority=r % 2)

    @pl.when((e == last_e) & (t < nvalid))
    def _():
        hcat = jnp.concatenate([hbuf[k] for k in range(EXPERTS_PER_GROUP)], axis=1)
        y = jnp.dot(hcat, wd_ref[...], preferred_element_type=f32)

        @pl.when(t >= 1)
        def _():
            wait_scatters(slot)
        base = slot * tm * ROW_SLAB
        for c in range(ROW_SLAB):
            ybuf[pl.ds(base + c, tm, stride=ROW_SLAB), :] = y[:, c * LANES:(c + 1) * LANES]

        @pl.when(t == nvalid - 1)
        def _():
            start_scatters(t, slot)
            wait_gathers(other)
            wait_scatters(other)
            wait_scatters(slot)

    @pl.when((e == last_e) & (t == nt - 1))
    def _():
        ybuf[0:tm * ROW_SLAB, :] = jnp.zeros((tm * ROW_SLAB, LANES), f32)
        for j in range(n_tok // tm, nt):
            @pl.when(j >= nvalid)
            def _():
                fill = pltpu.make_async_copy(
                    ybuf.at[0:tm * ROW_SLAB, :],
                    ytok_hbm.at[j * tm * ROW_SLAB:(j + 1) * tm * ROW_SLAB, :], ssem.at[0])
                fill.start()
                fill.wait()


def _moe(h2s, tables, wg, wu, wd, T, tm):
    nvalid, tile_g, dest, pads = tables
    nt = tile_g.shape[0]

    def w_index(t, e, nv, tg, de, pd):
        return (tg[t] * EXPERTS_PER_GROUP + jnp.where(t < nv[0], e, EXPERTS_PER_GROUP - 1), 0, 0)

    grid_spec = pltpu.PrefetchScalarGridSpec(
        num_scalar_prefetch=4,
        grid=(nt, EXPERTS_PER_GROUP),
        in_specs=[
            pl.BlockSpec((None, D_MODEL, D_EXPERT), w_index),
            pl.BlockSpec((None, D_MODEL, D_EXPERT), w_index),
            pl.BlockSpec((None, EXPERTS_PER_GROUP * D_EXPERT, D_MODEL),
                         lambda t, e, nv, tg, de, pd: (tg[t], 0, 0)),
            pl.BlockSpec(memory_space=pl.ANY),
        ],
        out_specs=pl.BlockSpec(memory_space=pl.ANY),
        scratch_shapes=[
            pltpu.SMEM(((nt + 1) * tm,), jnp.int32),
            pltpu.VMEM((2 * tm * SLAB_IN, LANES), f32),
            pltpu.VMEM((2 * tm * ROW_SLAB, LANES), f32),
            pltpu.VMEM((tm, D_MODEL), bf16),
            pltpu.VMEM((EXPERTS_PER_GROUP, tm, D_EXPERT), bf16),
            pltpu.SemaphoreType.DMA((2,)),
            pltpu.SemaphoreType.DMA((2,)),
        ],
    )
    return pl.pallas_call(
        functools.partial(_moe_body, tm=tm, nt=nt, n_tok=T),
        grid_spec=grid_spec,
        out_shape=jax.ShapeDtypeStruct(((nt + 1) * tm * ROW_SLAB, LANES), f32),
        compiler_params=pltpu.CompilerParams(
            dimension_semantics=("arbitrary", "arbitrary"),
            vmem_limit_bytes=VMEM_LIMIT),
        name="moe",
    )(nvalid, tile_g, dest, pads, wg, wu,
      wd.reshape(N_GROUPS, EXPERTS_PER_GROUP * D_EXPERT, D_MODEL), h2s)


def _combine_body(x1_ref, y_ref, o_ref):
    tm = x1_ref.shape[0]
    for c in range(ROW_SLAB):
        cols = slice(c * LANES, (c + 1) * LANES)
        o_ref[:, cols] = x1_ref[:, cols] + y_ref[pl.ds(c, tm, stride=ROW_SLAB), :]


def _combine(x1, ytok, tm=256):
    T = x1.shape[0]
    return pl.pallas_call(
        _combine_body,
        grid=(T // tm,),
        in_specs=[
            pl.BlockSpec((tm, D_MODEL), lambda i: (i, 0)),
            pl.BlockSpec((tm * ROW_SLAB, LANES), lambda i: (i, 0)),
        ],
        out_specs=pl.BlockSpec((tm, D_MODEL), lambda i: (i, 0)),
        out_shape=jax.ShapeDtypeStruct((T, D_MODEL), f32),
        compiler_params=pltpu.CompilerParams(
            dimension_semantics=("arbitrary",),
            vmem_limit_bytes=VMEM_LIMIT),
        name="combine",
    )(x1, ytok)


def _alibi_slopes(n):
    return jnp.exp2(-8.0 * jnp.arange(1, n + 1, dtype=f32) / n)


def kernel(x, g_mix, w_in, q_norm_swa, k_norm_swa, sinks, q_norm_moba, k_norm_moba,
           w_up_swa, w_up_moba, w_out, g_ffn, w_router_group, w_router_expert,
           w_gate_e, w_up_e, w_down_e):
    B, S, D = x.shape
    assert D == D_MODEL and S % MOBA_BLOCK == 0 and S % SWA_BLOCK == 0
    T = B * S
    x2 = x.reshape(T, D)

    proj = _inproj(x2, g_mix, w_in)

    oa = _swa(proj, sinks.astype(f32), _alibi_slopes(SWA_Q_HEADS), q_norm_swa, k_norm_swa, B, S)
    ob = _moba(proj, _alibi_slopes(MOBA_HEADS), q_norm_moba, k_norm_moba, B, S)

    wr = jnp.concatenate(
        [w_router_group,
         w_router_expert.transpose(1, 0, 2).reshape(D, N_EXPERTS),
         jnp.zeros((D, ROUTER_COLS - N_GROUPS - N_EXPERTS), f32)], axis=1)
    wr_hi = wr.astype(bf16)
    wr_lo = (wr - wr_hi.astype(f32)).astype(bf16)
    x1, h2s, gid = _mix(oa, ob, proj, x2, w_up_swa.astype(bf16), w_up_moba.astype(bf16),
                        w_out.astype(bf16), g_ffn, jnp.concatenate([wr_hi, wr_lo], axis=1))

    tables = _route_tables(gid, T, MOE_TILE)
    ytok = _moe(h2s, tables, w_gate_e.astype(bf16), w_up_e.astype(bf16), w_down_e.astype(bf16),
                T, MOE_TILE)
    y = _combine(x1, ytok)
    return y.reshape(B, S, D)
```

```python
import functools

import jax
import jax.numpy as jnp
from jax import lax
from jax.experimental import pallas as pl
from jax.experimental.pallas import tpu as pltpu

D_MODEL = 2048
HEAD_DIM = 64
ATTN_SCALE = HEAD_DIM ** -0.5
SWA_Q_HEADS = 16
SWA_KV_HEADS = 2
SWA_WINDOW = 128
SWA_BLOCK = 128
MOBA_HEADS = 16
MOBA_BLOCK = 256
MOBA_TOPK = 3
N_GROUPS = 4
EXPERTS_PER_GROUP = 4
N_EXPERTS = N_GROUPS * EXPERTS_PER_GROUP
D_EXPERT = 512
EPS = 1e-6

SWA_Q_DIM = SWA_Q_HEADS * HEAD_DIM
SWA_KV_DIM = SWA_KV_HEADS * HEAD_DIM
MOBA_DIM = MOBA_HEADS * HEAD_DIM
IN_COLS = SWA_Q_DIM + 2 * SWA_KV_DIM + 3 * MOBA_DIM + 2 * D_MODEL

LANES = 128
ROW_SLAB = D_MODEL // LANES
SLAB_IN = ROW_SLAB + 8
MOE_TILE = 512
VMEM_LIMIT = 56 * 1024 * 1024
NEG = -1e30

COL_QA = 0
COL_KA = COL_QA + SWA_Q_DIM // LANES
COL_VA = COL_KA + 1
COL_QB = COL_VA + 1
COL_KB = COL_QB + MOBA_DIM // LANES
COL_VB = COL_KB + MOBA_DIM // LANES
COL_GATE_A = COL_VB + MOBA_DIM // LANES
COL_GATE_B = COL_GATE_A + D_MODEL // LANES
GATE_BLOCK = 2 * LANES
assert (COL_GATE_A * LANES) % GATE_BLOCK == 0 and (COL_GATE_B * LANES) % GATE_BLOCK == 0

bf16 = jnp.bfloat16
f32 = jnp.float32


def _sigmoid(x):
    return 1.0 / (1.0 + jnp.exp(-x))


def _pair_rms(x, gain):
    lane = lax.broadcasted_iota(jnp.int32, x.shape, 1)
    lo = lane < HEAD_DIM
    sq = x * x
    s0 = jnp.sum(jnp.where(lo, sq, 0.0), axis=-1, keepdims=True)
    s1 = jnp.sum(jnp.where(lo, 0.0, sq), axis=-1, keepdims=True)
    r0 = lax.rsqrt(s0 * (1.0 / HEAD_DIM) + EPS)
    r1 = lax.rsqrt(s1 * (1.0 / HEAD_DIM) + EPS)
    return x * jnp.where(lo, r0, r1) * gain


def _dot_nt(a, b):
    return lax.dot_general(a, b, (((1,), (1,)), ((), ())), preferred_element_type=f32)


def _split2(x):
    hi = x.astype(bf16)
    lo = (x - hi.astype(f32)).astype(bf16)
    return hi, lo


CAST_CHUNK = 2 * LANES
NORM_ROWS = 512


def _inproj_body(x_ref, g_ref, w_ref, o_ref, h_ref):
    @pl.when(pl.program_id(1) == 0)
    def _():
        for r in range(0, x_ref.shape[0], NORM_ROWS):
            x = x_ref[r:r + NORM_ROWS, :]
            ms = jnp.mean(x * x, axis=-1, keepdims=True)
            h_ref[r:r + NORM_ROWS, :] = (x * lax.rsqrt(ms + EPS) * g_ref[...]).astype(bf16)

    n_chunks = w_ref.shape[1] // CAST_CHUNK
    cast = lambda c: w_ref[:, c * CAST_CHUNK:(c + 1) * CAST_CHUNK].astype(bf16)
    w_next = cast(0)
    for c in range(n_chunks):
        w_cur = w_next
        if c + 1 < n_chunks:
            w_next = cast(c + 1)
        o_ref[:, c * CAST_CHUNK:(c + 1) * CAST_CHUNK] = jnp.dot(
            h_ref[...], w_cur, preferred_element_type=f32)


def _inproj(x2, g, w, tm=2048, tn=768):
    T = x2.shape[0]
    N = w.shape[1]
    return pl.pallas_call(
        _inproj_body,
        grid=(T // tm, N // tn),
        in_specs=[
            pl.BlockSpec((tm, D_MODEL), lambda i, j: (i, 0), pipeline_mode=pl.Buffered(1)),
            pl.BlockSpec((1, D_MODEL), lambda i, j: (0, 0)),
            pl.BlockSpec((D_MODEL, tn), lambda i, j: (0, j)),
        ],
        out_specs=pl.BlockSpec((tm, tn), lambda i, j: (i, j)),
        out_shape=jax.ShapeDtypeStruct((T, N), f32),
        scratch_shapes=[pltpu.VMEM((tm, D_MODEL), bf16)],
        compiler_params=pltpu.CompilerParams(
            dimension_semantics=("arbitrary", "arbitrary"),
            vmem_limit_bytes=VMEM_LIMIT),
        name="inproj",
    )(x2, g.reshape(1, D_MODEL), w)


SWA_AHEAD = 2


def _swa_body(sinks_ref, slopes_ref, q_ref, kp_ref, kc_ref, vp_ref, vc_ref,
              qn_ref, kn_ref, o_ref):
    L = SWA_BLOCK
    n = pl.program_id(1)
    k2 = jnp.concatenate([kp_ref[...], kc_ref[...]], axis=0)
    k2n = _pair_rms(k2, kn_ref[...]).astype(bf16)
    v2t = jnp.concatenate([vp_ref[...], vc_ref[...]], axis=0).T.astype(bf16)
    qn = jnp.concatenate(
        [_pair_rms(q_ref[:, pp * LANES:(pp + 1) * LANES], qn_ref[...])
         for pp in range(SWA_Q_HEADS // 2)], axis=1) * ATTN_SCALE
    qt = qn.T

    key = lax.broadcasted_iota(jnp.int32, (2 * L, L), 0)
    qry = lax.broadcasted_iota(jnp.int32, (2 * L, L), 1)
    dist = qry + L - key
    ok = (dist >= 0) & (dist < SWA_WINDOW) & ((n > 0) | (key >= L))
    distf = dist.astype(f32)
    zeros = jnp.zeros((HEAD_DIM, L), f32)

    heads_per_kv = SWA_Q_HEADS // SWA_KV_HEADS

    def scores(h):
        g = h // heads_per_kv
        qh = qt[h * HEAD_DIM:(h + 1) * HEAD_DIM, :]
        qa = jnp.concatenate([qh, zeros] if g == 0 else [zeros, qh], axis=0).astype(bf16)
        return jnp.dot(k2n, qa, preferred_element_type=f32)

    def softmax(h, s):
        s = jnp.where(ok, s - slopes_ref[h] * distf, -jnp.inf)
        sink = sinks_ref[h]
        m = jnp.maximum(jnp.max(s, axis=0, keepdims=True), sink)
        p = jnp.exp(s - m)
        denom = jnp.sum(p, axis=0, keepdims=True) + jnp.exp(sink - m)
        return p.astype(bf16), denom

    outs = []

    def finish(h, p, denom):
        g = h // heads_per_kv
        o = jnp.dot(v2t, p, preferred_element_type=f32)
        outs.append(o[g * HEAD_DIM:(g + 1) * HEAD_DIM, :] / denom)

    pending = [scores(h) for h in range(SWA_AHEAD)]
    deferred = None
    for h in range(SWA_Q_HEADS):
        s = pending.pop(0)
        if h + SWA_AHEAD < SWA_Q_HEADS:
            pending.append(scores(h + SWA_AHEAD))
        p, denom = softmax(h, s)
        if deferred is not None:
            finish(*deferred)
        deferred = (h, p, denom)
    finish(*deferred)
    o_ref[...] = jnp.concatenate(outs, axis=0).T.astype(bf16)


def _swa(proj, sinks, slopes, q_norm, k_norm, B, S):
    L = SWA_BLOCK
    nb = S // L
    T = B * S
    smem = pl.BlockSpec(memory_space=pltpu.SMEM)

    def prev(b, n):
        return b * nb + jnp.maximum(n - 1, 0)

    return pl.pallas_call(
        _swa_body,
        grid=(B, nb),
        in_specs=[
            smem, smem,
            pl.BlockSpec((L, SWA_Q_DIM), lambda b, n: (b * nb + n, COL_QA // (SWA_Q_DIM // LANES))),
            pl.BlockSpec((L, LANES), lambda b, n: (prev(b, n), COL_KA)),
            pl.BlockSpec((L, LANES), lambda b, n: (b * nb + n, COL_KA)),
            pl.BlockSpec((L, LANES), lambda b, n: (prev(b, n), COL_VA)),
            pl.BlockSpec((L, LANES), lambda b, n: (b * nb + n, COL_VA)),
            pl.BlockSpec((1, LANES), lambda b, n: (0, 0)),
            pl.BlockSpec((1, LANES), lambda b, n: (0, 0)),
        ],
        out_specs=pl.BlockSpec((L, SWA_Q_DIM), lambda b, n: (b * nb + n, 0)),
        out_shape=jax.ShapeDtypeStruct((T, SWA_Q_DIM), bf16),
        compiler_params=pltpu.CompilerParams(
            dimension_semantics=("arbitrary", "arbitrary"),
            vmem_limit_bytes=VMEM_LIMIT),
        name="swa",
    )(sinks, slopes, proj, proj, proj, proj, proj,
      jnp.tile(q_norm, 2).reshape(1, LANES), jnp.tile(k_norm, 2).reshape(1, LANES))


N_BIAS_PARTS = 3
NB_ROWS = 8
N_ALIBI_COL = NB_ROWS * N_BIAS_PARTS
SCORE_AHEAD = 3


def _split3(x):
    p0 = x.astype(bf16).astype(f32)
    r1 = x - p0
    p1 = r1.astype(bf16).astype(f32)
    p2 = (r1 - p1).astype(bf16).astype(f32)
    return p0, p1, p2


def _moba_body(slopes_ref, q_ref, k_ref, v_ref, qn_ref, kn_ref, o_ref, kaug_ref, vt_ref, *, nb):
    L = MOBA_BLOCK
    S = nb * L
    p = pl.program_id(1)

    kn = _pair_rms(k_ref[...], kn_ref[...])
    lane = lax.broadcasted_iota(jnp.int32, (S, LANES), 1)
    krow = lax.broadcasted_iota(jnp.int32, (S, LANES), 0)
    kblk = krow // L
    kpos = (krow % L).astype(f32)
    lane_lo = lane < HEAD_DIM

    def aug_cols(cc, slope):
        hot = ((cc >= 0) & (cc < N_ALIBI_COL) & ((cc % NB_ROWS) == kblk)).astype(f32)
        a0, a1, a2 = _split3(slope * kpos)
        return jnp.where(cc == N_ALIBI_COL, a0,
                         jnp.where(cc == N_ALIBI_COL + 1, a1,
                                   jnp.where(cc == N_ALIBI_COL + 2, a2, hot)))

    kaug_ref[0] = jnp.where(lane_lo, kn, aug_cols(lane - HEAD_DIM, slopes_ref[2 * p])).astype(bf16)
    kaug_ref[1] = jnp.where(lane_lo, aug_cols(lane, slopes_ref[2 * p + 1]), kn).astype(bf16)
    vt_ref[...] = v_ref[...].T.astype(bf16)
    means = [jnp.mean(kn[n * L:(n + 1) * L, :], axis=0, keepdims=True) for n in range(nb)]
    means += [jnp.zeros((1, LANES), f32)] * (NB_ROWS - nb)
    km = jnp.concatenate(means, axis=0)
    lane8 = lax.broadcasted_iota(jnp.int32, (NB_ROWS, LANES), 1)
    km_pair = jnp.concatenate(
        [jnp.where(lane8 < HEAD_DIM, km, 0.0), jnp.where(lane8 < HEAD_DIM, 0.0, km)], axis=0)
    km_hi, km_lo = _split2(km_pair)

    qs = _pair_rms(q_ref[...], qn_ref[...]) * ATTN_SCALE
    qt = qs.T
    qt_hi, qt_lo = _split2(qt)
    gate = (jnp.dot(km_hi, qt_hi, preferred_element_type=f32)
            + (jnp.dot(km_hi, qt_lo, preferred_element_type=f32)
               + jnp.dot(km_lo, qt_hi, preferred_element_type=f32)))

    blk = lax.broadcasted_iota(jnp.int32, (NB_ROWS, S), 0)
    qblk = lax.broadcasted_iota(jnp.int32, (NB_ROWS, S), 1) // L
    past = blk < qblk
    r = lax.broadcasted_iota(jnp.int32, (L, L), 1)
    c = lax.broadcasted_iota(jnp.int32, (L, L), 0)
    causal = jnp.where(r >= c, 0.0, NEG)
    ones_rows = (blk < 3).astype(f32)

    qa = []
    for hh in range(2):
        g = jnp.where(past, gate[hh * NB_ROWS:(hh + 1) * NB_ROWS, :], -jnp.inf)
        rank = jnp.zeros((NB_ROWS, S), jnp.int32)
        for m in range(nb):
            gm = g[m:m + 1, :]
            ahead = (gm > g) | ((gm == g) & (m < blk))
            rank = rank + ahead.astype(jnp.int32)
        sel = past & (rank < MOBA_TOPK)
        slope = slopes_ref[2 * p + hh]
        bias = jnp.where(sel, (-slope * L) * (qblk - blk).astype(f32), NEG)
        bias = jnp.where(blk == qblk, 0.0, bias)
        b0, b1, b2 = _split3(bias)
        extra = jnp.concatenate(
            [b0, b1, b2, ones_rows,
             jnp.zeros((HEAD_DIM - N_ALIBI_COL - NB_ROWS, S), f32)], axis=0)
        if hh == 0:
            qa.append(jnp.concatenate([qt[:HEAD_DIM], extra], axis=0).astype(bf16))
        else:
            qa.append(jnp.concatenate([extra, qt[HEAD_DIM:]], axis=0).astype(bf16))

    def scores(i, hh):
        return jnp.dot(kaug_ref[hh, 0:(i + 1) * L, :], qa[hh][:, i * L:(i + 1) * L],
                       preferred_element_type=f32)

    units = [(i, hh) for i in range(nb) for hh in range(2)]
    pending = [scores(*u) for u in units[:SCORE_AHEAD]]
    outs = {}

    def finish(i, hh, e_all, l):
        acc = jnp.dot(vt_ref[hh * HEAD_DIM:(hh + 1) * HEAD_DIM, 0:(i + 1) * L], e_all,
                      preferred_element_type=f32)
        outs[(i, hh)] = acc / l
        if hh == 1:
            o_ref[i * L:(i + 1) * L, :] = jnp.concatenate(
                [outs.pop((i, 0)), outs.pop((i, 1))], axis=0).T.astype(bf16)

    deferred = None
    for idx, (i, hh) in enumerate(units):
        s = pending.pop(0)
        if idx + SCORE_AHEAD < len(units):
            pending.append(scores(*units[idx + SCORE_AHEAD]))
        tiles = [s[n * L:(n + 1) * L, :] for n in range(i)] + [s[i * L:(i + 1) * L, :] + causal]
        m = functools.reduce(jnp.maximum, [jnp.max(t, axis=0, keepdims=True) for t in tiles])
        es = [jnp.exp(t - m) for t in tiles]
        l = functools.reduce(lambda a, b: a + b, [jnp.sum(e, axis=0, keepdims=True) for e in es])
        e_all = jnp.concatenate([e.astype(bf16) for e in es], axis=0)
        if deferred is not None:
            finish(*deferred)
        deferred = (i, hh, e_all, l)
    finish(*deferred)


def _moba(proj, slopes, q_norm, k_norm, B, S):
    L = MOBA_BLOCK
    nb = S // L
    assert nb <= NB_ROWS
    T = B * S
    n_pairs = MOBA_HEADS // 2
    return pl.pallas_call(
        functools.partial(_moba_body, nb=nb),
        grid=(B, n_pairs),
        in_specs=[
            pl.BlockSpec(memory_space=pltpu.SMEM),
            pl.BlockSpec((S, LANES), lambda b, p: (b, COL_QB + p)),
            pl.BlockSpec((S, LANES), lambda b, p: (b, COL_KB + p)),
            pl.BlockSpec((S, LANES), lambda b, p: (b, COL_VB + p)),
            pl.BlockSpec((1, LANES), lambda b, p: (0, 0)),
            pl.BlockSpec((1, LANES), lambda b, p: (0, 0)),
        ],
        out_specs=pl.BlockSpec((S, LANES), lambda b, p: (b, p)),
        out_shape=jax.ShapeDtypeStruct((T, MOBA_DIM), bf16),
        scratch_shapes=[
            pltpu.VMEM((2, S, LANES), bf16),
            pltpu.VMEM((LANES, S), bf16),
        ],
        compiler_params=pltpu.CompilerParams(
            dimension_semantics=("arbitrary", "arbitrary"),
            vmem_limit_bytes=VMEM_LIMIT),
        name="moba",
    )(slopes, proj, proj, proj,
      jnp.tile(q_norm, 2).reshape(1, LANES), jnp.tile(k_norm, 2).reshape(1, LANES))


ROUTER_COLS = LANES


N_GATE_BLOCKS = D_MODEL // GATE_BLOCK
MIX_CHUNK = 2 * GATE_BLOCK


def _mix_body(*refs):
    ga_refs = refs[:N_GATE_BLOCKS]
    gb_refs = refs[N_GATE_BLOCKS:2 * N_GATE_BLOCKS]
    (oa_ref, ob_ref, x_ref, wua_ref, wub_ref, wout_ref, gffn_ref, wr_ref,
     x1_ref, h2_ref, gid_ref) = refs[2 * N_GATE_BLOCKS:]
    oa = oa_ref[...]
    ob = ob_ref[...]
    gpc = MIX_CHUNK // GATE_BLOCK

    def up(c):
        cols = slice(c * MIX_CHUNK, (c + 1) * MIX_CHUNK)
        return (jnp.dot(oa, wua_ref[:, cols], preferred_element_type=f32),
                jnp.dot(ob, wub_ref[:, cols], preferred_element_type=f32))

    x1 = x_ref[...]
    pending = up(0)
    for c in range(D_MODEL // MIX_CHUNK):
        ya, yb = pending
        if (c + 1) * MIX_CHUNK < D_MODEL:
            pending = up(c + 1)
        gate_a = jnp.concatenate([r[...] for r in ga_refs[c * gpc:(c + 1) * gpc]], axis=1)
        gate_b = jnp.concatenate([r[...] for r in gb_refs[c * gpc:(c + 1) * gpc]], axis=1)
        merged = (_sigmoid(gate_a) * ya + _sigmoid(gate_b) * yb).astype(bf16)
        x1 = x1 + jnp.dot(merged, wout_ref[c * MIX_CHUNK:(c + 1) * MIX_CHUNK, :],
                          preferred_element_type=f32)
    x1_ref[...] = x1
    ms = jnp.mean(x1 * x1, axis=-1, keepdims=True)
    h2 = x1 * lax.rsqrt(ms + EPS) * gffn_ref[...]
    tm = h2.shape[0]
    for c in range(ROW_SLAB):
        h2_ref[pl.ds(c, tm, stride=SLAB_IN), :] = h2[:, c * LANES:(c + 1) * LANES]

    h_hi, h_lo = _split2(h2)
    hi_all = jnp.dot(h_hi, wr_ref[...], preferred_element_type=f32)
    lo_hi = jnp.dot(h_lo, wr_ref[:, :ROUTER_COLS], preferred_element_type=f32)
    lt = (hi_all[:, :ROUTER_COLS] + (lo_hi + hi_all[:, ROUTER_COLS:])).T
    gl = [lt[g:g + 1, :] for g in range(N_GROUPS)]
    gmax = functools.reduce(jnp.maximum, gl)
    gsum = functools.reduce(lambda a, b: a + b, [jnp.exp(v - gmax) for v in gl])
    g_p = 1.0 / gsum
    g_i = jnp.full((1, tm), N_GROUPS - 1, jnp.int32)
    for g in reversed(range(N_GROUPS)):
        g_i = jnp.where(gl[g] == gmax, g, g_i)

    el = []
    for e in range(EXPERTS_PER_GROUP):
        v = jnp.zeros((1, tm), f32)
        for g in range(N_GROUPS):
            r = N_GROUPS + g * EXPERTS_PER_GROUP + e
            v = jnp.where(g_i == g, lt[r:r + 1, :], v)
        el.append(v)
    emax = functools.reduce(jnp.maximum, el)
    ex = [jnp.exp(v - emax) for v in el]
    esum = functools.reduce(lambda a, b: a + b, ex)
    ep = [v / esum for v in ex]
    p1 = functools.reduce(jnp.maximum, ep)
    i1 = jnp.full((1, tm), EXPERTS_PER_GROUP - 1, jnp.int32)
    for e in reversed(range(EXPERTS_PER_GROUP)):
        i1 = jnp.where(ep[e] == p1, e, i1)
    rest = [jnp.where(i1 == e, -1.0, ep[e]) for e in range(EXPERTS_PER_GROUP)]
    p2 = functools.reduce(jnp.maximum, rest)
    i2 = jnp.full((1, tm), EXPERTS_PER_GROUP - 1, jnp.int32)
    for e in reversed(range(EXPERTS_PER_GROUP)):
        i2 = jnp.where(rest[e] == p2, e, i2)
    w1 = g_p * (p1 / (p1 + p2))
    w2 = g_p * (p2 / (p1 + p2))
    gid_ref[...] = jnp.concatenate([g_i, jnp.zeros((7, tm), jnp.int32)], axis=0)
    cw = [jnp.where(i1 == e, w1, 0.0) + jnp.where(i2 == e, w2, 0.0) for e in range(EXPERTS_PER_GROUP)]
    cw_t = jnp.concatenate(cw + [jnp.zeros((LANES - EXPERTS_PER_GROUP, tm), f32)], axis=0).T
    for e in range(SLAB_IN - ROW_SLAB):
        if e < EXPERTS_PER_GROUP:
            row = jnp.broadcast_to(cw_t[:, e:e + 1], (tm, LANES))
        else:
            row = jnp.zeros((tm, LANES), f32)
        h2_ref[pl.ds(ROW_SLAB + e, tm, stride=SLAB_IN), :] = row


def _mix(oa, ob, proj, x2, wua, wub, wout, g_ffn, wr, tm=256):
    T = x2.shape[0]
    const = lambda i: (0, 0)
    single = pl.Buffered(1)

    def gate_specs(first_col):
        first = first_col * LANES // GATE_BLOCK
        return [pl.BlockSpec((tm, GATE_BLOCK), functools.partial(lambda i, j: (i, j), j=first + k))
                for k in range(N_GATE_BLOCKS)]

    return pl.pallas_call(
        _mix_body,
        grid=(T // tm,),
        in_specs=gate_specs(COL_GATE_A) + gate_specs(COL_GATE_B) + [
            pl.BlockSpec((tm, SWA_Q_DIM), lambda i: (i, 0)),
            pl.BlockSpec((tm, MOBA_DIM), lambda i: (i, 0)),
            pl.BlockSpec((tm, D_MODEL), lambda i: (i, 0)),
            pl.BlockSpec((SWA_Q_DIM, D_MODEL), const, pipeline_mode=single),
            pl.BlockSpec((MOBA_DIM, D_MODEL), const, pipeline_mode=single),
            pl.BlockSpec((D_MODEL, D_MODEL), const, pipeline_mode=single),
            pl.BlockSpec((1, D_MODEL), const),
            pl.BlockSpec((D_MODEL, 2 * ROUTER_COLS), const),
        ],
        out_specs=[
            pl.BlockSpec((tm, D_MODEL), lambda i: (i, 0)),
            pl.BlockSpec((tm * SLAB_IN, LANES), lambda i: (i, 0)),
            pl.BlockSpec((8, tm), lambda i: (0, i)),
        ],
        out_shape=[
            jax.ShapeDtypeStruct((T, D_MODEL), f32),
            jax.ShapeDtypeStruct((T * SLAB_IN, LANES), f32),
            jax.ShapeDtypeStruct((8, T), jnp.int32),
        ],
        compiler_params=pltpu.CompilerParams(
            dimension_semantics=("arbitrary",),
            vmem_limit_bytes=VMEM_LIMIT),
        name="mix",
    )(*([proj] * (2 * N_GATE_BLOCKS)), oa, ob, x2, wua, wub, wout, g_ffn.reshape(1, D_MODEL), wr)


def _route_tables(gid, T, tm):
    nt = T // tm + N_GROUPS
    g = gid[0]
    onehot = (g[:, None] == jnp.arange(N_GROUPS, dtype=jnp.int32)[None, :]).astype(jnp.int32)
    incl = jnp.cumsum(onehot, axis=0)
    pos = jnp.sum(onehot * incl, axis=1) - 1
    counts = incl[-1]
    padded = (counts + tm - 1) // tm * tm
    seg_end = jnp.cumsum(padded)
    seg_start = seg_end - padded
    dest = jnp.sum(onehot * seg_start[None, :], axis=1) + pos
    tile_start = jnp.arange(nt, dtype=jnp.int32) * tm
    tile_g = jnp.sum((tile_start[:, None] >= seg_end[None, :]).astype(jnp.int32), axis=1)
    tile_g = jnp.minimum(tile_g, N_GROUPS - 1)
    nvalid = (seg_end[-1] // tm).astype(jnp.int32).reshape(1)
    n_pad = padded - counts
    pads = jnp.concatenate([seg_start + counts, seg_end, jnp.cumsum(n_pad) - n_pad])
    return (nvalid, tile_g.astype(jnp.int32), dest.astype(jnp.int32), pads.astype(jnp.int32))


def _moe_body(nvalid_ref, tg_ref, dest_ref, pads_ref, wg_ref, wu_ref, wd_ref, h2s_hbm, ytok_hbm,
              asg, gbuf, ybuf, xbuf, hbuf, gsem, ssem, *, tm, nt, n_tok):
    t = pl.program_id(0)
    e = pl.program_id(1)
    last_e = EXPERTS_PER_GROUP - 1
    nvalid = nvalid_ref[0]
    slot = lax.rem(t, 2)
    other = 1 - slot

    def for_rows(fn):
        def body(r, carry):
            fn(r)
            return carry
        lax.fori_loop(0, tm, body, 0, unroll=8)

    def gather_copy(tok, s, r):
        return pltpu.make_async_copy(
            h2s_hbm.at[pl.ds(pl.multiple_of(tok * SLAB_IN, 8), SLAB_IN), :],
            gbuf.at[pl.ds(pl.multiple_of((s * tm + r) * SLAB_IN, 8), SLAB_IN), :], gsem.at[s])

    def scatter_copy(row, s, r):
        return pltpu.make_async_copy(
            ybuf.at[pl.ds(pl.multiple_of((s * tm + r) * ROW_SLAB, 8), ROW_SLAB), :],
            ytok_hbm.at[pl.ds(pl.multiple_of(row * ROW_SLAB, 8), ROW_SLAB), :], ssem.at[s])

    def start_gathers(tile, s):
        for_rows(lambda r: gather_copy(jnp.maximum(asg[tile * tm + r], 0), s, r).start())

    def start_scatters(tile, s):
        def one(r):
            a = asg[tile * tm + r]
            scatter_copy(jnp.where(a >= 0, a, n_tok - 1 - a), s, r).start()
        for_rows(one)

    def wait_gathers(s):
        for_rows(lambda r: gather_copy(0, s, r).wait())

    def wait_scatters(s):
        for_rows(lambda r: scatter_copy(0, s, r).wait())

    @pl.when((t == 0) & (e == 0))
    def _():
        def place(tok, carry):
            asg[dest_ref[tok]] = tok
            return carry
        lax.fori_loop(0, n_tok, place, 0, unroll=8)

        for g in range(N_GROUPS):
            first = pads_ref[g]
            code = -1 - pads_ref[2 * N_GROUPS + g] + first

            def mark(p, carry):
                asg[p] = code - p
                return carry
            lax.fori_loop(first, pads_ref[N_GROUPS + g], mark, 0)
        ybuf[...] = jnp.zeros(ybuf.shape, f32)
        start_gathers(0, 0)

    @pl.when((e == 0) & (t < nvalid))
    def _():
        wait_gathers(slot)
        base = slot * tm * SLAB_IN
        xbuf[...] = jnp.concatenate(
            [gbuf[pl.ds(base + c, tm, stride=SLAB_IN), :] for c in range(ROW_SLAB)],
            axis=1).astype(bf16)

    @pl.when(t < nvalid)
    def _():
        x = xbuf[...]
        gte = jnp.dot(x, wg_ref[...].astype(bf16), preferred_element_type=f32)
        up = jnp.dot(x, wu_ref[...].astype(bf16), preferred_element_type=f32)
        cw = gbuf[pl.ds(slot * tm * SLAB_IN + ROW_SLAB + e, tm, stride=SLAB_IN), :]
        hid = (gte * _sigmoid(gte)) * up * jnp.concatenate([cw] * (D_EXPERT // LANES), axis=1)
        hbuf[e] = hid.astype(bf16)
        quarter = tm // EXPERTS_PER_GROUP
        have_next = t + 1 < nvalid
        have_prev = t >= 1
        for r in range(quarter):
            row = e * quarter + r
            tok = jnp.where(have_next, jnp.maximum(asg[(t + 1) * tm + row], 0), 0)
            gather_copy(tok, other, row).start()
            a = asg[jnp.maximum(t - 1, 0) * tm + row]
            dst = jnp.where(have_prev, jnp.where(a >= 0, a, n_tok - 1 - a), nt * tm + row)
            scatter_copy(dst, other, row).start(priority=r % 2)

    @pl.when((e == last_e) & (t < nvalid))
    def _():
        hcat = jnp.concatenate([hbuf[k] for k in range(EXPERTS_PER_GROUP)], axis=1)
        y = jnp.dot(hcat, wd_ref[...], preferred_element_type=f32)

        @pl.when(t >= 1)
        def _():
            wait_scatters(slot)
        base = slot * tm * ROW_SLAB
        for c in range(ROW_SLAB):
            ybuf[pl.ds(base + c, tm, stride=ROW_SLAB), :] = y[:, c * LANES:(c + 1) * LANES]

        @pl.when(t == nvalid - 1)
        def _():
            start_scatters(t, slot)
            wait_gathers(other)
            wait_scatters(other)
            wait_scatters(slot)

    @pl.when((e == last_e) & (t == nt - 1))
    def _():
        ybuf[0:tm * ROW_SLAB, :] = jnp.zeros((tm * ROW_SLAB, LANES), f32)
        for j in range(n_tok // tm, nt):
            @pl.when(j >= nvalid)
            def _():
                fill = pltpu.make_async_copy(
                    ybuf.at[0:tm * ROW_SLAB, :],
                    ytok_hbm.at[j * tm * ROW_SLAB:(j + 1) * tm * ROW_SLAB, :], ssem.at[0])
                fill.start()
                fill.wait()


def _moe(h2s, tables, wg, wu, wd, T, tm):
    nvalid, tile_g, dest, pads = tables
    nt = tile_g.shape[0]

    def w_index(t, e, nv, tg, de, pd):
        return (tg[t] * EXPERTS_PER_GROUP + jnp.where(t < nv[0], e, EXPERTS_PER_GROUP - 1), 0, 0)

    grid_spec = pltpu.PrefetchScalarGridSpec(
        num_scalar_prefetch=4,
        grid=(nt, EXPERTS_PER_GROUP),
        in_specs=[
            pl.BlockSpec((None, D_MODEL, D_EXPERT), w_index),
            pl.BlockSpec((None, D_MODEL, D_EXPERT), w_index),
            pl.BlockSpec((None, EXPERTS_PER_GROUP * D_EXPERT, D_MODEL),
                         lambda t, e, nv, tg, de, pd: (tg[t], 0, 0), pipeline_mode=pl.Buffered(1)),
            pl.BlockSpec(memory_space=pl.ANY),
        ],
        out_specs=pl.BlockSpec(memory_space=pl.ANY),
        scratch_shapes=[
            pltpu.SMEM(((nt + 1) * tm,), jnp.int32),
            pltpu.VMEM((2 * tm * SLAB_IN, LANES), f32),
            pltpu.VMEM((2 * tm * ROW_SLAB, LANES), f32),
            pltpu.VMEM((tm, D_MODEL), bf16),
            pltpu.VMEM((EXPERTS_PER_GROUP, tm, D_EXPERT), bf16),
            pltpu.SemaphoreType.DMA((2,)),
            pltpu.SemaphoreType.DMA((2,)),
        ],
    )
    return pl.pallas_call(
        functools.partial(_moe_body, tm=tm, nt=nt, n_tok=T),
        grid_spec=grid_spec,
        out_shape=jax.ShapeDtypeStruct(((nt + 1) * tm * ROW_SLAB, LANES), f32),
        compiler_params=pltpu.CompilerParams(
            dimension_semantics=("arbitrary", "arbitrary"),
            vmem_limit_bytes=VMEM_LIMIT),
        name="moe",
    )(nvalid, tile_g, dest, pads, wg, wu,
      wd.reshape(N_GROUPS, EXPERTS_PER_GROUP * D_EXPERT, D_MODEL), h2s)


def _combine_body(x1_ref, y_ref, o_ref):
    tm = x1_ref.shape[0]
    for c in range(ROW_SLAB):
        cols = slice(c * LANES, (c + 1) * LANES)
        o_ref[:, cols] = x1_ref[:, cols] + y_ref[pl.ds(c, tm, stride=ROW_SLAB), :]


def _combine(x1, ytok, tm=256):
    T = x1.shape[0]
    return pl.pallas_call(
        _combine_body,
        grid=(T // tm,),
        in_specs=[
            pl.BlockSpec((tm, D_MODEL), lambda i: (i, 0)),
            pl.BlockSpec((tm * ROW_SLAB, LANES), lambda i: (i, 0)),
        ],
        out_specs=pl.BlockSpec((tm, D_MODEL), lambda i: (i, 0)),
        out_shape=jax.ShapeDtypeStruct((T, D_MODEL), f32),
        compiler_params=pltpu.CompilerParams(
            dimension_semantics=("arbitrary",),
            vmem_limit_bytes=VMEM_LIMIT),
        name="combine",
    )(x1, ytok)


def _alibi_slopes(n):
    return jnp.exp2(-8.0 * jnp.arange(1, n + 1, dtype=f32) / n)


def kernel(x, g_mix, w_in, q_norm_swa, k_norm_swa, sinks, q_norm_moba, k_norm_moba,
           w_up_swa, w_up_moba, w_out, g_ffn, w_router_group, w_router_expert,
           w_gate_e, w_up_e, w_down_e):
    B, S, D = x.shape
    assert D == D_MODEL and S % MOBA_BLOCK == 0 and S % SWA_BLOCK == 0
    T = B * S
    x2 = x.reshape(T, D)

    proj = _inproj(x2, g_mix, w_in)

    oa = _swa(proj, sinks.astype(f32), _alibi_slopes(SWA_Q_HEADS), q_norm_swa, k_norm_swa, B, S)
    ob = _moba(proj, _alibi_slopes(MOBA_HEADS), q_norm_moba, k_norm_moba, B, S)

    wr = jnp.concatenate(
        [w_router_group,
         w_router_expert.transpose(1, 0, 2).reshape(D, N_EXPERTS),
         jnp.zeros((D, ROUTER_COLS - N_GROUPS - N_EXPERTS), f32)], axis=1)
    wr_hi = wr.astype(bf16)
    wr_lo = (wr - wr_hi.astype(f32)).astype(bf16)
    x1, h2s, gid = _mix(oa, ob, proj, x2, w_up_swa.astype(bf16), w_up_moba.astype(bf16),
                        w_out.astype(bf16), g_ffn, jnp.concatenate([wr_hi, wr_lo], axis=1))

    tables = _route_tables(gid, T, MOE_TILE)
    ytok = _moe(h2s, tables, w_gate_e, w_up_e, w_down_e.astype(bf16),
                T, MOE_TILE)
    y = _combine(x1, ytok)
    return y.reshape(B, S, D)
```

```python
import functools

import jax
import jax.numpy as jnp
from jax import lax
from jax.experimental import pallas as pl
from jax.experimental.pallas import tpu as pltpu

D_MODEL = 2048
HEAD_DIM = 64
ATTN_SCALE = HEAD_DIM ** -0.5
SWA_Q_HEADS = 16
SWA_KV_HEADS = 2
SWA_WINDOW = 128
SWA_BLOCK = 128
MOBA_HEADS = 16
MOBA_BLOCK = 256
MOBA_TOPK = 3
N_GROUPS = 4
EXPERTS_PER_GROUP = 4
N_EXPERTS = N_GROUPS * EXPERTS_PER_GROUP
D_EXPERT = 512
EPS = 1e-6

SWA_Q_DIM = SWA_Q_HEADS * HEAD_DIM
SWA_KV_DIM = SWA_KV_HEADS * HEAD_DIM
MOBA_DIM = MOBA_HEADS * HEAD_DIM
IN_COLS = SWA_Q_DIM + 2 * SWA_KV_DIM + 3 * MOBA_DIM + 2 * D_MODEL

LANES = 128
ROW_SLAB = D_MODEL // LANES
SLAB_IN = ROW_SLAB + 8
MOE_TILE = 512
VMEM_LIMIT = 56 * 1024 * 1024
NEG = -1e30
LOG2E = 1.4426950408889634

COL_QA = 0
COL_KA = COL_QA + SWA_Q_DIM // LANES
COL_VA = COL_KA + 1
COL_QB = COL_VA + 1
COL_KB = COL_QB + MOBA_DIM // LANES
COL_VB = COL_KB + MOBA_DIM // LANES
COL_GATE_A = COL_VB + MOBA_DIM // LANES
COL_GATE_B = COL_GATE_A + D_MODEL // LANES
GATE_BLOCK = 2 * LANES
assert (COL_GATE_A * LANES) % GATE_BLOCK == 0 and (COL_GATE_B * LANES) % GATE_BLOCK == 0

bf16 = jnp.bfloat16
f32 = jnp.float32


def _sigmoid(x):
    return 1.0 / (1.0 + jnp.exp(-x))


def _pair_rms(x, gain):
    lane = lax.broadcasted_iota(jnp.int32, x.shape, 1)
    lo = lane < HEAD_DIM
    sq = x * x
    s0 = jnp.sum(jnp.where(lo, sq, 0.0), axis=-1, keepdims=True)
    s1 = jnp.sum(jnp.where(lo, 0.0, sq), axis=-1, keepdims=True)
    r0 = lax.rsqrt(s0 * (1.0 / HEAD_DIM) + EPS)
    r1 = lax.rsqrt(s1 * (1.0 / HEAD_DIM) + EPS)
    return x * jnp.where(lo, r0, r1) * gain


def _dot_nt(a, b):
    return lax.dot_general(a, b, (((1,), (1,)), ((), ())), preferred_element_type=f32)


def _split2(x):
    hi = x.astype(bf16)
    lo = (x - hi.astype(f32)).astype(bf16)
    return hi, lo


CAST_CHUNK = 2 * LANES
NORM_ROWS = 512


def _inproj_body(x_ref, g_ref, w_ref, o_ref, h_ref):
    @pl.when(pl.program_id(1) == 0)
    def _():
        for r in range(0, x_ref.shape[0], NORM_ROWS):
            x = x_ref[r:r + NORM_ROWS, :]
            ms = jnp.mean(x * x, axis=-1, keepdims=True)
            h_ref[r:r + NORM_ROWS, :] = (x * lax.rsqrt(ms + EPS) * g_ref[...]).astype(bf16)

    n_chunks = w_ref.shape[1] // CAST_CHUNK
    cast = lambda c: w_ref[:, c * CAST_CHUNK:(c + 1) * CAST_CHUNK].astype(bf16)
    w_next = cast(0)
    for c in range(n_chunks):
        w_cur = w_next
        if c + 1 < n_chunks:
            w_next = cast(c + 1)
        o_ref[:, c * CAST_CHUNK:(c + 1) * CAST_CHUNK] = jnp.dot(
            h_ref[...], w_cur, preferred_element_type=f32)


def _inproj(x2, g, w, tm=2048, tn=768):
    T = x2.shape[0]
    N = w.shape[1]
    return pl.pallas_call(
        _inproj_body,
        grid=(T // tm, N // tn),
        in_specs=[
            pl.BlockSpec((tm, D_MODEL), lambda i, j: (i, 0), pipeline_mode=pl.Buffered(1)),
            pl.BlockSpec((1, D_MODEL), lambda i, j: (0, 0)),
            pl.BlockSpec((D_MODEL, tn), lambda i, j: (0, j)),
        ],
        out_specs=pl.BlockSpec((tm, tn), lambda i, j: (i, j)),
        out_shape=jax.ShapeDtypeStruct((T, N), f32),
        scratch_shapes=[pltpu.VMEM((tm, D_MODEL), bf16)],
        compiler_params=pltpu.CompilerParams(
            dimension_semantics=("arbitrary", "arbitrary"),
            vmem_limit_bytes=VMEM_LIMIT),
        name="inproj",
    )(x2, g.reshape(1, D_MODEL), w)


SWA_AHEAD = 2


def _swa_body(sinks_ref, slopes_ref, q_ref, kp_ref, kc_ref, vp_ref, vc_ref,
              qn_ref, kn_ref, o_ref):
    L = SWA_BLOCK
    n = pl.program_id(1)
    k2 = jnp.concatenate([kp_ref[...], kc_ref[...]], axis=0)
    k2n = _pair_rms(k2, kn_ref[...]).astype(bf16)
    v2t = jnp.concatenate([vp_ref[...], vc_ref[...]], axis=0).T.astype(bf16)
    qn = jnp.concatenate(
        [_pair_rms(q_ref[:, pp * LANES:(pp + 1) * LANES], qn_ref[...])
         for pp in range(SWA_Q_HEADS // 2)], axis=1) * ATTN_SCALE
    qt = qn.T

    key = lax.broadcasted_iota(jnp.int32, (2 * L, L), 0)
    qry = lax.broadcasted_iota(jnp.int32, (2 * L, L), 1)
    dist = qry + L - key
    ok = (dist >= 0) & (dist < SWA_WINDOW) & ((n > 0) | (key >= L))
    distf = dist.astype(f32)
    zeros = jnp.zeros((HEAD_DIM, L), f32)

    heads_per_kv = SWA_Q_HEADS // SWA_KV_HEADS

    def scores(h):
        g = h // heads_per_kv
        qh = qt[h * HEAD_DIM:(h + 1) * HEAD_DIM, :]
        qa = jnp.concatenate([qh, zeros] if g == 0 else [zeros, qh], axis=0).astype(bf16)
        return jnp.dot(k2n, qa, preferred_element_type=f32)

    def softmax(h, s):
        s = jnp.where(ok, s - slopes_ref[h] * distf, -jnp.inf)
        sink = sinks_ref[h]
        m = jnp.maximum(jnp.max(s, axis=0, keepdims=True), sink)
        p = jnp.exp(s - m)
        denom = jnp.sum(p, axis=0, keepdims=True) + jnp.exp(sink - m)
        return p.astype(bf16), denom

    outs = []

    def finish(h, p, denom):
        g = h // heads_per_kv
        o = jnp.dot(v2t, p, preferred_element_type=f32)
        outs.append(o[g * HEAD_DIM:(g + 1) * HEAD_DIM, :] / denom)

    pending = [scores(h) for h in range(SWA_AHEAD)]
    deferred = None
    for h in range(SWA_Q_HEADS):
        s = pending.pop(0)
        if h + SWA_AHEAD < SWA_Q_HEADS:
            pending.append(scores(h + SWA_AHEAD))
        p, denom = softmax(h, s)
        if deferred is not None:
            finish(*deferred)
        deferred = (h, p, denom)
    finish(*deferred)
    o_ref[...] = jnp.concatenate(outs, axis=0).T.astype(bf16)


def _swa(proj, sinks, slopes, q_norm, k_norm, B, S):
    L = SWA_BLOCK
    nb = S // L
    T = B * S
    smem = pl.BlockSpec(memory_space=pltpu.SMEM)

    def prev(b, n):
        return b * nb + jnp.maximum(n - 1, 0)

    return pl.pallas_call(
        _swa_body,
        grid=(B, nb),
        in_specs=[
            smem, smem,
            pl.BlockSpec((L, SWA_Q_DIM), lambda b, n: (b * nb + n, COL_QA // (SWA_Q_DIM // LANES))),
            pl.BlockSpec((L, LANES), lambda b, n: (prev(b, n), COL_KA)),
            pl.BlockSpec((L, LANES), lambda b, n: (b * nb + n, COL_KA)),
            pl.BlockSpec((L, LANES), lambda b, n: (prev(b, n), COL_VA)),
            pl.BlockSpec((L, LANES), lambda b, n: (b * nb + n, COL_VA)),
            pl.BlockSpec((1, LANES), lambda b, n: (0, 0)),
            pl.BlockSpec((1, LANES), lambda b, n: (0, 0)),
        ],
        out_specs=pl.BlockSpec((L, SWA_Q_DIM), lambda b, n: (b * nb + n, 0)),
        out_shape=jax.ShapeDtypeStruct((T, SWA_Q_DIM), bf16),
        compiler_params=pltpu.CompilerParams(
            dimension_semantics=("arbitrary", "arbitrary"),
            vmem_limit_bytes=VMEM_LIMIT),
        name="swa",
    )(sinks, slopes, proj, proj, proj, proj, proj,
      jnp.tile(q_norm, 2).reshape(1, LANES), jnp.tile(k_norm, 2).reshape(1, LANES))


N_BIAS_PARTS = 3
NB_ROWS = 8
N_ALIBI_COL = NB_ROWS * N_BIAS_PARTS
SCORE_AHEAD = 3


def _split3(x):
    p0 = x.astype(bf16).astype(f32)
    r1 = x - p0
    p1 = r1.astype(bf16).astype(f32)
    p2 = (r1 - p1).astype(bf16).astype(f32)
    return p0, p1, p2


def _moba_body(slopes_ref, q_ref, k_ref, v_ref, qn_ref, kn_ref, o_ref, kaug_ref, vt_ref, *, nb):
    L = MOBA_BLOCK
    S = nb * L
    p = pl.program_id(1)

    kn = _pair_rms(k_ref[...], kn_ref[...])
    lane = lax.broadcasted_iota(jnp.int32, (L, LANES), 1)
    kpos = lax.broadcasted_iota(jnp.int32, (L, LANES), 0).astype(f32)
    lane_lo = lane < HEAD_DIM

    def alibi_cols(cc, slope):
        a0, a1, a2 = _split3((slope * LOG2E) * kpos)
        return jnp.where(cc == N_ALIBI_COL, a0,
                         jnp.where(cc == N_ALIBI_COL + 1, a1,
                                   jnp.where(cc == N_ALIBI_COL + 2, a2, 0.0)))

    cols = (lane - HEAD_DIM, lane)
    alibi = [alibi_cols(cols[hh], slopes_ref[2 * p + hh]) for hh in range(2)]
    for n in range(nb):
        kblock = kn[n * L:(n + 1) * L, :]
        for hh in range(2):
            cc = cols[hh]
            hot = (cc >= 0) & (cc < N_ALIBI_COL) & ((cc % NB_ROWS) == n)
            aug = jnp.where(hot, 1.0, alibi[hh])
            own = lane_lo if hh == 0 else jnp.logical_not(lane_lo)
            kaug_ref[hh, n * L:(n + 1) * L, :] = jnp.where(own, kblock, aug).astype(bf16)
    vt_ref[...] = v_ref[...].T.astype(bf16)
    means = [jnp.mean(kn[n * L:(n + 1) * L, :], axis=0, keepdims=True) for n in range(nb)]
    means += [jnp.zeros((1, LANES), f32)] * (NB_ROWS - nb)
    km = jnp.concatenate(means, axis=0)
    lane8 = lax.broadcasted_iota(jnp.int32, (NB_ROWS, LANES), 1)
    km_pair = jnp.concatenate(
        [jnp.where(lane8 < HEAD_DIM, km, 0.0), jnp.where(lane8 < HEAD_DIM, 0.0, km)], axis=0)
    km_hi, km_lo = _split2(km_pair)

    qs = _pair_rms(q_ref[...], qn_ref[...]) * (ATTN_SCALE * LOG2E)
    qt = qs.T
    qt_hi, qt_lo = _split2(qt)
    gate = (jnp.dot(km_hi, qt_hi, preferred_element_type=f32)
            + (jnp.dot(km_hi, qt_lo, preferred_element_type=f32)
               + jnp.dot(km_lo, qt_hi, preferred_element_type=f32)))

    blk = lax.broadcasted_iota(jnp.int32, (NB_ROWS, S), 0)
    qblk = lax.broadcasted_iota(jnp.int32, (NB_ROWS, S), 1) // L
    past = blk < qblk
    r = lax.broadcasted_iota(jnp.int32, (L, L), 1)
    c = lax.broadcasted_iota(jnp.int32, (L, L), 0)
    causal = jnp.where(r >= c, 0.0, NEG)
    ones_rows = (blk < 3).astype(f32)

    qa = []
    for hh in range(2):
        g = jnp.where(past, gate[hh * NB_ROWS:(hh + 1) * NB_ROWS, :], -jnp.inf)
        rank = jnp.zeros((NB_ROWS, S), jnp.int32)
        for m in range(nb):
            gm = g[m:m + 1, :]
            ahead = (gm > g) | ((gm == g) & (m < blk))
            rank = rank + ahead.astype(jnp.int32)
        sel = past & (rank < MOBA_TOPK)
        slope = slopes_ref[2 * p + hh]
        bias = jnp.where(sel, (-slope * L * LOG2E) * (qblk - blk).astype(f32), NEG)
        bias = jnp.where(blk == qblk, 0.0, bias)
        b0, b1, b2 = _split3(bias)
        extra = jnp.concatenate(
            [b0, b1, b2, ones_rows,
             jnp.zeros((HEAD_DIM - N_ALIBI_COL - NB_ROWS, S), f32)], axis=0)
        if hh == 0:
            qa.append(jnp.concatenate([qt[:HEAD_DIM], extra], axis=0).astype(bf16))
        else:
            qa.append(jnp.concatenate([extra, qt[HEAD_DIM:]], axis=0).astype(bf16))

    def scores(i, hh):
        return jnp.dot(kaug_ref[hh, 0:(i + 1) * L, :], qa[hh][:, i * L:(i + 1) * L],
                       preferred_element_type=f32)

    units = [(i, hh) for i in range(nb) for hh in range(2)]
    pending = [scores(*u) for u in units[:SCORE_AHEAD]]
    outs = {}

    def finish(i, hh, e_all, l):
        acc = jnp.dot(vt_ref[hh * HEAD_DIM:(hh + 1) * HEAD_DIM, 0:(i + 1) * L], e_all,
                      preferred_element_type=f32)
        outs[(i, hh)] = acc / l
        if hh == 1:
            o_ref[i * L:(i + 1) * L, :] = jnp.concatenate(
                [outs.pop((i, 0)), outs.pop((i, 1))], axis=0).T.astype(bf16)

    deferred = None
    for idx, (i, hh) in enumerate(units):
        s = pending.pop(0)
        if idx + SCORE_AHEAD < len(units):
            pending.append(scores(*units[idx + SCORE_AHEAD]))
        tiles = [s[n * L:(n + 1) * L, :] for n in range(i)] + [s[i * L:(i + 1) * L, :] + causal]
        m = functools.reduce(jnp.maximum, [jnp.max(t, axis=0, keepdims=True) for t in tiles])
        es = [jnp.exp2(t - m) for t in tiles]
        l = functools.reduce(lambda a, b: a + b, [jnp.sum(e, axis=0, keepdims=True) for e in es])
        e_all = jnp.concatenate([e.astype(bf16) for e in es], axis=0)
        if deferred is not None:
            finish(*deferred)
        deferred = (i, hh, e_all, l)
    finish(*deferred)


def _moba(proj, slopes, q_norm, k_norm, B, S):
    L = MOBA_BLOCK
    nb = S // L
    assert nb <= NB_ROWS
    T = B * S
    n_pairs = MOBA_HEADS // 2
    return pl.pallas_call(
        functools.partial(_moba_body, nb=nb),
        grid=(B, n_pairs),
        in_specs=[
            pl.BlockSpec(memory_space=pltpu.SMEM),
            pl.BlockSpec((S, LANES), lambda b, p: (b, COL_QB + p)),
            pl.BlockSpec((S, LANES), lambda b, p: (b, COL_KB + p)),
            pl.BlockSpec((S, LANES), lambda b, p: (b, COL_VB + p)),
            pl.BlockSpec((1, LANES), lambda b, p: (0, 0)),
            pl.BlockSpec((1, LANES), lambda b, p: (0, 0)),
        ],
        out_specs=pl.BlockSpec((S, LANES), lambda b, p: (b, p)),
        out_shape=jax.ShapeDtypeStruct((T, MOBA_DIM), bf16),
        scratch_shapes=[
            pltpu.VMEM((2, S, LANES), bf16),
            pltpu.VMEM((LANES, S), bf16),
        ],
        compiler_params=pltpu.CompilerParams(
            dimension_semantics=("arbitrary", "arbitrary"),
            vmem_limit_bytes=VMEM_LIMIT),
        name="moba",
    )(slopes, proj, proj, proj,
      jnp.tile(q_norm, 2).reshape(1, LANES), jnp.tile(k_norm, 2).reshape(1, LANES))


ROUTER_COLS = LANES


N_GATE_BLOCKS = D_MODEL // GATE_BLOCK
MIX_CHUNK = 2 * GATE_BLOCK


def _mix_body(*refs):
    ga_refs = refs[:N_GATE_BLOCKS]
    gb_refs = refs[N_GATE_BLOCKS:2 * N_GATE_BLOCKS]
    (oa_ref, ob_ref, x_ref, wua_ref, wub_ref, wout_ref, gffn_ref, wr_ref,
     x1_ref, h2_ref, gid_ref) = refs[2 * N_GATE_BLOCKS:]
    oa = oa_ref[...]
    ob = ob_ref[...]
    gpc = MIX_CHUNK // GATE_BLOCK

    def up(c):
        cols = slice(c * MIX_CHUNK, (c + 1) * MIX_CHUNK)
        return (jnp.dot(oa, wua_ref[:, cols], preferred_element_type=f32),
                jnp.dot(ob, wub_ref[:, cols], preferred_element_type=f32))

    x1 = x_ref[...]
    pending = up(0)
    for c in range(D_MODEL // MIX_CHUNK):
        ya, yb = pending
        if (c + 1) * MIX_CHUNK < D_MODEL:
            pending = up(c + 1)
        gate_a = jnp.concatenate([r[...] for r in ga_refs[c * gpc:(c + 1) * gpc]], axis=1)
        gate_b = jnp.concatenate([r[...] for r in gb_refs[c * gpc:(c + 1) * gpc]], axis=1)
        merged = (_sigmoid(gate_a) * ya + _sigmoid(gate_b) * yb).astype(bf16)
        x1 = x1 + jnp.dot(merged, wout_ref[c * MIX_CHUNK:(c + 1) * MIX_CHUNK, :],
                          preferred_element_type=f32)
    x1_ref[...] = x1
    ms = jnp.mean(x1 * x1, axis=-1, keepdims=True)
    h2 = x1 * lax.rsqrt(ms + EPS) * gffn_ref[...]
    tm = h2.shape[0]
    for c in range(ROW_SLAB):
        h2_ref[pl.ds(c, tm, stride=SLAB_IN), :] = h2[:, c * LANES:(c + 1) * LANES]

    h_hi, h_lo = _split2(h2)
    hi_all = jnp.dot(h_hi, wr_ref[...], preferred_element_type=f32)
    lo_hi = jnp.dot(h_lo, wr_ref[:, :ROUTER_COLS], preferred_element_type=f32)
    lt = (hi_all[:, :ROUTER_COLS] + (lo_hi + hi_all[:, ROUTER_COLS:])).T
    gl = [lt[g:g + 1, :] for g in range(N_GROUPS)]
    gmax = functools.reduce(jnp.maximum, gl)
    gsum = functools.reduce(lambda a, b: a + b, [jnp.exp(v - gmax) for v in gl])
    g_p = 1.0 / gsum
    g_i = jnp.full((1, tm), N_GROUPS - 1, jnp.int32)
    for g in reversed(range(N_GROUPS)):
        g_i = jnp.where(gl[g] == gmax, g, g_i)

    el = []
    for e in range(EXPERTS_PER_GROUP):
        v = jnp.zeros((1, tm), f32)
        for g in range(N_GROUPS):
            r = N_GROUPS + g * EXPERTS_PER_GROUP + e
            v = jnp.where(g_i == g, lt[r:r + 1, :], v)
        el.append(v)
    emax = functools.reduce(jnp.maximum, el)
    ex = [jnp.exp(v - emax) for v in el]
    esum = functools.reduce(lambda a, b: a + b, ex)
    ep = [v / esum for v in ex]
    p1 = functools.reduce(jnp.maximum, ep)
    i1 = jnp.full((1, tm), EXPERTS_PER_GROUP - 1, jnp.int32)
    for e in reversed(range(EXPERTS_PER_GROUP)):
        i1 = jnp.where(ep[e] == p1, e, i1)
    rest = [jnp.where(i1 == e, -1.0, ep[e]) for e in range(EXPERTS_PER_GROUP)]
    p2 = functools.reduce(jnp.maximum, rest)
    i2 = jnp.full((1, tm), EXPERTS_PER_GROUP - 1, jnp.int32)
    for e in reversed(range(EXPERTS_PER_GROUP)):
        i2 = jnp.where(rest[e] == p2, e, i2)
    w1 = g_p * (p1 / (p1 + p2))
    w2 = g_p * (p2 / (p1 + p2))
    gid_ref[...] = jnp.concatenate([g_i, jnp.zeros((7, tm), jnp.int32)], axis=0)
    cw = [jnp.where(i1 == e, w1, 0.0) + jnp.where(i2 == e, w2, 0.0) for e in range(EXPERTS_PER_GROUP)]
    cw_t = jnp.concatenate(cw + [jnp.zeros((LANES - EXPERTS_PER_GROUP, tm), f32)], axis=0).T
    for e in range(SLAB_IN - ROW_SLAB):
        if e < EXPERTS_PER_GROUP:
            row = jnp.broadcast_to(cw_t[:, e:e + 1], (tm, LANES))
        else:
            row = jnp.zeros((tm, LANES), f32)
        h2_ref[pl.ds(ROW_SLAB + e, tm, stride=SLAB_IN), :] = row


def _mix(oa, ob, proj, x2, wua, wub, wout, g_ffn, wr, tm=256):
    T = x2.shape[0]
    const = lambda i: (0, 0)
    single = pl.Buffered(1)

    def gate_specs(first_col):
        first = first_col * LANES // GATE_BLOCK
        return [pl.BlockSpec((tm, GATE_BLOCK), functools.partial(lambda i, j: (i, j), j=first + k))
                for k in range(N_GATE_BLOCKS)]

    return pl.pallas_call(
        _mix_body,
        grid=(T // tm,),
        in_specs=gate_specs(COL_GATE_A) + gate_specs(COL_GATE_B) + [
            pl.BlockSpec((tm, SWA_Q_DIM), lambda i: (i, 0)),
            pl.BlockSpec((tm, MOBA_DIM), lambda i: (i, 0)),
            pl.BlockSpec((tm, D_MODEL), lambda i: (i, 0)),
            pl.BlockSpec((SWA_Q_DIM, D_MODEL), const, pipeline_mode=single),
            pl.BlockSpec((MOBA_DIM, D_MODEL), const, pipeline_mode=single),
            pl.BlockSpec((D_MODEL, D_MODEL), const, pipeline_mode=single),
            pl.BlockSpec((1, D_MODEL), const),
            pl.BlockSpec((D_MODEL, 2 * ROUTER_COLS), const),
        ],
        out_specs=[
            pl.BlockSpec((tm, D_MODEL), lambda i: (i, 0)),
            pl.BlockSpec((tm * SLAB_IN, LANES), lambda i: (i, 0)),
            pl.BlockSpec((8, tm), lambda i: (0, i)),
        ],
        out_shape=[
            jax.ShapeDtypeStruct((T, D_MODEL), f32),
            jax.ShapeDtypeStruct((T * SLAB_IN, LANES), f32),
            jax.ShapeDtypeStruct((8, T), jnp.int32),
        ],
        compiler_params=pltpu.CompilerParams(
            dimension_semantics=("arbitrary",),
            vmem_limit_bytes=VMEM_LIMIT),
        name="mix",
    )(*([proj] * (2 * N_GATE_BLOCKS)), oa, ob, x2, wua, wub, wout, g_ffn.reshape(1, D_MODEL), wr)


def _route_tables(gid, T, tm):
    nt = T // tm + N_GROUPS
    g = gid[0]
    onehot = (g[:, None] == jnp.arange(N_GROUPS, dtype=jnp.int32)[None, :]).astype(jnp.int32)
    incl = jnp.cumsum(onehot, axis=0)
    pos = jnp.sum(onehot * incl, axis=1) - 1
    counts = incl[-1]
    padded = (counts + tm - 1) // tm * tm
    seg_end = jnp.cumsum(padded)
    seg_start = seg_end - padded
    dest = jnp.sum(onehot * seg_start[None, :], axis=1) + pos
    tile_start = jnp.arange(nt, dtype=jnp.int32) * tm
    tile_g = jnp.sum((tile_start[:, None] >= seg_end[None, :]).astype(jnp.int32), axis=1)
    tile_g = jnp.minimum(tile_g, N_GROUPS - 1)
    nvalid = (seg_end[-1] // tm).astype(jnp.int32).reshape(1)
    n_pad = padded - counts
    pads = jnp.concatenate([seg_start + counts, seg_end, jnp.cumsum(n_pad) - n_pad])
    return (nvalid, tile_g.astype(jnp.int32), dest.astype(jnp.int32), pads.astype(jnp.int32))


def _moe_body(nvalid_ref, tg_ref, dest_ref, pads_ref, wg_ref, wu_ref, wd_ref, h2s_hbm, ytok_hbm,
              asg, gbuf, ybuf, xbuf, hbuf, gsem, ssem, *, tm, nt, n_tok):
    t = pl.program_id(0)
    e = pl.program_id(1)
    last_e = EXPERTS_PER_GROUP - 1
    nvalid = nvalid_ref[0]
    slot = lax.rem(t, 2)
    other = 1 - slot

    def for_rows(fn):
        def body(r, carry):
            fn(r)
            return carry
        lax.fori_loop(0, tm, body, 0, unroll=8)

    def gather_copy(tok, s, r):
        return pltpu.make_async_copy(
            h2s_hbm.at[pl.ds(pl.multiple_of(tok * SLAB_IN, 8), SLAB_IN), :],
            gbuf.at[pl.ds(pl.multiple_of((s * tm + r) * SLAB_IN, 8), SLAB_IN), :], gsem.at[s])

    def scatter_copy(row, s, r):
        return pltpu.make_async_copy(
            ybuf.at[pl.ds(pl.multiple_of((s * tm + r) * ROW_SLAB, 8), ROW_SLAB), :],
            ytok_hbm.at[pl.ds(pl.multiple_of(row * ROW_SLAB, 8), ROW_SLAB), :], ssem.at[s])

    def start_gathers(tile, s):
        for_rows(lambda r: gather_copy(jnp.maximum(asg[tile * tm + r], 0), s, r).start())

    def start_scatters(tile, s):
        def one(r):
            a = asg[tile * tm + r]
            scatter_copy(jnp.where(a >= 0, a, n_tok - 1 - a), s, r).start()
        for_rows(one)

    def wait_gathers(s):
        for_rows(lambda r: gather_copy(0, s, r).wait())

    def wait_scatters(s):
        for_rows(lambda r: scatter_copy(0, s, r).wait())

    @pl.when((t == 0) & (e == 0))
    def _():
        def place(tok, carry):
            asg[dest_ref[tok]] = tok
            return carry
        lax.fori_loop(0, n_tok, place, 0, unroll=8)

        for g in range(N_GROUPS):
            first = pads_ref[g]
            code = -1 - pads_ref[2 * N_GROUPS + g] + first

            def mark(p, carry):
                asg[p] = code - p
                return carry
            lax.fori_loop(first, pads_ref[N_GROUPS + g], mark, 0)
        ybuf[...] = jnp.zeros(ybuf.shape, f32)
        start_gathers(0, 0)

    @pl.when((e == 0) & (t < nvalid))
    def _():
        wait_gathers(slot)
        base = slot * tm * SLAB_IN
        xbuf[...] = jnp.concatenate(
            [gbuf[pl.ds(base + c, tm, stride=SLAB_IN), :] for c in range(ROW_SLAB)],
            axis=1).astype(bf16)

    @pl.when(t < nvalid)
    def _():
        x = xbuf[...]
        gte = jnp.dot(x, wg_ref[...].astype(bf16), preferred_element_type=f32)
        up = jnp.dot(x, wu_ref[...].astype(bf16), preferred_element_type=f32)
        cw = gbuf[pl.ds(slot * tm * SLAB_IN + ROW_SLAB + e, tm, stride=SLAB_IN), :]
        hid = (gte * _sigmoid(gte)) * up * jnp.concatenate([cw] * (D_EXPERT // LANES), axis=1)
        hbuf[e] = hid.astype(bf16)
        quarter = tm // EXPERTS_PER_GROUP
        have_next = t + 1 < nvalid
        have_prev = t >= 1
        for r in range(quarter):
            row = e * quarter + r
            tok = jnp.where(have_next, jnp.maximum(asg[(t + 1) * tm + row], 0), 0)
            gather_copy(tok, other, row).start()
            a = asg[jnp.maximum(t - 1, 0) * tm + row]
            dst = jnp.where(have_prev, jnp.where(a >= 0, a, n_tok - 1 - a), nt * tm + row)
            scatter_copy(dst, other, row).start(priority=r % 2)

    @pl.when((e == last_e) & (t < nvalid))
    def _():
        hcat = jnp.concatenate([hbuf[k] for k in range(EXPERTS_PER_GROUP)], axis=1)
        y = jnp.dot(hcat, wd_ref[...], preferred_element_type=f32)

        @pl.when(t >= 1)
        def _():
            wait_scatters(slot)
        base = slot * tm * ROW_SLAB
        for c in range(ROW_SLAB):
            ybuf[pl.ds(base + c, tm, stride=ROW_SLAB), :] = y[:, c * LANES:(c + 1) * LANES]

        @pl.when(t == nvalid - 1)
        def _():
            start_scatters(t, slot)
            wait_gathers(other)
            wait_scatters(other)
            wait_scatters(slot)

    @pl.when((e == last_e) & (t == nt - 1))
    def _():
        ybuf[0:tm * ROW_SLAB, :] = jnp.zeros((tm * ROW_SLAB, LANES), f32)
        for j in range(n_tok // tm, nt):
            @pl.when(j >= nvalid)
            def _():
                fill = pltpu.make_async_copy(
                    ybuf.at[0:tm * ROW_SLAB, :],
                    ytok_hbm.at[j * tm * ROW_SLAB:(j + 1) * tm * ROW_SLAB, :], ssem.at[0])
                fill.start()
                fill.wait()


def _moe(h2s, tables, wg, wu, wd, T, tm):
    nvalid, tile_g, dest, pads = tables
    nt = tile_g.shape[0]

    def w_index(t, e, nv, tg, de, pd):
        return (tg[t] * EXPERTS_PER_GROUP + jnp.where(t < nv[0], e, EXPERTS_PER_GROUP - 1), 0, 0)

    grid_spec = pltpu.PrefetchScalarGridSpec(
        num_scalar_prefetch=4,
        grid=(nt, EXPERTS_PER_GROUP),
        in_specs=[
            pl.BlockSpec((None, D_MODEL, D_EXPERT), w_index),
            pl.BlockSpec((None, D_MODEL, D_EXPERT), w_index),
            pl.BlockSpec((None, EXPERTS_PER_GROUP * D_EXPERT, D_MODEL),
                         lambda t, e, nv, tg, de, pd: (tg[t], 0, 0), pipeline_mode=pl.Buffered(1)),
            pl.BlockSpec(memory_space=pl.ANY),
        ],
        out_specs=pl.BlockSpec(memory_space=pl.ANY),
        scratch_shapes=[
            pltpu.SMEM(((nt + 1) * tm,), jnp.int32),
            pltpu.VMEM((2 * tm * SLAB_IN, LANES), f32),
            pltpu.VMEM((2 * tm * ROW_SLAB, LANES), f32),
            pltpu.VMEM((tm, D_MODEL), bf16),
            pltpu.VMEM((EXPERTS_PER_GROUP, tm, D_EXPERT), bf16),
            pltpu.SemaphoreType.DMA((2,)),
            pltpu.SemaphoreType.DMA((2,)),
        ],
    )
    return pl.pallas_call(
        functools.partial(_moe_body, tm=tm, nt=nt, n_tok=T),
        grid_spec=grid_spec,
        out_shape=jax.ShapeDtypeStruct(((nt + 1) * tm * ROW_SLAB, LANES), f32),
        compiler_params=pltpu.CompilerParams(
            dimension_semantics=("arbitrary", "arbitrary"),
            vmem_limit_bytes=VMEM_LIMIT),
        name="moe",
    )(nvalid, tile_g, dest, pads, wg, wu,
      wd.reshape(N_GROUPS, EXPERTS_PER_GROUP * D_EXPERT, D_MODEL), h2s)


def _combine_body(x1_ref, y_ref, o_ref):
    tm = x1_ref.shape[0]
    for c in range(ROW_SLAB):
        cols = slice(c * LANES, (c + 1) * LANES)
        o_ref[:, cols] = x1_ref[:, cols] + y_ref[pl.ds(c, tm, stride=ROW_SLAB), :]


def _combine(x1, ytok, tm=256):
    T = x1.shape[0]
    return pl.pallas_call(
        _combine_body,
        grid=(T // tm,),
        in_specs=[
            pl.BlockSpec((tm, D_MODEL), lambda i: (i, 0)),
            pl.BlockSpec((tm * ROW_SLAB, LANES), lambda i: (i, 0)),
        ],
        out_specs=pl.BlockSpec((tm, D_MODEL), lambda i: (i, 0)),
        out_shape=jax.ShapeDtypeStruct((T, D_MODEL), f32),
        compiler_params=pltpu.CompilerParams(
            dimension_semantics=("arbitrary",),
            vmem_limit_bytes=VMEM_LIMIT),
        name="combine",
    )(x1, ytok)


def _alibi_slopes(n):
    return jnp.exp2(-8.0 * jnp.arange(1, n + 1, dtype=f32) / n)


def kernel(x, g_mix, w_in, q_norm_swa, k_norm_swa, sinks, q_norm_moba, k_norm_moba,
           w_up_swa, w_up_moba, w_out, g_ffn, w_router_group, w_router_expert,
           w_gate_e, w_up_e, w_down_e):
    B, S, D = x.shape
    assert D == D_MODEL and S % MOBA_BLOCK == 0 and S % SWA_BLOCK == 0
    T = B * S
    x2 = x.reshape(T, D)

    proj = _inproj(x2, g_mix, w_in)

    oa = _swa(proj, sinks.astype(f32), _alibi_slopes(SWA_Q_HEADS), q_norm_swa, k_norm_swa, B, S)
    ob = _moba(proj, _alibi_slopes(MOBA_HEADS), q_norm_moba, k_norm_moba, B, S)

    wr = jnp.concatenate(
        [w_router_group,
         w_router_expert.transpose(1, 0, 2).reshape(D, N_EXPERTS),
         jnp.zeros((D, ROUTER_COLS - N_GROUPS - N_EXPERTS), f32)], axis=1)
    wr_hi = wr.astype(bf16)
    wr_lo = (wr - wr_hi.astype(f32)).astype(bf16)
    x1, h2s, gid = _mix(oa, ob, proj, x2, w_up_swa.astype(bf16), w_up_moba.astype(bf16),
                        w_out.astype(bf16), g_ffn, jnp.concatenate([wr_hi, wr_lo], axis=1))

    tables = _route_tables(gid, T, MOE_TILE)
    ytok = _moe(h2s, tables, w_gate_e, w_up_e, w_down_e.astype(bf16),
                T, MOE_TILE)
    y = _combine(x1, ytok)
    return y.reshape(B, S, D)
```

```python
import functools

import jax
import jax.numpy as jnp
from jax import lax
from jax.experimental import pallas as pl
from jax.experimental.pallas import tpu as pltpu

D_MODEL = 2048
HEAD_DIM = 64
ATTN_SCALE = HEAD_DIM ** -0.5
SWA_Q_HEADS = 16
SWA_KV_HEADS = 2
SWA_WINDOW = 128
SWA_BLOCK = 128
MOBA_HEADS = 16
MOBA_BLOCK = 256
MOBA_TOPK = 3
N_GROUPS = 4
EXPERTS_PER_GROUP = 4
N_EXPERTS = N_GROUPS * EXPERTS_PER_GROUP
D_EXPERT = 512
EPS = 1e-6

SWA_Q_DIM = SWA_Q_HEADS * HEAD_DIM
SWA_KV_DIM = SWA_KV_HEADS * HEAD_DIM
MOBA_DIM = MOBA_HEADS * HEAD_DIM
IN_COLS = SWA_Q_DIM + 2 * SWA_KV_DIM + 3 * MOBA_DIM + 2 * D_MODEL

LANES = 128
ROW_SLAB = D_MODEL // LANES
SLAB_IN = ROW_SLAB + 8
MOE_TILE = 512
VMEM_LIMIT = 56 * 1024 * 1024
NEG = -1e30
LOG2E = 1.4426950408889634

COL_QA = 0
COL_KA = COL_QA + SWA_Q_DIM // LANES
COL_VA = COL_KA + 1
COL_QB = COL_VA + 1
COL_KB = COL_QB + MOBA_DIM // LANES
COL_VB = COL_KB + MOBA_DIM // LANES
COL_GATE_A = COL_VB + MOBA_DIM // LANES
COL_GATE_B = COL_GATE_A + D_MODEL // LANES
GATE_BLOCK = 2 * LANES
assert (COL_GATE_A * LANES) % GATE_BLOCK == 0 and (COL_GATE_B * LANES) % GATE_BLOCK == 0

bf16 = jnp.bfloat16
f32 = jnp.float32


def _sigmoid(x):
    return 1.0 / (1.0 + jnp.exp(-x))


def _pair_rms(x, gain):
    lane = lax.broadcasted_iota(jnp.int32, x.shape, 1)
    lo = lane < HEAD_DIM
    sq = x * x
    s0 = jnp.sum(jnp.where(lo, sq, 0.0), axis=-1, keepdims=True)
    s1 = jnp.sum(jnp.where(lo, 0.0, sq), axis=-1, keepdims=True)
    r0 = lax.rsqrt(s0 * (1.0 / HEAD_DIM) + EPS)
    r1 = lax.rsqrt(s1 * (1.0 / HEAD_DIM) + EPS)
    return x * jnp.where(lo, r0, r1) * gain


def _dot_nt(a, b):
    return lax.dot_general(a, b, (((1,), (1,)), ((), ())), preferred_element_type=f32)


def _split2(x):
    hi = x.astype(bf16)
    lo = (x - hi.astype(f32)).astype(bf16)
    return hi, lo


CAST_CHUNK = 2 * LANES
NORM_ROWS = 512


def _inproj_body(x_ref, g_ref, w_ref, o_ref, h_ref):
    @pl.when(pl.program_id(1) == 0)
    def _():
        for r in range(0, x_ref.shape[0], NORM_ROWS):
            x = x_ref[r:r + NORM_ROWS, :]
            ms = jnp.mean(x * x, axis=-1, keepdims=True)
            h_ref[r:r + NORM_ROWS, :] = (x * lax.rsqrt(ms + EPS) * g_ref[...]).astype(bf16)

    n_chunks = w_ref.shape[1] // CAST_CHUNK
    cast = lambda c: w_ref[:, c * CAST_CHUNK:(c + 1) * CAST_CHUNK].astype(bf16)
    w_next = cast(0)
    for c in range(n_chunks):
        w_cur = w_next
        if c + 1 < n_chunks:
            w_next = cast(c + 1)
        o_ref[:, c * CAST_CHUNK:(c + 1) * CAST_CHUNK] = jnp.dot(
            h_ref[...], w_cur, preferred_element_type=f32)


def _inproj(x2, g, w, tm=2048, tn=768):
    T = x2.shape[0]
    N = w.shape[1]
    return pl.pallas_call(
        _inproj_body,
        grid=(T // tm, N // tn),
        in_specs=[
            pl.BlockSpec((tm, D_MODEL), lambda i, j: (i, 0), pipeline_mode=pl.Buffered(1)),
            pl.BlockSpec((1, D_MODEL), lambda i, j: (0, 0)),
            pl.BlockSpec((D_MODEL, tn), lambda i, j: (0, j)),
        ],
        out_specs=pl.BlockSpec((tm, tn), lambda i, j: (i, j)),
        out_shape=jax.ShapeDtypeStruct((T, N), f32),
        scratch_shapes=[pltpu.VMEM((tm, D_MODEL), bf16)],
        compiler_params=pltpu.CompilerParams(
            dimension_semantics=("arbitrary", "arbitrary"),
            vmem_limit_bytes=VMEM_LIMIT),
        name="inproj",
    )(x2, g.reshape(1, D_MODEL), w)


SWA_AHEAD = 2


def _swa_body(sinks_ref, slopes_ref, q_ref, kp_ref, kc_ref, vp_ref, vc_ref,
              qn_ref, kn_ref, o_ref):
    L = SWA_BLOCK
    n = pl.program_id(1)
    k2 = jnp.concatenate([kp_ref[...], kc_ref[...]], axis=0)
    k2n = _pair_rms(k2, kn_ref[...]).astype(bf16)
    v2t = jnp.concatenate([vp_ref[...], vc_ref[...]], axis=0).T.astype(bf16)
    qn = jnp.concatenate(
        [_pair_rms(q_ref[:, pp * LANES:(pp + 1) * LANES], qn_ref[...])
         for pp in range(SWA_Q_HEADS // 2)], axis=1) * ATTN_SCALE
    qt = qn.T

    key = lax.broadcasted_iota(jnp.int32, (2 * L, L), 0)
    qry = lax.broadcasted_iota(jnp.int32, (2 * L, L), 1)
    dist = qry + L - key
    ok = (dist >= 0) & (dist < SWA_WINDOW) & ((n > 0) | (key >= L))
    distf = dist.astype(f32)
    zeros = jnp.zeros((HEAD_DIM, L), f32)

    heads_per_kv = SWA_Q_HEADS // SWA_KV_HEADS

    def scores(h):
        g = h // heads_per_kv
        qh = qt[h * HEAD_DIM:(h + 1) * HEAD_DIM, :]
        qa = jnp.concatenate([qh, zeros] if g == 0 else [zeros, qh], axis=0).astype(bf16)
        return jnp.dot(k2n, qa, preferred_element_type=f32)

    def softmax(h, s):
        s = jnp.where(ok, s - slopes_ref[h] * distf, -jnp.inf)
        sink = sinks_ref[h]
        m = jnp.maximum(jnp.max(s, axis=0, keepdims=True), sink)
        p = jnp.exp(s - m)
        denom = jnp.sum(p, axis=0, keepdims=True) + jnp.exp(sink - m)
        return p.astype(bf16), denom

    outs = []

    def finish(h, p, denom):
        g = h // heads_per_kv
        o = jnp.dot(v2t, p, preferred_element_type=f32)
        outs.append(o[g * HEAD_DIM:(g + 1) * HEAD_DIM, :] / denom)

    pending = [scores(h) for h in range(SWA_AHEAD)]
    deferred = None
    for h in range(SWA_Q_HEADS):
        s = pending.pop(0)
        if h + SWA_AHEAD < SWA_Q_HEADS:
            pending.append(scores(h + SWA_AHEAD))
        p, denom = softmax(h, s)
        if deferred is not None:
            finish(*deferred)
        deferred = (h, p, denom)
    finish(*deferred)
    o_ref[...] = jnp.concatenate(outs, axis=0).T.astype(bf16)


def _swa(proj, sinks, slopes, q_norm, k_norm, B, S):
    L = SWA_BLOCK
    nb = S // L
    T = B * S
    smem = pl.BlockSpec(memory_space=pltpu.SMEM)

    def prev(b, n):
        return b * nb + jnp.maximum(n - 1, 0)

    return pl.pallas_call(
        _swa_body,
        grid=(B, nb),
        in_specs=[
            smem, smem,
            pl.BlockSpec((L, SWA_Q_DIM), lambda b, n: (b * nb + n, COL_QA // (SWA_Q_DIM // LANES))),
            pl.BlockSpec((L, LANES), lambda b, n: (prev(b, n), COL_KA)),
            pl.BlockSpec((L, LANES), lambda b, n: (b * nb + n, COL_KA)),
            pl.BlockSpec((L, LANES), lambda b, n: (prev(b, n), COL_VA)),
            pl.BlockSpec((L, LANES), lambda b, n: (b * nb + n, COL_VA)),
            pl.BlockSpec((1, LANES), lambda b, n: (0, 0)),
            pl.BlockSpec((1, LANES), lambda b, n: (0, 0)),
        ],
        out_specs=pl.BlockSpec((L, SWA_Q_DIM), lambda b, n: (b * nb + n, 0)),
        out_shape=jax.ShapeDtypeStruct((T, SWA_Q_DIM), bf16),
        compiler_params=pltpu.CompilerParams(
            dimension_semantics=("arbitrary", "arbitrary"),
            vmem_limit_bytes=VMEM_LIMIT),
        name="swa",
    )(sinks, slopes, proj, proj, proj, proj, proj,
      jnp.tile(q_norm, 2).reshape(1, LANES), jnp.tile(k_norm, 2).reshape(1, LANES))


N_BIAS_PARTS = 3
NB_ROWS = 8
N_ALIBI_COL = NB_ROWS * N_BIAS_PARTS
SCORE_AHEAD = 3


def _split3(x):
    p0 = x.astype(bf16).astype(f32)
    r1 = x - p0
    p1 = r1.astype(bf16).astype(f32)
    p2 = (r1 - p1).astype(bf16).astype(f32)
    return p0, p1, p2


def _moba_body(slopes_ref, q_ref, k_ref, v_ref, qn_ref, kn_ref, o_ref, kaug_ref, vt_ref, *, nb):
    L = MOBA_BLOCK
    S = nb * L
    p = pl.program_id(1)

    kn = _pair_rms(k_ref[...], kn_ref[...])
    lane = lax.broadcasted_iota(jnp.int32, (L, LANES), 1)
    kpos = lax.broadcasted_iota(jnp.int32, (L, LANES), 0).astype(f32)
    lane_lo = lane < HEAD_DIM

    def alibi_cols(cc, slope):
        a0, a1, a2 = _split3((slope * LOG2E) * kpos)
        return jnp.where(cc == N_ALIBI_COL, a0,
                         jnp.where(cc == N_ALIBI_COL + 1, a1,
                                   jnp.where(cc == N_ALIBI_COL + 2, a2, 0.0)))

    cols = (lane - HEAD_DIM, lane)
    alibi = [alibi_cols(cols[hh], slopes_ref[2 * p + hh]) for hh in range(2)]
    for n in range(nb):
        kblock = kn[n * L:(n + 1) * L, :]
        for hh in range(2):
            cc = cols[hh]
            hot = (cc >= 0) & (cc < N_ALIBI_COL) & ((cc % NB_ROWS) == n)
            aug = jnp.where(hot, 1.0, alibi[hh])
            own = lane_lo if hh == 0 else jnp.logical_not(lane_lo)
            kaug_ref[hh, n * L:(n + 1) * L, :] = jnp.where(own, kblock, aug).astype(bf16)
    vt_ref[...] = v_ref[...].T.astype(bf16)
    means = [jnp.mean(kn[n * L:(n + 1) * L, :], axis=0, keepdims=True) for n in range(nb)]
    means += [jnp.zeros((1, LANES), f32)] * (NB_ROWS - nb)
    km = jnp.concatenate(means, axis=0)
    lane8 = lax.broadcasted_iota(jnp.int32, (NB_ROWS, LANES), 1)
    km_pair = jnp.concatenate(
        [jnp.where(lane8 < HEAD_DIM, km, 0.0), jnp.where(lane8 < HEAD_DIM, 0.0, km)], axis=0)
    km_hi, km_lo = _split2(km_pair)

    qs = _pair_rms(q_ref[...], qn_ref[...]) * (ATTN_SCALE * LOG2E)
    qt = qs.T
    qt_hi, qt_lo = _split2(qt)
    gate = (jnp.dot(km_hi, qt_hi, preferred_element_type=f32)
            + (jnp.dot(km_hi, qt_lo, preferred_element_type=f32)
               + jnp.dot(km_lo, qt_hi, preferred_element_type=f32)))

    blk = lax.broadcasted_iota(jnp.int32, (NB_ROWS, S), 0)
    qblk = lax.broadcasted_iota(jnp.int32, (NB_ROWS, S), 1) // L
    past = blk < qblk
    r = lax.broadcasted_iota(jnp.int32, (L, L), 1)
    c = lax.broadcasted_iota(jnp.int32, (L, L), 0)
    causal = jnp.where(r >= c, 0.0, NEG)
    ones_rows = (blk < 3).astype(f32)

    qa = []
    for hh in range(2):
        g = jnp.where(past, gate[hh * NB_ROWS:(hh + 1) * NB_ROWS, :], -jnp.inf)
        rank = jnp.zeros((NB_ROWS, S), jnp.int32)
        for m in range(nb):
            gm = g[m:m + 1, :]
            ahead = (gm > g) | ((gm == g) & (m < blk))
            rank = rank + ahead.astype(jnp.int32)
        sel = past & (rank < MOBA_TOPK)
        slope = slopes_ref[2 * p + hh]
        bias = jnp.where(sel, (-slope * L * LOG2E) * (qblk - blk).astype(f32), NEG)
        bias = jnp.where(blk == qblk, 0.0, bias)
        b0, b1, b2 = _split3(bias)
        extra = jnp.concatenate(
            [b0, b1, b2, ones_rows,
             jnp.zeros((HEAD_DIM - N_ALIBI_COL - NB_ROWS, S), f32)], axis=0)
        if hh == 0:
            qa.append(jnp.concatenate([qt[:HEAD_DIM], extra], axis=0).astype(bf16))
        else:
            qa.append(jnp.concatenate([extra, qt[HEAD_DIM:]], axis=0).astype(bf16))

    def scores(i, hh):
        return jnp.dot(kaug_ref[hh, 0:(i + 1) * L, :], qa[hh][:, i * L:(i + 1) * L],
                       preferred_element_type=f32)

    units = [(i, hh) for i in range(nb) for hh in range(2)]
    pending = [scores(*u) for u in units[:SCORE_AHEAD]]
    outs = {}

    def finish(i, hh, e_all, l):
        acc = jnp.dot(vt_ref[hh * HEAD_DIM:(hh + 1) * HEAD_DIM, 0:(i + 1) * L], e_all,
                      preferred_element_type=f32)
        outs[(i, hh)] = acc / l
        if hh == 1:
            o_ref[i * L:(i + 1) * L, :] = jnp.concatenate(
                [outs.pop((i, 0)), outs.pop((i, 1))], axis=0).T.astype(bf16)

    deferred = None
    for idx, (i, hh) in enumerate(units):
        s = pending.pop(0)
        if idx + SCORE_AHEAD < len(units):
            pending.append(scores(*units[idx + SCORE_AHEAD]))
        tiles = [s[n * L:(n + 1) * L, :] for n in range(i)] + [s[i * L:(i + 1) * L, :] + causal]
        m = functools.reduce(jnp.maximum, [jnp.max(t, axis=0, keepdims=True) for t in tiles])
        es = [jnp.exp2(t - m) for t in tiles]
        l = functools.reduce(lambda a, b: a + b, [jnp.sum(e, axis=0, keepdims=True) for e in es])
        e_all = jnp.concatenate([e.astype(bf16) for e in es], axis=0)
        if deferred is not None:
            finish(*deferred)
        deferred = (i, hh, e_all, l)
    finish(*deferred)


def _moba(proj, slopes, q_norm, k_norm, B, S):
    L = MOBA_BLOCK
    nb = S // L
    assert nb <= NB_ROWS
    T = B * S
    n_pairs = MOBA_HEADS // 2
    return pl.pallas_call(
        functools.partial(_moba_body, nb=nb),
        grid=(B, n_pairs),
        in_specs=[
            pl.BlockSpec(memory_space=pltpu.SMEM),
            pl.BlockSpec((S, LANES), lambda b, p: (b, COL_QB + p)),
            pl.BlockSpec((S, LANES), lambda b, p: (b, COL_KB + p)),
            pl.BlockSpec((S, LANES), lambda b, p: (b, COL_VB + p)),
            pl.BlockSpec((1, LANES), lambda b, p: (0, 0)),
            pl.BlockSpec((1, LANES), lambda b, p: (0, 0)),
        ],
        out_specs=pl.BlockSpec((S, LANES), lambda b, p: (b, p)),
        out_shape=jax.ShapeDtypeStruct((T, MOBA_DIM), bf16),
        scratch_shapes=[
            pltpu.VMEM((2, S, LANES), bf16),
            pltpu.VMEM((LANES, S), bf16),
        ],
        compiler_params=pltpu.CompilerParams(
            dimension_semantics=("arbitrary", "arbitrary"),
            vmem_limit_bytes=VMEM_LIMIT),
        name="moba",
    )(slopes, proj, proj, proj,
      jnp.tile(q_norm, 2).reshape(1, LANES), jnp.tile(k_norm, 2).reshape(1, LANES))


ROUTER_COLS = LANES


N_GATE_BLOCKS = D_MODEL // GATE_BLOCK
MIX_CHUNK = 2 * GATE_BLOCK


def _mix_body(*refs):
    ga_refs = refs[:N_GATE_BLOCKS]
    gb_refs = refs[N_GATE_BLOCKS:2 * N_GATE_BLOCKS]
    (oa_ref, ob_ref, x_ref, wua_ref, wub_ref, wout_ref, gffn_ref, wr_ref,
     x1_ref, h2_ref, gid_ref) = refs[2 * N_GATE_BLOCKS:]
    oa = oa_ref[...]
    ob = ob_ref[...]
    gpc = MIX_CHUNK // GATE_BLOCK

    def up(c):
        cols = slice(c * MIX_CHUNK, (c + 1) * MIX_CHUNK)
        return (jnp.dot(oa, wua_ref[:, cols], preferred_element_type=f32),
                jnp.dot(ob, wub_ref[:, cols], preferred_element_type=f32))

    x1 = x_ref[...]
    pending = up(0)
    for c in range(D_MODEL // MIX_CHUNK):
        ya, yb = pending
        if (c + 1) * MIX_CHUNK < D_MODEL:
            pending = up(c + 1)
        gate_a = jnp.concatenate([r[...] for r in ga_refs[c * gpc:(c + 1) * gpc]], axis=1)
        gate_b = jnp.concatenate([r[...] for r in gb_refs[c * gpc:(c + 1) * gpc]], axis=1)
        merged = (_sigmoid(gate_a) * ya + _sigmoid(gate_b) * yb).astype(bf16)
        x1 = x1 + jnp.dot(merged, wout_ref[c * MIX_CHUNK:(c + 1) * MIX_CHUNK, :],
                          preferred_element_type=f32)
    x1_ref[...] = x1
    ms = jnp.mean(x1 * x1, axis=-1, keepdims=True)
    h2 = x1 * lax.rsqrt(ms + EPS) * gffn_ref[...]
    tm = h2.shape[0]
    for c in range(ROW_SLAB):
        h2_ref[pl.ds(c, tm, stride=SLAB_IN), :] = h2[:, c * LANES:(c + 1) * LANES]

    h_hi, h_lo = _split2(h2)
    hi_all = jnp.dot(h_hi, wr_ref[...], preferred_element_type=f32)
    lo_hi = jnp.dot(h_lo, wr_ref[:, :ROUTER_COLS], preferred_element_type=f32)
    lt = (hi_all[:, :ROUTER_COLS] + (lo_hi + hi_all[:, ROUTER_COLS:])).T
    gl = [lt[g:g + 1, :] for g in range(N_GROUPS)]
    gmax = functools.reduce(jnp.maximum, gl)
    gsum = functools.reduce(lambda a, b: a + b, [jnp.exp(v - gmax) for v in gl])
    g_p = 1.0 / gsum
    g_i = jnp.full((1, tm), N_GROUPS - 1, jnp.int32)
    for g in reversed(range(N_GROUPS)):
        g_i = jnp.where(gl[g] == gmax, g, g_i)

    el = []
    for e in range(EXPERTS_PER_GROUP):
        v = jnp.zeros((1, tm), f32)
        for g in range(N_GROUPS):
            r = N_GROUPS + g * EXPERTS_PER_GROUP + e
            v = jnp.where(g_i == g, lt[r:r + 1, :], v)
        el.append(v)
    emax = functools.reduce(jnp.maximum, el)
    ex = [jnp.exp(v - emax) for v in el]
    esum = functools.reduce(lambda a, b: a + b, ex)
    ep = [v / esum for v in ex]
    p1 = functools.reduce(jnp.maximum, ep)
    i1 = jnp.full((1, tm), EXPERTS_PER_GROUP - 1, jnp.int32)
    for e in reversed(range(EXPERTS_PER_GROUP)):
        i1 = jnp.where(ep[e] == p1, e, i1)
    rest = [jnp.where(i1 == e, -1.0, ep[e]) for e in range(EXPERTS_PER_GROUP)]
    p2 = functools.reduce(jnp.maximum, rest)
    i2 = jnp.full((1, tm), EXPERTS_PER_GROUP - 1, jnp.int32)
    for e in reversed(range(EXPERTS_PER_GROUP)):
        i2 = jnp.where(rest[e] == p2, e, i2)
    w1 = g_p * (p1 / (p1 + p2))
    w2 = g_p * (p2 / (p1 + p2))
    gid_ref[...] = jnp.concatenate([g_i, jnp.zeros((7, tm), jnp.int32)], axis=0)
    cw = [jnp.where(i1 == e, w1, 0.0) + jnp.where(i2 == e, w2, 0.0) for e in range(EXPERTS_PER_GROUP)]
    cw_t = jnp.concatenate(cw + [jnp.zeros((LANES - EXPERTS_PER_GROUP, tm), f32)], axis=0).T
    for e in range(SLAB_IN - ROW_SLAB):
        if e < EXPERTS_PER_GROUP:
            row = jnp.broadcast_to(cw_t[:, e:e + 1], (tm, LANES))
        else:
            row = jnp.zeros((tm, LANES), f32)
        h2_ref[pl.ds(ROW_SLAB + e, tm, stride=SLAB_IN), :] = row


def _mix(oa, ob, proj, x2, wua, wub, wout, g_ffn, wr, tm=256):
    T = x2.shape[0]
    const = lambda i: (0, 0)
    single = pl.Buffered(1)

    def gate_specs(first_col):
        first = first_col * LANES // GATE_BLOCK
        return [pl.BlockSpec((tm, GATE_BLOCK), functools.partial(lambda i, j: (i, j), j=first + k))
                for k in range(N_GATE_BLOCKS)]

    return pl.pallas_call(
        _mix_body,
        grid=(T // tm,),
        in_specs=gate_specs(COL_GATE_A) + gate_specs(COL_GATE_B) + [
            pl.BlockSpec((tm, SWA_Q_DIM), lambda i: (i, 0)),
            pl.BlockSpec((tm, MOBA_DIM), lambda i: (i, 0)),
            pl.BlockSpec((tm, D_MODEL), lambda i: (i, 0)),
            pl.BlockSpec((SWA_Q_DIM, D_MODEL), const, pipeline_mode=single),
            pl.BlockSpec((MOBA_DIM, D_MODEL), const, pipeline_mode=single),
            pl.BlockSpec((D_MODEL, D_MODEL), const, pipeline_mode=single),
            pl.BlockSpec((1, D_MODEL), const),
            pl.BlockSpec((D_MODEL, 2 * ROUTER_COLS), const),
        ],
        out_specs=[
            pl.BlockSpec((tm, D_MODEL), lambda i: (i, 0)),
            pl.BlockSpec((tm * SLAB_IN, LANES), lambda i: (i, 0)),
            pl.BlockSpec((8, tm), lambda i: (0, i)),
        ],
        out_shape=[
            jax.ShapeDtypeStruct((T, D_MODEL), f32),
            jax.ShapeDtypeStruct((T * SLAB_IN, LANES), f32),
            jax.ShapeDtypeStruct((8, T), jnp.int32),
        ],
        compiler_params=pltpu.CompilerParams(
            dimension_semantics=("arbitrary",),
            vmem_limit_bytes=VMEM_LIMIT),
        name="mix",
    )(*([proj] * (2 * N_GATE_BLOCKS)), oa, ob, x2, wua, wub, wout, g_ffn.reshape(1, D_MODEL), wr)


def _route_tables(gid, T, tm):
    nt = T // tm + N_GROUPS
    g = gid[0]
    onehot = (g[:, None] == jnp.arange(N_GROUPS, dtype=jnp.int32)[None, :]).astype(jnp.int32)
    incl = jnp.cumsum(onehot, axis=0)
    pos = jnp.sum(onehot * incl, axis=1) - 1
    counts = incl[-1]
    padded = (counts + tm - 1) // tm * tm
    seg_end = jnp.cumsum(padded)
    seg_start = seg_end - padded
    dest = jnp.sum(onehot * seg_start[None, :], axis=1) + pos
    tile_start = jnp.arange(nt, dtype=jnp.int32) * tm
    tile_g = jnp.sum((tile_start[:, None] >= seg_end[None, :]).astype(jnp.int32), axis=1)
    tile_g = jnp.minimum(tile_g, N_GROUPS - 1)
    nvalid = (seg_end[-1] // tm).astype(jnp.int32).reshape(1)
    n_pad = padded - counts
    pads = jnp.concatenate([seg_start + counts, seg_end, jnp.cumsum(n_pad) - n_pad])
    return (nvalid, tile_g.astype(jnp.int32), dest.astype(jnp.int32), pads.astype(jnp.int32))


def _moe_body(nvalid_ref, tg_ref, dest_ref, pads_ref, wg_ref, wu_ref, wd_ref, h2s_hbm, ytok_hbm,
              asg, gbuf, ybuf, xbuf, hbuf, gsem, ssem, *, tm, nt, n_tok):
    t = pl.program_id(0)
    e = pl.program_id(1)
    last_e = EXPERTS_PER_GROUP - 1
    nvalid = nvalid_ref[0]
    slot = lax.rem(t, 2)
    other = 1 - slot

    def for_rows(fn):
        def body(r, carry):
            fn(r)
            return carry
        lax.fori_loop(0, tm, body, 0, unroll=8)

    def gather_copy(tok, s, r):
        return pltpu.make_async_copy(
            h2s_hbm.at[pl.ds(pl.multiple_of(tok * SLAB_IN, 8), SLAB_IN), :],
            gbuf.at[pl.ds(pl.multiple_of((s * tm + r) * SLAB_IN, 8), SLAB_IN), :], gsem.at[s])

    def scatter_copy(row, s, r):
        return pltpu.make_async_copy(
            ybuf.at[pl.ds(pl.multiple_of((s * tm + r) * ROW_SLAB, 8), ROW_SLAB), :],
            ytok_hbm.at[pl.ds(pl.multiple_of(row * ROW_SLAB, 8), ROW_SLAB), :], ssem.at[s])

    def start_gathers(tile, s):
        for_rows(lambda r: gather_copy(jnp.maximum(asg[tile * tm + r], 0), s, r).start())

    def start_scatters(tile, s):
        def one(r):
            a = asg[tile * tm + r]
            scatter_copy(jnp.where(a >= 0, a, n_tok - 1 - a), s, r).start()
        for_rows(one)

    def wait_gathers(s):
        for_rows(lambda r: gather_copy(0, s, r).wait())

    def wait_scatters(s):
        for_rows(lambda r: scatter_copy(0, s, r).wait())

    @pl.when((t == 0) & (e == 0))
    def _():
        def place(tok, carry):
            asg[dest_ref[tok]] = tok
            return carry
        lax.fori_loop(0, n_tok, place, 0, unroll=8)

        for g in range(N_GROUPS):
            first = pads_ref[g]
            code = -1 - pads_ref[2 * N_GROUPS + g] + first

            def mark(p, carry):
                asg[p] = code - p
                return carry
            lax.fori_loop(first, pads_ref[N_GROUPS + g], mark, 0)
        ybuf[...] = jnp.zeros(ybuf.shape, f32)
        start_gathers(0, 0)

    @pl.when((e == 0) & (t < nvalid))
    def _():
        wait_gathers(slot)
        base = slot * tm * SLAB_IN
        xbuf[...] = jnp.concatenate(
            [gbuf[pl.ds(base + c, tm, stride=SLAB_IN), :] for c in range(ROW_SLAB)],
            axis=1).astype(bf16)

    @pl.when(t < nvalid)
    def _():
        quarter = tm // EXPERTS_PER_GROUP
        have_next = t + 1 < nvalid
        have_prev = t >= 1
        nxt = jnp.minimum(t + 1, nvalid - 1)
        for r in range(quarter):
            row = e * quarter + r
            tok = jnp.where(have_next, jnp.maximum(asg[nxt * tm + row], 0), 0)
            gather_copy(tok, other, row).start()
            a = asg[jnp.maximum(t - 1, 0) * tm + row]
            dst = jnp.where(have_prev, jnp.where(a >= 0, a, n_tok - 1 - a), nt * tm + row)
            scatter_copy(dst, other, row).start(priority=r % 2)

        x = xbuf[...]
        gte = jnp.dot(x, wg_ref[...].astype(bf16), preferred_element_type=f32)
        up = jnp.dot(x, wu_ref[...].astype(bf16), preferred_element_type=f32)
        cw = gbuf[pl.ds(slot * tm * SLAB_IN + ROW_SLAB + e, tm, stride=SLAB_IN), :]
        hid = (gte * _sigmoid(gte)) * up * jnp.concatenate([cw] * (D_EXPERT // LANES), axis=1)
        hbuf[e] = hid.astype(bf16)

    @pl.when((e == last_e) & (t < nvalid))
    def _():
        hcat = jnp.concatenate([hbuf[k] for k in range(EXPERTS_PER_GROUP)], axis=1)
        y = jnp.dot(hcat, wd_ref[...], preferred_element_type=f32)

        @pl.when(t >= 1)
        def _():
            wait_scatters(slot)
        base = slot * tm * ROW_SLAB
        for c in range(ROW_SLAB):
            ybuf[pl.ds(base + c, tm, stride=ROW_SLAB), :] = y[:, c * LANES:(c + 1) * LANES]

        @pl.when(t == nvalid - 1)
        def _():
            start_scatters(t, slot)
            wait_gathers(other)
            wait_scatters(other)
            wait_scatters(slot)

    @pl.when((e == last_e) & (t == nt - 1))
    def _():
        ybuf[0:tm * ROW_SLAB, :] = jnp.zeros((tm * ROW_SLAB, LANES), f32)
        for j in range(n_tok // tm, nt):
            @pl.when(j >= nvalid)
            def _():
                fill = pltpu.make_async_copy(
                    ybuf.at[0:tm * ROW_SLAB, :],
                    ytok_hbm.at[j * tm * ROW_SLAB:(j + 1) * tm * ROW_SLAB, :], ssem.at[0])
                fill.start()
                fill.wait()


def _moe(h2s, tables, wg, wu, wd, T, tm):
    nvalid, tile_g, dest, pads = tables
    nt = tile_g.shape[0]

    def w_index(t, e, nv, tg, de, pd):
        return (tg[t] * EXPERTS_PER_GROUP + jnp.where(t < nv[0], e, EXPERTS_PER_GROUP - 1), 0, 0)

    grid_spec = pltpu.PrefetchScalarGridSpec(
        num_scalar_prefetch=4,
        grid=(nt, EXPERTS_PER_GROUP),
        in_specs=[
            pl.BlockSpec((None, D_MODEL, D_EXPERT), w_index),
            pl.BlockSpec((None, D_MODEL, D_EXPERT), w_index),
            pl.BlockSpec((None, EXPERTS_PER_GROUP * D_EXPERT, D_MODEL),
                         lambda t, e, nv, tg, de, pd: (tg[t], 0, 0), pipeline_mode=pl.Buffered(1)),
            pl.BlockSpec(memory_space=pl.ANY),
        ],
        out_specs=pl.BlockSpec(memory_space=pl.ANY),
        scratch_shapes=[
            pltpu.SMEM(((nt + 1) * tm,), jnp.int32),
            pltpu.VMEM((2 * tm * SLAB_IN, LANES), f32),
            pltpu.VMEM((2 * tm * ROW_SLAB, LANES), f32),
            pltpu.VMEM((tm, D_MODEL), bf16),
            pltpu.VMEM((EXPERTS_PER_GROUP, tm, D_EXPERT), bf16),
            pltpu.SemaphoreType.DMA((2,)),
            pltpu.SemaphoreType.DMA((2,)),
        ],
    )
    return pl.pallas_call(
        functools.partial(_moe_body, tm=tm, nt=nt, n_tok=T),
        grid_spec=grid_spec,
        out_shape=jax.ShapeDtypeStruct(((nt + 1) * tm * ROW_SLAB, LANES), f32),
        compiler_params=pltpu.CompilerParams(
            dimension_semantics=("arbitrary", "arbitrary"),
            vmem_limit_bytes=VMEM_LIMIT),
        name="moe",
    )(nvalid, tile_g, dest, pads, wg, wu,
      wd.reshape(N_GROUPS, EXPERTS_PER_GROUP * D_EXPERT, D_MODEL), h2s)


def _combine_body(x1_ref, y_ref, o_ref):
    tm = x1_ref.shape[0]
    for c in range(ROW_SLAB):
        cols = slice(c * LANES, (c + 1) * LANES)
        o_ref[:, cols] = x1_ref[:, cols] + y_ref[pl.ds(c, tm, stride=ROW_SLAB), :]


def _combine(x1, ytok, tm=256):
    T = x1.shape[0]
    return pl.pallas_call(
        _combine_body,
        grid=(T // tm,),
        in_specs=[
            pl.BlockSpec((tm, D_MODEL), lambda i: (i, 0)),
            pl.BlockSpec((tm * ROW_SLAB, LANES), lambda i: (i, 0)),
        ],
        out_specs=pl.BlockSpec((tm, D_MODEL), lambda i: (i, 0)),
        out_shape=jax.ShapeDtypeStruct((T, D_MODEL), f32),
        compiler_params=pltpu.CompilerParams(
            dimension_semantics=("arbitrary",),
            vmem_limit_bytes=VMEM_LIMIT),
        name="combine",
    )(x1, ytok)


def _alibi_slopes(n):
    return jnp.exp2(-8.0 * jnp.arange(1, n + 1, dtype=f32) / n)


def kernel(x, g_mix, w_in, q_norm_swa, k_norm_swa, sinks, q_norm_moba, k_norm_moba,
           w_up_swa, w_up_moba, w_out, g_ffn, w_router_group, w_router_expert,
           w_gate_e, w_up_e, w_down_e):
    B, S, D = x.shape
    assert D == D_MODEL and S % MOBA_BLOCK == 0 and S % SWA_BLOCK == 0
    T = B * S
    x2 = x.reshape(T, D)

    proj = _inproj(x2, g_mix, w_in)

    oa = _swa(proj, sinks.astype(f32), _alibi_slopes(SWA_Q_HEADS), q_norm_swa, k_norm_swa, B, S)
    ob = _moba(proj, _alibi_slopes(MOBA_HEADS), q_norm_moba, k_norm_moba, B, S)

    wr = jnp.concatenate(
        [w_router_group,
         w_router_expert.transpose(1, 0, 2).reshape(D, N_EXPERTS),
         jnp.zeros((D, ROUTER_COLS - N_GROUPS - N_EXPERTS), f32)], axis=1)
    wr_hi = wr.astype(bf16)
    wr_lo = (wr - wr_hi.astype(f32)).astype(bf16)
    x1, h2s, gid = _mix(oa, ob, proj, x2, w_up_swa.astype(bf16), w_up_moba.astype(bf16),
                        w_out.astype(bf16), g_ffn, jnp.concatenate([wr_hi, wr_lo], axis=1))

    tables = _route_tables(gid, T, MOE_TILE)
    ytok = _moe(h2s, tables, w_gate_e, w_up_e, w_down_e.astype(bf16),
                T, MOE_TILE)
    y = _combine(x1, ytok)
    return y.reshape(B, S, D)
```

```python
import functools

import jax
import jax.numpy as jnp
from jax import lax
from jax.experimental import pallas as pl
from jax.experimental.pallas import tpu as pltpu

D_MODEL = 2048
HEAD_DIM = 64
ATTN_SCALE = HEAD_DIM ** -0.5
SWA_Q_HEADS = 16
SWA_KV_HEADS = 2
SWA_WINDOW = 128
SWA_BLOCK = 128
MOBA_HEADS = 16
MOBA_BLOCK = 256
MOBA_TOPK = 3
N_GROUPS = 4
EXPERTS_PER_GROUP = 4
N_EXPERTS = N_GROUPS * EXPERTS_PER_GROUP
D_EXPERT = 512
EPS = 1e-6

SWA_Q_DIM = SWA_Q_HEADS * HEAD_DIM
SWA_KV_DIM = SWA_KV_HEADS * HEAD_DIM
MOBA_DIM = MOBA_HEADS * HEAD_DIM
IN_COLS = SWA_Q_DIM + 2 * SWA_KV_DIM + 3 * MOBA_DIM + 2 * D_MODEL

LANES = 128
ROW_SLAB = D_MODEL // LANES
SLAB_IN = ROW_SLAB + 8
MOE_TILE = 512
PROJ_CHUNK = 2 * LANES
VMEM_LIMIT = 56 * 1024 * 1024
NEG = -1e30
LOG2E = 1.4426950408889634

COL_QA = 0
COL_KA = COL_QA + SWA_Q_DIM // LANES
COL_VA = COL_KA + 1
COL_QB = COL_VA + 1
COL_KB = COL_QB + MOBA_DIM // LANES
COL_VB = COL_KB + MOBA_DIM // LANES
COL_GATE_A = COL_VB + MOBA_DIM // LANES
COL_GATE_B = COL_GATE_A + D_MODEL // LANES
GATE_BLOCK = 2 * LANES
assert (COL_GATE_A * LANES) % GATE_BLOCK == 0 and (COL_GATE_B * LANES) % GATE_BLOCK == 0

bf16 = jnp.bfloat16
f32 = jnp.float32


def _sigmoid(x):
    return 1.0 / (1.0 + jnp.exp(-x))


def _pair_rms(x, gain):
    lane = lax.broadcasted_iota(jnp.int32, x.shape, 1)
    lo = lane < HEAD_DIM
    sq = x * x
    s0 = jnp.sum(jnp.where(lo, sq, 0.0), axis=-1, keepdims=True)
    s1 = jnp.sum(jnp.where(lo, 0.0, sq), axis=-1, keepdims=True)
    r0 = lax.rsqrt(s0 * (1.0 / HEAD_DIM) + EPS)
    r1 = lax.rsqrt(s1 * (1.0 / HEAD_DIM) + EPS)
    return x * jnp.where(lo, r0, r1) * gain


def _dot_nt(a, b):
    return lax.dot_general(a, b, (((1,), (1,)), ((), ())), preferred_element_type=f32)


def _split2(x):
    hi = x.astype(bf16)
    lo = (x - hi.astype(f32)).astype(bf16)
    return hi, lo


CAST_CHUNK = 2 * LANES
NORM_ROWS = 512


def _inproj_body(x_ref, g_ref, w_ref, o_ref, h_ref):
    @pl.when(pl.program_id(1) == 0)
    def _():
        for r in range(0, x_ref.shape[0], NORM_ROWS):
            x = x_ref[r:r + NORM_ROWS, :]
            ms = jnp.mean(x * x, axis=-1, keepdims=True)
            h_ref[r:r + NORM_ROWS, :] = (x * lax.rsqrt(ms + EPS) * g_ref[...]).astype(bf16)

    n_chunks = w_ref.shape[1] // CAST_CHUNK
    cast = lambda c: w_ref[:, c * CAST_CHUNK:(c + 1) * CAST_CHUNK].astype(bf16)
    w_next = cast(0)
    for c in range(n_chunks):
        w_cur = w_next
        if c + 1 < n_chunks:
            w_next = cast(c + 1)
        o_ref[:, c * CAST_CHUNK:(c + 1) * CAST_CHUNK] = jnp.dot(
            h_ref[...], w_cur, preferred_element_type=f32)


def _inproj(x2, g, w, tm=2048, tn=768):
    T = x2.shape[0]
    N = w.shape[1]
    return pl.pallas_call(
        _inproj_body,
        grid=(T // tm, N // tn),
        in_specs=[
            pl.BlockSpec((tm, D_MODEL), lambda i, j: (i, 0), pipeline_mode=pl.Buffered(1)),
            pl.BlockSpec((1, D_MODEL), lambda i, j: (0, 0)),
            pl.BlockSpec((D_MODEL, tn), lambda i, j: (0, j)),
        ],
        out_specs=pl.BlockSpec((tm, tn), lambda i, j: (i, j)),
        out_shape=jax.ShapeDtypeStruct((T, N), f32),
        scratch_shapes=[pltpu.VMEM((tm, D_MODEL), bf16)],
        compiler_params=pltpu.CompilerParams(
            dimension_semantics=("arbitrary", "arbitrary"),
            vmem_limit_bytes=VMEM_LIMIT),
        name="inproj",
    )(x2, g.reshape(1, D_MODEL), w)


SWA_AHEAD = 2


def _swa_body(sinks_ref, slopes_ref, q_ref, kp_ref, kc_ref, vp_ref, vc_ref,
              qn_ref, kn_ref, o_ref):
    L = SWA_BLOCK
    n = pl.program_id(1)
    k2 = jnp.concatenate([kp_ref[...], kc_ref[...]], axis=0)
    k2n = _pair_rms(k2, kn_ref[...]).astype(bf16)
    v2t = jnp.concatenate([vp_ref[...], vc_ref[...]], axis=0).T.astype(bf16)
    qn = jnp.concatenate(
        [_pair_rms(q_ref[:, pp * LANES:(pp + 1) * LANES], qn_ref[...])
         for pp in range(SWA_Q_HEADS // 2)], axis=1) * ATTN_SCALE
    qt = qn.T

    key = lax.broadcasted_iota(jnp.int32, (2 * L, L), 0)
    qry = lax.broadcasted_iota(jnp.int32, (2 * L, L), 1)
    dist = qry + L - key
    ok = (dist >= 0) & (dist < SWA_WINDOW) & ((n > 0) | (key >= L))
    distf = dist.astype(f32)
    zeros = jnp.zeros((HEAD_DIM, L), f32)

    heads_per_kv = SWA_Q_HEADS // SWA_KV_HEADS

    def scores(h):
        g = h // heads_per_kv
        qh = qt[h * HEAD_DIM:(h + 1) * HEAD_DIM, :]
        qa = jnp.concatenate([qh, zeros] if g == 0 else [zeros, qh], axis=0).astype(bf16)
        return jnp.dot(k2n, qa, preferred_element_type=f32)

    def softmax(h, s):
        s = jnp.where(ok, s - slopes_ref[h] * distf, -jnp.inf)
        sink = sinks_ref[h]
        m = jnp.maximum(jnp.max(s, axis=0, keepdims=True), sink)
        p = jnp.exp(s - m)
        denom = jnp.sum(p, axis=0, keepdims=True) + jnp.exp(sink - m)
        return p.astype(bf16), denom

    outs = []

    def finish(h, p, denom):
        g = h // heads_per_kv
        o = jnp.dot(v2t, p, preferred_element_type=f32)
        outs.append(o[g * HEAD_DIM:(g + 1) * HEAD_DIM, :] / denom)

    pending = [scores(h) for h in range(SWA_AHEAD)]
    deferred = None
    for h in range(SWA_Q_HEADS):
        s = pending.pop(0)
        if h + SWA_AHEAD < SWA_Q_HEADS:
            pending.append(scores(h + SWA_AHEAD))
        p, denom = softmax(h, s)
        if deferred is not None:
            finish(*deferred)
        deferred = (h, p, denom)
    finish(*deferred)
    o_ref[...] = jnp.concatenate(outs, axis=0).T.astype(bf16)


def _swa(proj, sinks, slopes, q_norm, k_norm, B, S):
    L = SWA_BLOCK
    nb = S // L
    T = B * S
    smem = pl.BlockSpec(memory_space=pltpu.SMEM)

    def prev(b, n):
        return b * nb + jnp.maximum(n - 1, 0)

    return pl.pallas_call(
        _swa_body,
        grid=(B, nb),
        in_specs=[
            smem, smem,
            pl.BlockSpec((L, SWA_Q_DIM), lambda b, n: (b * nb + n, COL_QA // (SWA_Q_DIM // LANES))),
            pl.BlockSpec((L, LANES), lambda b, n: (prev(b, n), COL_KA)),
            pl.BlockSpec((L, LANES), lambda b, n: (b * nb + n, COL_KA)),
            pl.BlockSpec((L, LANES), lambda b, n: (prev(b, n), COL_VA)),
            pl.BlockSpec((L, LANES), lambda b, n: (b * nb + n, COL_VA)),
            pl.BlockSpec((1, LANES), lambda b, n: (0, 0)),
            pl.BlockSpec((1, LANES), lambda b, n: (0, 0)),
        ],
        out_specs=pl.BlockSpec((L, SWA_Q_DIM), lambda b, n: (b * nb + n, 0)),
        out_shape=jax.ShapeDtypeStruct((T, SWA_Q_DIM), bf16),
        compiler_params=pltpu.CompilerParams(
            dimension_semantics=("arbitrary", "arbitrary"),
            vmem_limit_bytes=VMEM_LIMIT),
        name="swa",
    )(sinks, slopes, proj, proj, proj, proj, proj,
      jnp.tile(q_norm, 2).reshape(1, LANES), jnp.tile(k_norm, 2).reshape(1, LANES))


N_BIAS_PARTS = 3
NB_ROWS = 8
N_ALIBI_COL = NB_ROWS * N_BIAS_PARTS
SCORE_AHEAD = 3


def _split3(x):
    p0 = x.astype(bf16).astype(f32)
    r1 = x - p0
    p1 = r1.astype(bf16).astype(f32)
    p2 = (r1 - p1).astype(bf16).astype(f32)
    return p0, p1, p2


def _moba_body(slopes_ref, q_ref, k_ref, v_ref, qn_ref, kn_ref, o_ref, kaug_ref, vt_ref, *, nb):
    L = MOBA_BLOCK
    S = nb * L
    p = pl.program_id(1)

    kn = _pair_rms(k_ref[...], kn_ref[...])
    lane = lax.broadcasted_iota(jnp.int32, (L, LANES), 1)
    kpos = lax.broadcasted_iota(jnp.int32, (L, LANES), 0).astype(f32)
    lane_lo = lane < HEAD_DIM

    def alibi_cols(cc, slope):
        a0, a1, a2 = _split3((slope * LOG2E) * kpos)
        return jnp.where(cc == N_ALIBI_COL, a0,
                         jnp.where(cc == N_ALIBI_COL + 1, a1,
                                   jnp.where(cc == N_ALIBI_COL + 2, a2, 0.0)))

    cols = (lane - HEAD_DIM, lane)
    alibi = [alibi_cols(cols[hh], slopes_ref[2 * p + hh]) for hh in range(2)]
    for n in range(nb):
        kblock = kn[n * L:(n + 1) * L, :]
        for hh in range(2):
            cc = cols[hh]
            hot = (cc >= 0) & (cc < N_ALIBI_COL) & ((cc % NB_ROWS) == n)
            aug = jnp.where(hot, 1.0, alibi[hh])
            own = lane_lo if hh == 0 else jnp.logical_not(lane_lo)
            kaug_ref[hh, n * L:(n + 1) * L, :] = jnp.where(own, kblock, aug).astype(bf16)
    vt_ref[...] = v_ref[...].T.astype(bf16)
    means = [jnp.mean(kn[n * L:(n + 1) * L, :], axis=0, keepdims=True) for n in range(nb)]
    means += [jnp.zeros((1, LANES), f32)] * (NB_ROWS - nb)
    km = jnp.concatenate(means, axis=0)
    lane8 = lax.broadcasted_iota(jnp.int32, (NB_ROWS, LANES), 1)
    km_pair = jnp.concatenate(
        [jnp.where(lane8 < HEAD_DIM, km, 0.0), jnp.where(lane8 < HEAD_DIM, 0.0, km)], axis=0)
    km_hi, km_lo = _split2(km_pair)

    qs = _pair_rms(q_ref[...], qn_ref[...]) * (ATTN_SCALE * LOG2E)
    qt = qs.T
    qt_hi, qt_lo = _split2(qt)
    gate = (jnp.dot(km_hi, qt_hi, preferred_element_type=f32)
            + (jnp.dot(km_hi, qt_lo, preferred_element_type=f32)
               + jnp.dot(km_lo, qt_hi, preferred_element_type=f32)))

    blk = lax.broadcasted_iota(jnp.int32, (NB_ROWS, S), 0)
    qblk = lax.broadcasted_iota(jnp.int32, (NB_ROWS, S), 1) // L
    past = blk < qblk
    r = lax.broadcasted_iota(jnp.int32, (L, L), 1)
    c = lax.broadcasted_iota(jnp.int32, (L, L), 0)
    causal = jnp.where(r >= c, 0.0, NEG)
    ones_rows = (blk < 3).astype(f32)

    qa = []
    for hh in range(2):
        g = jnp.where(past, gate[hh * NB_ROWS:(hh + 1) * NB_ROWS, :], -jnp.inf)
        rank = jnp.zeros((NB_ROWS, S), jnp.int32)
        for m in range(nb):
            gm = g[m:m + 1, :]
            ahead = (gm > g) | ((gm == g) & (m < blk))
            rank = rank + ahead.astype(jnp.int32)
        sel = past & (rank < MOBA_TOPK)
        slope = slopes_ref[2 * p + hh]
        bias = jnp.where(sel, (-slope * L * LOG2E) * (qblk - blk).astype(f32), NEG)
        bias = jnp.where(blk == qblk, 0.0, bias)
        b0, b1, b2 = _split3(bias)
        extra = jnp.concatenate(
            [b0, b1, b2, ones_rows,
             jnp.zeros((HEAD_DIM - N_ALIBI_COL - NB_ROWS, S), f32)], axis=0)
        if hh == 0:
            qa.append(jnp.concatenate([qt[:HEAD_DIM], extra], axis=0).astype(bf16))
        else:
            qa.append(jnp.concatenate([extra, qt[HEAD_DIM:]], axis=0).astype(bf16))

    def scores(i, hh):
        return jnp.dot(kaug_ref[hh, 0:(i + 1) * L, :], qa[hh][:, i * L:(i + 1) * L],
                       preferred_element_type=f32)

    units = [(i, hh) for i in range(nb) for hh in range(2)]
    pending = [scores(*u) for u in units[:SCORE_AHEAD]]
    outs = {}

    def finish(i, hh, e_all, l):
        acc = jnp.dot(vt_ref[hh * HEAD_DIM:(hh + 1) * HEAD_DIM, 0:(i + 1) * L], e_all,
                      preferred_element_type=f32)
        outs[(i, hh)] = acc / l
        if hh == 1:
            o_ref[i * L:(i + 1) * L, :] = jnp.concatenate(
                [outs.pop((i, 0)), outs.pop((i, 1))], axis=0).T.astype(bf16)

    deferred = None
    for idx, (i, hh) in enumerate(units):
        s = pending.pop(0)
        if idx + SCORE_AHEAD < len(units):
            pending.append(scores(*units[idx + SCORE_AHEAD]))
        tiles = [s[n * L:(n + 1) * L, :] for n in range(i)] + [s[i * L:(i + 1) * L, :] + causal]
        m = functools.reduce(jnp.maximum, [jnp.max(t, axis=0, keepdims=True) for t in tiles])
        es = [jnp.exp2(t - m) for t in tiles]
        l = functools.reduce(lambda a, b: a + b, [jnp.sum(e, axis=0, keepdims=True) for e in es])
        e_all = jnp.concatenate([e.astype(bf16) for e in es], axis=0)
        if deferred is not None:
            finish(*deferred)
        deferred = (i, hh, e_all, l)
    finish(*deferred)


def _moba(proj, slopes, q_norm, k_norm, B, S):
    L = MOBA_BLOCK
    nb = S // L
    assert nb <= NB_ROWS
    T = B * S
    n_pairs = MOBA_HEADS // 2
    return pl.pallas_call(
        functools.partial(_moba_body, nb=nb),
        grid=(B, n_pairs),
        in_specs=[
            pl.BlockSpec(memory_space=pltpu.SMEM),
            pl.BlockSpec((S, LANES), lambda b, p: (b, COL_QB + p)),
            pl.BlockSpec((S, LANES), lambda b, p: (b, COL_KB + p)),
            pl.BlockSpec((S, LANES), lambda b, p: (b, COL_VB + p)),
            pl.BlockSpec((1, LANES), lambda b, p: (0, 0)),
            pl.BlockSpec((1, LANES), lambda b, p: (0, 0)),
        ],
        out_specs=pl.BlockSpec((S, LANES), lambda b, p: (b, p)),
        out_shape=jax.ShapeDtypeStruct((T, MOBA_DIM), bf16),
        scratch_shapes=[
            pltpu.VMEM((2, S, LANES), bf16),
            pltpu.VMEM((LANES, S), bf16),
        ],
        compiler_params=pltpu.CompilerParams(
            dimension_semantics=("arbitrary", "arbitrary"),
            vmem_limit_bytes=VMEM_LIMIT),
        name="moba",
    )(slopes, proj, proj, proj,
      jnp.tile(q_norm, 2).reshape(1, LANES), jnp.tile(k_norm, 2).reshape(1, LANES))


ROUTER_COLS = LANES


N_GATE_BLOCKS = D_MODEL // GATE_BLOCK
MIX_CHUNK = 2 * GATE_BLOCK


def _mix_body(*refs):
    ga_refs = refs[:N_GATE_BLOCKS]
    gb_refs = refs[N_GATE_BLOCKS:2 * N_GATE_BLOCKS]
    (oa_ref, ob_ref, x_ref, wua_ref, wub_ref, wout_ref, gffn_ref, wr_ref,
     x1_ref, h2_ref, gid_ref) = refs[2 * N_GATE_BLOCKS:]
    oa = oa_ref[...]
    ob = ob_ref[...]
    gpc = MIX_CHUNK // GATE_BLOCK

    def up(c):
        cols = slice(c * MIX_CHUNK, (c + 1) * MIX_CHUNK)
        return (jnp.dot(oa, wua_ref[:, cols], preferred_element_type=f32),
                jnp.dot(ob, wub_ref[:, cols], preferred_element_type=f32))

    x1 = x_ref[...]
    pending = up(0)
    for c in range(D_MODEL // MIX_CHUNK):
        ya, yb = pending
        if (c + 1) * MIX_CHUNK < D_MODEL:
            pending = up(c + 1)
        gate_a = jnp.concatenate([r[...] for r in ga_refs[c * gpc:(c + 1) * gpc]], axis=1)
        gate_b = jnp.concatenate([r[...] for r in gb_refs[c * gpc:(c + 1) * gpc]], axis=1)
        merged = (_sigmoid(gate_a) * ya + _sigmoid(gate_b) * yb).astype(bf16)
        x1 = x1 + jnp.dot(merged, wout_ref[c * MIX_CHUNK:(c + 1) * MIX_CHUNK, :],
                          preferred_element_type=f32)
    x1_ref[...] = x1
    ms = jnp.mean(x1 * x1, axis=-1, keepdims=True)
    h2 = x1 * lax.rsqrt(ms + EPS) * gffn_ref[...]
    tm = h2.shape[0]
    for c in range(ROW_SLAB):
        h2_ref[pl.ds(c, tm, stride=SLAB_IN), :] = h2[:, c * LANES:(c + 1) * LANES]

    h_hi, h_lo = _split2(h2)
    hi_all = jnp.dot(h_hi, wr_ref[...], preferred_element_type=f32)
    lo_hi = jnp.dot(h_lo, wr_ref[:, :ROUTER_COLS], preferred_element_type=f32)
    lt = (hi_all[:, :ROUTER_COLS] + (lo_hi + hi_all[:, ROUTER_COLS:])).T
    gl = [lt[g:g + 1, :] for g in range(N_GROUPS)]
    gmax = functools.reduce(jnp.maximum, gl)
    gsum = functools.reduce(lambda a, b: a + b, [jnp.exp(v - gmax) for v in gl])
    g_p = 1.0 / gsum
    g_i = jnp.full((1, tm), N_GROUPS - 1, jnp.int32)
    for g in reversed(range(N_GROUPS)):
        g_i = jnp.where(gl[g] == gmax, g, g_i)

    el = []
    for e in range(EXPERTS_PER_GROUP):
        v = jnp.zeros((1, tm), f32)
        for g in range(N_GROUPS):
            r = N_GROUPS + g * EXPERTS_PER_GROUP + e
            v = jnp.where(g_i == g, lt[r:r + 1, :], v)
        el.append(v)
    emax = functools.reduce(jnp.maximum, el)
    ex = [jnp.exp(v - emax) for v in el]
    esum = functools.reduce(lambda a, b: a + b, ex)
    ep = [v / esum for v in ex]
    p1 = functools.reduce(jnp.maximum, ep)
    i1 = jnp.full((1, tm), EXPERTS_PER_GROUP - 1, jnp.int32)
    for e in reversed(range(EXPERTS_PER_GROUP)):
        i1 = jnp.where(ep[e] == p1, e, i1)
    rest = [jnp.where(i1 == e, -1.0, ep[e]) for e in range(EXPERTS_PER_GROUP)]
    p2 = functools.reduce(jnp.maximum, rest)
    i2 = jnp.full((1, tm), EXPERTS_PER_GROUP - 1, jnp.int32)
    for e in reversed(range(EXPERTS_PER_GROUP)):
        i2 = jnp.where(rest[e] == p2, e, i2)
    w1 = g_p * (p1 / (p1 + p2))
    w2 = g_p * (p2 / (p1 + p2))
    gid_ref[...] = jnp.concatenate([g_i, jnp.zeros((7, tm), jnp.int32)], axis=0)
    cw = [jnp.where(i1 == e, w1, 0.0) + jnp.where(i2 == e, w2, 0.0) for e in range(EXPERTS_PER_GROUP)]
    cw_t = jnp.concatenate(cw + [jnp.zeros((LANES - EXPERTS_PER_GROUP, tm), f32)], axis=0).T
    for e in range(SLAB_IN - ROW_SLAB):
        if e < EXPERTS_PER_GROUP:
            row = jnp.broadcast_to(cw_t[:, e:e + 1], (tm, LANES))
        else:
            row = jnp.zeros((tm, LANES), f32)
        h2_ref[pl.ds(ROW_SLAB + e, tm, stride=SLAB_IN), :] = row


def _mix(oa, ob, proj, x2, wua, wub, wout, g_ffn, wr, tm=256):
    T = x2.shape[0]
    const = lambda i: (0, 0)
    single = pl.Buffered(1)

    def gate_specs(first_col):
        first = first_col * LANES // GATE_BLOCK
        return [pl.BlockSpec((tm, GATE_BLOCK), functools.partial(lambda i, j: (i, j), j=first + k))
                for k in range(N_GATE_BLOCKS)]

    return pl.pallas_call(
        _mix_body,
        grid=(T // tm,),
        in_specs=gate_specs(COL_GATE_A) + gate_specs(COL_GATE_B) + [
            pl.BlockSpec((tm, SWA_Q_DIM), lambda i: (i, 0)),
            pl.BlockSpec((tm, MOBA_DIM), lambda i: (i, 0)),
            pl.BlockSpec((tm, D_MODEL), lambda i: (i, 0)),
            pl.BlockSpec((SWA_Q_DIM, D_MODEL), const, pipeline_mode=single),
            pl.BlockSpec((MOBA_DIM, D_MODEL), const, pipeline_mode=single),
            pl.BlockSpec((D_MODEL, D_MODEL), const, pipeline_mode=single),
            pl.BlockSpec((1, D_MODEL), const),
            pl.BlockSpec((D_MODEL, 2 * ROUTER_COLS), const),
        ],
        out_specs=[
            pl.BlockSpec((tm, D_MODEL), lambda i: (i, 0)),
            pl.BlockSpec((tm * SLAB_IN, LANES), lambda i: (i, 0)),
            pl.BlockSpec((8, tm), lambda i: (0, i)),
        ],
        out_shape=[
            jax.ShapeDtypeStruct((T, D_MODEL), f32),
            jax.ShapeDtypeStruct((T * SLAB_IN, LANES), f32),
            jax.ShapeDtypeStruct((8, T), jnp.int32),
        ],
        compiler_params=pltpu.CompilerParams(
            dimension_semantics=("arbitrary",),
            vmem_limit_bytes=VMEM_LIMIT),
        name="mix",
    )(*([proj] * (2 * N_GATE_BLOCKS)), oa, ob, x2, wua, wub, wout, g_ffn.reshape(1, D_MODEL), wr)


def _route_tables(gid, T, tm):
    nt = T // tm + N_GROUPS
    g = gid[0]
    onehot = (g[:, None] == jnp.arange(N_GROUPS, dtype=jnp.int32)[None, :]).astype(jnp.int32)
    incl = jnp.cumsum(onehot, axis=0)
    pos = jnp.sum(onehot * incl, axis=1) - 1
    counts = incl[-1]
    padded = (counts + tm - 1) // tm * tm
    seg_end = jnp.cumsum(padded)
    seg_start = seg_end - padded
    dest = jnp.sum(onehot * seg_start[None, :], axis=1) + pos
    tile_start = jnp.arange(nt, dtype=jnp.int32) * tm
    tile_g = jnp.sum((tile_start[:, None] >= seg_end[None, :]).astype(jnp.int32), axis=1)
    tile_g = jnp.minimum(tile_g, N_GROUPS - 1)
    nvalid = (seg_end[-1] // tm).astype(jnp.int32).reshape(1)
    n_pad = padded - counts
    pads = jnp.concatenate([seg_start + counts, seg_end, jnp.cumsum(n_pad) - n_pad])
    return (nvalid, tile_g.astype(jnp.int32), dest.astype(jnp.int32), pads.astype(jnp.int32))


def _moe_body(nvalid_ref, tg_ref, dest_ref, pads_ref, wg_ref, wu_ref, wd_ref, h2s_hbm, ytok_hbm,
              asg, gbuf, ybuf, xbuf, hbuf, gsem, ssem, *, tm, nt, n_tok):
    t = pl.program_id(0)
    e = pl.program_id(1)
    last_e = EXPERTS_PER_GROUP - 1
    nvalid = nvalid_ref[0]
    slot = lax.rem(t, 2)
    other = 1 - slot

    def for_rows(fn):
        def body(r, carry):
            fn(r)
            return carry
        lax.fori_loop(0, tm, body, 0, unroll=8)

    def gather_copy(tok, s, r):
        return pltpu.make_async_copy(
            h2s_hbm.at[pl.ds(pl.multiple_of(tok * SLAB_IN, 8), SLAB_IN), :],
            gbuf.at[pl.ds(pl.multiple_of((s * tm + r) * SLAB_IN, 8), SLAB_IN), :], gsem.at[s])

    def scatter_copy(row, s, r):
        return pltpu.make_async_copy(
            ybuf.at[pl.ds(pl.multiple_of((s * tm + r) * ROW_SLAB, 8), ROW_SLAB), :],
            ytok_hbm.at[pl.ds(pl.multiple_of(row * ROW_SLAB, 8), ROW_SLAB), :], ssem.at[s])

    def start_gathers(tile, s):
        for_rows(lambda r: gather_copy(jnp.maximum(asg[tile * tm + r], 0), s, r).start())

    def start_scatters(tile, s):
        def one(r):
            a = asg[tile * tm + r]
            scatter_copy(jnp.where(a >= 0, a, n_tok - 1 - a), s, r).start()
        for_rows(one)

    def wait_gathers(s):
        for_rows(lambda r: gather_copy(0, s, r).wait())

    def wait_scatters(s):
        for_rows(lambda r: scatter_copy(0, s, r).wait())

    @pl.when((t == 0) & (e == 0))
    def _():
        def place(tok, carry):
            asg[dest_ref[tok]] = tok
            return carry
        lax.fori_loop(0, n_tok, place, 0, unroll=8)

        for g in range(N_GROUPS):
            first = pads_ref[g]
            code = -1 - pads_ref[2 * N_GROUPS + g] + first

            def mark(p, carry):
                asg[p] = code - p
                return carry
            lax.fori_loop(first, pads_ref[N_GROUPS + g], mark, 0)
        ybuf[...] = jnp.zeros(ybuf.shape, f32)
        start_gathers(0, 0)

    @pl.when((e == 0) & (t < nvalid))
    def _():
        wait_gathers(slot)
        base = slot * tm * SLAB_IN
        xbuf[...] = jnp.concatenate(
            [gbuf[pl.ds(base + c, tm, stride=SLAB_IN), :] for c in range(ROW_SLAB)],
            axis=1).astype(bf16)

    @pl.when(t < nvalid)
    def _():
        quarter = tm // EXPERTS_PER_GROUP
        have_next = t + 1 < nvalid
        have_prev = t >= 1
        nxt = jnp.minimum(t + 1, nvalid - 1)

        def issue_rows(r0, r1):
            for r in range(r0, r1):
                row = e * quarter + r
                tok = jnp.where(have_next, jnp.maximum(asg[nxt * tm + row], 0), 0)
                gather_copy(tok, other, row).start()
                a = asg[jnp.maximum(t - 1, 0) * tm + row]
                dst = jnp.where(have_prev, jnp.where(a >= 0, a, n_tok - 1 - a), nt * tm + row)
                scatter_copy(dst, other, row).start(priority=r % 2)

        x = xbuf[...]
        n_chunks = D_EXPERT // PROJ_CHUNK
        per_piece = quarter // (2 * n_chunks)
        pieces = {"g": [], "u": []}
        for i, (name, w_ref) in enumerate((n, w) for c in range(n_chunks)
                                          for n, w in (("g", wg_ref), ("u", wu_ref))):
            c = i // 2
            w = w_ref[:, c * PROJ_CHUNK:(c + 1) * PROJ_CHUNK].astype(bf16)
            pieces[name].append(jnp.dot(x, w, preferred_element_type=f32))
            issue_rows(i * per_piece, (i + 1) * per_piece)
        gte = jnp.concatenate(pieces["g"], axis=1)
        up = jnp.concatenate(pieces["u"], axis=1)
        cw = gbuf[pl.ds(slot * tm * SLAB_IN + ROW_SLAB + e, tm, stride=SLAB_IN), :]
        hid = (gte * _sigmoid(gte)) * up * jnp.concatenate([cw] * (D_EXPERT // LANES), axis=1)
        hbuf[e] = hid.astype(bf16)

    @pl.when((e == last_e) & (t < nvalid))
    def _():
        hcat = jnp.concatenate([hbuf[k] for k in range(EXPERTS_PER_GROUP)], axis=1)
        y = jnp.dot(hcat, wd_ref[...], preferred_element_type=f32)

        @pl.when(t >= 1)
        def _():
            wait_scatters(slot)
        base = slot * tm * ROW_SLAB
        for c in range(ROW_SLAB):
            ybuf[pl.ds(base + c, tm, stride=ROW_SLAB), :] = y[:, c * LANES:(c + 1) * LANES]

        @pl.when(t == nvalid - 1)
        def _():
            start_scatters(t, slot)
            wait_gathers(other)
            wait_scatters(other)
            wait_scatters(slot)

    @pl.when((e == last_e) & (t == nt - 1))
    def _():
        ybuf[0:tm * ROW_SLAB, :] = jnp.zeros((tm * ROW_SLAB, LANES), f32)
        for j in range(n_tok // tm, nt):
            @pl.when(j >= nvalid)
            def _():
                fill = pltpu.make_async_copy(
                    ybuf.at[0:tm * ROW_SLAB, :],
                    ytok_hbm.at[j * tm * ROW_SLAB:(j + 1) * tm * ROW_SLAB, :], ssem.at[0])
                fill.start()
                fill.wait()


def _moe(h2s, tables, wg, wu, wd, T, tm):
    nvalid, tile_g, dest, pads = tables
    nt = tile_g.shape[0]

    def w_index(t, e, nv, tg, de, pd):
        return (tg[t] * EXPERTS_PER_GROUP + jnp.where(t < nv[0], e, EXPERTS_PER_GROUP - 1), 0, 0)

    grid_spec = pltpu.PrefetchScalarGridSpec(
        num_scalar_prefetch=4,
        grid=(nt, EXPERTS_PER_GROUP),
        in_specs=[
            pl.BlockSpec((None, D_MODEL, D_EXPERT), w_index),
            pl.BlockSpec((None, D_MODEL, D_EXPERT), w_index),
            pl.BlockSpec((None, EXPERTS_PER_GROUP * D_EXPERT, D_MODEL),
                         lambda t, e, nv, tg, de, pd: (tg[t], 0, 0), pipeline_mode=pl.Buffered(1)),
            pl.BlockSpec(memory_space=pl.ANY),
        ],
        out_specs=pl.BlockSpec(memory_space=pl.ANY),
        scratch_shapes=[
            pltpu.SMEM(((nt + 1) * tm,), jnp.int32),
            pltpu.VMEM((2 * tm * SLAB_IN, LANES), f32),
            pltpu.VMEM((2 * tm * ROW_SLAB, LANES), f32),
            pltpu.VMEM((tm, D_MODEL), bf16),
            pltpu.VMEM((EXPERTS_PER_GROUP, tm, D_EXPERT), bf16),
            pltpu.SemaphoreType.DMA((2,)),
            pltpu.SemaphoreType.DMA((2,)),
        ],
    )
    return pl.pallas_call(
        functools.partial(_moe_body, tm=tm, nt=nt, n_tok=T),
        grid_spec=grid_spec,
        out_shape=jax.ShapeDtypeStruct(((nt + 1) * tm * ROW_SLAB, LANES), f32),
        compiler_params=pltpu.CompilerParams(
            dimension_semantics=("arbitrary", "arbitrary"),
            vmem_limit_bytes=VMEM_LIMIT),
        name="moe",
    )(nvalid, tile_g, dest, pads, wg, wu,
      wd.reshape(N_GROUPS, EXPERTS_PER_GROUP * D_EXPERT, D_MODEL), h2s)


def _combine_body(x1_ref, y_ref, o_ref):
    tm = x1_ref.shape[0]
    for c in range(ROW_SLAB):
        cols = slice(c * LANES, (c + 1) * LANES)
        o_ref[:, cols] = x1_ref[:, cols] + y_ref[pl.ds(c, tm, stride=ROW_SLAB), :]


def _combine(x1, ytok, tm=256):
    T = x1.shape[0]
    return pl.pallas_call(
        _combine_body,
        grid=(T // tm,),
        in_specs=[
            pl.BlockSpec((tm, D_MODEL), lambda i: (i, 0)),
            pl.BlockSpec((tm * ROW_SLAB, LANES), lambda i: (i, 0)),
        ],
        out_specs=pl.BlockSpec((tm, D_MODEL), lambda i: (i, 0)),
        out_shape=jax.ShapeDtypeStruct((T, D_MODEL), f32),
        compiler_params=pltpu.CompilerParams(
            dimension_semantics=("arbitrary",),
            vmem_limit_bytes=VMEM_LIMIT),
        name="combine",
    )(x1, ytok)


def _alibi_slopes(n):
    return jnp.exp2(-8.0 * jnp.arange(1, n + 1, dtype=f32) / n)


def kernel(x, g_mix, w_in, q_norm_swa, k_norm_swa, sinks, q_norm_moba, k_norm_moba,
           w_up_swa, w_up_moba, w_out, g_ffn, w_router_group, w_router_expert,
           w_gate_e, w_up_e, w_down_e):
    B, S, D = x.shape
    assert D == D_MODEL and S % MOBA_BLOCK == 0 and S % SWA_BLOCK == 0
    T = B * S
    x2 = x.reshape(T, D)

    proj = _inproj(x2, g_mix, w_in)

    oa = _swa(proj, sinks.astype(f32), _alibi_slopes(SWA_Q_HEADS), q_norm_swa, k_norm_swa, B, S)
    ob = _moba(proj, _alibi_slopes(MOBA_HEADS), q_norm_moba, k_norm_moba, B, S)

    wr = jnp.concatenate(
        [w_router_group,
         w_router_expert.transpose(1, 0, 2).reshape(D, N_EXPERTS),
         jnp.zeros((D, ROUTER_COLS - N_GROUPS - N_EXPERTS), f32)], axis=1)
    wr_hi = wr.astype(bf16)
    wr_lo = (wr - wr_hi.astype(f32)).astype(bf16)
    x1, h2s, gid = _mix(oa, ob, proj, x2, w_up_swa.astype(bf16), w_up_moba.astype(bf16),
                        w_out.astype(bf16), g_ffn, jnp.concatenate([wr_hi, wr_lo], axis=1))

    tables = _route_tables(gid, T, MOE_TILE)
    ytok = _moe(h2s, tables, w_gate_e, w_up_e, w_down_e.astype(bf16),
                T, MOE_TILE)
    y = _combine(x1, ytok)
    return y.reshape(B, S, D)
```

```python
import functools

import jax
import jax.numpy as jnp
from jax import lax
from jax.experimental import pallas as pl
from jax.experimental.pallas import tpu as pltpu

D_MODEL = 2048
HEAD_DIM = 64
ATTN_SCALE = HEAD_DIM ** -0.5
SWA_Q_HEADS = 16
SWA_KV_HEADS = 2
SWA_WINDOW = 128
SWA_BLOCK = 128
MOBA_HEADS = 16
MOBA_BLOCK = 256
MOBA_TOPK = 3
N_GROUPS = 4
EXPERTS_PER_GROUP = 4
N_EXPERTS = N_GROUPS * EXPERTS_PER_GROUP
D_EXPERT = 512
EPS = 1e-6

SWA_Q_DIM = SWA_Q_HEADS * HEAD_DIM
MOBA_DIM = MOBA_HEADS * HEAD_DIM

LANES = 128
ROW_SLAB = D_MODEL // LANES
SLAB_IN = ROW_SLAB + 8
MOE_TILE = 512
PROJ_CHUNK = 2 * LANES
VMEM_LIMIT = 56 * 1024 * 1024
NEG = -1e30
LOG2E = 1.4426950408889634

COL_QA = 0
COL_KA = COL_QA + SWA_Q_DIM // LANES
COL_VA = COL_KA + 1
COL_QB = COL_VA + 1
COL_KB = COL_QB + MOBA_DIM // LANES
COL_VB = COL_KB + MOBA_DIM // LANES
COL_GATE_A = COL_VB + MOBA_DIM // LANES
COL_GATE_B = COL_GATE_A + D_MODEL // LANES
GATE_BLOCK = 2 * LANES
assert (COL_GATE_A * LANES) % GATE_BLOCK == 0 and (COL_GATE_B * LANES) % GATE_BLOCK == 0

bf16 = jnp.bfloat16
f32 = jnp.float32


def _sigmoid(x):
    return 1.0 / (1.0 + jnp.exp(-x))


def _pair_rms(x, gain):
    lane = lax.broadcasted_iota(jnp.int32, x.shape, 1)
    lo = lane < HEAD_DIM
    sq = x * x
    s0 = jnp.sum(jnp.where(lo, sq, 0.0), axis=-1, keepdims=True)
    s1 = jnp.sum(jnp.where(lo, 0.0, sq), axis=-1, keepdims=True)
    r0 = lax.rsqrt(s0 * (1.0 / HEAD_DIM) + EPS)
    r1 = lax.rsqrt(s1 * (1.0 / HEAD_DIM) + EPS)
    return x * jnp.where(lo, r0, r1) * gain


def _split2(x):
    hi = x.astype(bf16)
    lo = (x - hi.astype(f32)).astype(bf16)
    return hi, lo


CAST_CHUNK = 2 * LANES
NORM_ROWS = 512


def _inproj_body(x_ref, g_ref, w_ref, o_ref, h_ref):
    @pl.when(pl.program_id(1) == 0)
    def _():
        for r in range(0, x_ref.shape[0], NORM_ROWS):
            x = x_ref[r:r + NORM_ROWS, :]
            ms = jnp.mean(x * x, axis=-1, keepdims=True)
            h_ref[r:r + NORM_ROWS, :] = (x * lax.rsqrt(ms + EPS) * g_ref[...]).astype(bf16)

    n_chunks = w_ref.shape[1] // CAST_CHUNK
    cast = lambda c: w_ref[:, c * CAST_CHUNK:(c + 1) * CAST_CHUNK].astype(bf16)
    w_next = cast(0)
    for c in range(n_chunks):
        w_cur = w_next
        if c + 1 < n_chunks:
            w_next = cast(c + 1)
        o_ref[:, c * CAST_CHUNK:(c + 1) * CAST_CHUNK] = jnp.dot(
            h_ref[...], w_cur, preferred_element_type=f32)


def _inproj(x2, g, w, tm=2048, tn=768):
    T = x2.shape[0]
    N = w.shape[1]
    return pl.pallas_call(
        _inproj_body,
        grid=(T // tm, N // tn),
        in_specs=[
            pl.BlockSpec((tm, D_MODEL), lambda i, j: (i, 0), pipeline_mode=pl.Buffered(1)),
            pl.BlockSpec((1, D_MODEL), lambda i, j: (0, 0)),
            pl.BlockSpec((D_MODEL, tn), lambda i, j: (0, j)),
        ],
        out_specs=pl.BlockSpec((tm, tn), lambda i, j: (i, j)),
        out_shape=jax.ShapeDtypeStruct((T, N), f32),
        scratch_shapes=[pltpu.VMEM((tm, D_MODEL), bf16)],
        compiler_params=pltpu.CompilerParams(
            dimension_semantics=("arbitrary", "arbitrary"),
            vmem_limit_bytes=VMEM_LIMIT),
        name="inproj",
    )(x2, g.reshape(1, D_MODEL), w)


SWA_AHEAD = 2


def _swa_body(sinks_ref, slopes_ref, q_ref, kp_ref, kc_ref, vp_ref, vc_ref,
              qn_ref, kn_ref, o_ref):
    L = SWA_BLOCK
    n = pl.program_id(1)
    k2 = jnp.concatenate([kp_ref[...], kc_ref[...]], axis=0)
    k2n = _pair_rms(k2, kn_ref[...]).astype(bf16)
    v2t = jnp.concatenate([vp_ref[...], vc_ref[...]], axis=0).T.astype(bf16)
    qn = jnp.concatenate(
        [_pair_rms(q_ref[:, pp * LANES:(pp + 1) * LANES], qn_ref[...])
         for pp in range(SWA_Q_HEADS // 2)], axis=1) * (ATTN_SCALE * LOG2E)
    qt = qn.T

    key = lax.broadcasted_iota(jnp.int32, (2 * L, L), 0)
    qry = lax.broadcasted_iota(jnp.int32, (2 * L, L), 1)
    dist = qry + L - key
    ok = (dist >= 0) & (dist < SWA_WINDOW) & ((n > 0) | (key >= L))
    distf = dist.astype(f32)
    zeros = jnp.zeros((HEAD_DIM, L), f32)

    heads_per_kv = SWA_Q_HEADS // SWA_KV_HEADS

    def scores(h):
        g = h // heads_per_kv
        qh = qt[h * HEAD_DIM:(h + 1) * HEAD_DIM, :]
        qa = jnp.concatenate([qh, zeros] if g == 0 else [zeros, qh], axis=0).astype(bf16)
        return jnp.dot(k2n, qa, preferred_element_type=f32)

    def softmax(h, s):
        s = jnp.where(ok, s - (slopes_ref[h] * LOG2E) * distf, -jnp.inf)
        sink = sinks_ref[h] * LOG2E
        m = jnp.maximum(jnp.max(s, axis=0, keepdims=True), sink)
        p = jnp.exp2(s - m)
        denom = jnp.sum(p, axis=0, keepdims=True) + jnp.exp2(sink - m)
        return p.astype(bf16), denom

    outs = []

    def finish(h, p, denom):
        g = h // heads_per_kv
        o = jnp.dot(v2t, p, preferred_element_type=f32)
        outs.append(o[g * HEAD_DIM:(g + 1) * HEAD_DIM, :] / denom)

    pending = [scores(h) for h in range(SWA_AHEAD)]
    deferred = None
    for h in range(SWA_Q_HEADS):
        s = pending.pop(0)
        if h + SWA_AHEAD < SWA_Q_HEADS:
            pending.append(scores(h + SWA_AHEAD))
        p, denom = softmax(h, s)
        if deferred is not None:
            finish(*deferred)
        deferred = (h, p, denom)
    finish(*deferred)
    o_ref[...] = jnp.concatenate(outs, axis=0).T.astype(bf16)


def _swa(proj, sinks, slopes, q_norm, k_norm, B, S):
    L = SWA_BLOCK
    nb = S // L
    T = B * S
    smem = pl.BlockSpec(memory_space=pltpu.SMEM)

    def prev(b, n):
        return b * nb + jnp.maximum(n - 1, 0)

    return pl.pallas_call(
        _swa_body,
        grid=(B, nb),
        in_specs=[
            smem, smem,
            pl.BlockSpec((L, SWA_Q_DIM), lambda b, n: (b * nb + n, COL_QA // (SWA_Q_DIM // LANES))),
            pl.BlockSpec((L, LANES), lambda b, n: (prev(b, n), COL_KA)),
            pl.BlockSpec((L, LANES), lambda b, n: (b * nb + n, COL_KA)),
            pl.BlockSpec((L, LANES), lambda b, n: (prev(b, n), COL_VA)),
            pl.BlockSpec((L, LANES), lambda b, n: (b * nb + n, COL_VA)),
            pl.BlockSpec((1, LANES), lambda b, n: (0, 0)),
            pl.BlockSpec((1, LANES), lambda b, n: (0, 0)),
        ],
        out_specs=pl.BlockSpec((L, SWA_Q_DIM), lambda b, n: (b * nb + n, 0)),
        out_shape=jax.ShapeDtypeStruct((T, SWA_Q_DIM), bf16),
        compiler_params=pltpu.CompilerParams(
            dimension_semantics=("arbitrary", "arbitrary"),
            vmem_limit_bytes=VMEM_LIMIT),
        name="swa",
    )(sinks, slopes, proj, proj, proj, proj, proj,
      jnp.tile(q_norm, 2).reshape(1, LANES), jnp.tile(k_norm, 2).reshape(1, LANES))


N_BIAS_PARTS = 3
NB_ROWS = 8
N_ALIBI_COL = NB_ROWS * N_BIAS_PARTS
VT_ROWS = HEAD_DIM + 16
SCORE_AHEAD = 3


def _split3(x):
    p0 = x.astype(bf16).astype(f32)
    r1 = x - p0
    p1 = r1.astype(bf16).astype(f32)
    p2 = (r1 - p1).astype(bf16).astype(f32)
    return p0, p1, p2


def _moba_body(slopes_ref, q_ref, k_ref, v_ref, qn_ref, kn_ref, o_ref, kaug_ref, vt_ref, *, nb):
    L = MOBA_BLOCK
    S = nb * L
    p = pl.program_id(1)

    kn = _pair_rms(k_ref[...], kn_ref[...])
    lane = lax.broadcasted_iota(jnp.int32, (L, LANES), 1)
    kpos = lax.broadcasted_iota(jnp.int32, (L, LANES), 0).astype(f32)
    lane_lo = lane < HEAD_DIM

    def alibi_cols(cc, slope):
        a0, a1, a2 = _split3((slope * LOG2E) * kpos)
        return jnp.where(cc == N_ALIBI_COL, a0,
                         jnp.where(cc == N_ALIBI_COL + 1, a1,
                                   jnp.where(cc == N_ALIBI_COL + 2, a2, 0.0)))

    cols = (lane - HEAD_DIM, lane)
    alibi = [alibi_cols(cols[hh], slopes_ref[2 * p + hh]) for hh in range(2)]
    for n in range(nb):
        kblock = kn[n * L:(n + 1) * L, :]
        for hh in range(2):
            cc = cols[hh]
            hot = (cc >= 0) & (cc < N_ALIBI_COL) & ((cc % NB_ROWS) == n)
            aug = jnp.where(hot, 1.0, alibi[hh])
            own = lane_lo if hh == 0 else jnp.logical_not(lane_lo)
            kaug_ref[hh, n * L:(n + 1) * L, :] = jnp.where(own, kblock, aug).astype(bf16)
    vt = v_ref[...].T
    vrow = lax.broadcasted_iota(jnp.int32, (VT_ROWS - HEAD_DIM, S), 0)
    ones_then_zeros = (vrow == 0).astype(f32)
    for hh in range(2):
        vt_ref[hh] = jnp.concatenate(
            [vt[hh * HEAD_DIM:(hh + 1) * HEAD_DIM, :], ones_then_zeros], axis=0).astype(bf16)
    means = [jnp.mean(kn[n * L:(n + 1) * L, :], axis=0, keepdims=True) for n in range(nb)]
    means += [jnp.zeros((1, LANES), f32)] * (NB_ROWS - nb)
    km = jnp.concatenate(means, axis=0)
    lane8 = lax.broadcasted_iota(jnp.int32, (NB_ROWS, LANES), 1)
    km_pair = jnp.concatenate(
        [jnp.where(lane8 < HEAD_DIM, km, 0.0), jnp.where(lane8 < HEAD_DIM, 0.0, km)], axis=0)
    km_hi, km_lo = _split2(km_pair)

    qs = _pair_rms(q_ref[...], qn_ref[...]) * (ATTN_SCALE * LOG2E)
    qt = qs.T
    qt_hi, qt_lo = _split2(qt)
    gate = (jnp.dot(km_hi, qt_hi, preferred_element_type=f32)
            + (jnp.dot(km_hi, qt_lo, preferred_element_type=f32)
               + jnp.dot(km_lo, qt_hi, preferred_element_type=f32)))

    blk = lax.broadcasted_iota(jnp.int32, (NB_ROWS, S), 0)
    qblk = lax.broadcasted_iota(jnp.int32, (NB_ROWS, S), 1) // L
    past = blk < qblk
    r = lax.broadcasted_iota(jnp.int32, (L, L), 1)
    c = lax.broadcasted_iota(jnp.int32, (L, L), 0)
    causal = jnp.where(r >= c, 0.0, NEG)
    ones_rows = (blk < 3).astype(f32)

    qa = []
    for hh in range(2):
        g = jnp.where(past, gate[hh * NB_ROWS:(hh + 1) * NB_ROWS, :], -jnp.inf)
        rank = jnp.zeros((NB_ROWS, S), jnp.int32)
        for m in range(nb):
            gm = g[m:m + 1, :]
            ahead = (gm > g) | ((gm == g) & (m < blk))
            rank = rank + ahead.astype(jnp.int32)
        sel = past & (rank < MOBA_TOPK)
        slope = slopes_ref[2 * p + hh]
        bias = jnp.where(sel, (-slope * L * LOG2E) * (qblk - blk).astype(f32), NEG)
        bias = jnp.where(blk == qblk, 0.0, bias)
        b0, b1, b2 = _split3(bias)
        extra = jnp.concatenate(
            [b0, b1, b2, ones_rows,
             jnp.zeros((HEAD_DIM - N_ALIBI_COL - NB_ROWS, S), f32)], axis=0)
        if hh == 0:
            qa.append(jnp.concatenate([qt[:HEAD_DIM], extra], axis=0).astype(bf16))
        else:
            qa.append(jnp.concatenate([extra, qt[HEAD_DIM:]], axis=0).astype(bf16))

    def scores(i, hh):
        return jnp.dot(kaug_ref[hh, 0:(i + 1) * L, :], qa[hh][:, i * L:(i + 1) * L],
                       preferred_element_type=f32)

    units = [(i, hh) for i in range(nb) for hh in range(2)]
    pending = [scores(*u) for u in units[:SCORE_AHEAD]]
    outs = {}

    def finish(i, hh, e_all):
        acc = jnp.dot(vt_ref[hh, :, 0:(i + 1) * L], e_all, preferred_element_type=f32)
        outs[(i, hh)] = acc[:HEAD_DIM, :] / acc[HEAD_DIM:HEAD_DIM + 1, :]
        if hh == 1:
            o_ref[i * L:(i + 1) * L, :] = jnp.concatenate(
                [outs.pop((i, 0)), outs.pop((i, 1))], axis=0).T.astype(bf16)

    deferred = None
    for idx, (i, hh) in enumerate(units):
        s = pending.pop(0)
        if idx + SCORE_AHEAD < len(units):
            pending.append(scores(*units[idx + SCORE_AHEAD]))
        tiles = [s[n * L:(n + 1) * L, :] for n in range(i)] + [s[i * L:(i + 1) * L, :] + causal]
        m = functools.reduce(jnp.maximum, [jnp.max(t, axis=0, keepdims=True) for t in tiles])
        e_all = jnp.concatenate([jnp.exp2(t - m).astype(bf16) for t in tiles], axis=0)
        if deferred is not None:
            finish(*deferred)
        deferred = (i, hh, e_all)
    finish(*deferred)


def _moba(proj, slopes, q_norm, k_norm, B, S):
    L = MOBA_BLOCK
    nb = S // L
    assert nb <= NB_ROWS
    T = B * S
    n_pairs = MOBA_HEADS // 2
    return pl.pallas_call(
        functools.partial(_moba_body, nb=nb),
        grid=(B, n_pairs),
        in_specs=[
            pl.BlockSpec(memory_space=pltpu.SMEM),
            pl.BlockSpec((S, LANES), lambda b, p: (b, COL_QB + p)),
            pl.BlockSpec((S, LANES), lambda b, p: (b, COL_KB + p)),
            pl.BlockSpec((S, LANES), lambda b, p: (b, COL_VB + p)),
            pl.BlockSpec((1, LANES), lambda b, p: (0, 0)),
            pl.BlockSpec((1, LANES), lambda b, p: (0, 0)),
        ],
        out_specs=pl.BlockSpec((S, LANES), lambda b, p: (b, p)),
        out_shape=jax.ShapeDtypeStruct((T, MOBA_DIM), bf16),
        scratch_shapes=[
            pltpu.VMEM((2, S, LANES), bf16),
            pltpu.VMEM((2, VT_ROWS, S), bf16),
        ],
        compiler_params=pltpu.CompilerParams(
            dimension_semantics=("arbitrary", "arbitrary"),
            vmem_limit_bytes=VMEM_LIMIT),
        name="moba",
    )(slopes, proj, proj, proj,
      jnp.tile(q_norm, 2).reshape(1, LANES), jnp.tile(k_norm, 2).reshape(1, LANES))


ROUTER_COLS = LANES


N_GATE_BLOCKS = D_MODEL // GATE_BLOCK
MIX_CHUNK = 2 * GATE_BLOCK


def _mix_body(*refs):
    ga_refs = refs[:N_GATE_BLOCKS]
    gb_refs = refs[N_GATE_BLOCKS:2 * N_GATE_BLOCKS]
    (oa_ref, ob_ref, x_ref, wua_ref, wub_ref, wout_ref, gffn_ref, wr_ref,
     x1_ref, h2_ref, gid_ref) = refs[2 * N_GATE_BLOCKS:]
    oa = oa_ref[...]
    ob = ob_ref[...]
    gpc = MIX_CHUNK // GATE_BLOCK

    def up(c):
        cols = slice(c * MIX_CHUNK, (c + 1) * MIX_CHUNK)
        return (jnp.dot(oa, wua_ref[:, cols], preferred_element_type=f32),
                jnp.dot(ob, wub_ref[:, cols], preferred_element_type=f32))

    x1 = x_ref[...]
    pending = up(0)
    for c in range(D_MODEL // MIX_CHUNK):
        ya, yb = pending
        if (c + 1) * MIX_CHUNK < D_MODEL:
            pending = up(c + 1)
        gate_a = jnp.concatenate([r[...] for r in ga_refs[c * gpc:(c + 1) * gpc]], axis=1)
        gate_b = jnp.concatenate([r[...] for r in gb_refs[c * gpc:(c + 1) * gpc]], axis=1)
        merged = (_sigmoid(gate_a) * ya + _sigmoid(gate_b) * yb).astype(bf16)
        x1 = x1 + jnp.dot(merged, wout_ref[c * MIX_CHUNK:(c + 1) * MIX_CHUNK, :],
                          preferred_element_type=f32)
    x1_ref[...] = x1
    ms = jnp.mean(x1 * x1, axis=-1, keepdims=True)
    h2 = x1 * lax.rsqrt(ms + EPS) * gffn_ref[...]
    tm = h2.shape[0]
    for c in range(ROW_SLAB):
        h2_ref[pl.ds(c, tm, stride=SLAB_IN), :] = h2[:, c * LANES:(c + 1) * LANES]

    h_hi, h_lo = _split2(h2)
    hi_all = jnp.dot(h_hi, wr_ref[...], preferred_element_type=f32)
    lo_hi = jnp.dot(h_lo, wr_ref[:, :ROUTER_COLS], preferred_element_type=f32)
    lt = (hi_all[:, :ROUTER_COLS] + (lo_hi + hi_all[:, ROUTER_COLS:])).T
    gl = [lt[g:g + 1, :] for g in range(N_GROUPS)]
    gmax = functools.reduce(jnp.maximum, gl)
    gsum = functools.reduce(lambda a, b: a + b, [jnp.exp(v - gmax) for v in gl])
    g_p = 1.0 / gsum
    g_i = jnp.full((1, tm), N_GROUPS - 1, jnp.int32)
    for g in reversed(range(N_GROUPS)):
        g_i = jnp.where(gl[g] == gmax, g, g_i)

    el = []
    for e in range(EXPERTS_PER_GROUP):
        v = jnp.zeros((1, tm), f32)
        for g in range(N_GROUPS):
            r = N_GROUPS + g * EXPERTS_PER_GROUP + e
            v = jnp.where(g_i == g, lt[r:r + 1, :], v)
        el.append(v)
    emax = functools.reduce(jnp.maximum, el)
    ex = [jnp.exp(v - emax) for v in el]
    esum = functools.reduce(lambda a, b: a + b, ex)
    ep = [v / esum for v in ex]
    p1 = functools.reduce(jnp.maximum, ep)
    i1 = jnp.full((1, tm), EXPERTS_PER_GROUP - 1, jnp.int32)
    for e in reversed(range(EXPERTS_PER_GROUP)):
        i1 = jnp.where(ep[e] == p1, e, i1)
    rest = [jnp.where(i1 == e, -1.0, ep[e]) for e in range(EXPERTS_PER_GROUP)]
    p2 = functools.reduce(jnp.maximum, rest)
    i2 = jnp.full((1, tm), EXPERTS_PER_GROUP - 1, jnp.int32)
    for e in reversed(range(EXPERTS_PER_GROUP)):
        i2 = jnp.where(rest[e] == p2, e, i2)
    w1 = g_p * (p1 / (p1 + p2))
    w2 = g_p * (p2 / (p1 + p2))
    gid_ref[...] = jnp.concatenate([g_i, jnp.zeros((7, tm), jnp.int32)], axis=0)
    cw = [jnp.where(i1 == e, w1, 0.0) + jnp.where(i2 == e, w2, 0.0) for e in range(EXPERTS_PER_GROUP)]
    cw_t = jnp.concatenate(cw + [jnp.zeros((LANES - EXPERTS_PER_GROUP, tm), f32)], axis=0).T
    for e in range(SLAB_IN - ROW_SLAB):
        if e < EXPERTS_PER_GROUP:
            row = jnp.broadcast_to(cw_t[:, e:e + 1], (tm, LANES))
        else:
            row = jnp.zeros((tm, LANES), f32)
        h2_ref[pl.ds(ROW_SLAB + e, tm, stride=SLAB_IN), :] = row


def _mix(oa, ob, proj, x2, wua, wub, wout, g_ffn, wr, tm=256):
    T = x2.shape[0]
    const = lambda i: (0, 0)
    single = pl.Buffered(1)

    def gate_specs(first_col):
        first = first_col * LANES // GATE_BLOCK
        return [pl.BlockSpec((tm, GATE_BLOCK), functools.partial(lambda i, j: (i, j), j=first + k))
                for k in range(N_GATE_BLOCKS)]

    return pl.pallas_call(
        _mix_body,
        grid=(T // tm,),
        in_specs=gate_specs(COL_GATE_A) + gate_specs(COL_GATE_B) + [
            pl.BlockSpec((tm, SWA_Q_DIM), lambda i: (i, 0)),
            pl.BlockSpec((tm, MOBA_DIM), lambda i: (i, 0)),
            pl.BlockSpec((tm, D_MODEL), lambda i: (i, 0)),
            pl.BlockSpec((SWA_Q_DIM, D_MODEL), const, pipeline_mode=single),
            pl.BlockSpec((MOBA_DIM, D_MODEL), const, pipeline_mode=single),
            pl.BlockSpec((D_MODEL, D_MODEL), const, pipeline_mode=single),
            pl.BlockSpec((1, D_MODEL), const),
            pl.BlockSpec((D_MODEL, 2 * ROUTER_COLS), const),
        ],
        out_specs=[
            pl.BlockSpec((tm, D_MODEL), lambda i: (i, 0)),
            pl.BlockSpec((tm * SLAB_IN, LANES), lambda i: (i, 0)),
            pl.BlockSpec((8, tm), lambda i: (0, i)),
        ],
        out_shape=[
            jax.ShapeDtypeStruct((T, D_MODEL), f32),
            jax.ShapeDtypeStruct((T * SLAB_IN, LANES), f32),
            jax.ShapeDtypeStruct((8, T), jnp.int32),
        ],
        compiler_params=pltpu.CompilerParams(
            dimension_semantics=("arbitrary",),
            vmem_limit_bytes=VMEM_LIMIT),
        name="mix",
    )(*([proj] * (2 * N_GATE_BLOCKS)), oa, ob, x2, wua, wub, wout, g_ffn.reshape(1, D_MODEL), wr)


def _route_tables(gid, T, tm):
    nt = T // tm + N_GROUPS
    g = gid[0]
    onehot = (g[:, None] == jnp.arange(N_GROUPS, dtype=jnp.int32)[None, :]).astype(jnp.int32)
    incl = jnp.cumsum(onehot, axis=0)
    pos = jnp.sum(onehot * incl, axis=1) - 1
    counts = incl[-1]
    padded = (counts + tm - 1) // tm * tm
    seg_end = jnp.cumsum(padded)
    seg_start = seg_end - padded
    dest = jnp.sum(onehot * seg_start[None, :], axis=1) + pos
    tile_start = jnp.arange(nt, dtype=jnp.int32) * tm
    tile_g = jnp.sum((tile_start[:, None] >= seg_end[None, :]).astype(jnp.int32), axis=1)
    tile_g = jnp.minimum(tile_g, N_GROUPS - 1)
    nvalid = (seg_end[-1] // tm).astype(jnp.int32).reshape(1)
    n_pad = padded - counts
    pads = jnp.concatenate([seg_start + counts, seg_end, jnp.cumsum(n_pad) - n_pad])
    return (nvalid, tile_g.astype(jnp.int32), dest.astype(jnp.int32), pads.astype(jnp.int32))


def _moe_body(nvalid_ref, tg_ref, dest_ref, pads_ref, wg_ref, wu_ref, wd_ref, h2s_hbm, ytok_hbm,
              asg, gbuf, ybuf, xbuf, hbuf, gsem, ssem, *, tm, nt, n_tok):
    t = pl.program_id(0)
    e = pl.program_id(1)
    last_e = EXPERTS_PER_GROUP - 1
    nvalid = nvalid_ref[0]
    slot = lax.rem(t, 2)
    other = 1 - slot

    def for_rows(fn):
        def body(r, carry):
            fn(r)
            return carry
        lax.fori_loop(0, tm, body, 0, unroll=8)

    def gather_copy(tok, s, r):
        return pltpu.make_async_copy(
            h2s_hbm.at[pl.ds(pl.multiple_of(tok * SLAB_IN, 8), SLAB_IN), :],
            gbuf.at[pl.ds(pl.multiple_of((s * tm + r) * SLAB_IN, 8), SLAB_IN), :], gsem.at[s])

    def scatter_copy(row, s, r):
        return pltpu.make_async_copy(
            ybuf.at[pl.ds(pl.multiple_of((s * tm + r) * ROW_SLAB, 8), ROW_SLAB), :],
            ytok_hbm.at[pl.ds(pl.multiple_of(row * ROW_SLAB, 8), ROW_SLAB), :], ssem.at[s])

    def start_gathers(tile, s):
        for_rows(lambda r: gather_copy(jnp.maximum(asg[tile * tm + r], 0), s, r).start())

    def start_scatters(tile, s):
        def one(r):
            a = asg[tile * tm + r]
            scatter_copy(jnp.where(a >= 0, a, n_tok - 1 - a), s, r).start()
        for_rows(one)

    def wait_gathers(s):
        for_rows(lambda r: gather_copy(0, s, r).wait())

    def wait_scatters(s):
        for_rows(lambda r: scatter_copy(0, s, r).wait())

    @pl.when((t == 0) & (e == 0))
    def _():
        def place(tok, carry):
            asg[dest_ref[tok]] = tok
            return carry
        lax.fori_loop(0, n_tok, place, 0, unroll=8)

        for g in range(N_GROUPS):
            first = pads_ref[g]
            code = -1 - pads_ref[2 * N_GROUPS + g] + first

            def mark(p, carry):
                asg[p] = code - p
                return carry
            lax.fori_loop(first, pads_ref[N_GROUPS + g], mark, 0)
        ybuf[...] = jnp.zeros(ybuf.shape, f32)
        start_gathers(0, 0)

    @pl.when((e == 0) & (t < nvalid))
    def _():
        wait_gathers(slot)
        base = slot * tm * SLAB_IN
        xbuf[...] = jnp.concatenate(
            [gbuf[pl.ds(base + c, tm, stride=SLAB_IN), :] for c in range(ROW_SLAB)],
            axis=1).astype(bf16)

    @pl.when(t < nvalid)
    def _():
        quarter = tm // EXPERTS_PER_GROUP
        have_next = t + 1 < nvalid
        have_prev = t >= 1
        nxt = jnp.minimum(t + 1, nvalid - 1)

        def issue_rows(r0, r1):
            for r in range(r0, r1):
                row = e * quarter + r
                tok = jnp.where(have_next, jnp.maximum(asg[nxt * tm + row], 0), 0)
                gather_copy(tok, other, row).start()
                a = asg[jnp.maximum(t - 1, 0) * tm + row]
                dst = jnp.where(have_prev, jnp.where(a >= 0, a, n_tok - 1 - a), nt * tm + row)
                scatter_copy(dst, other, row).start(priority=r % 2)

        x = xbuf[...]
        n_chunks = D_EXPERT // PROJ_CHUNK
        per_piece = quarter // (2 * n_chunks)
        pieces = {"g": [], "u": []}
        for i, (name, w_ref) in enumerate((n, w) for c in range(n_chunks)
                                          for n, w in (("g", wg_ref), ("u", wu_ref))):
            c = i // 2
            w = w_ref[:, c * PROJ_CHUNK:(c + 1) * PROJ_CHUNK].astype(bf16)
            pieces[name].append(jnp.dot(x, w, preferred_element_type=f32))
            issue_rows(i * per_piece, (i + 1) * per_piece)
        gte = jnp.concatenate(pieces["g"], axis=1)
        up = jnp.concatenate(pieces["u"], axis=1)
        cw = gbuf[pl.ds(slot * tm * SLAB_IN + ROW_SLAB + e, tm, stride=SLAB_IN), :]
        hid = (gte * _sigmoid(gte)) * up * jnp.concatenate([cw] * (D_EXPERT // LANES), axis=1)
        hbuf[e] = hid.astype(bf16)

    @pl.when((e == last_e) & (t < nvalid))
    def _():
        hcat = jnp.concatenate([hbuf[k] for k in range(EXPERTS_PER_GROUP)], axis=1)
        y = jnp.dot(hcat, wd_ref[...], preferred_element_type=f32)

        @pl.when(t >= 1)
        def _():
            wait_scatters(slot)
        base = slot * tm * ROW_SLAB
        for c in range(ROW_SLAB):
            ybuf[pl.ds(base + c, tm, stride=ROW_SLAB), :] = y[:, c * LANES:(c + 1) * LANES]

        @pl.when(t == nvalid - 1)
        def _():
            start_scatters(t, slot)
            wait_gathers(other)
            wait_scatters(other)
            wait_scatters(slot)

    @pl.when((e == last_e) & (t == nt - 1))
    def _():
        ybuf[0:tm * ROW_SLAB, :] = jnp.zeros((tm * ROW_SLAB, LANES), f32)
        for j in range(n_tok // tm, nt):
            @pl.when(j >= nvalid)
            def _():
                fill = pltpu.make_async_copy(
                    ybuf.at[0:tm * ROW_SLAB, :],
                    ytok_hbm.at[j * tm * ROW_SLAB:(j + 1) * tm * ROW_SLAB, :], ssem.at[0])
                fill.start()
                fill.wait()


def _moe(h2s, tables, wg, wu, wd, T, tm):
    nvalid, tile_g, dest, pads = tables
    nt = tile_g.shape[0]

    def w_index(t, e, nv, tg, de, pd):
        return (tg[t] * EXPERTS_PER_GROUP + jnp.where(t < nv[0], e, EXPERTS_PER_GROUP - 1), 0, 0)

    grid_spec = pltpu.PrefetchScalarGridSpec(
        num_scalar_prefetch=4,
        grid=(nt, EXPERTS_PER_GROUP),
        in_specs=[
            pl.BlockSpec((None, D_MODEL, D_EXPERT), w_index),
            pl.BlockSpec((None, D_MODEL, D_EXPERT), w_index),
            pl.BlockSpec((None, EXPERTS_PER_GROUP * D_EXPERT, D_MODEL),
                         lambda t, e, nv, tg, de, pd: (tg[t], 0, 0), pipeline_mode=pl.Buffered(1)),
            pl.BlockSpec(memory_space=pl.ANY),
        ],
        out_specs=pl.BlockSpec(memory_space=pl.ANY),
        scratch_shapes=[
            pltpu.SMEM((nt * tm,), jnp.int32),
            pltpu.VMEM((2 * tm * SLAB_IN, LANES), f32),
            pltpu.VMEM((2 * tm * ROW_SLAB, LANES), f32),
            pltpu.VMEM((tm, D_MODEL), bf16),
            pltpu.VMEM((EXPERTS_PER_GROUP, tm, D_EXPERT), bf16),
            pltpu.SemaphoreType.DMA((2,)),
            pltpu.SemaphoreType.DMA((2,)),
        ],
    )
    return pl.pallas_call(
        functools.partial(_moe_body, tm=tm, nt=nt, n_tok=T),
        grid_spec=grid_spec,
        out_shape=jax.ShapeDtypeStruct(((nt + 1) * tm * ROW_SLAB, LANES), f32),
        compiler_params=pltpu.CompilerParams(
            dimension_semantics=("arbitrary", "arbitrary"),
            vmem_limit_bytes=VMEM_LIMIT),
        name="moe",
    )(nvalid, tile_g, dest, pads, wg, wu,
      wd.reshape(N_GROUPS, EXPERTS_PER_GROUP * D_EXPERT, D_MODEL), h2s)


def _combine_body(x1_ref, y_ref, o_ref):
    tm = x1_ref.shape[0]
    for c in range(ROW_SLAB):
        cols = slice(c * LANES, (c + 1) * LANES)
        o_ref[:, cols] = x1_ref[:, cols] + y_ref[pl.ds(c, tm, stride=ROW_SLAB), :]


def _combine(x1, ytok, tm=256):
    T = x1.shape[0]
    return pl.pallas_call(
        _combine_body,
        grid=(T // tm,),
        in_specs=[
            pl.BlockSpec((tm, D_MODEL), lambda i: (i, 0)),
            pl.BlockSpec((tm * ROW_SLAB, LANES), lambda i: (i, 0)),
        ],
        out_specs=pl.BlockSpec((tm, D_MODEL), lambda i: (i, 0)),
        out_shape=jax.ShapeDtypeStruct((T, D_MODEL), f32),
        compiler_params=pltpu.CompilerParams(
            dimension_semantics=("arbitrary",),
            vmem_limit_bytes=VMEM_LIMIT),
        name="combine",
    )(x1, ytok)


def _alibi_slopes(n):
    return jnp.exp2(-8.0 * jnp.arange(1, n + 1, dtype=f32) / n)


def kernel(x, g_mix, w_in, q_norm_swa, k_norm_swa, sinks, q_norm_moba, k_norm_moba,
           w_up_swa, w_up_moba, w_out, g_ffn, w_router_group, w_router_expert,
           w_gate_e, w_up_e, w_down_e):
    B, S, D = x.shape
    assert D == D_MODEL and S % MOBA_BLOCK == 0 and S % SWA_BLOCK == 0
    T = B * S
    x2 = x.reshape(T, D)

    proj = _inproj(x2, g_mix, w_in)

    oa = _swa(proj, sinks.astype(f32), _alibi_slopes(SWA_Q_HEADS), q_norm_swa, k_norm_swa, B, S)
    ob = _moba(proj, _alibi_slopes(MOBA_HEADS), q_norm_moba, k_norm_moba, B, S)

    wr = jnp.concatenate(
        [w_router_group,
         w_router_expert.transpose(1, 0, 2).reshape(D, N_EXPERTS),
         jnp.zeros((D, ROUTER_COLS - N_GROUPS - N_EXPERTS), f32)], axis=1)
    wr_hi = wr.astype(bf16)
    wr_lo = (wr - wr_hi.astype(f32)).astype(bf16)
    x1, h2s, gid = _mix(oa, ob, proj, x2, w_up_swa.astype(bf16), w_up_moba.astype(bf16),
                        w_out.astype(bf16), g_ffn, jnp.concatenate([wr_hi, wr_lo], axis=1))

    tables = _route_tables(gid, T, MOE_TILE)
    ytok = _moe(h2s, tables, w_gate_e, w_up_e, w_down_e.astype(bf16),
                T, MOE_TILE)
    y = _combine(x1, ytok)
    return y.reshape(B, S, D)
```

```python
import functools

import jax
import jax.numpy as jnp
from jax import lax
from jax.experimental import pallas as pl
from jax.experimental.pallas import tpu as pltpu

D_MODEL = 2048
HEAD_DIM = 64
ATTN_SCALE = HEAD_DIM ** -0.5
SWA_Q_HEADS = 16
SWA_KV_HEADS = 2
SWA_WINDOW = 128
SWA_BLOCK = 128
MOBA_HEADS = 16
MOBA_BLOCK = 256
MOBA_TOPK = 3
N_GROUPS = 4
EXPERTS_PER_GROUP = 4
N_EXPERTS = N_GROUPS * EXPERTS_PER_GROUP
D_EXPERT = 512
EPS = 1e-6

SWA_Q_DIM = SWA_Q_HEADS * HEAD_DIM
MOBA_DIM = MOBA_HEADS * HEAD_DIM

LANES = 128
ROW_SLAB = D_MODEL // LANES
SLAB_IN = ROW_SLAB + 8
MOE_TILE = 512
PROJ_CHUNK = 2 * LANES
VMEM_LIMIT = 56 * 1024 * 1024
NEG = -1e30
LOG2E = 1.4426950408889634

COL_QA = 0
COL_KA = COL_QA + SWA_Q_DIM // LANES
COL_VA = COL_KA + 1
COL_QB = COL_VA + 1
COL_KB = COL_QB + MOBA_DIM // LANES
COL_VB = COL_KB + MOBA_DIM // LANES
COL_GATE_A = COL_VB + MOBA_DIM // LANES
COL_GATE_B = COL_GATE_A + D_MODEL // LANES
GATE_BLOCK = 2 * LANES
assert (COL_GATE_A * LANES) % GATE_BLOCK == 0 and (COL_GATE_B * LANES) % GATE_BLOCK == 0

bf16 = jnp.bfloat16
f32 = jnp.float32


def _sigmoid(x):
    return 1.0 / (1.0 + jnp.exp(-x))


def _pair_rms(x, gain):
    lane = lax.broadcasted_iota(jnp.int32, x.shape, 1)
    lo = lane < HEAD_DIM
    sq = x * x
    s0 = jnp.sum(jnp.where(lo, sq, 0.0), axis=-1, keepdims=True)
    s1 = jnp.sum(jnp.where(lo, 0.0, sq), axis=-1, keepdims=True)
    r0 = lax.rsqrt(s0 * (1.0 / HEAD_DIM) + EPS)
    r1 = lax.rsqrt(s1 * (1.0 / HEAD_DIM) + EPS)
    return x * jnp.where(lo, r0, r1) * gain


def _split2(x):
    hi = x.astype(bf16)
    lo = (x - hi.astype(f32)).astype(bf16)
    return hi, lo


CAST_CHUNK = 2 * LANES
NORM_ROWS = 512


def _inproj_body(x_ref, g_ref, w_ref, o_ref, h_ref):
    @pl.when(pl.program_id(1) == 0)
    def _():
        for r in range(0, x_ref.shape[0], NORM_ROWS):
            x = x_ref[r:r + NORM_ROWS, :]
            ms = jnp.mean(x * x, axis=-1, keepdims=True)
            h_ref[r:r + NORM_ROWS, :] = (x * lax.rsqrt(ms + EPS) * g_ref[...]).astype(bf16)

    n_chunks = w_ref.shape[1] // CAST_CHUNK
    cast = lambda c: w_ref[:, c * CAST_CHUNK:(c + 1) * CAST_CHUNK].astype(bf16)
    w_next = cast(0)
    for c in range(n_chunks):
        w_cur = w_next
        if c + 1 < n_chunks:
            w_next = cast(c + 1)
        o_ref[:, c * CAST_CHUNK:(c + 1) * CAST_CHUNK] = jnp.dot(
            h_ref[...], w_cur, preferred_element_type=f32)


def _inproj(x2, g, w, tm=2048, tn=768):
    T = x2.shape[0]
    N = w.shape[1]
    return pl.pallas_call(
        _inproj_body,
        grid=(T // tm, N // tn),
        in_specs=[
            pl.BlockSpec((tm, D_MODEL), lambda i, j: (i, 0), pipeline_mode=pl.Buffered(1)),
            pl.BlockSpec((1, D_MODEL), lambda i, j: (0, 0)),
            pl.BlockSpec((D_MODEL, tn), lambda i, j: (0, j)),
        ],
        out_specs=pl.BlockSpec((tm, tn), lambda i, j: (i, j)),
        out_shape=jax.ShapeDtypeStruct((T, N), f32),
        scratch_shapes=[pltpu.VMEM((tm, D_MODEL), bf16)],
        compiler_params=pltpu.CompilerParams(
            dimension_semantics=("arbitrary", "arbitrary"),
            vmem_limit_bytes=VMEM_LIMIT),
        name="inproj",
    )(x2, g.reshape(1, D_MODEL), w)


SWA_AHEAD = 2


def _swa_body(sinks_ref, slopes_ref, q_ref, kp_ref, kc_ref, vp_ref, vc_ref,
              qn_ref, kn_ref, o_ref):
    L = SWA_BLOCK
    n = pl.program_id(1)
    k2 = jnp.concatenate([kp_ref[...], kc_ref[...]], axis=0)
    k2n = _pair_rms(k2, kn_ref[...]).astype(bf16)
    v2t = jnp.concatenate([vp_ref[...], vc_ref[...]], axis=0).T.astype(bf16)
    qn = jnp.concatenate(
        [_pair_rms(q_ref[:, pp * LANES:(pp + 1) * LANES], qn_ref[...])
         for pp in range(SWA_Q_HEADS // 2)], axis=1) * (ATTN_SCALE * LOG2E)
    qt = qn.T

    key = lax.broadcasted_iota(jnp.int32, (2 * L, L), 0)
    qry = lax.broadcasted_iota(jnp.int32, (2 * L, L), 1)
    dist = qry + L - key
    ok = (dist >= 0) & (dist < SWA_WINDOW) & ((n > 0) | (key >= L))
    distf = dist.astype(f32)
    zeros = jnp.zeros((HEAD_DIM, L), f32)

    heads_per_kv = SWA_Q_HEADS // SWA_KV_HEADS

    def scores(h):
        g = h // heads_per_kv
        qh = qt[h * HEAD_DIM:(h + 1) * HEAD_DIM, :]
        qa = jnp.concatenate([qh, zeros] if g == 0 else [zeros, qh], axis=0).astype(bf16)
        return jnp.dot(k2n, qa, preferred_element_type=f32)

    def softmax(h, s):
        s = jnp.where(ok, s - (slopes_ref[h] * LOG2E) * distf, -jnp.inf)
        sink = sinks_ref[h] * LOG2E
        m = jnp.maximum(jnp.max(s, axis=0, keepdims=True), sink)
        p = jnp.exp2(s - m)
        denom = jnp.sum(p, axis=0, keepdims=True) + jnp.exp2(sink - m)
        return p.astype(bf16), denom

    outs = []

    def finish(h, p, denom):
        g = h // heads_per_kv
        o = jnp.dot(v2t, p, preferred_element_type=f32)
        outs.append(o[g * HEAD_DIM:(g + 1) * HEAD_DIM, :] / denom)

    pending = [scores(h) for h in range(SWA_AHEAD)]
    deferred = None
    for h in range(SWA_Q_HEADS):
        s = pending.pop(0)
        if h + SWA_AHEAD < SWA_Q_HEADS:
            pending.append(scores(h + SWA_AHEAD))
        p, denom = softmax(h, s)
        if deferred is not None:
            finish(*deferred)
        deferred = (h, p, denom)
    finish(*deferred)
    o_ref[...] = jnp.concatenate(outs, axis=0).T.astype(bf16)


def _swa(proj, sinks, slopes, q_norm, k_norm, B, S):
    L = SWA_BLOCK
    nb = S // L
    T = B * S
    smem = pl.BlockSpec(memory_space=pltpu.SMEM)

    def prev(b, n):
        return b * nb + jnp.maximum(n - 1, 0)

    return pl.pallas_call(
        _swa_body,
        grid=(B, nb),
        in_specs=[
            smem, smem,
            pl.BlockSpec((L, SWA_Q_DIM), lambda b, n: (b * nb + n, COL_QA // (SWA_Q_DIM // LANES))),
            pl.BlockSpec((L, LANES), lambda b, n: (prev(b, n), COL_KA)),
            pl.BlockSpec((L, LANES), lambda b, n: (b * nb + n, COL_KA)),
            pl.BlockSpec((L, LANES), lambda b, n: (prev(b, n), COL_VA)),
            pl.BlockSpec((L, LANES), lambda b, n: (b * nb + n, COL_VA)),
            pl.BlockSpec((1, LANES), lambda b, n: (0, 0)),
            pl.BlockSpec((1, LANES), lambda b, n: (0, 0)),
        ],
        out_specs=pl.BlockSpec((L, SWA_Q_DIM), lambda b, n: (b * nb + n, 0)),
        out_shape=jax.ShapeDtypeStruct((T, SWA_Q_DIM), bf16),
        compiler_params=pltpu.CompilerParams(
            dimension_semantics=("arbitrary", "arbitrary"),
            vmem_limit_bytes=VMEM_LIMIT),
        name="swa",
    )(sinks, slopes, proj, proj, proj, proj, proj,
      jnp.tile(q_norm, 2).reshape(1, LANES), jnp.tile(k_norm, 2).reshape(1, LANES))


N_BIAS_PARTS = 3
NB_ROWS = 8
N_ALIBI_COL = NB_ROWS * N_BIAS_PARTS
VT_ROWS = HEAD_DIM + 16
SCORE_AHEAD = 3


def _split3(x):
    p0 = x.astype(bf16).astype(f32)
    r1 = x - p0
    p1 = r1.astype(bf16).astype(f32)
    p2 = (r1 - p1).astype(bf16).astype(f32)
    return p0, p1, p2


def _moba_body(slopes_ref, q_ref, k_ref, v_ref, qn_ref, kn_ref, o_ref, kaug_ref, vt_ref, *, nb):
    L = MOBA_BLOCK
    S = nb * L
    p = pl.program_id(1)

    kn = _pair_rms(k_ref[...], kn_ref[...])
    lane = lax.broadcasted_iota(jnp.int32, (L, LANES), 1)
    kpos = lax.broadcasted_iota(jnp.int32, (L, LANES), 0).astype(f32)
    lane_lo = lane < HEAD_DIM

    def alibi_cols(cc, slope):
        a0, a1, a2 = _split3((slope * LOG2E) * kpos)
        return jnp.where(cc == N_ALIBI_COL, a0,
                         jnp.where(cc == N_ALIBI_COL + 1, a1,
                                   jnp.where(cc == N_ALIBI_COL + 2, a2, 0.0)))

    cols = (lane - HEAD_DIM, lane)
    alibi = [alibi_cols(cols[hh], slopes_ref[2 * p + hh]) for hh in range(2)]
    for n in range(nb):
        kblock = kn[n * L:(n + 1) * L, :]
        for hh in range(2):
            cc = cols[hh]
            hot = (cc >= 0) & (cc < N_ALIBI_COL) & ((cc % NB_ROWS) == n)
            aug = jnp.where(hot, 1.0, alibi[hh])
            own = lane_lo if hh == 0 else jnp.logical_not(lane_lo)
            kaug_ref[hh, n * L:(n + 1) * L, :] = jnp.where(own, kblock, aug).astype(bf16)
    vt = v_ref[...].T
    vrow = lax.broadcasted_iota(jnp.int32, (VT_ROWS - HEAD_DIM, S), 0)
    ones_then_zeros = (vrow == 0).astype(f32)
    for hh in range(2):
        vt_ref[hh] = jnp.concatenate(
            [vt[hh * HEAD_DIM:(hh + 1) * HEAD_DIM, :], ones_then_zeros], axis=0).astype(bf16)
    means = [jnp.mean(kn[n * L:(n + 1) * L, :], axis=0, keepdims=True) for n in range(nb)]
    means += [jnp.zeros((1, LANES), f32)] * (NB_ROWS - nb)
    km = jnp.concatenate(means, axis=0)
    lane8 = lax.broadcasted_iota(jnp.int32, (NB_ROWS, LANES), 1)
    km_pair = jnp.concatenate(
        [jnp.where(lane8 < HEAD_DIM, km, 0.0), jnp.where(lane8 < HEAD_DIM, 0.0, km)], axis=0)
    km_hi, km_lo = _split2(km_pair)

    qs = _pair_rms(q_ref[...], qn_ref[...]) * (ATTN_SCALE * LOG2E)
    qt = qs.T
    qt_hi, qt_lo = _split2(qt)
    gate = (jnp.dot(km_hi, qt_hi, preferred_element_type=f32)
            + (jnp.dot(km_hi, qt_lo, preferred_element_type=f32)
               + jnp.dot(km_lo, qt_hi, preferred_element_type=f32)))

    blk = lax.broadcasted_iota(jnp.int32, (NB_ROWS, S), 0)
    qblk = lax.broadcasted_iota(jnp.int32, (NB_ROWS, S), 1) // L
    past = blk < qblk
    r = lax.broadcasted_iota(jnp.int32, (L, L), 1)
    c = lax.broadcasted_iota(jnp.int32, (L, L), 0)
    causal = jnp.where(r >= c, 0.0, NEG)
    ones_rows = (blk < 3).astype(f32)

    qa = []
    for hh in range(2):
        g = jnp.where(past, gate[hh * NB_ROWS:(hh + 1) * NB_ROWS, :], -jnp.inf)
        rank = jnp.zeros((NB_ROWS, S), jnp.int32)
        for m in range(nb):
            gm = g[m:m + 1, :]
            ahead = (gm > g) | ((gm == g) & (m < blk))
            rank = rank + ahead.astype(jnp.int32)
        sel = past & (rank < MOBA_TOPK)
        slope = slopes_ref[2 * p + hh]
        bias = jnp.where(sel, (-slope * L * LOG2E) * (qblk - blk).astype(f32), NEG)
        bias = jnp.where(blk == qblk, 0.0, bias)
        b0, b1, b2 = _split3(bias)
        extra = jnp.concatenate(
            [b0, b1, b2, ones_rows,
             jnp.zeros((HEAD_DIM - N_ALIBI_COL - NB_ROWS, S), f32)], axis=0)
        if hh == 0:
            qa.append(jnp.concatenate([qt[:HEAD_DIM], extra], axis=0).astype(bf16))
        else:
            qa.append(jnp.concatenate([extra, qt[HEAD_DIM:]], axis=0).astype(bf16))

    def scores(i, hh):
        return jnp.dot(kaug_ref[hh, 0:(i + 1) * L, :], qa[hh][:, i * L:(i + 1) * L],
                       preferred_element_type=f32)

    units = [(i, hh) for i in range(nb) for hh in range(2)]
    pending = [scores(*u) for u in units[:SCORE_AHEAD]]
    outs = {}

    def finish(i, hh, e_all):
        acc = jnp.dot(vt_ref[hh, :, 0:(i + 1) * L], e_all, preferred_element_type=f32)
        outs[(i, hh)] = acc[:HEAD_DIM, :] / acc[HEAD_DIM:HEAD_DIM + 1, :]
        if hh == 1:
            o_ref[i * L:(i + 1) * L, :] = jnp.concatenate(
                [outs.pop((i, 0)), outs.pop((i, 1))], axis=0).T.astype(bf16)

    deferred = None
    for idx, (i, hh) in enumerate(units):
        s = pending.pop(0)
        if idx + SCORE_AHEAD < len(units):
            pending.append(scores(*units[idx + SCORE_AHEAD]))
        tiles = [s[n * L:(n + 1) * L, :] for n in range(i)] + [s[i * L:(i + 1) * L, :] + causal]
        m = functools.reduce(jnp.maximum, [jnp.max(t, axis=0, keepdims=True) for t in tiles])
        e_all = jnp.concatenate([jnp.exp2(t - m).astype(bf16) for t in tiles], axis=0)
        if deferred is not None:
            finish(*deferred)
        deferred = (i, hh, e_all)
    finish(*deferred)


def _moba(proj, slopes, q_norm, k_norm, B, S):
    L = MOBA_BLOCK
    nb = S // L
    assert nb <= NB_ROWS
    T = B * S
    n_pairs = MOBA_HEADS // 2
    return pl.pallas_call(
        functools.partial(_moba_body, nb=nb),
        grid=(B, n_pairs),
        in_specs=[
            pl.BlockSpec(memory_space=pltpu.SMEM),
            pl.BlockSpec((S, LANES), lambda b, p: (b, COL_QB + p)),
            pl.BlockSpec((S, LANES), lambda b, p: (b, COL_KB + p)),
            pl.BlockSpec((S, LANES), lambda b, p: (b, COL_VB + p)),
            pl.BlockSpec((1, LANES), lambda b, p: (0, 0)),
            pl.BlockSpec((1, LANES), lambda b, p: (0, 0)),
        ],
        out_specs=pl.BlockSpec((S, LANES), lambda b, p: (b, p)),
        out_shape=jax.ShapeDtypeStruct((T, MOBA_DIM), bf16),
        scratch_shapes=[
            pltpu.VMEM((2, S, LANES), bf16),
            pltpu.VMEM((2, VT_ROWS, S), bf16),
        ],
        compiler_params=pltpu.CompilerParams(
            dimension_semantics=("arbitrary", "arbitrary"),
            vmem_limit_bytes=VMEM_LIMIT),
        name="moba",
    )(slopes, proj, proj, proj,
      jnp.tile(q_norm, 2).reshape(1, LANES), jnp.tile(k_norm, 2).reshape(1, LANES))


ROUTER_COLS = LANES


N_GATE_BLOCKS = D_MODEL // GATE_BLOCK
MIX_CHUNK = 2 * GATE_BLOCK


def _mix_body(*refs):
    ga_refs = refs[:N_GATE_BLOCKS]
    gb_refs = refs[N_GATE_BLOCKS:2 * N_GATE_BLOCKS]
    (oa_ref, ob_ref, x_ref, wua_ref, wub_ref, wout_ref, gffn_ref, wr_ref,
     x1_ref, h2_ref, gid_ref) = refs[2 * N_GATE_BLOCKS:]
    oa = oa_ref[...]
    ob = ob_ref[...]
    gpc = MIX_CHUNK // GATE_BLOCK

    def up(c):
        cols = slice(c * MIX_CHUNK, (c + 1) * MIX_CHUNK)
        return (jnp.dot(oa, wua_ref[:, cols], preferred_element_type=f32),
                jnp.dot(ob, wub_ref[:, cols], preferred_element_type=f32))

    x1 = x_ref[...]
    pending = up(0)
    for c in range(D_MODEL // MIX_CHUNK):
        ya, yb = pending
        if (c + 1) * MIX_CHUNK < D_MODEL:
            pending = up(c + 1)
        gate_a = jnp.concatenate([r[...] for r in ga_refs[c * gpc:(c + 1) * gpc]], axis=1)
        gate_b = jnp.concatenate([r[...] for r in gb_refs[c * gpc:(c + 1) * gpc]], axis=1)
        merged = (_sigmoid(gate_a) * ya + _sigmoid(gate_b) * yb).astype(bf16)
        x1 = x1 + jnp.dot(merged, wout_ref[c * MIX_CHUNK:(c + 1) * MIX_CHUNK, :],
                          preferred_element_type=f32)
    x1_ref[...] = x1
    ms = jnp.mean(x1 * x1, axis=-1, keepdims=True)
    h2 = x1 * lax.rsqrt(ms + EPS) * gffn_ref[...]
    tm = h2.shape[0]
    for c in range(ROW_SLAB):
        h2_ref[pl.ds(c, tm, stride=SLAB_IN), :] = h2[:, c * LANES:(c + 1) * LANES]

    h_hi, h_lo = _split2(h2)
    hi_all = jnp.dot(h_hi, wr_ref[...], preferred_element_type=f32)
    lo_hi = jnp.dot(h_lo, wr_ref[:, :ROUTER_COLS], preferred_element_type=f32)
    lt = (hi_all[:, :ROUTER_COLS] + (lo_hi + hi_all[:, ROUTER_COLS:])).T
    gl = [lt[g:g + 1, :] for g in range(N_GROUPS)]
    gmax = functools.reduce(jnp.maximum, gl)
    gsum = functools.reduce(lambda a, b: a + b, [jnp.exp(v - gmax) for v in gl])
    g_p = 1.0 / gsum
    g_i = jnp.full((1, tm), N_GROUPS - 1, jnp.int32)
    for g in reversed(range(N_GROUPS)):
        g_i = jnp.where(gl[g] == gmax, g, g_i)

    el = []
    for e in range(EXPERTS_PER_GROUP):
        v = jnp.zeros((1, tm), f32)
        for g in range(N_GROUPS):
            r = N_GROUPS + g * EXPERTS_PER_GROUP + e
            v = jnp.where(g_i == g, lt[r:r + 1, :], v)
        el.append(v)
    emax = functools.reduce(jnp.maximum, el)
    ex = [jnp.exp(v - emax) for v in el]
    esum = functools.reduce(lambda a, b: a + b, ex)
    ep = [v / esum for v in ex]
    p1 = functools.reduce(jnp.maximum, ep)
    i1 = jnp.full((1, tm), EXPERTS_PER_GROUP - 1, jnp.int32)
    for e in reversed(range(EXPERTS_PER_GROUP)):
        i1 = jnp.where(ep[e] == p1, e, i1)
    rest = [jnp.where(i1 == e, -1.0, ep[e]) for e in range(EXPERTS_PER_GROUP)]
    p2 = functools.reduce(jnp.maximum, rest)
    i2 = jnp.full((1, tm), EXPERTS_PER_GROUP - 1, jnp.int32)
    for e in reversed(range(EXPERTS_PER_GROUP)):
        i2 = jnp.where(rest[e] == p2, e, i2)
    w1 = g_p * (p1 / (p1 + p2))
    w2 = g_p * (p2 / (p1 + p2))
    gid_ref[...] = jnp.concatenate([g_i, jnp.zeros((7, tm), jnp.int32)], axis=0)
    cw = [jnp.where(i1 == e, w1, 0.0) + jnp.where(i2 == e, w2, 0.0) for e in range(EXPERTS_PER_GROUP)]
    cw_t = jnp.concatenate(cw + [jnp.zeros((LANES - EXPERTS_PER_GROUP, tm), f32)], axis=0).T
    for e in range(SLAB_IN - ROW_SLAB):
        if e < EXPERTS_PER_GROUP:
            row = jnp.broadcast_to(cw_t[:, e:e + 1], (tm, LANES))
        else:
            row = jnp.zeros((tm, LANES), f32)
        h2_ref[pl.ds(ROW_SLAB + e, tm, stride=SLAB_IN), :] = row


def _mix(oa, ob, proj, x2, wua, wub, wout, g_ffn, wr, tm=256):
    T = x2.shape[0]
    const = lambda i: (0, 0)
    single = pl.Buffered(1)

    def gate_specs(first_col):
        first = first_col * LANES // GATE_BLOCK
        return [pl.BlockSpec((tm, GATE_BLOCK), functools.partial(lambda i, j: (i, j), j=first + k))
                for k in range(N_GATE_BLOCKS)]

    return pl.pallas_call(
        _mix_body,
        grid=(T // tm,),
        in_specs=gate_specs(COL_GATE_A) + gate_specs(COL_GATE_B) + [
            pl.BlockSpec((tm, SWA_Q_DIM), lambda i: (i, 0)),
            pl.BlockSpec((tm, MOBA_DIM), lambda i: (i, 0)),
            pl.BlockSpec((tm, D_MODEL), lambda i: (i, 0)),
            pl.BlockSpec((SWA_Q_DIM, D_MODEL), const, pipeline_mode=single),
            pl.BlockSpec((MOBA_DIM, D_MODEL), const, pipeline_mode=single),
            pl.BlockSpec((D_MODEL, D_MODEL), const, pipeline_mode=single),
            pl.BlockSpec((1, D_MODEL), const),
            pl.BlockSpec((D_MODEL, 2 * ROUTER_COLS), const),
        ],
        out_specs=[
            pl.BlockSpec((tm, D_MODEL), lambda i: (i, 0)),
            pl.BlockSpec((tm * SLAB_IN, LANES), lambda i: (i, 0)),
            pl.BlockSpec((8, tm), lambda i: (0, i)),
        ],
        out_shape=[
            jax.ShapeDtypeStruct((T, D_MODEL), f32),
            jax.ShapeDtypeStruct((T * SLAB_IN, LANES), f32),
            jax.ShapeDtypeStruct((8, T), jnp.int32),
        ],
        compiler_params=pltpu.CompilerParams(
            dimension_semantics=("arbitrary",),
            vmem_limit_bytes=VMEM_LIMIT),
        name="mix",
    )(*([proj] * (2 * N_GATE_BLOCKS)), oa, ob, x2, wua, wub, wout, g_ffn.reshape(1, D_MODEL), wr)


def _route_tables(gid, T, tm):
    nt = T // tm + N_GROUPS
    g = gid[0]
    onehot = (g[:, None] == jnp.arange(N_GROUPS, dtype=jnp.int32)[None, :]).astype(jnp.int32)
    incl = jnp.cumsum(onehot, axis=0)
    pos = jnp.sum(onehot * incl, axis=1) - 1
    counts = incl[-1]
    padded = (counts + tm - 1) // tm * tm
    seg_end = jnp.cumsum(padded)
    seg_start = seg_end - padded
    dest = jnp.sum(onehot * seg_start[None, :], axis=1) + pos
    tile_start = jnp.arange(nt, dtype=jnp.int32) * tm
    tile_g = jnp.sum((tile_start[:, None] >= seg_end[None, :]).astype(jnp.int32), axis=1)
    tile_g = jnp.minimum(tile_g, N_GROUPS - 1)
    nvalid = (seg_end[-1] // tm).astype(jnp.int32).reshape(1)
    n_pad = padded - counts
    pads = jnp.concatenate([seg_start + counts, seg_end, jnp.cumsum(n_pad) - n_pad])
    return (nvalid, tile_g.astype(jnp.int32), dest.astype(jnp.int32), pads.astype(jnp.int32))


def _moe_body(nvalid_ref, tg_ref, dest_ref, pads_ref, wg_ref, wu_ref, wd_ref, h2s_hbm, ytok_hbm,
              asg, gbuf, ybuf, xbuf, hbuf, gsem, ssem, *, tm, nt, n_tok):
    t = pl.program_id(0)
    e = pl.program_id(1)
    last_e = EXPERTS_PER_GROUP - 1
    nvalid = nvalid_ref[0]
    slot = lax.rem(t, 2)
    other = 1 - slot

    def for_rows(fn):
        def body(r, carry):
            fn(r)
            return carry
        lax.fori_loop(0, tm, body, 0, unroll=8)

    def gather_copy(tok, s, r):
        return pltpu.make_async_copy(
            h2s_hbm.at[pl.ds(pl.multiple_of(tok * SLAB_IN, 8), SLAB_IN), :],
            gbuf.at[pl.ds(pl.multiple_of((s * tm + r) * SLAB_IN, 8), SLAB_IN), :], gsem.at[s])

    def scatter_copy(row, s, r):
        return pltpu.make_async_copy(
            ybuf.at[pl.ds(pl.multiple_of((s * tm + r) * ROW_SLAB, 8), ROW_SLAB), :],
            ytok_hbm.at[pl.ds(pl.multiple_of(row * ROW_SLAB, 8), ROW_SLAB), :], ssem.at[s])

    def start_gathers(tile, s):
        for_rows(lambda r: gather_copy(jnp.maximum(asg[tile * tm + r], 0), s, r).start())

    def start_scatters(tile, s):
        def one(r):
            a = asg[tile * tm + r]
            scatter_copy(jnp.where(a >= 0, a, n_tok - 1 - a), s, r).start()
        for_rows(one)

    def wait_gathers(s):
        for_rows(lambda r: gather_copy(0, s, r).wait())

    def wait_scatters(s):
        for_rows(lambda r: scatter_copy(0, s, r).wait())

    @pl.when((t == 0) & (e == 0))
    def _():
        def place(tok, carry):
            asg[dest_ref[tok]] = tok
            return carry
        lax.fori_loop(0, n_tok, place, 0, unroll=8)

        for g in range(N_GROUPS):
            first = pads_ref[g]
            code = -1 - pads_ref[2 * N_GROUPS + g] + first

            def mark(p, carry):
                asg[p] = code - p
                return carry
            lax.fori_loop(first, pads_ref[N_GROUPS + g], mark, 0)
        ybuf[...] = jnp.zeros(ybuf.shape, f32)
        start_gathers(0, 0)

    @pl.when((e == 0) & (t < nvalid))
    def _():
        wait_gathers(slot)
        base = slot * tm * SLAB_IN
        xbuf[...] = jnp.concatenate(
            [gbuf[pl.ds(base + c, tm, stride=SLAB_IN), :] for c in range(ROW_SLAB)],
            axis=1).astype(bf16)

    @pl.when(t < nvalid)
    def _():
        quarter = tm // EXPERTS_PER_GROUP
        have_next = t + 1 < nvalid
        have_prev = t >= 1
        nxt = jnp.minimum(t + 1, nvalid - 1)

        def issue_rows(r0, r1):
            for r in range(r0, r1):
                row = e * quarter + r
                tok = jnp.where(have_next, jnp.maximum(asg[nxt * tm + row], 0), 0)
                gather_copy(tok, other, row).start(priority=(r + 1) % 2)
                a = asg[jnp.maximum(t - 1, 0) * tm + row]
                dst = jnp.where(have_prev, jnp.where(a >= 0, a, n_tok - 1 - a), nt * tm + row)
                scatter_copy(dst, other, row).start(priority=r % 2)

        x = xbuf[...]
        n_chunks = D_EXPERT // PROJ_CHUNK
        per_piece = quarter // (2 * n_chunks)
        pieces = {"g": [], "u": []}
        for i, (name, w_ref) in enumerate((n, w) for c in range(n_chunks)
                                          for n, w in (("g", wg_ref), ("u", wu_ref))):
            c = i // 2
            w = w_ref[:, c * PROJ_CHUNK:(c + 1) * PROJ_CHUNK].astype(bf16)
            pieces[name].append(jnp.dot(x, w, preferred_element_type=f32))
            issue_rows(i * per_piece, (i + 1) * per_piece)
        gte = jnp.concatenate(pieces["g"], axis=1)
        up = jnp.concatenate(pieces["u"], axis=1)
        cw = gbuf[pl.ds(slot * tm * SLAB_IN + ROW_SLAB + e, tm, stride=SLAB_IN), :]
        hid = (gte * _sigmoid(gte)) * up * jnp.concatenate([cw] * (D_EXPERT // LANES), axis=1)
        hbuf[e] = hid.astype(bf16)

    @pl.when((e == last_e) & (t < nvalid))
    def _():
        hcat = jnp.concatenate([hbuf[k] for k in range(EXPERTS_PER_GROUP)], axis=1)
        y = jnp.dot(hcat, wd_ref[...], preferred_element_type=f32)

        @pl.when(t >= 1)
        def _():
            wait_scatters(slot)
        base = slot * tm * ROW_SLAB
        for c in range(ROW_SLAB):
            ybuf[pl.ds(base + c, tm, stride=ROW_SLAB), :] = y[:, c * LANES:(c + 1) * LANES]

        @pl.when(t == nvalid - 1)
        def _():
            start_scatters(t, slot)
            wait_gathers(other)
            wait_scatters(other)
            wait_scatters(slot)

    @pl.when((e == last_e) & (t == nt - 1))
    def _():
        ybuf[0:tm * ROW_SLAB, :] = jnp.zeros((tm * ROW_SLAB, LANES), f32)
        for j in range(n_tok // tm, nt):
            @pl.when(j >= nvalid)
            def _():
                fill = pltpu.make_async_copy(
                    ybuf.at[0:tm * ROW_SLAB, :],
                    ytok_hbm.at[j * tm * ROW_SLAB:(j + 1) * tm * ROW_SLAB, :], ssem.at[0])
                fill.start()
                fill.wait()


def _moe(h2s, tables, wg, wu, wd, T, tm):
    nvalid, tile_g, dest, pads = tables
    nt = tile_g.shape[0]

    def w_index(t, e, nv, tg, de, pd):
        return (tg[t] * EXPERTS_PER_GROUP + jnp.where(t < nv[0], e, EXPERTS_PER_GROUP - 1), 0, 0)

    grid_spec = pltpu.PrefetchScalarGridSpec(
        num_scalar_prefetch=4,
        grid=(nt, EXPERTS_PER_GROUP),
        in_specs=[
            pl.BlockSpec((None, D_MODEL, D_EXPERT), w_index),
            pl.BlockSpec((None, D_MODEL, D_EXPERT), w_index),
            pl.BlockSpec((None, EXPERTS_PER_GROUP * D_EXPERT, D_MODEL),
                         lambda t, e, nv, tg, de, pd: (tg[t], 0, 0), pipeline_mode=pl.Buffered(1)),
            pl.BlockSpec(memory_space=pl.ANY),
        ],
        out_specs=pl.BlockSpec(memory_space=pl.ANY),
        scratch_shapes=[
            pltpu.SMEM((nt * tm,), jnp.int32),
            pltpu.VMEM((2 * tm * SLAB_IN, LANES), f32),
            pltpu.VMEM((2 * tm * ROW_SLAB, LANES), f32),
            pltpu.VMEM((tm, D_MODEL), bf16),
            pltpu.VMEM((EXPERTS_PER_GROUP, tm, D_EXPERT), bf16),
            pltpu.SemaphoreType.DMA((2,)),
            pltpu.SemaphoreType.DMA((2,)),
        ],
    )
    return pl.pallas_call(
        functools.partial(_moe_body, tm=tm, nt=nt, n_tok=T),
        grid_spec=grid_spec,
        out_shape=jax.ShapeDtypeStruct(((nt + 1) * tm * ROW_SLAB, LANES), f32),
        compiler_params=pltpu.CompilerParams(
            dimension_semantics=("arbitrary", "arbitrary"),
            vmem_limit_bytes=VMEM_LIMIT),
        name="moe",
    )(nvalid, tile_g, dest, pads, wg, wu,
      wd.reshape(N_GROUPS, EXPERTS_PER_GROUP * D_EXPERT, D_MODEL), h2s)


def _combine_body(x1_ref, y_ref, o_ref):
    tm = x1_ref.shape[0]
    for c in range(ROW_SLAB):
        cols = slice(c * LANES, (c + 1) * LANES)
        o_ref[:, cols] = x1_ref[:, cols] + y_ref[pl.ds(c, tm, stride=ROW_SLAB), :]


def _combine(x1, ytok, tm=256):
    T = x1.shape[0]
    return pl.pallas_call(
        _combine_body,
        grid=(T // tm,),
        in_specs=[
            pl.BlockSpec((tm, D_MODEL), lambda i: (i, 0)),
            pl.BlockSpec((tm * ROW_SLAB, LANES), lambda i: (i, 0)),
        ],
        out_specs=pl.BlockSpec((tm, D_MODEL), lambda i: (i, 0)),
        out_shape=jax.ShapeDtypeStruct((T, D_MODEL), f32),
        compiler_params=pltpu.CompilerParams(
            dimension_semantics=("arbitrary",),
            vmem_limit_bytes=VMEM_LIMIT),
        name="combine",
    )(x1, ytok)


def _alibi_slopes(n):
    return jnp.exp2(-8.0 * jnp.arange(1, n + 1, dtype=f32) / n)


def kernel(x, g_mix, w_in, q_norm_swa, k_norm_swa, sinks, q_norm_moba, k_norm_moba,
           w_up_swa, w_up_moba, w_out, g_ffn, w_router_group, w_router_expert,
           w_gate_e, w_up_e, w_down_e):
    B, S, D = x.shape
    assert D == D_MODEL and S % MOBA_BLOCK == 0 and S % SWA_BLOCK == 0
    T = B * S
    x2 = x.reshape(T, D)

    proj = _inproj(x2, g_mix, w_in)

    oa = _swa(proj, sinks.astype(f32), _alibi_slopes(SWA_Q_HEADS), q_norm_swa, k_norm_swa, B, S)
    ob = _moba(proj, _alibi_slopes(MOBA_HEADS), q_norm_moba, k_norm_moba, B, S)

    wr = jnp.concatenate(
        [w_router_group,
         w_router_expert.transpose(1, 0, 2).reshape(D, N_EXPERTS),
         jnp.zeros((D, ROUTER_COLS - N_GROUPS - N_EXPERTS), f32)], axis=1)
    wr_hi = wr.astype(bf16)
    wr_lo = (wr - wr_hi.astype(f32)).astype(bf16)
    x1, h2s, gid = _mix(oa, ob, proj, x2, w_up_swa.astype(bf16), w_up_moba.astype(bf16),
                        w_out.astype(bf16), g_ffn, jnp.concatenate([wr_hi, wr_lo], axis=1))

    tables = _route_tables(gid, T, MOE_TILE)
    ytok = _moe(h2s, tables, w_gate_e, w_up_e, w_down_e.astype(bf16),
                T, MOE_TILE)
    y = _combine(x1, ytok)
    return y.reshape(B, S, D)
```

```python
import functools

import jax
import jax.numpy as jnp
from jax import lax
from jax.experimental import pallas as pl
from jax.experimental.pallas import tpu as pltpu

D_MODEL = 2048
HEAD_DIM = 64
ATTN_SCALE = HEAD_DIM ** -0.5
SWA_Q_HEADS = 16
SWA_KV_HEADS = 2
SWA_WINDOW = 128
SWA_BLOCK = 128
MOBA_HEADS = 16
MOBA_BLOCK = 256
MOBA_TOPK = 3
N_GROUPS = 4
EXPERTS_PER_GROUP = 4
N_EXPERTS = N_GROUPS * EXPERTS_PER_GROUP
D_EXPERT = 512
EPS = 1e-6

SWA_Q_DIM = SWA_Q_HEADS * HEAD_DIM
MOBA_DIM = MOBA_HEADS * HEAD_DIM

LANES = 128
ROW_SLAB = D_MODEL // LANES
SLAB_IN = ROW_SLAB + 8
MOE_TILE = 512
PROJ_CHUNK = 2 * LANES
VMEM_LIMIT = 56 * 1024 * 1024
NEG = -1e30
LOG2E = 1.4426950408889634

COL_QA = 0
COL_KA = COL_QA + SWA_Q_DIM // LANES
COL_VA = COL_KA + 1
COL_QB = COL_VA + 1
COL_KB = COL_QB + MOBA_DIM // LANES
COL_VB = COL_KB + MOBA_DIM // LANES
COL_GATE_A = COL_VB + MOBA_DIM // LANES
COL_GATE_B = COL_GATE_A + D_MODEL // LANES
GATE_BLOCK = 2 * LANES
assert (COL_GATE_A * LANES) % GATE_BLOCK == 0 and (COL_GATE_B * LANES) % GATE_BLOCK == 0

bf16 = jnp.bfloat16
f32 = jnp.float32


def _sigmoid(x):
    return 1.0 / (1.0 + jnp.exp(-x))


def _pair_rms(x, gain):
    lane = lax.broadcasted_iota(jnp.int32, x.shape, 1)
    lo = lane < HEAD_DIM
    sq = x * x
    s0 = jnp.sum(jnp.where(lo, sq, 0.0), axis=-1, keepdims=True)
    s1 = jnp.sum(jnp.where(lo, 0.0, sq), axis=-1, keepdims=True)
    r0 = lax.rsqrt(s0 * (1.0 / HEAD_DIM) + EPS)
    r1 = lax.rsqrt(s1 * (1.0 / HEAD_DIM) + EPS)
    return x * jnp.where(lo, r0, r1) * gain


def _split2(x):
    hi = x.astype(bf16)
    lo = (x - hi.astype(f32)).astype(bf16)
    return hi, lo


CAST_CHUNK = 2 * LANES
NORM_ROWS = 512


def _inproj_body(x_ref, g_ref, w_ref, o_ref, h_ref):
    @pl.when(pl.program_id(1) == 0)
    def _():
        for r in range(0, x_ref.shape[0], NORM_ROWS):
            x = x_ref[r:r + NORM_ROWS, :]
            ms = jnp.mean(x * x, axis=-1, keepdims=True)
            h_ref[r:r + NORM_ROWS, :] = (x * lax.rsqrt(ms + EPS) * g_ref[...]).astype(bf16)

    n_chunks = w_ref.shape[1] // CAST_CHUNK
    cast = lambda c: w_ref[:, c * CAST_CHUNK:(c + 1) * CAST_CHUNK].astype(bf16)
    w_next = cast(0)
    for c in range(n_chunks):
        w_cur = w_next
        if c + 1 < n_chunks:
            w_next = cast(c + 1)
        o_ref[:, c * CAST_CHUNK:(c + 1) * CAST_CHUNK] = jnp.dot(
            h_ref[...], w_cur, preferred_element_type=f32)


def _inproj(x2, g, w, tm=2048, tn=768):
    T = x2.shape[0]
    N = w.shape[1]
    return pl.pallas_call(
        _inproj_body,
        grid=(T // tm, N // tn),
        in_specs=[
            pl.BlockSpec((tm, D_MODEL), lambda i, j: (i, 0), pipeline_mode=pl.Buffered(1)),
            pl.BlockSpec((1, D_MODEL), lambda i, j: (0, 0)),
            pl.BlockSpec((D_MODEL, tn), lambda i, j: (0, j)),
        ],
        out_specs=pl.BlockSpec((tm, tn), lambda i, j: (i, j)),
        out_shape=jax.ShapeDtypeStruct((T, N), f32),
        scratch_shapes=[pltpu.VMEM((tm, D_MODEL), bf16)],
        compiler_params=pltpu.CompilerParams(
            dimension_semantics=("arbitrary", "arbitrary"),
            vmem_limit_bytes=VMEM_LIMIT),
        name="inproj",
    )(x2, g.reshape(1, D_MODEL), w)


SWA_AHEAD = 2


def _swa_body(sinks_ref, slopes_ref, q_ref, kp_ref, kc_ref, vp_ref, vc_ref,
              qn_ref, kn_ref, o_ref):
    L = SWA_BLOCK
    n = pl.program_id(1)
    k2 = jnp.concatenate([kp_ref[...], kc_ref[...]], axis=0)
    k2n = _pair_rms(k2, kn_ref[...]).astype(bf16)
    v2t = jnp.concatenate([vp_ref[...], vc_ref[...]], axis=0).T.astype(bf16)
    qn = jnp.concatenate(
        [_pair_rms(q_ref[:, pp * LANES:(pp + 1) * LANES], qn_ref[...])
         for pp in range(SWA_Q_HEADS // 2)], axis=1) * (ATTN_SCALE * LOG2E)
    qt = qn.T

    key = lax.broadcasted_iota(jnp.int32, (2 * L, L), 0)
    qry = lax.broadcasted_iota(jnp.int32, (2 * L, L), 1)
    dist = qry + L - key
    ok = (dist >= 0) & (dist < SWA_WINDOW) & ((n > 0) | (key >= L))
    distf = dist.astype(f32)
    zeros = jnp.zeros((HEAD_DIM, L), f32)

    heads_per_kv = SWA_Q_HEADS // SWA_KV_HEADS

    def scores(h):
        g = h // heads_per_kv
        qh = qt[h * HEAD_DIM:(h + 1) * HEAD_DIM, :]
        qa = jnp.concatenate([qh, zeros] if g == 0 else [zeros, qh], axis=0).astype(bf16)
        return jnp.dot(k2n, qa, preferred_element_type=f32)

    def softmax(h, s):
        s = jnp.where(ok, s - (slopes_ref[h] * LOG2E) * distf, -jnp.inf)
        sink = sinks_ref[h] * LOG2E
        m = jnp.maximum(jnp.max(s, axis=0, keepdims=True), sink)
        p = jnp.exp2(s - m)
        denom = jnp.sum(p, axis=0, keepdims=True) + jnp.exp2(sink - m)
        return p.astype(bf16), denom

    outs = []

    def finish(h, p, denom):
        g = h // heads_per_kv
        o = jnp.dot(v2t, p, preferred_element_type=f32)
        outs.append(o[g * HEAD_DIM:(g + 1) * HEAD_DIM, :] / denom)

    pending = [scores(h) for h in range(SWA_AHEAD)]
    deferred = None
    for h in range(SWA_Q_HEADS):
        s = pending.pop(0)
        if h + SWA_AHEAD < SWA_Q_HEADS:
            pending.append(scores(h + SWA_AHEAD))
        p, denom = softmax(h, s)
        if deferred is not None:
            finish(*deferred)
        deferred = (h, p, denom)
    finish(*deferred)
    o_ref[...] = jnp.concatenate(outs, axis=0).T.astype(bf16)


def _swa(proj, sinks, slopes, q_norm, k_norm, B, S):
    L = SWA_BLOCK
    nb = S // L
    T = B * S
    smem = pl.BlockSpec(memory_space=pltpu.SMEM)

    def prev(b, n):
        return b * nb + jnp.maximum(n - 1, 0)

    return pl.pallas_call(
        _swa_body,
        grid=(B, nb),
        in_specs=[
            smem, smem,
            pl.BlockSpec((L, SWA_Q_DIM), lambda b, n: (b * nb + n, COL_QA // (SWA_Q_DIM // LANES))),
            pl.BlockSpec((L, LANES), lambda b, n: (prev(b, n), COL_KA)),
            pl.BlockSpec((L, LANES), lambda b, n: (b * nb + n, COL_KA)),
            pl.BlockSpec((L, LANES), lambda b, n: (prev(b, n), COL_VA)),
            pl.BlockSpec((L, LANES), lambda b, n: (b * nb + n, COL_VA)),
            pl.BlockSpec((1, LANES), lambda b, n: (0, 0)),
            pl.BlockSpec((1, LANES), lambda b, n: (0, 0)),
        ],
        out_specs=pl.BlockSpec((L, SWA_Q_DIM), lambda b, n: (b * nb + n, 0)),
        out_shape=jax.ShapeDtypeStruct((T, SWA_Q_DIM), bf16),
        compiler_params=pltpu.CompilerParams(
            dimension_semantics=("arbitrary", "arbitrary"),
            vmem_limit_bytes=VMEM_LIMIT),
        name="swa",
    )(sinks, slopes, proj, proj, proj, proj, proj,
      jnp.tile(q_norm, 2).reshape(1, LANES), jnp.tile(k_norm, 2).reshape(1, LANES))


N_BIAS_PARTS = 3
NB_ROWS = 8
N_ALIBI_COL = NB_ROWS * N_BIAS_PARTS
VT_ROWS = HEAD_DIM + 16
SCORE_AHEAD = 3


def _split3(x):
    p0 = x.astype(bf16).astype(f32)
    r1 = x - p0
    p1 = r1.astype(bf16).astype(f32)
    p2 = (r1 - p1).astype(bf16).astype(f32)
    return p0, p1, p2


def _moba_body(slopes_ref, q_ref, k_ref, v_ref, qn_ref, kn_ref, *rest, nb, n_cast):
    cast_in, (o_ref, *cast_out) = rest[:n_cast], rest[n_cast:2 * n_cast + 1]
    kaug_ref, vt_ref = rest[2 * n_cast + 1:]
    for src, dst in zip(cast_in, cast_out):
        dst[...] = src[...].astype(bf16)
    L = MOBA_BLOCK
    S = nb * L
    p = pl.program_id(1)

    kn = _pair_rms(k_ref[...], kn_ref[...])
    lane = lax.broadcasted_iota(jnp.int32, (L, LANES), 1)
    kpos = lax.broadcasted_iota(jnp.int32, (L, LANES), 0).astype(f32)
    lane_lo = lane < HEAD_DIM

    def alibi_cols(cc, slope):
        a0, a1, a2 = _split3((slope * LOG2E) * kpos)
        return jnp.where(cc == N_ALIBI_COL, a0,
                         jnp.where(cc == N_ALIBI_COL + 1, a1,
                                   jnp.where(cc == N_ALIBI_COL + 2, a2, 0.0)))

    cols = (lane - HEAD_DIM, lane)
    alibi = [alibi_cols(cols[hh], slopes_ref[2 * p + hh]) for hh in range(2)]
    for n in range(nb):
        kblock = kn[n * L:(n + 1) * L, :]
        for hh in range(2):
            cc = cols[hh]
            hot = (cc >= 0) & (cc < N_ALIBI_COL) & ((cc % NB_ROWS) == n)
            aug = jnp.where(hot, 1.0, alibi[hh])
            own = lane_lo if hh == 0 else jnp.logical_not(lane_lo)
            kaug_ref[hh, n * L:(n + 1) * L, :] = jnp.where(own, kblock, aug).astype(bf16)
    vt = v_ref[...].T
    vrow = lax.broadcasted_iota(jnp.int32, (VT_ROWS - HEAD_DIM, S), 0)
    ones_then_zeros = (vrow == 0).astype(f32)
    for hh in range(2):
        vt_ref[hh] = jnp.concatenate(
            [vt[hh * HEAD_DIM:(hh + 1) * HEAD_DIM, :], ones_then_zeros], axis=0).astype(bf16)
    means = [jnp.mean(kn[n * L:(n + 1) * L, :], axis=0, keepdims=True) for n in range(nb)]
    means += [jnp.zeros((1, LANES), f32)] * (NB_ROWS - nb)
    km = jnp.concatenate(means, axis=0)
    lane8 = lax.broadcasted_iota(jnp.int32, (NB_ROWS, LANES), 1)
    km_pair = jnp.concatenate(
        [jnp.where(lane8 < HEAD_DIM, km, 0.0), jnp.where(lane8 < HEAD_DIM, 0.0, km)], axis=0)
    km_hi, km_lo = _split2(km_pair)

    qs = _pair_rms(q_ref[...], qn_ref[...]) * (ATTN_SCALE * LOG2E)
    qt = qs.T
    qt_hi, qt_lo = _split2(qt)
    gate = (jnp.dot(km_hi, qt_hi, preferred_element_type=f32)
            + (jnp.dot(km_hi, qt_lo, preferred_element_type=f32)
               + jnp.dot(km_lo, qt_hi, preferred_element_type=f32)))

    blk = lax.broadcasted_iota(jnp.int32, (NB_ROWS, S), 0)
    qblk = lax.broadcasted_iota(jnp.int32, (NB_ROWS, S), 1) // L
    past = blk < qblk
    r = lax.broadcasted_iota(jnp.int32, (L, L), 1)
    c = lax.broadcasted_iota(jnp.int32, (L, L), 0)
    causal = jnp.where(r >= c, 0.0, NEG)
    ones_rows = (blk < 3).astype(f32)

    qa = []
    for hh in range(2):
        g = jnp.where(past, gate[hh * NB_ROWS:(hh + 1) * NB_ROWS, :], -jnp.inf)
        rank = jnp.zeros((NB_ROWS, S), jnp.int32)
        for m in range(nb):
            gm = g[m:m + 1, :]
            ahead = (gm > g) | ((gm == g) & (m < blk))
            rank = rank + ahead.astype(jnp.int32)
        sel = past & (rank < MOBA_TOPK)
        slope = slopes_ref[2 * p + hh]
        bias = jnp.where(sel, (-slope * L * LOG2E) * (qblk - blk).astype(f32), NEG)
        bias = jnp.where(blk == qblk, 0.0, bias)
        b0, b1, b2 = _split3(bias)
        extra = jnp.concatenate(
            [b0, b1, b2, ones_rows,
             jnp.zeros((HEAD_DIM - N_ALIBI_COL - NB_ROWS, S), f32)], axis=0)
        if hh == 0:
            qa.append(jnp.concatenate([qt[:HEAD_DIM], extra], axis=0).astype(bf16))
        else:
            qa.append(jnp.concatenate([extra, qt[HEAD_DIM:]], axis=0).astype(bf16))

    def scores(i, hh):
        return jnp.dot(kaug_ref[hh, 0:(i + 1) * L, :], qa[hh][:, i * L:(i + 1) * L],
                       preferred_element_type=f32)

    units = [(i, hh) for i in range(nb) for hh in range(2)]
    pending = [scores(*u) for u in units[:SCORE_AHEAD]]
    outs = {}

    def finish(i, hh, e_all):
        acc = jnp.dot(vt_ref[hh, :, 0:(i + 1) * L], e_all, preferred_element_type=f32)
        outs[(i, hh)] = acc[:HEAD_DIM, :] / acc[HEAD_DIM:HEAD_DIM + 1, :]
        if hh == 1:
            o_ref[i * L:(i + 1) * L, :] = jnp.concatenate(
                [outs.pop((i, 0)), outs.pop((i, 1))], axis=0).T.astype(bf16)

    deferred = None
    for idx, (i, hh) in enumerate(units):
        s = pending.pop(0)
        if idx + SCORE_AHEAD < len(units):
            pending.append(scores(*units[idx + SCORE_AHEAD]))
        tiles = [s[n * L:(n + 1) * L, :] for n in range(i)] + [s[i * L:(i + 1) * L, :] + causal]
        m = functools.reduce(jnp.maximum, [jnp.max(t, axis=0, keepdims=True) for t in tiles])
        e_all = jnp.concatenate([jnp.exp2(t - m).astype(bf16) for t in tiles], axis=0)
        if deferred is not None:
            finish(*deferred)
        deferred = (i, hh, e_all)
    finish(*deferred)


def _moba(proj, slopes, q_norm, k_norm, B, S, to_cast):
    L = MOBA_BLOCK
    nb = S // L
    assert nb <= NB_ROWS
    T = B * S
    n_pairs = MOBA_HEADS // 2
    n_steps = B * n_pairs
    cast_specs = [pl.BlockSpec((w.shape[0] // n_steps, w.shape[1]), lambda b, p: (b * n_pairs + p, 0))
                  for w in to_cast]
    return pl.pallas_call(
        functools.partial(_moba_body, nb=nb, n_cast=len(to_cast)),
        grid=(B, n_pairs),
        in_specs=[
            pl.BlockSpec(memory_space=pltpu.SMEM),
            pl.BlockSpec((S, LANES), lambda b, p: (b, COL_QB + p)),
            pl.BlockSpec((S, LANES), lambda b, p: (b, COL_KB + p)),
            pl.BlockSpec((S, LANES), lambda b, p: (b, COL_VB + p)),
            pl.BlockSpec((1, LANES), lambda b, p: (0, 0)),
            pl.BlockSpec((1, LANES), lambda b, p: (0, 0)),
        ] + cast_specs,
        out_specs=[pl.BlockSpec((S, LANES), lambda b, p: (b, p))] + cast_specs,
        out_shape=[jax.ShapeDtypeStruct((T, MOBA_DIM), bf16)]
        + [jax.ShapeDtypeStruct(w.shape, bf16) for w in to_cast],
        scratch_shapes=[
            pltpu.VMEM((2, S, LANES), bf16),
            pltpu.VMEM((2, VT_ROWS, S), bf16),
        ],
        compiler_params=pltpu.CompilerParams(
            dimension_semantics=("arbitrary", "arbitrary"),
            vmem_limit_bytes=VMEM_LIMIT),
        name="moba",
    )(slopes, proj, proj, proj,
      jnp.tile(q_norm, 2).reshape(1, LANES), jnp.tile(k_norm, 2).reshape(1, LANES), *to_cast)


ROUTER_COLS = LANES


N_GATE_BLOCKS = D_MODEL // GATE_BLOCK
MIX_CHUNK = 2 * GATE_BLOCK


def _mix_body(*refs):
    ga_refs = refs[:N_GATE_BLOCKS]
    gb_refs = refs[N_GATE_BLOCKS:2 * N_GATE_BLOCKS]
    (oa_ref, ob_ref, x_ref, wua_ref, wub_ref, wout_ref, gffn_ref, wr_ref, wd_ref,
     x1_ref, h2_ref, gid_ref, wd_out_ref) = refs[2 * N_GATE_BLOCKS:]
    wd_out_ref[...] = wd_ref[...].astype(bf16)
    oa = oa_ref[...]
    ob = ob_ref[...]
    gpc = MIX_CHUNK // GATE_BLOCK

    def up(c):
        cols = slice(c * MIX_CHUNK, (c + 1) * MIX_CHUNK)
        return (jnp.dot(oa, wua_ref[:, cols], preferred_element_type=f32),
                jnp.dot(ob, wub_ref[:, cols], preferred_element_type=f32))

    x1 = x_ref[...]
    pending = up(0)
    for c in range(D_MODEL // MIX_CHUNK):
        ya, yb = pending
        if (c + 1) * MIX_CHUNK < D_MODEL:
            pending = up(c + 1)
        gate_a = jnp.concatenate([r[...] for r in ga_refs[c * gpc:(c + 1) * gpc]], axis=1)
        gate_b = jnp.concatenate([r[...] for r in gb_refs[c * gpc:(c + 1) * gpc]], axis=1)
        merged = (_sigmoid(gate_a) * ya + _sigmoid(gate_b) * yb).astype(bf16)
        x1 = x1 + jnp.dot(merged, wout_ref[c * MIX_CHUNK:(c + 1) * MIX_CHUNK, :],
                          preferred_element_type=f32)
    x1_ref[...] = x1
    ms = jnp.mean(x1 * x1, axis=-1, keepdims=True)
    h2 = x1 * lax.rsqrt(ms + EPS) * gffn_ref[...]
    tm = h2.shape[0]
    for c in range(ROW_SLAB):
        h2_ref[pl.ds(c, tm, stride=SLAB_IN), :] = h2[:, c * LANES:(c + 1) * LANES]

    h_hi, h_lo = _split2(h2)
    hi_all = jnp.dot(h_hi, wr_ref[...], preferred_element_type=f32)
    lo_hi = jnp.dot(h_lo, wr_ref[:, :ROUTER_COLS], preferred_element_type=f32)
    lt = (hi_all[:, :ROUTER_COLS] + (lo_hi + hi_all[:, ROUTER_COLS:])).T
    gl = [lt[g:g + 1, :] for g in range(N_GROUPS)]
    gmax = functools.reduce(jnp.maximum, gl)
    gsum = functools.reduce(lambda a, b: a + b, [jnp.exp(v - gmax) for v in gl])
    g_p = 1.0 / gsum
    g_i = jnp.full((1, tm), N_GROUPS - 1, jnp.int32)
    for g in reversed(range(N_GROUPS)):
        g_i = jnp.where(gl[g] == gmax, g, g_i)

    el = []
    for e in range(EXPERTS_PER_GROUP):
        v = jnp.zeros((1, tm), f32)
        for g in range(N_GROUPS):
            r = N_GROUPS + g * EXPERTS_PER_GROUP + e
            v = jnp.where(g_i == g, lt[r:r + 1, :], v)
        el.append(v)
    emax = functools.reduce(jnp.maximum, el)
    ex = [jnp.exp(v - emax) for v in el]
    esum = functools.reduce(lambda a, b: a + b, ex)
    ep = [v / esum for v in ex]
    p1 = functools.reduce(jnp.maximum, ep)
    i1 = jnp.full((1, tm), EXPERTS_PER_GROUP - 1, jnp.int32)
    for e in reversed(range(EXPERTS_PER_GROUP)):
        i1 = jnp.where(ep[e] == p1, e, i1)
    rest = [jnp.where(i1 == e, -1.0, ep[e]) for e in range(EXPERTS_PER_GROUP)]
    p2 = functools.reduce(jnp.maximum, rest)
    i2 = jnp.full((1, tm), EXPERTS_PER_GROUP - 1, jnp.int32)
    for e in reversed(range(EXPERTS_PER_GROUP)):
        i2 = jnp.where(rest[e] == p2, e, i2)
    w1 = g_p * (p1 / (p1 + p2))
    w2 = g_p * (p2 / (p1 + p2))
    gid_ref[...] = jnp.concatenate([g_i, jnp.zeros((7, tm), jnp.int32)], axis=0)
    cw = [jnp.where(i1 == e, w1, 0.0) + jnp.where(i2 == e, w2, 0.0) for e in range(EXPERTS_PER_GROUP)]
    cw_t = jnp.concatenate(cw + [jnp.zeros((LANES - EXPERTS_PER_GROUP, tm), f32)], axis=0).T
    for e in range(SLAB_IN - ROW_SLAB):
        if e < EXPERTS_PER_GROUP:
            row = jnp.broadcast_to(cw_t[:, e:e + 1], (tm, LANES))
        else:
            row = jnp.zeros((tm, LANES), f32)
        h2_ref[pl.ds(ROW_SLAB + e, tm, stride=SLAB_IN), :] = row


def _mix(oa, ob, proj, x2, wua, wub, wout, g_ffn, wr, wd, tm=256):
    T = x2.shape[0]
    wd_spec = pl.BlockSpec((wd.shape[0] // (T // tm), wd.shape[1]), lambda i: (i, 0))
    const = lambda i: (0, 0)
    single = pl.Buffered(1)

    def gate_specs(first_col):
        first = first_col * LANES // GATE_BLOCK
        return [pl.BlockSpec((tm, GATE_BLOCK), functools.partial(lambda i, j: (i, j), j=first + k))
                for k in range(N_GATE_BLOCKS)]

    return pl.pallas_call(
        _mix_body,
        grid=(T // tm,),
        in_specs=gate_specs(COL_GATE_A) + gate_specs(COL_GATE_B) + [
            pl.BlockSpec((tm, SWA_Q_DIM), lambda i: (i, 0)),
            pl.BlockSpec((tm, MOBA_DIM), lambda i: (i, 0)),
            pl.BlockSpec((tm, D_MODEL), lambda i: (i, 0)),
            pl.BlockSpec((SWA_Q_DIM, D_MODEL), const, pipeline_mode=single),
            pl.BlockSpec((MOBA_DIM, D_MODEL), const, pipeline_mode=single),
            pl.BlockSpec((D_MODEL, D_MODEL), const, pipeline_mode=single),
            pl.BlockSpec((1, D_MODEL), const),
            pl.BlockSpec((D_MODEL, 2 * ROUTER_COLS), const),
            wd_spec,
        ],
        out_specs=[
            pl.BlockSpec((tm, D_MODEL), lambda i: (i, 0)),
            pl.BlockSpec((tm * SLAB_IN, LANES), lambda i: (i, 0)),
            pl.BlockSpec((8, tm), lambda i: (0, i)),
            wd_spec,
        ],
        out_shape=[
            jax.ShapeDtypeStruct((T, D_MODEL), f32),
            jax.ShapeDtypeStruct((T * SLAB_IN, LANES), f32),
            jax.ShapeDtypeStruct((8, T), jnp.int32),
            jax.ShapeDtypeStruct(wd.shape, bf16),
        ],
        compiler_params=pltpu.CompilerParams(
            dimension_semantics=("arbitrary",),
            vmem_limit_bytes=VMEM_LIMIT),
        name="mix",
    )(*([proj] * (2 * N_GATE_BLOCKS)), oa, ob, x2, wua, wub, wout, g_ffn.reshape(1, D_MODEL), wr, wd)


def _route_tables(gid, T, tm):
    nt = T // tm + N_GROUPS
    g = gid[0]
    onehot = (g[:, None] == jnp.arange(N_GROUPS, dtype=jnp.int32)[None, :]).astype(jnp.int32)
    incl = jnp.cumsum(onehot, axis=0)
    pos = jnp.sum(onehot * incl, axis=1) - 1
    counts = incl[-1]
    padded = (counts + tm - 1) // tm * tm
    seg_end = jnp.cumsum(padded)
    seg_start = seg_end - padded
    dest = jnp.sum(onehot * seg_start[None, :], axis=1) + pos
    tile_start = jnp.arange(nt, dtype=jnp.int32) * tm
    tile_g = jnp.sum((tile_start[:, None] >= seg_end[None, :]).astype(jnp.int32), axis=1)
    tile_g = jnp.minimum(tile_g, N_GROUPS - 1)
    nvalid = (seg_end[-1] // tm).astype(jnp.int32).reshape(1)
    n_pad = padded - counts
    pads = jnp.concatenate([seg_start + counts, seg_end, jnp.cumsum(n_pad) - n_pad])
    return (nvalid, tile_g.astype(jnp.int32), dest.astype(jnp.int32), pads.astype(jnp.int32))


def _moe_body(nvalid_ref, tg_ref, dest_ref, pads_ref, wg_ref, wu_ref, wd_ref, h2s_hbm, ytok_hbm,
              asg, gbuf, ybuf, xbuf, hbuf, gsem, ssem, *, tm, nt, n_tok):
    t = pl.program_id(0)
    e = pl.program_id(1)
    last_e = EXPERTS_PER_GROUP - 1
    nvalid = nvalid_ref[0]
    slot = lax.rem(t, 2)
    other = 1 - slot

    def for_rows(fn):
        def body(r, carry):
            fn(r)
            return carry
        lax.fori_loop(0, tm, body, 0, unroll=8)

    def gather_copy(tok, s, r):
        return pltpu.make_async_copy(
            h2s_hbm.at[pl.ds(pl.multiple_of(tok * SLAB_IN, 8), SLAB_IN), :],
            gbuf.at[pl.ds(pl.multiple_of((s * tm + r) * SLAB_IN, 8), SLAB_IN), :], gsem.at[s])

    def scatter_copy(row, s, r):
        return pltpu.make_async_copy(
            ybuf.at[pl.ds(pl.multiple_of((s * tm + r) * ROW_SLAB, 8), ROW_SLAB), :],
            ytok_hbm.at[pl.ds(pl.multiple_of(row * ROW_SLAB, 8), ROW_SLAB), :], ssem.at[s])

    def start_gathers(tile, s):
        for_rows(lambda r: gather_copy(jnp.maximum(asg[tile * tm + r], 0), s, r).start())

    def start_scatters(tile, s):
        def one(r):
            a = asg[tile * tm + r]
            scatter_copy(jnp.where(a >= 0, a, n_tok - 1 - a), s, r).start()
        for_rows(one)

    def wait_gathers(s):
        for_rows(lambda r: gather_copy(0, s, r).wait())

    def wait_scatters(s):
        for_rows(lambda r: scatter_copy(0, s, r).wait())

    @pl.when((t == 0) & (e == 0))
    def _():
        def place(tok, carry):
            asg[dest_ref[tok]] = tok
            return carry
        lax.fori_loop(0, n_tok, place, 0, unroll=8)

        for g in range(N_GROUPS):
            first = pads_ref[g]
            code = -1 - pads_ref[2 * N_GROUPS + g] + first

            def mark(p, carry):
                asg[p] = code - p
                return carry
            lax.fori_loop(first, pads_ref[N_GROUPS + g], mark, 0)
        ybuf[...] = jnp.zeros(ybuf.shape, f32)
        start_gathers(0, 0)

    @pl.when((e == 0) & (t < nvalid))
    def _():
        wait_gathers(slot)
        base = slot * tm * SLAB_IN
        xbuf[...] = jnp.concatenate(
            [gbuf[pl.ds(base + c, tm, stride=SLAB_IN), :] for c in range(ROW_SLAB)],
            axis=1).astype(bf16)

    @pl.when(t < nvalid)
    def _():
        quarter = tm // EXPERTS_PER_GROUP
        have_next = t + 1 < nvalid
        have_prev = t >= 1
        nxt = jnp.minimum(t + 1, nvalid - 1)

        def issue_rows(r0, r1):
            for r in range(r0, r1):
                row = e * quarter + r
                tok = jnp.where(have_next, jnp.maximum(asg[nxt * tm + row], 0), 0)
                gather_copy(tok, other, row).start()
                a = asg[jnp.maximum(t - 1, 0) * tm + row]
                dst = jnp.where(have_prev, jnp.where(a >= 0, a, n_tok - 1 - a), nt * tm + row)
                scatter_copy(dst, other, row).start(priority=r % 2)

        x = xbuf[...]
        n_chunks = D_EXPERT // PROJ_CHUNK
        per_piece = quarter // (2 * n_chunks)
        pieces = {"g": [], "u": []}
        for i, (name, w_ref) in enumerate((n, w) for c in range(n_chunks)
                                          for n, w in (("g", wg_ref), ("u", wu_ref))):
            c = i // 2
            w = w_ref[:, c * PROJ_CHUNK:(c + 1) * PROJ_CHUNK].astype(bf16)
            pieces[name].append(jnp.dot(x, w, preferred_element_type=f32))
            issue_rows(i * per_piece, (i + 1) * per_piece)
        gte = jnp.concatenate(pieces["g"], axis=1)
        up = jnp.concatenate(pieces["u"], axis=1)
        cw = gbuf[pl.ds(slot * tm * SLAB_IN + ROW_SLAB + e, tm, stride=SLAB_IN), :]
        hid = (gte * _sigmoid(gte)) * up * jnp.concatenate([cw] * (D_EXPERT // LANES), axis=1)
        hbuf[e] = hid.astype(bf16)

    @pl.when((e == last_e) & (t < nvalid))
    def _():
        hcat = jnp.concatenate([hbuf[k] for k in range(EXPERTS_PER_GROUP)], axis=1)
        y = jnp.dot(hcat, wd_ref[...], preferred_element_type=f32)

        @pl.when(t >= 1)
        def _():
            wait_scatters(slot)
        base = slot * tm * ROW_SLAB
        for c in range(ROW_SLAB):
            ybuf[pl.ds(base + c, tm, stride=ROW_SLAB), :] = y[:, c * LANES:(c + 1) * LANES]

        @pl.when(t == nvalid - 1)
        def _():
            start_scatters(t, slot)
            wait_gathers(other)
            wait_scatters(other)
            wait_scatters(slot)

    @pl.when((e == last_e) & (t == nt - 1))
    def _():
        ybuf[0:tm * ROW_SLAB, :] = jnp.zeros((tm * ROW_SLAB, LANES), f32)
        for j in range(n_tok // tm, nt):
            @pl.when(j >= nvalid)
            def _():
                fill = pltpu.make_async_copy(
                    ybuf.at[0:tm * ROW_SLAB, :],
                    ytok_hbm.at[j * tm * ROW_SLAB:(j + 1) * tm * ROW_SLAB, :], ssem.at[0])
                fill.start()
                fill.wait()


def _moe(h2s, tables, wg, wu, wd, T, tm):
    nvalid, tile_g, dest, pads = tables
    nt = tile_g.shape[0]

    def w_index(t, e, nv, tg, de, pd):
        return (tg[t] * EXPERTS_PER_GROUP + jnp.where(t < nv[0], e, EXPERTS_PER_GROUP - 1), 0, 0)

    grid_spec = pltpu.PrefetchScalarGridSpec(
        num_scalar_prefetch=4,
        grid=(nt, EXPERTS_PER_GROUP),
        in_specs=[
            pl.BlockSpec((None, D_MODEL, D_EXPERT), w_index),
            pl.BlockSpec((None, D_MODEL, D_EXPERT), w_index),
            pl.BlockSpec((None, EXPERTS_PER_GROUP * D_EXPERT, D_MODEL),
                         lambda t, e, nv, tg, de, pd: (tg[t], 0, 0), pipeline_mode=pl.Buffered(1)),
            pl.BlockSpec(memory_space=pl.ANY),
        ],
        out_specs=pl.BlockSpec(memory_space=pl.ANY),
        scratch_shapes=[
            pltpu.SMEM((nt * tm,), jnp.int32),
            pltpu.VMEM((2 * tm * SLAB_IN, LANES), f32),
            pltpu.VMEM((2 * tm * ROW_SLAB, LANES), f32),
            pltpu.VMEM((tm, D_MODEL), bf16),
            pltpu.VMEM((EXPERTS_PER_GROUP, tm, D_EXPERT), bf16),
            pltpu.SemaphoreType.DMA((2,)),
            pltpu.SemaphoreType.DMA((2,)),
        ],
    )
    return pl.pallas_call(
        functools.partial(_moe_body, tm=tm, nt=nt, n_tok=T),
        grid_spec=grid_spec,
        out_shape=jax.ShapeDtypeStruct(((nt + 1) * tm * ROW_SLAB, LANES), f32),
        compiler_params=pltpu.CompilerParams(
            dimension_semantics=("arbitrary", "arbitrary"),
            vmem_limit_bytes=VMEM_LIMIT),
        name="moe",
    )(nvalid, tile_g, dest, pads, wg, wu,
      wd.reshape(N_GROUPS, EXPERTS_PER_GROUP * D_EXPERT, D_MODEL), h2s)


def _combine_body(x1_ref, y_ref, o_ref):
    tm = x1_ref.shape[0]
    for c in range(ROW_SLAB):
        cols = slice(c * LANES, (c + 1) * LANES)
        o_ref[:, cols] = x1_ref[:, cols] + y_ref[pl.ds(c, tm, stride=ROW_SLAB), :]


def _combine(x1, ytok, tm=256):
    T = x1.shape[0]
    return pl.pallas_call(
        _combine_body,
        grid=(T // tm,),
        in_specs=[
            pl.BlockSpec((tm, D_MODEL), lambda i: (i, 0)),
            pl.BlockSpec((tm * ROW_SLAB, LANES), lambda i: (i, 0)),
        ],
        out_specs=pl.BlockSpec((tm, D_MODEL), lambda i: (i, 0)),
        out_shape=jax.ShapeDtypeStruct((T, D_MODEL), f32),
        compiler_params=pltpu.CompilerParams(
            dimension_semantics=("arbitrary",),
            vmem_limit_bytes=VMEM_LIMIT),
        name="combine",
    )(x1, ytok)


def _alibi_slopes(n):
    return jnp.exp2(-8.0 * jnp.arange(1, n + 1, dtype=f32) / n)


def kernel(x, g_mix, w_in, q_norm_swa, k_norm_swa, sinks, q_norm_moba, k_norm_moba,
           w_up_swa, w_up_moba, w_out, g_ffn, w_router_group, w_router_expert,
           w_gate_e, w_up_e, w_down_e):
    B, S, D = x.shape
    assert D == D_MODEL and S % MOBA_BLOCK == 0 and S % SWA_BLOCK == 0
    T = B * S
    x2 = x.reshape(T, D)

    proj = _inproj(x2, g_mix, w_in)

    oa = _swa(proj, sinks.astype(f32), _alibi_slopes(SWA_Q_HEADS), q_norm_swa, k_norm_swa, B, S)
    ob, wua, wub, wout = _moba(proj, _alibi_slopes(MOBA_HEADS), q_norm_moba, k_norm_moba, B, S,
                               (w_up_swa, w_up_moba, w_out))

    wr = jnp.concatenate(
        [w_router_group,
         w_router_expert.transpose(1, 0, 2).reshape(D, N_EXPERTS),
         jnp.zeros((D, ROUTER_COLS - N_GROUPS - N_EXPERTS), f32)], axis=1)
    wr_hi = wr.astype(bf16)
    wr_lo = (wr - wr_hi.astype(f32)).astype(bf16)
    x1, h2s, gid, wd = _mix(oa, ob, proj, x2, wua, wub, wout, g_ffn,
                            jnp.concatenate([wr_hi, wr_lo], axis=1),
                            w_down_e.reshape(N_EXPERTS * D_EXPERT, D_MODEL))

    tables = _route_tables(gid, T, MOE_TILE)
    ytok = _moe(h2s, tables, w_gate_e, w_up_e, wd, T, MOE_TILE)
    y = _combine(x1, ytok)
    return y.reshape(B, S, D)
```

```python
import functools

import jax
import jax.numpy as jnp
from jax import lax
from jax.experimental import pallas as pl
from jax.experimental.pallas import tpu as pltpu

D_MODEL = 2048
HEAD_DIM = 64
ATTN_SCALE = HEAD_DIM ** -0.5
SWA_Q_HEADS = 16
SWA_KV_HEADS = 2
SWA_WINDOW = 128
SWA_BLOCK = 128
MOBA_HEADS = 16
MOBA_BLOCK = 256
MOBA_TOPK = 3
N_GROUPS = 4
EXPERTS_PER_GROUP = 4
N_EXPERTS = N_GROUPS * EXPERTS_PER_GROUP
D_EXPERT = 512
EPS = 1e-6

SWA_Q_DIM = SWA_Q_HEADS * HEAD_DIM
MOBA_DIM = MOBA_HEADS * HEAD_DIM

LANES = 128
ROW_SLAB = D_MODEL // LANES
SLAB_IN = ROW_SLAB + 8
MOE_TILE = 512
PROJ_CHUNK = 2 * LANES
VMEM_LIMIT = 56 * 1024 * 1024
NEG = -1e30
LOG2E = 1.4426950408889634

COL_QA = 0
COL_KA = COL_QA + SWA_Q_DIM // LANES
COL_VA = COL_KA + 1
COL_QB = COL_VA + 1
COL_KB = COL_QB + MOBA_DIM // LANES
COL_VB = COL_KB + MOBA_DIM // LANES
COL_GATE_A = COL_VB + MOBA_DIM // LANES
COL_GATE_B = COL_GATE_A + D_MODEL // LANES
GATE_BLOCK = 2 * LANES
assert (COL_GATE_A * LANES) % GATE_BLOCK == 0 and (COL_GATE_B * LANES) % GATE_BLOCK == 0

bf16 = jnp.bfloat16
f32 = jnp.float32


def _sigmoid(x):
    return 1.0 / (1.0 + jnp.exp(-x))


def _pair_rms(x, gain):
    lane = lax.broadcasted_iota(jnp.int32, x.shape, 1)
    lo = lane < HEAD_DIM
    sq = x * x
    s0 = jnp.sum(jnp.where(lo, sq, 0.0), axis=-1, keepdims=True)
    s1 = jnp.sum(jnp.where(lo, 0.0, sq), axis=-1, keepdims=True)
    r0 = lax.rsqrt(s0 * (1.0 / HEAD_DIM) + EPS)
    r1 = lax.rsqrt(s1 * (1.0 / HEAD_DIM) + EPS)
    return x * jnp.where(lo, r0, r1) * gain


def _split2(x):
    hi = x.astype(bf16)
    lo = (x - hi.astype(f32)).astype(bf16)
    return hi, lo


CAST_CHUNK = 2 * LANES
NORM_ROWS = 512


def _inproj_body(x_ref, g_ref, w_ref, o_ref, h_ref):
    @pl.when(pl.program_id(1) == 0)
    def _():
        for r in range(0, x_ref.shape[0], NORM_ROWS):
            x = x_ref[r:r + NORM_ROWS, :]
            ms = jnp.mean(x * x, axis=-1, keepdims=True)
            h_ref[r:r + NORM_ROWS, :] = (x * lax.rsqrt(ms + EPS) * g_ref[...]).astype(bf16)

    n_chunks = w_ref.shape[1] // CAST_CHUNK
    cast = lambda c: w_ref[:, c * CAST_CHUNK:(c + 1) * CAST_CHUNK].astype(bf16)
    w_next = cast(0)
    for c in range(n_chunks):
        w_cur = w_next
        if c + 1 < n_chunks:
            w_next = cast(c + 1)
        o_ref[:, c * CAST_CHUNK:(c + 1) * CAST_CHUNK] = jnp.dot(
            h_ref[...], w_cur, preferred_element_type=f32)


def _inproj(x2, g, w, tm=2048, tn=768):
    T = x2.shape[0]
    N = w.shape[1]
    return pl.pallas_call(
        _inproj_body,
        grid=(T // tm, N // tn),
        in_specs=[
            pl.BlockSpec((tm, D_MODEL), lambda i, j: (i, 0), pipeline_mode=pl.Buffered(1)),
            pl.BlockSpec((1, D_MODEL), lambda i, j: (0, 0)),
            pl.BlockSpec((D_MODEL, tn), lambda i, j: (0, j)),
        ],
        out_specs=pl.BlockSpec((tm, tn), lambda i, j: (i, j)),
        out_shape=jax.ShapeDtypeStruct((T, N), f32),
        scratch_shapes=[pltpu.VMEM((tm, D_MODEL), bf16)],
        compiler_params=pltpu.CompilerParams(
            dimension_semantics=("arbitrary", "arbitrary"),
            vmem_limit_bytes=VMEM_LIMIT),
        name="inproj",
    )(x2, g.reshape(1, D_MODEL), w)


SWA_AHEAD = 2


def _swa_body(sinks_ref, slopes_ref, q_ref, kp_ref, kc_ref, vp_ref, vc_ref,
              qn_ref, kn_ref, *rest, n_cast):
    cast_in, (o_ref, *cast_out) = rest[:n_cast], rest[n_cast:]
    for src, dst in zip(cast_in, cast_out):
        dst[...] = src[...].astype(bf16)
    L = SWA_BLOCK
    n = pl.program_id(1)
    k2 = jnp.concatenate([kp_ref[...], kc_ref[...]], axis=0)
    k2n = _pair_rms(k2, kn_ref[...]).astype(bf16)
    v2t = jnp.concatenate([vp_ref[...], vc_ref[...]], axis=0).T.astype(bf16)
    qn = jnp.concatenate(
        [_pair_rms(q_ref[:, pp * LANES:(pp + 1) * LANES], qn_ref[...])
         for pp in range(SWA_Q_HEADS // 2)], axis=1) * (ATTN_SCALE * LOG2E)
    qt = qn.T

    key = lax.broadcasted_iota(jnp.int32, (2 * L, L), 0)
    qry = lax.broadcasted_iota(jnp.int32, (2 * L, L), 1)
    dist = qry + L - key
    ok = (dist >= 0) & (dist < SWA_WINDOW) & ((n > 0) | (key >= L))
    distf = dist.astype(f32)
    zeros = jnp.zeros((HEAD_DIM, L), f32)

    heads_per_kv = SWA_Q_HEADS // SWA_KV_HEADS

    def scores(h):
        g = h // heads_per_kv
        qh = qt[h * HEAD_DIM:(h + 1) * HEAD_DIM, :]
        qa = jnp.concatenate([qh, zeros] if g == 0 else [zeros, qh], axis=0).astype(bf16)
        return jnp.dot(k2n, qa, preferred_element_type=f32)

    def softmax(h, s):
        s = jnp.where(ok, s - (slopes_ref[h] * LOG2E) * distf, -jnp.inf)
        sink = sinks_ref[h] * LOG2E
        m = jnp.maximum(jnp.max(s, axis=0, keepdims=True), sink)
        p = jnp.exp2(s - m)
        denom = jnp.sum(p, axis=0, keepdims=True) + jnp.exp2(sink - m)
        return p.astype(bf16), denom

    outs = []

    def finish(h, p, denom):
        g = h // heads_per_kv
        o = jnp.dot(v2t, p, preferred_element_type=f32)
        outs.append(o[g * HEAD_DIM:(g + 1) * HEAD_DIM, :] / denom)

    pending = [scores(h) for h in range(SWA_AHEAD)]
    deferred = None
    for h in range(SWA_Q_HEADS):
        s = pending.pop(0)
        if h + SWA_AHEAD < SWA_Q_HEADS:
            pending.append(scores(h + SWA_AHEAD))
        p, denom = softmax(h, s)
        if deferred is not None:
            finish(*deferred)
        deferred = (h, p, denom)
    finish(*deferred)
    o_ref[...] = jnp.concatenate(outs, axis=0).T.astype(bf16)


def _swa(proj, sinks, slopes, q_norm, k_norm, B, S, to_cast):
    L = SWA_BLOCK
    nb = S // L
    T = B * S
    smem = pl.BlockSpec(memory_space=pltpu.SMEM)
    cast_specs = [pl.BlockSpec((w.shape[0] // (B * nb), w.shape[1]), lambda b, n: (b * nb + n, 0))
                  for w in to_cast]

    def prev(b, n):
        return b * nb + jnp.maximum(n - 1, 0)

    return pl.pallas_call(
        functools.partial(_swa_body, n_cast=len(to_cast)),
        grid=(B, nb),
        in_specs=[
            smem, smem,
            pl.BlockSpec((L, SWA_Q_DIM), lambda b, n: (b * nb + n, COL_QA // (SWA_Q_DIM // LANES))),
            pl.BlockSpec((L, LANES), lambda b, n: (prev(b, n), COL_KA)),
            pl.BlockSpec((L, LANES), lambda b, n: (b * nb + n, COL_KA)),
            pl.BlockSpec((L, LANES), lambda b, n: (prev(b, n), COL_VA)),
            pl.BlockSpec((L, LANES), lambda b, n: (b * nb + n, COL_VA)),
            pl.BlockSpec((1, LANES), lambda b, n: (0, 0)),
            pl.BlockSpec((1, LANES), lambda b, n: (0, 0)),
        ] + cast_specs,
        out_specs=[pl.BlockSpec((L, SWA_Q_DIM), lambda b, n: (b * nb + n, 0))] + cast_specs,
        out_shape=[jax.ShapeDtypeStruct((T, SWA_Q_DIM), bf16)]
        + [jax.ShapeDtypeStruct(w.shape, bf16) for w in to_cast],
        compiler_params=pltpu.CompilerParams(
            dimension_semantics=("arbitrary", "arbitrary"),
            vmem_limit_bytes=VMEM_LIMIT),
        name="swa",
    )(sinks, slopes, proj, proj, proj, proj, proj,
      jnp.tile(q_norm, 2).reshape(1, LANES), jnp.tile(k_norm, 2).reshape(1, LANES), *to_cast)


N_BIAS_PARTS = 3
NB_ROWS = 8
N_ALIBI_COL = NB_ROWS * N_BIAS_PARTS
VT_ROWS = HEAD_DIM + 16
SCORE_AHEAD = 3


def _split3(x):
    p0 = x.astype(bf16).astype(f32)
    r1 = x - p0
    p1 = r1.astype(bf16).astype(f32)
    p2 = (r1 - p1).astype(bf16).astype(f32)
    return p0, p1, p2


def _moba_body(slopes_ref, q_ref, k_ref, v_ref, qn_ref, kn_ref, *rest, nb, n_cast):
    cast_in, (o_ref, *cast_out) = rest[:n_cast], rest[n_cast:2 * n_cast + 1]
    kaug_ref, vt_ref = rest[2 * n_cast + 1:]
    for src, dst in zip(cast_in, cast_out):
        dst[...] = src[...].astype(bf16)
    L = MOBA_BLOCK
    S = nb * L
    p = pl.program_id(1)

    kn = _pair_rms(k_ref[...], kn_ref[...])
    lane = lax.broadcasted_iota(jnp.int32, (L, LANES), 1)
    kpos = lax.broadcasted_iota(jnp.int32, (L, LANES), 0).astype(f32)
    lane_lo = lane < HEAD_DIM

    def alibi_cols(cc, slope):
        a0, a1, a2 = _split3((slope * LOG2E) * kpos)
        return jnp.where(cc == N_ALIBI_COL, a0,
                         jnp.where(cc == N_ALIBI_COL + 1, a1,
                                   jnp.where(cc == N_ALIBI_COL + 2, a2, 0.0)))

    cols = (lane - HEAD_DIM, lane)
    alibi = [alibi_cols(cols[hh], slopes_ref[2 * p + hh]) for hh in range(2)]
    for n in range(nb):
        kblock = kn[n * L:(n + 1) * L, :]
        for hh in range(2):
            cc = cols[hh]
            hot = (cc >= 0) & (cc < N_ALIBI_COL) & ((cc % NB_ROWS) == n)
            aug = jnp.where(hot, 1.0, alibi[hh])
            own = lane_lo if hh == 0 else jnp.logical_not(lane_lo)
            kaug_ref[hh, n * L:(n + 1) * L, :] = jnp.where(own, kblock, aug).astype(bf16)
    vt = v_ref[...].T
    vrow = lax.broadcasted_iota(jnp.int32, (VT_ROWS - HEAD_DIM, S), 0)
    ones_then_zeros = (vrow == 0).astype(f32)
    for hh in range(2):
        vt_ref[hh] = jnp.concatenate(
            [vt[hh * HEAD_DIM:(hh + 1) * HEAD_DIM, :], ones_then_zeros], axis=0).astype(bf16)
    means = [jnp.mean(kn[n * L:(n + 1) * L, :], axis=0, keepdims=True) for n in range(nb)]
    means += [jnp.zeros((1, LANES), f32)] * (NB_ROWS - nb)
    km = jnp.concatenate(means, axis=0)
    lane8 = lax.broadcasted_iota(jnp.int32, (NB_ROWS, LANES), 1)
    km_pair = jnp.concatenate(
        [jnp.where(lane8 < HEAD_DIM, km, 0.0), jnp.where(lane8 < HEAD_DIM, 0.0, km)], axis=0)
    km_hi, km_lo = _split2(km_pair)

    qs = _pair_rms(q_ref[...], qn_ref[...]) * (ATTN_SCALE * LOG2E)
    qt = qs.T
    qt_hi, qt_lo = _split2(qt)
    gate = (jnp.dot(km_hi, qt_hi, preferred_element_type=f32)
            + (jnp.dot(km_hi, qt_lo, preferred_element_type=f32)
               + jnp.dot(km_lo, qt_hi, preferred_element_type=f32)))

    blk = lax.broadcasted_iota(jnp.int32, (NB_ROWS, S), 0)
    qblk = lax.broadcasted_iota(jnp.int32, (NB_ROWS, S), 1) // L
    past = blk < qblk
    r = lax.broadcasted_iota(jnp.int32, (L, L), 1)
    c = lax.broadcasted_iota(jnp.int32, (L, L), 0)
    causal = jnp.where(r >= c, 0.0, NEG)
    ones_rows = (blk < 3).astype(f32)

    qa = []
    for hh in range(2):
        g = jnp.where(past, gate[hh * NB_ROWS:(hh + 1) * NB_ROWS, :], -jnp.inf)
        rank = jnp.zeros((NB_ROWS, S), jnp.int32)
        for m in range(nb):
            gm = g[m:m + 1, :]
            ahead = (gm > g) | ((gm == g) & (m < blk))
            rank = rank + ahead.astype(jnp.int32)
        sel = past & (rank < MOBA_TOPK)
        slope = slopes_ref[2 * p + hh]
        bias = jnp.where(sel, (-slope * L * LOG2E) * (qblk - blk).astype(f32), NEG)
        bias = jnp.where(blk == qblk, 0.0, bias)
        b0, b1, b2 = _split3(bias)
        extra = jnp.concatenate(
            [b0, b1, b2, ones_rows,
             jnp.zeros((HEAD_DIM - N_ALIBI_COL - NB_ROWS, S), f32)], axis=0)
        if hh == 0:
            qa.append(jnp.concatenate([qt[:HEAD_DIM], extra], axis=0).astype(bf16))
        else:
            qa.append(jnp.concatenate([extra, qt[HEAD_DIM:]], axis=0).astype(bf16))

    def scores(i, hh):
        return jnp.dot(kaug_ref[hh, 0:(i + 1) * L, :], qa[hh][:, i * L:(i + 1) * L],
                       preferred_element_type=f32)

    units = [(i, hh) for i in range(nb) for hh in range(2)]
    pending = [scores(*u) for u in units[:SCORE_AHEAD]]
    outs = {}

    def finish(i, hh, e_all):
        acc = jnp.dot(vt_ref[hh, :, 0:(i + 1) * L], e_all, preferred_element_type=f32)
        outs[(i, hh)] = acc[:HEAD_DIM, :] / acc[HEAD_DIM:HEAD_DIM + 1, :]
        if hh == 1:
            o_ref[i * L:(i + 1) * L, :] = jnp.concatenate(
                [outs.pop((i, 0)), outs.pop((i, 1))], axis=0).T.astype(bf16)

    deferred = None
    for idx, (i, hh) in enumerate(units):
        s = pending.pop(0)
        if idx + SCORE_AHEAD < len(units):
            pending.append(scores(*units[idx + SCORE_AHEAD]))
        tiles = [s[n * L:(n + 1) * L, :] for n in range(i)] + [s[i * L:(i + 1) * L, :] + causal]
        m = functools.reduce(jnp.maximum, [jnp.max(t, axis=0, keepdims=True) for t in tiles])
        e_all = jnp.concatenate([jnp.exp2(t - m).astype(bf16) for t in tiles], axis=0)
        if deferred is not None:
            finish(*deferred)
        deferred = (i, hh, e_all)
    finish(*deferred)


def _moba(proj, slopes, q_norm, k_norm, B, S, to_cast):
    L = MOBA_BLOCK
    nb = S // L
    assert nb <= NB_ROWS
    T = B * S
    n_pairs = MOBA_HEADS // 2
    n_steps = B * n_pairs
    cast_specs = [pl.BlockSpec((w.shape[0] // n_steps, w.shape[1]), lambda b, p: (b * n_pairs + p, 0))
                  for w in to_cast]
    return pl.pallas_call(
        functools.partial(_moba_body, nb=nb, n_cast=len(to_cast)),
        grid=(B, n_pairs),
        in_specs=[
            pl.BlockSpec(memory_space=pltpu.SMEM),
            pl.BlockSpec((S, LANES), lambda b, p: (b, COL_QB + p)),
            pl.BlockSpec((S, LANES), lambda b, p: (b, COL_KB + p)),
            pl.BlockSpec((S, LANES), lambda b, p: (b, COL_VB + p)),
            pl.BlockSpec((1, LANES), lambda b, p: (0, 0)),
            pl.BlockSpec((1, LANES), lambda b, p: (0, 0)),
        ] + cast_specs,
        out_specs=[pl.BlockSpec((S, LANES), lambda b, p: (b, p))] + cast_specs,
        out_shape=[jax.ShapeDtypeStruct((T, MOBA_DIM), bf16)]
        + [jax.ShapeDtypeStruct(w.shape, bf16) for w in to_cast],
        scratch_shapes=[
            pltpu.VMEM((2, S, LANES), bf16),
            pltpu.VMEM((2, VT_ROWS, S), bf16),
        ],
        compiler_params=pltpu.CompilerParams(
            dimension_semantics=("arbitrary", "arbitrary"),
            vmem_limit_bytes=VMEM_LIMIT),
        name="moba",
    )(slopes, proj, proj, proj,
      jnp.tile(q_norm, 2).reshape(1, LANES), jnp.tile(k_norm, 2).reshape(1, LANES), *to_cast)


ROUTER_COLS = LANES


N_GATE_BLOCKS = D_MODEL // GATE_BLOCK
MIX_CHUNK = 2 * GATE_BLOCK


def _mix_body(*refs):
    ga_refs = refs[:N_GATE_BLOCKS]
    gb_refs = refs[N_GATE_BLOCKS:2 * N_GATE_BLOCKS]
    (oa_ref, ob_ref, x_ref, wua_ref, wub_ref, wout_ref, gffn_ref, wr_ref, wd_ref,
     x1_ref, h2_ref, gid_ref, wd_out_ref) = refs[2 * N_GATE_BLOCKS:]
    wd_out_ref[...] = wd_ref[...].astype(bf16)
    oa = oa_ref[...]
    ob = ob_ref[...]
    gpc = MIX_CHUNK // GATE_BLOCK

    def up(c):
        cols = slice(c * MIX_CHUNK, (c + 1) * MIX_CHUNK)
        return (jnp.dot(oa, wua_ref[:, cols], preferred_element_type=f32),
                jnp.dot(ob, wub_ref[:, cols], preferred_element_type=f32))

    x1 = x_ref[...]
    pending = up(0)
    for c in range(D_MODEL // MIX_CHUNK):
        ya, yb = pending
        if (c + 1) * MIX_CHUNK < D_MODEL:
            pending = up(c + 1)
        gate_a = jnp.concatenate([r[...] for r in ga_refs[c * gpc:(c + 1) * gpc]], axis=1)
        gate_b = jnp.concatenate([r[...] for r in gb_refs[c * gpc:(c + 1) * gpc]], axis=1)
        merged = (_sigmoid(gate_a) * ya + _sigmoid(gate_b) * yb).astype(bf16)
        x1 = x1 + jnp.dot(merged, wout_ref[c * MIX_CHUNK:(c + 1) * MIX_CHUNK, :],
                          preferred_element_type=f32)
    x1_ref[...] = x1
    ms = jnp.mean(x1 * x1, axis=-1, keepdims=True)
    h2 = x1 * lax.rsqrt(ms + EPS) * gffn_ref[...]
    tm = h2.shape[0]
    for c in range(ROW_SLAB):
        h2_ref[pl.ds(c, tm, stride=SLAB_IN), :] = h2[:, c * LANES:(c + 1) * LANES]

    h_hi, h_lo = _split2(h2)
    hi_all = jnp.dot(h_hi, wr_ref[...], preferred_element_type=f32)
    lo_hi = jnp.dot(h_lo, wr_ref[:, :ROUTER_COLS], preferred_element_type=f32)
    lt = (hi_all[:, :ROUTER_COLS] + (lo_hi + hi_all[:, ROUTER_COLS:])).T
    gl = [lt[g:g + 1, :] for g in range(N_GROUPS)]
    gmax = functools.reduce(jnp.maximum, gl)
    gsum = functools.reduce(lambda a, b: a + b, [jnp.exp(v - gmax) for v in gl])
    g_p = 1.0 / gsum
    g_i = jnp.full((1, tm), N_GROUPS - 1, jnp.int32)
    for g in reversed(range(N_GROUPS)):
        g_i = jnp.where(gl[g] == gmax, g, g_i)

    el = []
    for e in range(EXPERTS_PER_GROUP):
        v = jnp.zeros((1, tm), f32)
        for g in range(N_GROUPS):
            r = N_GROUPS + g * EXPERTS_PER_GROUP + e
            v = jnp.where(g_i == g, lt[r:r + 1, :], v)
        el.append(v)
    emax = functools.reduce(jnp.maximum, el)
    ex = [jnp.exp(v - emax) for v in el]
    esum = functools.reduce(lambda a, b: a + b, ex)
    ep = [v / esum for v in ex]
    p1 = functools.reduce(jnp.maximum, ep)
    i1 = jnp.full((1, tm), EXPERTS_PER_GROUP - 1, jnp.int32)
    for e in reversed(range(EXPERTS_PER_GROUP)):
        i1 = jnp.where(ep[e] == p1, e, i1)
    rest = [jnp.where(i1 == e, -1.0, ep[e]) for e in range(EXPERTS_PER_GROUP)]
    p2 = functools.reduce(jnp.maximum, rest)
    i2 = jnp.full((1, tm), EXPERTS_PER_GROUP - 1, jnp.int32)
    for e in reversed(range(EXPERTS_PER_GROUP)):
        i2 = jnp.where(rest[e] == p2, e, i2)
    w1 = g_p * (p1 / (p1 + p2))
    w2 = g_p * (p2 / (p1 + p2))
    gid_ref[...] = jnp.concatenate([g_i, jnp.zeros((7, tm), jnp.int32)], axis=0)
    cw = [jnp.where(i1 == e, w1, 0.0) + jnp.where(i2 == e, w2, 0.0) for e in range(EXPERTS_PER_GROUP)]
    cw_t = jnp.concatenate(cw + [jnp.zeros((LANES - EXPERTS_PER_GROUP, tm), f32)], axis=0).T
    for e in range(SLAB_IN - ROW_SLAB):
        if e < EXPERTS_PER_GROUP:
            row = jnp.broadcast_to(cw_t[:, e:e + 1], (tm, LANES))
        else:
            row = jnp.zeros((tm, LANES), f32)
        h2_ref[pl.ds(ROW_SLAB + e, tm, stride=SLAB_IN), :] = row


def _mix(oa, ob, proj, x2, wua, wub, wout, g_ffn, wr, wd, tm=256):
    T = x2.shape[0]
    wd_spec = pl.BlockSpec((wd.shape[0] // (T // tm), wd.shape[1]), lambda i: (i, 0))
    const = lambda i: (0, 0)
    single = pl.Buffered(1)

    def gate_specs(first_col):
        first = first_col * LANES // GATE_BLOCK
        return [pl.BlockSpec((tm, GATE_BLOCK), functools.partial(lambda i, j: (i, j), j=first + k))
                for k in range(N_GATE_BLOCKS)]

    return pl.pallas_call(
        _mix_body,
        grid=(T // tm,),
        in_specs=gate_specs(COL_GATE_A) + gate_specs(COL_GATE_B) + [
            pl.BlockSpec((tm, SWA_Q_DIM), lambda i: (i, 0)),
            pl.BlockSpec((tm, MOBA_DIM), lambda i: (i, 0)),
            pl.BlockSpec((tm, D_MODEL), lambda i: (i, 0)),
            pl.BlockSpec((SWA_Q_DIM, D_MODEL), const, pipeline_mode=single),
            pl.BlockSpec((MOBA_DIM, D_MODEL), const, pipeline_mode=single),
            pl.BlockSpec((D_MODEL, D_MODEL), const, pipeline_mode=single),
            pl.BlockSpec((1, D_MODEL), const),
            pl.BlockSpec((D_MODEL, 2 * ROUTER_COLS), const),
            wd_spec,
        ],
        out_specs=[
            pl.BlockSpec((tm, D_MODEL), lambda i: (i, 0)),
            pl.BlockSpec((tm * SLAB_IN, LANES), lambda i: (i, 0)),
            pl.BlockSpec((8, tm), lambda i: (0, i)),
            wd_spec,
        ],
        out_shape=[
            jax.ShapeDtypeStruct((T, D_MODEL), f32),
            jax.ShapeDtypeStruct((T * SLAB_IN, LANES), f32),
            jax.ShapeDtypeStruct((8, T), jnp.int32),
            jax.ShapeDtypeStruct(wd.shape, bf16),
        ],
        compiler_params=pltpu.CompilerParams(
            dimension_semantics=("arbitrary",),
            vmem_limit_bytes=VMEM_LIMIT),
        name="mix",
    )(*([proj] * (2 * N_GATE_BLOCKS)), oa, ob, x2, wua, wub, wout, g_ffn.reshape(1, D_MODEL), wr, wd)


def _route_tables(gid, T, tm):
    nt = T // tm + N_GROUPS
    g = gid[0]
    onehot = (g[:, None] == jnp.arange(N_GROUPS, dtype=jnp.int32)[None, :]).astype(jnp.int32)
    incl = jnp.cumsum(onehot, axis=0)
    pos = jnp.sum(onehot * incl, axis=1) - 1
    counts = incl[-1]
    padded = (counts + tm - 1) // tm * tm
    seg_end = jnp.cumsum(padded)
    seg_start = seg_end - padded
    dest = jnp.sum(onehot * seg_start[None, :], axis=1) + pos
    tile_start = jnp.arange(nt, dtype=jnp.int32) * tm
    tile_g = jnp.sum((tile_start[:, None] >= seg_end[None, :]).astype(jnp.int32), axis=1)
    tile_g = jnp.minimum(tile_g, N_GROUPS - 1)
    nvalid = (seg_end[-1] // tm).astype(jnp.int32).reshape(1)
    n_pad = padded - counts
    pads = jnp.concatenate([seg_start + counts, seg_end, jnp.cumsum(n_pad) - n_pad])
    return (nvalid, tile_g.astype(jnp.int32), dest.astype(jnp.int32), pads.astype(jnp.int32))


def _moe_body(nvalid_ref, tg_ref, dest_ref, pads_ref, wg_ref, wu_ref, wd_ref, h2s_hbm, ytok_hbm,
              asg, gbuf, ybuf, xbuf, hbuf, gsem, ssem, *, tm, nt, n_tok):
    t = pl.program_id(0)
    e = pl.program_id(1)
    last_e = EXPERTS_PER_GROUP - 1
    nvalid = nvalid_ref[0]
    slot = lax.rem(t, 2)
    other = 1 - slot

    def for_rows(fn):
        def body(r, carry):
            fn(r)
            return carry
        lax.fori_loop(0, tm, body, 0, unroll=8)

    def gather_copy(tok, s, r):
        return pltpu.make_async_copy(
            h2s_hbm.at[pl.ds(pl.multiple_of(tok * SLAB_IN, 8), SLAB_IN), :],
            gbuf.at[pl.ds(pl.multiple_of((s * tm + r) * SLAB_IN, 8), SLAB_IN), :], gsem.at[s])

    def scatter_copy(row, s, r):
        return pltpu.make_async_copy(
            ybuf.at[pl.ds(pl.multiple_of((s * tm + r) * ROW_SLAB, 8), ROW_SLAB), :],
            ytok_hbm.at[pl.ds(pl.multiple_of(row * ROW_SLAB, 8), ROW_SLAB), :], ssem.at[s])

    def start_gathers(tile, s):
        for_rows(lambda r: gather_copy(jnp.maximum(asg[tile * tm + r], 0), s, r).start())

    def start_scatters(tile, s):
        def one(r):
            a = asg[tile * tm + r]
            scatter_copy(jnp.where(a >= 0, a, n_tok - 1 - a), s, r).start()
        for_rows(one)

    def wait_gathers(s):
        for_rows(lambda r: gather_copy(0, s, r).wait())

    def wait_scatters(s):
        for_rows(lambda r: scatter_copy(0, s, r).wait())

    @pl.when((t == 0) & (e == 0))
    def _():
        def place(tok, carry):
            asg[dest_ref[tok]] = tok
            return carry
        lax.fori_loop(0, n_tok, place, 0, unroll=8)

        for g in range(N_GROUPS):
            first = pads_ref[g]
            code = -1 - pads_ref[2 * N_GROUPS + g] + first

            def mark(p, carry):
                asg[p] = code - p
                return carry
            lax.fori_loop(first, pads_ref[N_GROUPS + g], mark, 0)
        ybuf[...] = jnp.zeros(ybuf.shape, f32)
        start_gathers(0, 0)

    @pl.when((e == 0) & (t < nvalid))
    def _():
        wait_gathers(slot)
        base = slot * tm * SLAB_IN
        xbuf[...] = jnp.concatenate(
            [gbuf[pl.ds(base + c, tm, stride=SLAB_IN), :] for c in range(ROW_SLAB)],
            axis=1).astype(bf16)

    @pl.when(t < nvalid)
    def _():
        quarter = tm // EXPERTS_PER_GROUP
        have_next = t + 1 < nvalid
        have_prev = t >= 1
        nxt = jnp.minimum(t + 1, nvalid - 1)

        def issue_rows(r0, r1):
            for r in range(r0, r1):
                row = e * quarter + r
                tok = jnp.where(have_next, jnp.maximum(asg[nxt * tm + row], 0), 0)
                gather_copy(tok, other, row).start()
                a = asg[jnp.maximum(t - 1, 0) * tm + row]
                dst = jnp.where(have_prev, jnp.where(a >= 0, a, n_tok - 1 - a), nt * tm + row)
                scatter_copy(dst, other, row).start(priority=r % 2)

        x = xbuf[...]
        n_chunks = D_EXPERT // PROJ_CHUNK
        per_piece = quarter // (2 * n_chunks)
        pieces = {"g": [], "u": []}
        for i, (name, w_ref) in enumerate((n, w) for c in range(n_chunks)
                                          for n, w in (("g", wg_ref), ("u", wu_ref))):
            c = i // 2
            w = w_ref[:, c * PROJ_CHUNK:(c + 1) * PROJ_CHUNK].astype(bf16)
            pieces[name].append(jnp.dot(x, w, preferred_element_type=f32))
            issue_rows(i * per_piece, (i + 1) * per_piece)
        gte = jnp.concatenate(pieces["g"], axis=1)
        up = jnp.concatenate(pieces["u"], axis=1)
        cw = gbuf[pl.ds(slot * tm * SLAB_IN + ROW_SLAB + e, tm, stride=SLAB_IN), :]
        hid = (gte * _sigmoid(gte)) * up * jnp.concatenate([cw] * (D_EXPERT // LANES), axis=1)
        hbuf[e] = hid.astype(bf16)

    @pl.when((e == last_e) & (t < nvalid))
    def _():
        hcat = jnp.concatenate([hbuf[k] for k in range(EXPERTS_PER_GROUP)], axis=1)
        y = jnp.dot(hcat, wd_ref[...], preferred_element_type=f32)

        @pl.when(t >= 1)
        def _():
            wait_scatters(slot)
        base = slot * tm * ROW_SLAB
        for c in range(ROW_SLAB):
            ybuf[pl.ds(base + c, tm, stride=ROW_SLAB), :] = y[:, c * LANES:(c + 1) * LANES]

        @pl.when(t == nvalid - 1)
        def _():
            start_scatters(t, slot)
            wait_gathers(other)
            wait_scatters(other)
            wait_scatters(slot)

    @pl.when((e == last_e) & (t == nt - 1))
    def _():
        ybuf[0:tm * ROW_SLAB, :] = jnp.zeros((tm * ROW_SLAB, LANES), f32)
        for j in range(n_tok // tm, nt):
            @pl.when(j >= nvalid)
            def _():
                fill = pltpu.make_async_copy(
                    ybuf.at[0:tm * ROW_SLAB, :],
                    ytok_hbm.at[j * tm * ROW_SLAB:(j + 1) * tm * ROW_SLAB, :], ssem.at[0])
                fill.start()
                fill.wait()


def _moe(h2s, tables, wg, wu, wd, T, tm):
    nvalid, tile_g, dest, pads = tables
    nt = tile_g.shape[0]

    def w_index(t, e, nv, tg, de, pd):
        return (tg[t] * EXPERTS_PER_GROUP + jnp.where(t < nv[0], e, EXPERTS_PER_GROUP - 1), 0, 0)

    grid_spec = pltpu.PrefetchScalarGridSpec(
        num_scalar_prefetch=4,
        grid=(nt, EXPERTS_PER_GROUP),
        in_specs=[
            pl.BlockSpec((None, D_MODEL, D_EXPERT), w_index),
            pl.BlockSpec((None, D_MODEL, D_EXPERT), w_index),
            pl.BlockSpec((None, EXPERTS_PER_GROUP * D_EXPERT, D_MODEL),
                         lambda t, e, nv, tg, de, pd: (tg[t], 0, 0), pipeline_mode=pl.Buffered(1)),
            pl.BlockSpec(memory_space=pl.ANY),
        ],
        out_specs=pl.BlockSpec(memory_space=pl.ANY),
        scratch_shapes=[
            pltpu.SMEM((nt * tm,), jnp.int32),
            pltpu.VMEM((2 * tm * SLAB_IN, LANES), f32),
            pltpu.VMEM((2 * tm * ROW_SLAB, LANES), f32),
            pltpu.VMEM((tm, D_MODEL), bf16),
            pltpu.VMEM((EXPERTS_PER_GROUP, tm, D_EXPERT), bf16),
            pltpu.SemaphoreType.DMA((2,)),
            pltpu.SemaphoreType.DMA((2,)),
        ],
    )
    return pl.pallas_call(
        functools.partial(_moe_body, tm=tm, nt=nt, n_tok=T),
        grid_spec=grid_spec,
        out_shape=jax.ShapeDtypeStruct(((nt + 1) * tm * ROW_SLAB, LANES), f32),
        compiler_params=pltpu.CompilerParams(
            dimension_semantics=("arbitrary", "arbitrary"),
            vmem_limit_bytes=VMEM_LIMIT),
        name="moe",
    )(nvalid, tile_g, dest, pads, wg, wu,
      wd.reshape(N_GROUPS, EXPERTS_PER_GROUP * D_EXPERT, D_MODEL), h2s)


def _combine_body(x1_ref, y_ref, o_ref):
    tm = x1_ref.shape[0]
    for c in range(ROW_SLAB):
        cols = slice(c * LANES, (c + 1) * LANES)
        o_ref[:, cols] = x1_ref[:, cols] + y_ref[pl.ds(c, tm, stride=ROW_SLAB), :]


def _combine(x1, ytok, tm=256):
    T = x1.shape[0]
    return pl.pallas_call(
        _combine_body,
        grid=(T // tm,),
        in_specs=[
            pl.BlockSpec((tm, D_MODEL), lambda i: (i, 0)),
            pl.BlockSpec((tm * ROW_SLAB, LANES), lambda i: (i, 0)),
        ],
        out_specs=pl.BlockSpec((tm, D_MODEL), lambda i: (i, 0)),
        out_shape=jax.ShapeDtypeStruct((T, D_MODEL), f32),
        compiler_params=pltpu.CompilerParams(
            dimension_semantics=("arbitrary",),
            vmem_limit_bytes=VMEM_LIMIT),
        name="combine",
    )(x1, ytok)


def _alibi_slopes(n):
    return jnp.exp2(-8.0 * jnp.arange(1, n + 1, dtype=f32) / n)


def kernel(x, g_mix, w_in, q_norm_swa, k_norm_swa, sinks, q_norm_moba, k_norm_moba,
           w_up_swa, w_up_moba, w_out, g_ffn, w_router_group, w_router_expert,
           w_gate_e, w_up_e, w_down_e):
    B, S, D = x.shape
    assert D == D_MODEL and S % MOBA_BLOCK == 0 and S % SWA_BLOCK == 0
    T = B * S
    x2 = x.reshape(T, D)

    proj = _inproj(x2, g_mix, w_in)

    oa, wg, wu = _swa(proj, sinks.astype(f32), _alibi_slopes(SWA_Q_HEADS), q_norm_swa, k_norm_swa,
                      B, S, (w_gate_e.reshape(N_EXPERTS * D_MODEL, D_EXPERT),
                             w_up_e.reshape(N_EXPERTS * D_MODEL, D_EXPERT)))
    ob, wua, wub, wout = _moba(proj, _alibi_slopes(MOBA_HEADS), q_norm_moba, k_norm_moba, B, S,
                               (w_up_swa, w_up_moba, w_out))

    wr = jnp.concatenate(
        [w_router_group,
         w_router_expert.transpose(1, 0, 2).reshape(D, N_EXPERTS),
         jnp.zeros((D, ROUTER_COLS - N_GROUPS - N_EXPERTS), f32)], axis=1)
    wr_hi = wr.astype(bf16)
    wr_lo = (wr - wr_hi.astype(f32)).astype(bf16)
    x1, h2s, gid, wd = _mix(oa, ob, proj, x2, wua, wub, wout, g_ffn,
                            jnp.concatenate([wr_hi, wr_lo], axis=1),
                            w_down_e.reshape(N_EXPERTS * D_EXPERT, D_MODEL))

    tables = _route_tables(gid, T, MOE_TILE)
    ytok = _moe(h2s, tables, wg.reshape(w_gate_e.shape), wu.reshape(w_up_e.shape), wd, T, MOE_TILE)
    y = _combine(x1, ytok)
    return y.reshape(B, S, D)
```

```python
import functools

import jax
import jax.numpy as jnp
from jax import lax
from jax.experimental import pallas as pl
from jax.experimental.pallas import tpu as pltpu

D_MODEL = 2048
HEAD_DIM = 64
ATTN_SCALE = HEAD_DIM ** -0.5
SWA_Q_HEADS = 16
SWA_KV_HEADS = 2
SWA_WINDOW = 128
SWA_BLOCK = 128
MOBA_HEADS = 16
MOBA_BLOCK = 256
MOBA_TOPK = 3
N_GROUPS = 4
EXPERTS_PER_GROUP = 4
N_EXPERTS = N_GROUPS * EXPERTS_PER_GROUP
D_EXPERT = 512
EPS = 1e-6

SWA_Q_DIM = SWA_Q_HEADS * HEAD_DIM
MOBA_DIM = MOBA_HEADS * HEAD_DIM

LANES = 128
ROW_SLAB = D_MODEL // LANES
SLAB_IN = ROW_SLAB + 8
MOE_TILE = 512
PROJ_CHUNK = 2 * LANES
VMEM_LIMIT = 56 * 1024 * 1024
NEG = -1e30
LOG2E = 1.4426950408889634

COL_QA = 0
COL_KA = COL_QA + SWA_Q_DIM // LANES
COL_VA = COL_KA + 1
COL_QB = COL_VA + 1
COL_KB = COL_QB + MOBA_DIM // LANES
COL_VB = COL_KB + MOBA_DIM // LANES
COL_GATE_A = COL_VB + MOBA_DIM // LANES
COL_GATE_B = COL_GATE_A + D_MODEL // LANES
GATE_BLOCK = 2 * LANES
assert (COL_GATE_A * LANES) % GATE_BLOCK == 0 and (COL_GATE_B * LANES) % GATE_BLOCK == 0

bf16 = jnp.bfloat16
f32 = jnp.float32


def _sigmoid(x):
    return 1.0 / (1.0 + jnp.exp(-x))


def _pair_rms(x, gain):
    lane = lax.broadcasted_iota(jnp.int32, x.shape, 1)
    lo = lane < HEAD_DIM
    sq = x * x
    s0 = jnp.sum(jnp.where(lo, sq, 0.0), axis=-1, keepdims=True)
    s1 = jnp.sum(jnp.where(lo, 0.0, sq), axis=-1, keepdims=True)
    r0 = lax.rsqrt(s0 * (1.0 / HEAD_DIM) + EPS)
    r1 = lax.rsqrt(s1 * (1.0 / HEAD_DIM) + EPS)
    return x * jnp.where(lo, r0, r1) * gain


def _split2(x):
    hi = x.astype(bf16)
    lo = (x - hi.astype(f32)).astype(bf16)
    return hi, lo


CAST_CHUNK = 2 * LANES
NORM_ROWS = 512


def _inproj_body(x_ref, g_ref, w_ref, o_ref, h_ref):
    @pl.when(pl.program_id(1) == 0)
    def _():
        for r in range(0, x_ref.shape[0], NORM_ROWS):
            x = x_ref[r:r + NORM_ROWS, :]
            ms = jnp.mean(x * x, axis=-1, keepdims=True)
            h_ref[r:r + NORM_ROWS, :] = (x * lax.rsqrt(ms + EPS) * g_ref[...]).astype(bf16)

    n_chunks = w_ref.shape[1] // CAST_CHUNK
    cast = lambda c: w_ref[:, c * CAST_CHUNK:(c + 1) * CAST_CHUNK].astype(bf16)
    w_next = cast(0)
    for c in range(n_chunks):
        w_cur = w_next
        if c + 1 < n_chunks:
            w_next = cast(c + 1)
        o_ref[:, c * CAST_CHUNK:(c + 1) * CAST_CHUNK] = jnp.dot(
            h_ref[...], w_cur, preferred_element_type=f32)


def _inproj(x2, g, w, tm=2048, tn=768):
    T = x2.shape[0]
    N = w.shape[1]
    return pl.pallas_call(
        _inproj_body,
        grid=(T // tm, N // tn),
        in_specs=[
            pl.BlockSpec((tm, D_MODEL), lambda i, j: (i, 0), pipeline_mode=pl.Buffered(1)),
            pl.BlockSpec((1, D_MODEL), lambda i, j: (0, 0)),
            pl.BlockSpec((D_MODEL, tn), lambda i, j: (0, j)),
        ],
        out_specs=pl.BlockSpec((tm, tn), lambda i, j: (i, j)),
        out_shape=jax.ShapeDtypeStruct((T, N), f32),
        scratch_shapes=[pltpu.VMEM((tm, D_MODEL), bf16)],
        compiler_params=pltpu.CompilerParams(
            dimension_semantics=("arbitrary", "arbitrary"),
            vmem_limit_bytes=VMEM_LIMIT),
        name="inproj",
    )(x2, g.reshape(1, D_MODEL), w)


SWA_AHEAD = 2


def _swa_body(sinks_ref, slopes_ref, q_ref, kp_ref, kc_ref, vp_ref, vc_ref,
              qn_ref, kn_ref, *rest, n_cast):
    cast_in, (o_ref, *cast_out) = rest[:n_cast], rest[n_cast:]
    for src, dst in zip(cast_in, cast_out):
        dst[...] = src[...].astype(bf16)
    L = SWA_BLOCK
    n = pl.program_id(1)
    k2 = jnp.concatenate([kp_ref[...], kc_ref[...]], axis=0)
    k2n = _pair_rms(k2, kn_ref[...]).astype(bf16)
    v2t = jnp.concatenate([vp_ref[...], vc_ref[...]], axis=0).T.astype(bf16)
    qn = jnp.concatenate(
        [_pair_rms(q_ref[:, pp * LANES:(pp + 1) * LANES], qn_ref[...])
         for pp in range(SWA_Q_HEADS // 2)], axis=1) * (ATTN_SCALE * LOG2E)
    qt = qn.T

    key = lax.broadcasted_iota(jnp.int32, (2 * L, L), 0)
    qry = lax.broadcasted_iota(jnp.int32, (2 * L, L), 1)
    dist = qry + L - key
    ok = (dist >= 0) & (dist < SWA_WINDOW) & ((n > 0) | (key >= L))
    distf = dist.astype(f32)
    zeros = jnp.zeros((HEAD_DIM, L), f32)

    heads_per_kv = SWA_Q_HEADS // SWA_KV_HEADS

    def scores(h):
        g = h // heads_per_kv
        qh = qt[h * HEAD_DIM:(h + 1) * HEAD_DIM, :]
        qa = jnp.concatenate([qh, zeros] if g == 0 else [zeros, qh], axis=0).astype(bf16)
        return jnp.dot(k2n, qa, preferred_element_type=f32)

    def softmax(h, s):
        s = jnp.where(ok, s - (slopes_ref[h] * LOG2E) * distf, -jnp.inf)
        sink = sinks_ref[h] * LOG2E
        m = jnp.maximum(jnp.max(s, axis=0, keepdims=True), sink)
        p = jnp.exp2(s - m)
        denom = jnp.sum(p, axis=0, keepdims=True) + jnp.exp2(sink - m)
        return p.astype(bf16), denom

    outs = []

    def finish(h, p, denom):
        g = h // heads_per_kv
        o = jnp.dot(v2t, p, preferred_element_type=f32)
        outs.append(o[g * HEAD_DIM:(g + 1) * HEAD_DIM, :] / denom)

    pending = [scores(h) for h in range(SWA_AHEAD)]
    deferred = None
    for h in range(SWA_Q_HEADS):
        s = pending.pop(0)
        if h + SWA_AHEAD < SWA_Q_HEADS:
            pending.append(scores(h + SWA_AHEAD))
        p, denom = softmax(h, s)
        if deferred is not None:
            finish(*deferred)
        deferred = (h, p, denom)
    finish(*deferred)
    o_ref[...] = jnp.concatenate(outs, axis=0).T.astype(bf16)


def _swa(proj, sinks, slopes, q_norm, k_norm, B, S, to_cast):
    L = SWA_BLOCK
    nb = S // L
    T = B * S
    smem = pl.BlockSpec(memory_space=pltpu.SMEM)
    cast_specs = [pl.BlockSpec((w.shape[0] // (B * nb), w.shape[1]), lambda b, n: (b * nb + n, 0))
                  for w in to_cast]

    def prev(b, n):
        return b * nb + jnp.maximum(n - 1, 0)

    return pl.pallas_call(
        functools.partial(_swa_body, n_cast=len(to_cast)),
        grid=(B, nb),
        in_specs=[
            smem, smem,
            pl.BlockSpec((L, SWA_Q_DIM), lambda b, n: (b * nb + n, COL_QA // (SWA_Q_DIM // LANES))),
            pl.BlockSpec((L, LANES), lambda b, n: (prev(b, n), COL_KA)),
            pl.BlockSpec((L, LANES), lambda b, n: (b * nb + n, COL_KA)),
            pl.BlockSpec((L, LANES), lambda b, n: (prev(b, n), COL_VA)),
            pl.BlockSpec((L, LANES), lambda b, n: (b * nb + n, COL_VA)),
            pl.BlockSpec((1, LANES), lambda b, n: (0, 0)),
            pl.BlockSpec((1, LANES), lambda b, n: (0, 0)),
        ] + cast_specs,
        out_specs=[pl.BlockSpec((L, SWA_Q_DIM), lambda b, n: (b * nb + n, 0))] + cast_specs,
        out_shape=[jax.ShapeDtypeStruct((T, SWA_Q_DIM), bf16)]
        + [jax.ShapeDtypeStruct(w.shape, bf16) for w in to_cast],
        compiler_params=pltpu.CompilerParams(
            dimension_semantics=("arbitrary", "arbitrary"),
            vmem_limit_bytes=VMEM_LIMIT),
        name="swa",
    )(sinks, slopes, proj, proj, proj, proj, proj,
      jnp.tile(q_norm, 2).reshape(1, LANES), jnp.tile(k_norm, 2).reshape(1, LANES), *to_cast)


N_BIAS_PARTS = 3
NB_ROWS = 8
N_ALIBI_COL = NB_ROWS * N_BIAS_PARTS
VT_ROWS = HEAD_DIM + 16
SCORE_AHEAD = 3


def _split3(x):
    p0 = x.astype(bf16).astype(f32)
    r1 = x - p0
    p1 = r1.astype(bf16).astype(f32)
    p2 = (r1 - p1).astype(bf16).astype(f32)
    return p0, p1, p2


def _moba_body(slopes_ref, q_ref, k_ref, v_ref, qn_ref, kn_ref, *rest, nb, n_cast):
    cast_in, (o_ref, *cast_out) = rest[:n_cast], rest[n_cast:2 * n_cast + 1]
    kaug_ref, vt_ref = rest[2 * n_cast + 1:]
    for src, dst in zip(cast_in, cast_out):
        dst[...] = src[...].astype(bf16)
    L = MOBA_BLOCK
    S = nb * L
    p = pl.program_id(1)

    kn = _pair_rms(k_ref[...], kn_ref[...])
    lane = lax.broadcasted_iota(jnp.int32, (L, LANES), 1)
    kpos = lax.broadcasted_iota(jnp.int32, (L, LANES), 0).astype(f32)
    lane_lo = lane < HEAD_DIM

    def alibi_cols(cc, slope):
        a0, a1, a2 = _split3((slope * LOG2E) * kpos)
        return jnp.where(cc == N_ALIBI_COL, a0,
                         jnp.where(cc == N_ALIBI_COL + 1, a1,
                                   jnp.where(cc == N_ALIBI_COL + 2, a2, 0.0)))

    cols = (lane - HEAD_DIM, lane)
    alibi = [alibi_cols(cols[hh], slopes_ref[2 * p + hh]) for hh in range(2)]
    for n in range(nb):
        kblock = kn[n * L:(n + 1) * L, :]
        for hh in range(2):
            cc = cols[hh]
            hot = (cc >= 0) & (cc < N_ALIBI_COL) & ((cc % NB_ROWS) == n)
            aug = jnp.where(hot, 1.0, alibi[hh])
            own = lane_lo if hh == 0 else jnp.logical_not(lane_lo)
            kaug_ref[hh, n * L:(n + 1) * L, :] = jnp.where(own, kblock, aug).astype(bf16)
    vt = v_ref[...].T
    vrow = lax.broadcasted_iota(jnp.int32, (VT_ROWS - HEAD_DIM, S), 0)
    ones_then_zeros = (vrow == 0).astype(f32)
    for hh in range(2):
        vt_ref[hh] = jnp.concatenate(
            [vt[hh * HEAD_DIM:(hh + 1) * HEAD_DIM, :], ones_then_zeros], axis=0).astype(bf16)
    means = [jnp.mean(kn[n * L:(n + 1) * L, :], axis=0, keepdims=True) for n in range(nb)]
    means += [jnp.zeros((1, LANES), f32)] * (NB_ROWS - nb)
    km = jnp.concatenate(means, axis=0)
    lane8 = lax.broadcasted_iota(jnp.int32, (NB_ROWS, LANES), 1)
    km_pair = jnp.concatenate(
        [jnp.where(lane8 < HEAD_DIM, km, 0.0), jnp.where(lane8 < HEAD_DIM, 0.0, km)], axis=0)
    km_hi, km_lo = _split2(km_pair)

    qs = _pair_rms(q_ref[...], qn_ref[...]) * (ATTN_SCALE * LOG2E)
    qt = qs.T
    qt_hi, qt_lo = _split2(qt)
    gate = (jnp.dot(km_hi, qt_hi, preferred_element_type=f32)
            + (jnp.dot(km_hi, qt_lo, preferred_element_type=f32)
               + jnp.dot(km_lo, qt_hi, preferred_element_type=f32)))

    blk = lax.broadcasted_iota(jnp.int32, (NB_ROWS, S), 0)
    qblk = lax.broadcasted_iota(jnp.int32, (NB_ROWS, S), 1) // L
    past = blk < qblk
    r = lax.broadcasted_iota(jnp.int32, (L, L), 1)
    c = lax.broadcasted_iota(jnp.int32, (L, L), 0)
    causal = jnp.where(r >= c, 0.0, NEG)
    ones_rows = (blk < 3).astype(f32)

    qa = []
    for hh in range(2):
        g = jnp.where(past, gate[hh * NB_ROWS:(hh + 1) * NB_ROWS, :], -jnp.inf)
        rank = jnp.zeros((NB_ROWS, S), jnp.int32)
        for m in range(nb):
            gm = g[m:m + 1, :]
            ahead = (gm > g) | ((gm == g) & (m < blk))
            rank = rank + ahead.astype(jnp.int32)
        sel = past & (rank < MOBA_TOPK)
        slope = slopes_ref[2 * p + hh]
        bias = jnp.where(sel, (-slope * L * LOG2E) * (qblk - blk).astype(f32), NEG)
        bias = jnp.where(blk == qblk, 0.0, bias)
        b0, b1, b2 = _split3(bias)
        extra = jnp.concatenate(
            [b0, b1, b2, ones_rows,
             jnp.zeros((HEAD_DIM - N_ALIBI_COL - NB_ROWS, S), f32)], axis=0)
        if hh == 0:
            qa.append(jnp.concatenate([qt[:HEAD_DIM], extra], axis=0).astype(bf16))
        else:
            qa.append(jnp.concatenate([extra, qt[HEAD_DIM:]], axis=0).astype(bf16))

    def scores(i, hh):
        return jnp.dot(kaug_ref[hh, 0:(i + 1) * L, :], qa[hh][:, i * L:(i + 1) * L],
                       preferred_element_type=f32)

    units = [(i, hh) for i in range(nb) for hh in range(2)]
    pending = [scores(*u) for u in units[:SCORE_AHEAD]]
    outs = {}

    def finish(i, hh, e_all):
        acc = jnp.dot(vt_ref[hh, :, 0:(i + 1) * L], e_all, preferred_element_type=f32)
        outs[(i, hh)] = acc[:HEAD_DIM, :] / acc[HEAD_DIM:HEAD_DIM + 1, :]
        if hh == 1:
            o_ref[i * L:(i + 1) * L, :] = jnp.concatenate(
                [outs.pop((i, 0)), outs.pop((i, 1))], axis=0).T.astype(bf16)

    deferred = None
    for idx, (i, hh) in enumerate(units):
        s = pending.pop(0)
        if idx + SCORE_AHEAD < len(units):
            pending.append(scores(*units[idx + SCORE_AHEAD]))
        tiles = [s[n * L:(n + 1) * L, :] for n in range(i)] + [s[i * L:(i + 1) * L, :] + causal]
        m = functools.reduce(jnp.maximum, [jnp.max(t, axis=0, keepdims=True) for t in tiles])
        e_all = jnp.concatenate([jnp.exp2(t - m).astype(bf16) for t in tiles], axis=0)
        if deferred is not None:
            finish(*deferred)
        deferred = (i, hh, e_all)
    finish(*deferred)


def _moba(proj, slopes, q_norm, k_norm, B, S, to_cast):
    L = MOBA_BLOCK
    nb = S // L
    assert nb <= NB_ROWS
    T = B * S
    n_pairs = MOBA_HEADS // 2
    n_steps = B * n_pairs
    cast_specs = [pl.BlockSpec((w.shape[0] // n_steps, w.shape[1]), lambda b, p: (b * n_pairs + p, 0))
                  for w in to_cast]
    return pl.pallas_call(
        functools.partial(_moba_body, nb=nb, n_cast=len(to_cast)),
        grid=(B, n_pairs),
        in_specs=[
            pl.BlockSpec(memory_space=pltpu.SMEM),
            pl.BlockSpec((S, LANES), lambda b, p: (b, COL_QB + p)),
            pl.BlockSpec((S, LANES), lambda b, p: (b, COL_KB + p)),
            pl.BlockSpec((S, LANES), lambda b, p: (b, COL_VB + p)),
            pl.BlockSpec((1, LANES), lambda b, p: (0, 0)),
            pl.BlockSpec((1, LANES), lambda b, p: (0, 0)),
        ] + cast_specs,
        out_specs=[pl.BlockSpec((S, LANES), lambda b, p: (b, p))] + cast_specs,
        out_shape=[jax.ShapeDtypeStruct((T, MOBA_DIM), bf16)]
        + [jax.ShapeDtypeStruct(w.shape, bf16) for w in to_cast],
        scratch_shapes=[
            pltpu.VMEM((2, S, LANES), bf16),
            pltpu.VMEM((2, VT_ROWS, S), bf16),
        ],
        compiler_params=pltpu.CompilerParams(
            dimension_semantics=("arbitrary", "arbitrary"),
            vmem_limit_bytes=VMEM_LIMIT),
        name="moba",
    )(slopes, proj, proj, proj,
      jnp.tile(q_norm, 2).reshape(1, LANES), jnp.tile(k_norm, 2).reshape(1, LANES), *to_cast)


ROUTER_COLS = LANES


N_GATE_BLOCKS = D_MODEL // GATE_BLOCK
MIX_CHUNK = 2 * GATE_BLOCK


def _mix_body(*refs):
    ga_refs = refs[:N_GATE_BLOCKS]
    gb_refs = refs[N_GATE_BLOCKS:2 * N_GATE_BLOCKS]
    (oa_ref, ob_ref, x_ref, wua_ref, wub_ref, wout_ref, gffn_ref, wr_ref, wd_ref,
     x1_ref, h2_ref, gid_ref, wd_out_ref) = refs[2 * N_GATE_BLOCKS:]
    wd_out_ref[...] = wd_ref[...].astype(bf16)
    oa = oa_ref[...]
    ob = ob_ref[...]
    gpc = MIX_CHUNK // GATE_BLOCK

    def up(c):
        cols = slice(c * MIX_CHUNK, (c + 1) * MIX_CHUNK)
        return (jnp.dot(oa, wua_ref[:, cols], preferred_element_type=f32),
                jnp.dot(ob, wub_ref[:, cols], preferred_element_type=f32))

    x1 = x_ref[...]
    pending = up(0)
    for c in range(D_MODEL // MIX_CHUNK):
        ya, yb = pending
        if (c + 1) * MIX_CHUNK < D_MODEL:
            pending = up(c + 1)
        gate_a = jnp.concatenate([r[...] for r in ga_refs[c * gpc:(c + 1) * gpc]], axis=1)
        gate_b = jnp.concatenate([r[...] for r in gb_refs[c * gpc:(c + 1) * gpc]], axis=1)
        merged = (_sigmoid(gate_a) * ya + _sigmoid(gate_b) * yb).astype(bf16)
        x1 = x1 + jnp.dot(merged, wout_ref[c * MIX_CHUNK:(c + 1) * MIX_CHUNK, :],
                          preferred_element_type=f32)
    x1_ref[...] = x1
    ms = jnp.mean(x1 * x1, axis=-1, keepdims=True)
    h2 = x1 * lax.rsqrt(ms + EPS) * gffn_ref[...]
    tm = h2.shape[0]
    for c in range(ROW_SLAB):
        h2_ref[pl.ds(c, tm, stride=SLAB_IN), :] = h2[:, c * LANES:(c + 1) * LANES]

    h_hi, h_lo = _split2(h2)
    hi_all = jnp.dot(h_hi, wr_ref[...], preferred_element_type=f32)
    lo_hi = jnp.dot(h_lo, wr_ref[:, :ROUTER_COLS], preferred_element_type=f32)
    lt = (hi_all[:, :ROUTER_COLS] + (lo_hi + hi_all[:, ROUTER_COLS:])).T
    gl = [lt[g:g + 1, :] for g in range(N_GROUPS)]
    gmax = functools.reduce(jnp.maximum, gl)
    gsum = functools.reduce(lambda a, b: a + b, [jnp.exp(v - gmax) for v in gl])
    g_p = 1.0 / gsum
    g_i = jnp.full((1, tm), N_GROUPS - 1, jnp.int32)
    for g in reversed(range(N_GROUPS)):
        g_i = jnp.where(gl[g] == gmax, g, g_i)

    el = []
    for e in range(EXPERTS_PER_GROUP):
        v = jnp.zeros((1, tm), f32)
        for g in range(N_GROUPS):
            r = N_GROUPS + g * EXPERTS_PER_GROUP + e
            v = jnp.where(g_i == g, lt[r:r + 1, :], v)
        el.append(v)
    emax = functools.reduce(jnp.maximum, el)
    ex = [jnp.exp(v - emax) for v in el]
    esum = functools.reduce(lambda a, b: a + b, ex)
    ep = [v / esum for v in ex]
    p1 = functools.reduce(jnp.maximum, ep)
    i1 = jnp.full((1, tm), EXPERTS_PER_GROUP - 1, jnp.int32)
    for e in reversed(range(EXPERTS_PER_GROUP)):
        i1 = jnp.where(ep[e] == p1, e, i1)
    rest = [jnp.where(i1 == e, -1.0, ep[e]) for e in range(EXPERTS_PER_GROUP)]
    p2 = functools.reduce(jnp.maximum, rest)
    i2 = jnp.full((1, tm), EXPERTS_PER_GROUP - 1, jnp.int32)
    for e in reversed(range(EXPERTS_PER_GROUP)):
        i2 = jnp.where(rest[e] == p2, e, i2)
    w1 = g_p * (p1 / (p1 + p2))
    w2 = g_p * (p2 / (p1 + p2))
    gid_ref[...] = jnp.concatenate([g_i, jnp.zeros((7, tm), jnp.int32)], axis=0)
    cw = [jnp.where(i1 == e, w1, 0.0) + jnp.where(i2 == e, w2, 0.0) for e in range(EXPERTS_PER_GROUP)]
    cw_t = jnp.concatenate(cw + [jnp.zeros((LANES - EXPERTS_PER_GROUP, tm), f32)], axis=0).T
    for e in range(SLAB_IN - ROW_SLAB):
        if e < EXPERTS_PER_GROUP:
            row = jnp.broadcast_to(cw_t[:, e:e + 1], (tm, LANES))
        else:
            row = jnp.zeros((tm, LANES), f32)
        h2_ref[pl.ds(ROW_SLAB + e, tm, stride=SLAB_IN), :] = row


def _mix(oa, ob, proj, x2, wua, wub, wout, g_ffn, wr, wd, tm=256):
    T = x2.shape[0]
    wd_spec = pl.BlockSpec((wd.shape[0] // (T // tm), wd.shape[1]), lambda i: (i, 0))
    const = lambda i: (0, 0)
    single = pl.Buffered(1)

    def gate_specs(first_col):
        first = first_col * LANES // GATE_BLOCK
        return [pl.BlockSpec((tm, GATE_BLOCK), functools.partial(lambda i, j: (i, j), j=first + k))
                for k in range(N_GATE_BLOCKS)]

    return pl.pallas_call(
        _mix_body,
        grid=(T // tm,),
        in_specs=gate_specs(COL_GATE_A) + gate_specs(COL_GATE_B) + [
            pl.BlockSpec((tm, SWA_Q_DIM), lambda i: (i, 0)),
            pl.BlockSpec((tm, MOBA_DIM), lambda i: (i, 0)),
            pl.BlockSpec((tm, D_MODEL), lambda i: (i, 0)),
            pl.BlockSpec((SWA_Q_DIM, D_MODEL), const, pipeline_mode=single),
            pl.BlockSpec((MOBA_DIM, D_MODEL), const, pipeline_mode=single),
            pl.BlockSpec((D_MODEL, D_MODEL), const, pipeline_mode=single),
            pl.BlockSpec((1, D_MODEL), const),
            pl.BlockSpec((D_MODEL, 2 * ROUTER_COLS), const),
            wd_spec,
        ],
        out_specs=[
            pl.BlockSpec((tm, D_MODEL), lambda i: (i, 0)),
            pl.BlockSpec((tm * SLAB_IN, LANES), lambda i: (i, 0)),
            pl.BlockSpec((8, tm), lambda i: (0, i)),
            wd_spec,
        ],
        out_shape=[
            jax.ShapeDtypeStruct((T, D_MODEL), f32),
            jax.ShapeDtypeStruct((T * SLAB_IN, LANES), f32),
            jax.ShapeDtypeStruct((8, T), jnp.int32),
            jax.ShapeDtypeStruct(wd.shape, bf16),
        ],
        compiler_params=pltpu.CompilerParams(
            dimension_semantics=("arbitrary",),
            vmem_limit_bytes=VMEM_LIMIT),
        name="mix",
    )(*([proj] * (2 * N_GATE_BLOCKS)), oa, ob, x2, wua, wub, wout, g_ffn.reshape(1, D_MODEL), wr, wd)


def _route_tables(gid, T, tm):
    nt = T // tm + N_GROUPS
    g = gid[0]
    onehot = (g[:, None] == jnp.arange(N_GROUPS, dtype=jnp.int32)[None, :]).astype(jnp.int32)
    incl = jnp.cumsum(onehot, axis=0)
    pos = jnp.sum(onehot * incl, axis=1) - 1
    counts = incl[-1]
    padded = (counts + tm - 1) // tm * tm
    seg_end = jnp.cumsum(padded)
    seg_start = seg_end - padded
    dest = jnp.sum(onehot * seg_start[None, :], axis=1) + pos
    tile_start = jnp.arange(nt, dtype=jnp.int32) * tm
    tile_g = jnp.sum((tile_start[:, None] >= seg_end[None, :]).astype(jnp.int32), axis=1)
    tile_g = jnp.minimum(tile_g, N_GROUPS - 1)
    nvalid = (seg_end[-1] // tm).astype(jnp.int32).reshape(1)
    n_pad = padded - counts
    pads = jnp.concatenate([seg_start + counts, seg_end, jnp.cumsum(n_pad) - n_pad])
    return (nvalid, tile_g.astype(jnp.int32), dest.astype(jnp.int32), pads.astype(jnp.int32))


def _moe_body(nvalid_ref, tg_ref, dest_ref, pads_ref, wg_ref, wu_ref, wd_ref, h2s_hbm, ytok_hbm,
              asg, gbuf, ybuf, xbuf, hbuf, gsem, ssem, *, tm, nt, n_tok):
    t = pl.program_id(0)
    e = pl.program_id(1)
    last_e = EXPERTS_PER_GROUP - 1
    nvalid = nvalid_ref[0]
    slot = lax.rem(t, 2)
    other = 1 - slot

    def for_rows(fn):
        def body(r, carry):
            fn(r)
            return carry
        lax.fori_loop(0, tm, body, 0, unroll=8)

    def gather_copy(tok, s, r):
        return pltpu.make_async_copy(
            h2s_hbm.at[pl.ds(pl.multiple_of(tok * SLAB_IN, 8), SLAB_IN), :],
            gbuf.at[pl.ds(pl.multiple_of((s * tm + r) * SLAB_IN, 8), SLAB_IN), :], gsem.at[s])

    def scatter_copy(row, s, r):
        return pltpu.make_async_copy(
            ybuf.at[pl.ds(pl.multiple_of((s * tm + r) * ROW_SLAB, 8), ROW_SLAB), :],
            ytok_hbm.at[pl.ds(pl.multiple_of(row * ROW_SLAB, 8), ROW_SLAB), :], ssem.at[s])

    def start_gathers(tile, s):
        for_rows(lambda r: gather_copy(jnp.maximum(asg[tile * tm + r], 0), s, r).start())

    def start_scatters(tile, s):
        def one(r):
            a = asg[tile * tm + r]
            scatter_copy(jnp.where(a >= 0, a, n_tok - 1 - a), s, r).start()
        for_rows(one)

    def wait_gathers(s):
        for_rows(lambda r: gather_copy(0, s, r).wait())

    def wait_scatters(s):
        for_rows(lambda r: scatter_copy(0, s, r).wait())

    @pl.when((t == 0) & (e == 0))
    def _():
        def place(tok, carry):
            asg[dest_ref[tok]] = tok
            return carry
        lax.fori_loop(0, n_tok, place, 0, unroll=8)

        for g in range(N_GROUPS):
            first = pads_ref[g]
            code = -1 - pads_ref[2 * N_GROUPS + g] + first

            def mark(p, carry):
                asg[p] = code - p
                return carry
            lax.fori_loop(first, pads_ref[N_GROUPS + g], mark, 0)
        ybuf[...] = jnp.zeros(ybuf.shape, f32)
        start_gathers(0, 0)

    @pl.when((e == 0) & (t < nvalid))
    def _():
        wait_gathers(slot)
        base = slot * tm * SLAB_IN
        xbuf[...] = jnp.concatenate(
            [gbuf[pl.ds(base + c, tm, stride=SLAB_IN), :] for c in range(ROW_SLAB)],
            axis=1).astype(bf16)

    @pl.when(t < nvalid)
    def _():
        quarter = tm // EXPERTS_PER_GROUP
        have_next = t + 1 < nvalid
        have_prev = t >= 1
        nxt = jnp.minimum(t + 1, nvalid - 1)

        def issue_rows(r0, r1):
            for r in range(r0, r1):
                row = e * quarter + r
                tok = jnp.where(have_next, jnp.maximum(asg[nxt * tm + row], 0), 0)
                gather_copy(tok, other, row).start()
                a = asg[jnp.maximum(t - 1, 0) * tm + row]
                dst = jnp.where(have_prev, jnp.where(a >= 0, a, n_tok - 1 - a), nt * tm + row)
                scatter_copy(dst, other, row).start(priority=r % 2)

        x = xbuf[...]
        n_chunks = D_EXPERT // PROJ_CHUNK
        per_piece = quarter // (2 * n_chunks)
        pieces = {"g": [], "u": []}
        for i, (name, w_ref) in enumerate((n, w) for c in range(n_chunks)
                                          for n, w in (("g", wg_ref), ("u", wu_ref))):
            c = i // 2
            w = w_ref[:, c * PROJ_CHUNK:(c + 1) * PROJ_CHUNK].astype(bf16)
            pieces[name].append(jnp.dot(x, w, preferred_element_type=f32))
            issue_rows(i * per_piece, (i + 1) * per_piece)
        gte = jnp.concatenate(pieces["g"], axis=1)
        up = jnp.concatenate(pieces["u"], axis=1)
        cw = gbuf[pl.ds(slot * tm * SLAB_IN + ROW_SLAB + e, tm, stride=SLAB_IN), :]
        hid = (gte * _sigmoid(gte)) * up * jnp.concatenate([cw] * (D_EXPERT // LANES), axis=1)
        hbuf[e] = hid.astype(bf16)

    @pl.when((e == last_e) & (t < nvalid))
    def _():
        hcat = jnp.concatenate([hbuf[k] for k in range(EXPERTS_PER_GROUP)], axis=1)
        y = jnp.dot(hcat, wd_ref[...], preferred_element_type=f32)

        @pl.when(t >= 1)
        def _():
            wait_scatters(slot)
        base = slot * tm * ROW_SLAB
        for c in range(ROW_SLAB):
            ybuf[pl.ds(base + c, tm, stride=ROW_SLAB), :] = y[:, c * LANES:(c + 1) * LANES]

        @pl.when(t == nvalid - 1)
        def _():
            start_scatters(t, slot)
            wait_gathers(other)
            wait_scatters(other)
            wait_scatters(slot)

    @pl.when((e == last_e) & (t == nt - 1))
    def _():
        ybuf[0:tm * ROW_SLAB, :] = jnp.zeros((tm * ROW_SLAB, LANES), f32)
        for j in range(n_tok // tm, nt):
            @pl.when(j >= nvalid)
            def _():
                fill = pltpu.make_async_copy(
                    ybuf.at[0:tm * ROW_SLAB, :],
                    ytok_hbm.at[j * tm * ROW_SLAB:(j + 1) * tm * ROW_SLAB, :], ssem.at[0])
                fill.start()
                fill.wait()


def _moe(h2s, tables, wg, wu, wd, T, tm):
    nvalid, tile_g, dest, pads = tables
    nt = tile_g.shape[0]

    def w_index(t, e, nv, tg, de, pd):
        return (tg[t] * EXPERTS_PER_GROUP + jnp.where(t < nv[0], e, EXPERTS_PER_GROUP - 1), 0, 0)

    grid_spec = pltpu.PrefetchScalarGridSpec(
        num_scalar_prefetch=4,
        grid=(nt, EXPERTS_PER_GROUP),
        in_specs=[
            pl.BlockSpec((None, D_MODEL, D_EXPERT), w_index),
            pl.BlockSpec((None, D_MODEL, D_EXPERT), w_index),
            pl.BlockSpec((None, EXPERTS_PER_GROUP * D_EXPERT, D_MODEL),
                         lambda t, e, nv, tg, de, pd: (tg[t], 0, 0), pipeline_mode=pl.Buffered(1)),
            pl.BlockSpec(memory_space=pl.ANY),
        ],
        out_specs=pl.BlockSpec(memory_space=pl.ANY),
        scratch_shapes=[
            pltpu.SMEM((nt * tm,), jnp.int32),
            pltpu.VMEM((2 * tm * SLAB_IN, LANES), f32),
            pltpu.VMEM((2 * tm * ROW_SLAB, LANES), f32),
            pltpu.VMEM((tm, D_MODEL), bf16),
            pltpu.VMEM((EXPERTS_PER_GROUP, tm, D_EXPERT), bf16),
            pltpu.SemaphoreType.DMA((2,)),
            pltpu.SemaphoreType.DMA((2,)),
        ],
    )
    return pl.pallas_call(
        functools.partial(_moe_body, tm=tm, nt=nt, n_tok=T),
        grid_spec=grid_spec,
        out_shape=jax.ShapeDtypeStruct(((nt + 1) * tm * ROW_SLAB, LANES), f32),
        compiler_params=pltpu.CompilerParams(
            dimension_semantics=("arbitrary", "arbitrary"),
            vmem_limit_bytes=VMEM_LIMIT),
        name="moe",
    )(nvalid, tile_g, dest, pads, wg, wu,
      wd.reshape(N_GROUPS, EXPERTS_PER_GROUP * D_EXPERT, D_MODEL), h2s)


def _combine_body(x1_ref, y_ref, o_ref):
    tm = x1_ref.shape[0]
    for c in range(ROW_SLAB):
        cols = slice(c * LANES, (c + 1) * LANES)
        o_ref[:, cols] = x1_ref[:, cols] + y_ref[pl.ds(c, tm, stride=ROW_SLAB), :]


def _combine(x1, ytok, tm=256):
    T = x1.shape[0]
    return pl.pallas_call(
        _combine_body,
        grid=(T // tm,),
        in_specs=[
            pl.BlockSpec((tm, D_MODEL), lambda i: (i, 0)),
            pl.BlockSpec((tm * ROW_SLAB, LANES), lambda i: (i, 0)),
        ],
        out_specs=pl.BlockSpec((tm, D_MODEL), lambda i: (i, 0)),
        out_shape=jax.ShapeDtypeStruct((T, D_MODEL), f32),
        compiler_params=pltpu.CompilerParams(
            dimension_semantics=("arbitrary",),
            vmem_limit_bytes=VMEM_LIMIT),
        name="combine",
    )(x1, ytok)


def _alibi_slopes(n):
    return jnp.exp2(-8.0 * jnp.arange(1, n + 1, dtype=f32) / n)


def kernel(x, g_mix, w_in, q_norm_swa, k_norm_swa, sinks, q_norm_moba, k_norm_moba,
           w_up_swa, w_up_moba, w_out, g_ffn, w_router_group, w_router_expert,
           w_gate_e, w_up_e, w_down_e):
    B, S, D = x.shape
    assert D == D_MODEL and S % MOBA_BLOCK == 0 and S % SWA_BLOCK == 0
    T = B * S
    x2 = x.reshape(T, D)

    proj = _inproj(x2, g_mix, w_in)

    oa, = _swa(proj, sinks.astype(f32), _alibi_slopes(SWA_Q_HEADS), q_norm_swa, k_norm_swa,
               B, S, ())
    ob, wua, wub, wout, wg, wu = _moba(
        proj, _alibi_slopes(MOBA_HEADS), q_norm_moba, k_norm_moba, B, S,
        (w_up_swa, w_up_moba, w_out,
         w_gate_e.reshape(N_EXPERTS * D_MODEL, D_EXPERT),
         w_up_e.reshape(N_EXPERTS * D_MODEL, D_EXPERT)))

    wr = jnp.concatenate(
        [w_router_group,
         w_router_expert.transpose(1, 0, 2).reshape(D, N_EXPERTS),
         jnp.zeros((D, ROUTER_COLS - N_GROUPS - N_EXPERTS), f32)], axis=1)
    wr_hi = wr.astype(bf16)
    wr_lo = (wr - wr_hi.astype(f32)).astype(bf16)
    x1, h2s, gid, wd = _mix(oa, ob, proj, x2, wua, wub, wout, g_ffn,
                            jnp.concatenate([wr_hi, wr_lo], axis=1),
                            w_down_e.reshape(N_EXPERTS * D_EXPERT, D_MODEL))

    tables = _route_tables(gid, T, MOE_TILE)
    ytok = _moe(h2s, tables, wg.reshape(w_gate_e.shape), wu.reshape(w_up_e.shape), wd, T, MOE_TILE)
    y = _combine(x1, ytok)
    return y.reshape(B, S, D)
```
